```python
import math
import jax, jax.numpy as jnp
from jax import lax
import numpy as np

D_MODEL = 1024
BATCH = 32
SEQ = 2048
DEPTH = 4

SSM_WIDTH = D_MODEL // 4
SSM_GROUP = 16
SSM_GROUPS = SSM_WIDTH // SSM_GROUP
SSM_STATE = 64
HEAD_DIM = 64
DSA_HEADS = (3 * D_MODEL // 8) // HEAD_DIM
DSA_WIDTH = DSA_HEADS * HEAD_DIM
DSA_LATENT = 2 * HEAD_DIM
IDX_HEADS = 4
IDX_DIM = 64
TOPK_MAX = 256
FOX_HEADS = (3 * D_MODEL // 8) // HEAD_DIM
FOX_WIDTH = FOX_HEADS * HEAD_DIM
REL_BUCKETS = 32
REL_MAX_EXACT = 16
REL_MAX_DIST = 128
FFN_HIDDEN = 2816
BLOCK_Q = 128
EPS = 1e-6
NEG = -1e30
IN_SPLITS = (SSM_WIDTH, DSA_WIDTH, DSA_LATENT, IDX_HEADS * IDX_DIM, IDX_DIM, IDX_HEADS,
             FOX_WIDTH, FOX_WIDTH, FOX_WIDTH, FOX_HEADS, D_MODEL, D_MODEL, D_MODEL)
IN_WIDTH = sum(IN_SPLITS)

kernel_name = 'hybrid_s5_dsa_fox_gated_block'


def _split_points():
    pts, acc = [], 0
    for w in IN_SPLITS[:-1]:
        acc += w
        pts.append(acc)
    return pts


def rms_norm(x, g):
    xf = x.astype(jnp.float32)
    y = xf * lax.rsqrt(jnp.mean(xf * xf, axis=-1, keepdims=True) + EPS)
    return (y * g.astype(jnp.float32)).astype(x.dtype)


def swiglu(h, w_gate, w_up, w_down):
    return (jax.nn.silu(h @ w_gate) * (h @ w_up)) @ w_down


def t5_bucket(dist):
    d = jnp.maximum(dist, 0)
    df = jnp.maximum(d, 1).astype(jnp.float32)
    log_b = REL_MAX_EXACT + (jnp.log(df / REL_MAX_EXACT) / math.log(REL_MAX_DIST / REL_MAX_EXACT)
                             * (REL_BUCKETS - REL_MAX_EXACT)).astype(jnp.int32)
    log_b = jnp.minimum(log_b, REL_BUCKETS - 1)
    return jnp.where(d < REL_MAX_EXACT, d, log_b)


def _cplx_scan_combine(e1, e2):
    a1r, a1i, b1r, b1i = e1
    a2r, a2i, b2r, b2i = e2
    return (a2r * a1r - a2i * a1i,
            a2r * a1i + a2i * a1r,
            a2r * b1r - a2i * b1i + b2r,
            a2r * b1i + a2i * b1r + b2i)


def s5_branch(u, lam_re, lam_im, log_dt, b_re, b_im, c_re, c_im, d_skip, w_glu):
    bsz, seq, _ = u.shape
    f32 = jnp.float32
    uf = u.astype(f32).reshape(bsz, seq, SSM_GROUPS, SSM_GROUP)
    dt = jnp.exp(log_dt.astype(f32))[:, None]
    lr, li = lam_re.astype(f32), lam_im.astype(f32)
    mag = jnp.exp(lr * dt)
    ab_re, ab_im = mag * jnp.cos(li * dt), mag * jnp.sin(li * dt)
    den = lr * lr + li * li
    nr, ni = ab_re - 1.0, ab_im
    s_re = (nr * lr + ni * li) / den
    s_im = (ni * lr - nr * li) / den
    br, bi = b_re.astype(f32), b_im.astype(f32)
    bb_re = s_re[..., None] * br - s_im[..., None] * bi
    bb_im = s_re[..., None] * bi + s_im[..., None] * br
    bu_re = jnp.einsum('gnp,bsgp->bsgn', bb_re, uf)
    bu_im = jnp.einsum('gnp,bsgp->bsgn', bb_im, uf)
    a_re = jnp.broadcast_to(ab_re, bu_re.shape)
    a_im = jnp.broadcast_to(ab_im, bu_im.shape)
    _, _, h_re, h_im = lax.associative_scan(_cplx_scan_combine, (a_re, a_im, bu_re, bu_im), axis=1)
    y = (jnp.einsum('gpn,bsgn->bsgp', c_re.astype(f32), h_re)
         - jnp.einsum('gpn,bsgn->bsgp', c_im.astype(f32), h_im))
    y = y.reshape(bsz, seq, SSM_WIDTH) + d_skip.astype(f32) * u.astype(f32)
    z = jax.nn.gelu(y).astype(u.dtype) @ w_glu
    val, gate = jnp.split(z, 2, axis=-1)
    return val * jax.nn.sigmoid(gate)


def dsa_branch(q, c_kv, q_idx, k_idx, w_idx, kv_norm, w_uk, w_uv, rel_bias, n_keys):
    bsz, seq = q.shape[0], q.shape[1]
    f32 = jnp.float32
    c = rms_norm(c_kv, kv_norm)
    k = c @ w_uk
    v = c @ w_uv
    qi = q_idx.astype(f32) * (IDX_DIM ** -0.5)
    ki = k_idx.astype(f32)
    wi = w_idx.astype(f32) * (IDX_HEADS ** -0.5)
    scale = HEAD_DIM ** -0.5
    gather = jax.vmap(lambda kb, ib: kb[ib])
    outs = []
    for start in range(0, seq, BLOCK_Q):
        end = start + BLOCK_Q
        qpos = jnp.arange(start, end)
        kpos = jnp.arange(end)
        causal = kpos[None, :] <= qpos[:, None]
        dots = jnp.einsum('bqhd,bkd->bqhk', qi[:, start:end], ki[:, :end])
        score = jnp.einsum('bqh,bqhk->bqk', wi[:, start:end], jax.nn.relu(dots))
        score = jnp.where(causal[None], score, NEG)
        n_sel = min(n_keys, end)
        _, sel = lax.top_k(score, n_sel)
        k_sel = gather(k, sel)
        v_sel = gather(v, sel)
        valid = sel <= qpos[None, :, None]
        bias = jnp.moveaxis(rel_bias[t5_bucket(qpos[None, :, None] - sel)], -1, 2)
        s = (jnp.einsum('bqhd,bqkd->bqhk', q[:, start:end], k_sel).astype(f32) * scale
             + bias.astype(f32))
        s = jnp.where(valid[:, :, None, :], s, NEG)
        p = jax.nn.softmax(s, axis=-1).astype(v.dtype)
        outs.append(jnp.einsum('bqhk,bqkd->bqhd', p, v_sel))
    return jnp.concatenate(outs, axis=1).reshape(bsz, seq, DSA_WIDTH)


def fox_branch(q, k, v, f_logit):
    bsz, seq = q.shape[0], q.shape[1]
    f32 = jnp.float32
    cum = jnp.swapaxes(jnp.cumsum(jax.nn.log_sigmoid(f_logit.astype(f32)), axis=1), 1, 2)
    scale = HEAD_DIM ** -0.5
    outs = []
    for start in range(0, seq, BLOCK_Q):
        end = start + BLOCK_Q
        qpos = jnp.arange(start, end)
        kpos = jnp.arange(end)
        causal = kpos[None, :] <= qpos[:, None]
        s = (jnp.einsum('bqhd,bkhd->bhqk', q[:, start:end], k[:, :end]).astype(f32) * scale
             + cum[:, :, start:end, None] - cum[:, :, None, :end])
        s = jnp.where(causal[None, None], s, NEG)
        p = jax.nn.softmax(s, axis=-1).astype(v.dtype)
        outs.append(jnp.einsum('bhqk,bkhd->bqhd', p, v[:, :end]))
    return jnp.concatenate(outs, axis=1).reshape(bsz, seq, FOX_WIDTH)


def setup_inputs(seed: int = 0) -> dict:
    key = jax.random.key(seed)
    ks = jax.random.split(key, 30)
    L, D, F = DEPTH, D_MODEL, FFN_HIDDEN
    G, N, P = SSM_GROUPS, SSM_STATE, SSM_GROUP

    def nrm(k, shape, scale):
        return jax.random.normal(k, shape, jnp.float32) * scale

    def gain(k, shape):
        return 1.0 + 0.01 * jax.random.normal(k, shape, jnp.float32)

    lam_im0 = jnp.pi * jnp.arange(N, dtype=jnp.float32)
    return {
        'x': jax.random.normal(ks[0], (BATCH, SEQ, D), jnp.float32),
        'ffn1_norm': gain(ks[1], (L, D)),
        'ffn1_w_gate': nrm(ks[2], (L, D, F), D ** -0.5),
        'ffn1_w_up': nrm(ks[3], (L, D, F), D ** -0.5),
        'ffn1_w_down': nrm(ks[4], (L, F, D), F ** -0.5),
        'mix_norm': gain(ks[5], (L, D)),
        'w_in': nrm(ks[6], (L, D, IN_WIDTH), D ** -0.5),
        'ssm_lambda_re': -0.5 + 0.01 * jax.random.normal(ks[7], (L, G, N), jnp.float32),
        'ssm_lambda_im': lam_im0 + 0.01 * jax.random.normal(ks[8], (L, G, N), jnp.float32),
        'ssm_log_dt': jax.random.uniform(ks[9], (L, G), jnp.float32, math.log(1e-3), math.log(1e-1)),
        'ssm_b_re': nrm(ks[10], (L, G, N, P), (2 * P) ** -0.5),
        'ssm_b_im': nrm(ks[11], (L, G, N, P), (2 * P) ** -0.5),
        'ssm_c_re': nrm(ks[12], (L, G, P, N), (2 * N) ** -0.5),
        'ssm_c_im': nrm(ks[13], (L, G, P, N), (2 * N) ** -0.5),
        'ssm_d': nrm(ks[14], (L, SSM_WIDTH), 1.0),
        'ssm_w_glu': nrm(ks[15], (L, SSM_WIDTH, 2 * SSM_WIDTH), SSM_WIDTH ** -0.5),
        'dsa_kv_norm': gain(ks[16], (L, DSA_LATENT)),
        'dsa_w_uk': nrm(ks[17], (L, DSA_LATENT, HEAD_DIM), DSA_LATENT ** -0.5),
        'dsa_w_uv': nrm(ks[18], (L, DSA_LATENT, HEAD_DIM), DSA_LATENT ** -0.5),
        'rel_bias': nrm(ks[19], (REL_BUCKETS, DSA_HEADS), 0.1),
        'fox_b_f': jax.random.uniform(ks[20], (L, FOX_HEADS), jnp.float32, 1.0, 4.0),
        'w_branch_ssm': nrm(ks[21], (L, SSM_WIDTH, D), SSM_WIDTH ** -0.5),
        'w_branch_dsa': nrm(ks[22], (L, DSA_WIDTH, D), DSA_WIDTH ** -0.5),
        'w_branch_fox': nrm(ks[23], (L, FOX_WIDTH, D), FOX_WIDTH ** -0.5),
        'w_out': nrm(ks[24], (L, D, D), D ** -0.5),
        'ffn2_norm': gain(ks[25], (L, D)),
        'ffn2_w_gate': nrm(ks[26], (L, D, F), D ** -0.5),
        'ffn2_w_up': nrm(ks[27], (L, D, F), D ** -0.5),
        'ffn2_w_down': nrm(ks[28], (L, F, D), F ** -0.5),
        'final_norm': gain(ks[29], (D,)),
    }


def reference(x, ffn1_norm, ffn1_w_gate, ffn1_w_up, ffn1_w_down, mix_norm, w_in,
              ssm_lambda_re, ssm_lambda_im, ssm_log_dt, ssm_b_re, ssm_b_im, ssm_c_re, ssm_c_im,
              ssm_d, ssm_w_glu, dsa_kv_norm, dsa_w_uk, dsa_w_uv, rel_bias, fox_b_f,
              w_branch_ssm, w_branch_dsa, w_branch_fox, w_out,
              ffn2_norm, ffn2_w_gate, ffn2_w_up, ffn2_w_down, final_norm):
    bsz, seq, _ = x.shape
    n_keys = min(TOPK_MAX, seq // 4)
    pts = _split_points()
    for l in range(DEPTH):
        h = rms_norm(x, ffn1_norm[l])
        x = x + 0.5 * swiglu(h, ffn1_w_gate[l], ffn1_w_up[l], ffn1_w_down[l])
        h = rms_norm(x, mix_norm[l])
        (u, q_b, c_kv, q_i, k_i, w_i, q_c, k_c, v_c, f_c, g_a, g_b, g_c) = jnp.split(h @ w_in[l], pts, axis=-1)
        a = s5_branch(u, ssm_lambda_re[l], ssm_lambda_im[l], ssm_log_dt[l], ssm_b_re[l], ssm_b_im[l],
                      ssm_c_re[l], ssm_c_im[l], ssm_d[l], ssm_w_glu[l])
        b = dsa_branch(q_b.reshape(bsz, seq, DSA_HEADS, HEAD_DIM), c_kv,
                       q_i.reshape(bsz, seq, IDX_HEADS, IDX_DIM), k_i, w_i,
                       dsa_kv_norm[l], dsa_w_uk[l], dsa_w_uv[l], rel_bias, n_keys)
        c = fox_branch(q_c.reshape(bsz, seq, FOX_HEADS, HEAD_DIM),
                       k_c.reshape(bsz, seq, FOX_HEADS, HEAD_DIM),
                       v_c.reshape(bsz, seq, FOX_HEADS, HEAD_DIM),
                       f_c + fox_b_f[l])
        merged = (jax.nn.sigmoid(g_a) * (a @ w_branch_ssm[l])
                  + jax.nn.sigmoid(g_b) * (b @ w_branch_dsa[l])
                  + jax.nn.sigmoid(g_c) * (c @ w_branch_fox[l]))
        x = x + merged @ w_out[l]
        h = rms_norm(x, ffn2_norm[l])
        x = x + 0.5 * swiglu(h, ffn2_w_gate[l], ffn2_w_up[l], ffn2_w_down[l])
    return rms_norm(x, final_norm)
```

```python
import functools
import math

import numpy as np
import jax
import jax.numpy as jnp
from jax import lax
from jax.experimental import pallas as pl
from jax.experimental.pallas import tpu as pltpu

F32 = jnp.float32
BF16 = jnp.bfloat16

D_MODEL = 1024
SSM_WIDTH = D_MODEL // 4
SSM_GROUP = 16
SSM_GROUPS = SSM_WIDTH // SSM_GROUP
SSM_STATE = 64
HEAD_DIM = 64
DSA_HEADS = 6
DSA_WIDTH = DSA_HEADS * HEAD_DIM
DSA_LATENT = 2 * HEAD_DIM
IDX_HEADS = 4
IDX_DIM = 64
TOPK_MAX = 256
FOX_HEADS = 6
FOX_WIDTH = FOX_HEADS * HEAD_DIM
REL_BUCKETS = 32
REL_MAX_EXACT = 16
REL_MAX_DIST = 128
FFN_HIDDEN = 2816
EPS = 1e-6
NEG = -1e30

LANES = 128
S5_CHUNK = 8
S5_COLS = S5_CHUNK * SSM_WIDTH
S5_STATE = 2 * SSM_GROUPS * SSM_STATE
ATT_TILE = 256
VMEM_LIMIT = 56 * 1024 * 1024

_SEG = dict(u=256, qb=384, qi=256, ki=128, ckv=128, aux=128, qkv=1152, g=3072)
AUX_F = 0
AUX_W = 8


def _cparams(sem):
    return pltpu.CompilerParams(dimension_semantics=sem, vmem_limit_bytes=VMEM_LIMIT)


def _resident(shape):
    nd = len(shape)
    return pl.BlockSpec(shape, lambda *_: (0,) * nd, pipeline_mode=pl.Buffered(1))


def _rms(x, g):
    return x * lax.rsqrt(jnp.mean(x * x, axis=-1, keepdims=True) + EPS) * g


def _sigmoid(x):
    return 1.0 / (1.0 + jnp.exp(-x))


def _dot(a, b):
    return jnp.dot(a, b, preferred_element_type=F32)


def _dot_nt(a, b):
    return lax.dot_general(a, b, (((1,), (1,)), ((), ())), preferred_element_type=F32)


def _ffn_body(x, g_ref, wg_ref, wu_ref, wd_ref, fc):
    h = _rms(x, g_ref[...]).astype(BF16)
    acc = None
    for c in range(FFN_HIDDEN // fc):
        sl = slice(c * fc, (c + 1) * fc)
        gt = _dot(h, wg_ref[:, sl])
        up = _dot(h, wu_ref[:, sl])
        a = (gt * _sigmoid(gt) * up).astype(BF16)
        d = _dot(a, wd_ref[sl, :])
        acc = d if acc is None else acc + d
    return x + 0.5 * acc


def _ffn_kernel(x_ref, g_ref, wg_ref, wu_ref, wd_ref, o_ref, *, fc):
    o_ref[...] = _ffn_body(x_ref[...], g_ref, wg_ref, wu_ref, wd_ref, fc)


def _ffn_final_kernel(x_ref, g_ref, wg_ref, wu_ref, wd_ref, fn_ref, o_ref, *, fc):
    y = _ffn_body(x_ref[...], g_ref, wg_ref, wu_ref, wd_ref, fc)
    o_ref[...] = _rms(y, fn_ref[...])


def _ffn(xt, norm, wg, wu, wd, final_norm=None, tm=512, fc=1408):
    m, d = xt.shape
    f = wg.shape[1]
    row = pl.BlockSpec((tm, d), lambda i: (i, 0))
    vec = _resident((1, d))
    in_specs = [row, vec, _resident((d, f)), _resident((d, f)), _resident((f, d))]
    args = [xt, norm.reshape(1, d), wg, wu, wd]
    if final_norm is None:
        body = functools.partial(_ffn_kernel, fc=fc)
    else:
        body = functools.partial(_ffn_final_kernel, fc=fc)
        in_specs.append(vec)
        args.append(final_norm.reshape(1, d))
    return pl.pallas_call(
        body, grid=(m // tm,), in_specs=in_specs, out_specs=row,
        out_shape=jax.ShapeDtypeStruct((m, d), F32),
        compiler_params=_cparams(("parallel",)), name="ffn")(*args)


def _inproj_kernel(x_ref, g_ref, w_ref, kvn_ref, wkv_ref,
                   u_ref, qb_ref, qi_ref, ki_ref, kv_ref, aux_ref, qkv_ref, gate_ref):
    h = _rms(x_ref[...], g_ref[...]).astype(BF16)
    off = 0
    outs = dict(u=u_ref, qb=qb_ref, qi=qi_ref, ki=ki_ref, aux=aux_ref, qkv=qkv_ref, g=gate_ref)
    for name, width in _SEG.items():
        r = _dot(h, w_ref[:, off:off + width])
        off += width
        if name == "ckv":
            c = _rms(r, kvn_ref[...]).astype(BF16)
            kv_ref[...] = _dot(c, wkv_ref[...]).astype(BF16)
        else:
            outs[name][...] = r.astype(outs[name].dtype)


def _inproj(xt, norm, w_all, kv_norm, w_kv, tm=512):
    m, d = xt.shape
    wtot = w_all.shape[1]
    widths = dict(u=(256, F32), qb=(384, BF16), qi=(256, BF16), ki=(128, BF16),
                  kv=(256, BF16), aux=(128, F32), qkv=(1152, BF16), g=(3072, BF16))
    out_shape = [jax.ShapeDtypeStruct((m, w), dt) for w, dt in widths.values()]
    out_specs = [pl.BlockSpec((tm, w), lambda i: (i, 0)) for w, _ in widths.values()]
    return pl.pallas_call(
        _inproj_kernel, grid=(m // tm,),
        in_specs=[pl.BlockSpec((tm, d), lambda i: (i, 0)), _resident((1, d)),
                  _resident((d, wtot)), _resident((1, DSA_LATENT)),
                  _resident((DSA_LATENT, 256))],
        out_specs=out_specs, out_shape=out_shape,
        compiler_params=_cparams(("parallel",)), name="inproj",
    )(xt, norm.reshape(1, d), w_all, kv_norm.reshape(1, DSA_LATENT), w_kv)


def _s5a_kernel(u_ref, mi_ref, ms_ref, yi_ref, p_ref):
    u = u_ref[...].astype(BF16)
    yi_ref[...] = _dot(u, mi_ref[...])
    p_ref[...] = _dot(u, ms_ref[...])


def _s5a(u8, m_intra, m_sum, tr=512):
    r = u8.shape[0]
    row = pl.BlockSpec((tr, S5_COLS), lambda i: (i, 0))
    srow = pl.BlockSpec((tr, S5_STATE), lambda i: (i, 0))
    return pl.pallas_call(
        _s5a_kernel, grid=(r // tr,),
        in_specs=[row, _resident((S5_COLS, S5_COLS)), _resident((S5_COLS, S5_STATE))],
        out_specs=[row, srow],
        out_shape=[jax.ShapeDtypeStruct((r, S5_COLS), F32),
                   jax.ShapeDtypeStruct((r, S5_STATE), F32)],
        compiler_params=_cparams(("parallel",)), name="s5_chunk")(u8, m_intra, m_sum)


def _s5b_kernel(pr_ref, pi_ref, ar_ref, ai_ref, hr_ref, hi_ref, *, nb, nc):
    ar = ar_ref[...]
    ai = ai_ref[...]
    cols = ar.shape[1]

    def step(c, carry):
        hr, hi = carry
        rows = pl.ds(c, nb, stride=nc)
        hr_ref[rows, :] = hr.astype(hr_ref.dtype)
        hi_ref[rows, :] = hi.astype(hi_ref.dtype)
        pr = pr_ref[rows, :]
        pi = pi_ref[rows, :]
        return (ar * hr - ai * hi + pr, ar * hi + ai * hr + pi)

    z = jnp.zeros((nb, cols), F32)
    lax.fori_loop(0, nc, step, (z, z))


def _s5b(p, a_re, a_im, nc, nb, tcol=LANES):
    r = p.shape[0]
    half = S5_STATE // 2
    ncb = half // tcol
    blk = pl.BlockSpec((nb * nc, tcol), lambda i, j: (i, j))
    im_blk = pl.BlockSpec((nb * nc, tcol), lambda i, j: (i, j + ncb))
    a_blk = pl.BlockSpec((1, tcol), lambda i, j: (0, j))
    return pl.pallas_call(
        functools.partial(_s5b_kernel, nb=nb, nc=nc), grid=(r // (nb * nc), ncb),
        in_specs=[blk, im_blk, a_blk, a_blk],
        out_specs=[blk, blk],
        out_shape=[jax.ShapeDtypeStruct((r, half), F32)] * 2,
        compiler_params=_cparams(("parallel", "parallel")), name="s5_scan",
    )(p, p, a_re, a_im)


def _s5c_kernel(yi_ref, hr_ref, hi_ref, mo_ref, y_ref):
    half = S5_STATE // 2
    y = yi_ref[...]
    y = y + _dot(hr_ref[...].astype(BF16), mo_ref[:half, :])
    y = y + _dot(hi_ref[...].astype(BF16), mo_ref[half:, :])
    y_ref[...] = y


def _s5c(yi, h_re, h_im, m_out, tr=512):
    r = yi.shape[0]
    half = S5_STATE // 2
    row = pl.BlockSpec((tr, S5_COLS), lambda i: (i, 0))
    return pl.pallas_call(
        _s5c_kernel, grid=(r // tr,),
        in_specs=[row, pl.BlockSpec((tr, half), lambda i: (i, 0)),
                  pl.BlockSpec((tr, half), lambda i: (i, 0)),
                  _resident((S5_STATE, S5_COLS))],
        out_specs=row, out_shape=jax.ShapeDtypeStruct((r, S5_COLS), F32),
        compiler_params=_cparams(("parallel",)), name="s5_out")(yi, h_re, h_im, m_out)


def _s5_matrices(lam_re, lam_im, log_dt, b_re, b_im, c_re, c_im):
    hp = lax.Precision.HIGHEST
    t = S5_CHUNK
    dt = jnp.exp(log_dt)[:, None]
    lr, li = lam_re, lam_im
    mag = jnp.exp(lr * dt)
    ab_re, ab_im = mag * jnp.cos(li * dt), mag * jnp.sin(li * dt)
    den = lr * lr + li * li
    nr, ni = ab_re - 1.0, ab_im
    s_re = (nr * lr + ni * li) / den
    s_im = (ni * lr - nr * li) / den
    bb_re = s_re[..., None] * b_re - s_im[..., None] * b_im
    bb_im = s_re[..., None] * b_im + s_im[..., None] * b_re
    pr, pi = [jnp.ones_like(ab_re)], [jnp.zeros_like(ab_re)]
    for _ in range(t):
        pr.append(pr[-1] * ab_re - pi[-1] * ab_im)
        pi.append(pr[-2] * ab_im + pi[-1] * ab_re)
    pw_re, pw_im = jnp.stack(pr), jnp.stack(pi)
    ab_b_re = pw_re[..., None] * bb_re - pw_im[..., None] * bb_im
    ab_b_im = pw_re[..., None] * bb_im + pw_im[..., None] * bb_re
    kern = (jnp.einsum('gpn,dgnq->dgpq', c_re, ab_b_re[:t], precision=hp)
            - jnp.einsum('gpn,dgnq->dgpq', c_im, ab_b_im[:t], precision=hp))
    eye_g = jnp.eye(SSM_GROUPS, dtype=F32)
    lag = np.arange(t)[None, :] - np.arange(t)[:, None]
    sel = jnp.asarray((lag[None] == np.arange(t)[:, None, None]).astype(np.float32))
    m_intra = jnp.einsum('dji,dgpq,gh->jgqihp', sel, kern, eye_g, precision=hp)
    m_intra = m_intra.reshape(S5_COLS, S5_COLS)
    sb_re = ab_b_re[:t][::-1]
    sb_im = ab_b_im[:t][::-1]
    m_sum_re = jnp.einsum('jgnq,gh->jgqhn', sb_re, eye_g, precision=hp)
    m_sum_im = jnp.einsum('jgnq,gh->jgqhn', sb_im, eye_g, precision=hp)
    m_sum = jnp.concatenate([m_sum_re.reshape(S5_COLS, -1), m_sum_im.reshape(S5_COLS, -1)], axis=1)
    w_re = c_re[None] * pw_re[1:, :, None, :] - c_im[None] * pw_im[1:, :, None, :]
    w_im = c_re[None] * pw_im[1:, :, None, :] + c_im[None] * pw_re[1:, :, None, :]
    m_out_re = jnp.einsum('igpn,gh->gnihp', w_re, eye_g, precision=hp)
    m_out_im = jnp.einsum('igpn,gh->gnihp', -w_im, eye_g, precision=hp)
    m_out = jnp.concatenate([m_out_re.reshape(-1, S5_COLS), m_out_im.reshape(-1, S5_COLS)], axis=0)
    a_re = pw_re[t].reshape(1, -1)
    a_im = pw_im[t].reshape(1, -1)
    return m_intra.astype(BF16), m_sum.astype(BF16), m_out.astype(BF16), a_re, a_im


def _split_heads(q_ref, qm_ref, n_heads):
    tq = q_ref.shape[0]
    low = lax.broadcasted_iota(jnp.int32, (tq, LANES), 1) < HEAD_DIM
    for h in range(n_heads):
        pair = q_ref[:, (h // 2) * LANES:(h // 2 + 1) * LANES]
        keep = low if h % 2 == 0 else jnp.logical_not(low)
        qm_ref[h] = jnp.where(keep, pair, jnp.zeros_like(pair))


def _softmax_step(h, s, v_aug, m_ref, acc_ref):
    m_old = m_ref[h]
    m_new = jnp.maximum(m_old, jnp.max(s, axis=1, keepdims=True))
    alpha = jnp.exp(m_old - m_new)
    p = jnp.exp(s - jnp.concatenate([m_new] * (s.shape[1] // LANES), axis=1))
    acc_ref[h] = acc_ref[h] * alpha + _dot(p.astype(BF16), v_aug)
    m_ref[h] = m_new


def _finish_heads(acc_ref, o_ref, n_heads):
    tq = o_ref.shape[0]
    low = lax.broadcasted_iota(jnp.int32, (tq, LANES), 1) < HEAD_DIM
    for pr in range(n_heads // 2):
        ae = acc_ref[2 * pr]
        ao = acc_ref[2 * pr + 1]
        oe = ae / pltpu.roll(ae, HEAD_DIM, 1)
        oo = ao / pltpu.roll(ao, HEAD_DIM, 1)
        o_ref[:, pr * LANES:(pr + 1) * LANES] = jnp.where(low, oe, oo).astype(o_ref.dtype)


def _augment_values(vv, ve_ref, vo_ref, base):
    low = lax.broadcasted_iota(jnp.int32, vv.shape, 1) < HEAD_DIM
    one = jnp.ones_like(vv)
    ve_ref[base] = jnp.where(low, vv, one)
    vo_ref[base] = jnp.where(low, one, vv)


def _dsa_kernel(q_ref, qi_ref, aux_ref, kv_ref, ki_ref, band_ref, o_ref,
                score_ref, ve_ref, vo_ref, qm_ref, m_ref, acc_ref, thr_ref, jcut_ref,
                *, n_keys, seq):
    tq = ATT_TILE
    qt = pl.program_id(1)
    n_chunks = qt + 1
    kf = float(n_keys)

    @pl.when(qt == 0)
    def _():
        _augment_values(kv_ref[:, LANES:2 * LANES], ve_ref, vo_ref, 0)

    qi = qi_ref[...]
    low = lax.broadcasted_iota(jnp.int32, (tq, LANES), 1) < HEAD_DIM
    parts = []
    for h in range(IDX_HEADS):
        pair = qi[:, (h // 2) * LANES:(h // 2 + 1) * LANES]
        keep = low if h % 2 == 0 else jnp.logical_not(low)
        parts.append(jnp.where(keep, pair, jnp.zeros_like(pair)))
    q_stack = jnp.concatenate(parts, axis=0)
    head_w = [aux_ref[:, AUX_W + h:AUX_W + h + 1] * (IDX_HEADS ** -0.5 * IDX_DIM ** -0.5)
              for h in range(IDX_HEADS)]
    col_minus_row = (lax.broadcasted_iota(jnp.int32, (tq, tq), 1)
                     - lax.broadcasted_iota(jnp.int32, (tq, tq), 0))

    def score_body(j, carry):
        mx, mn = carry
        kc = ki_ref[pl.ds(pl.multiple_of(j * tq, tq), tq), :]
        d = _dot_nt(q_stack, kc)
        sc = head_w[0] * jnp.maximum(d[0:tq], 0.0)
        for h in range(1, IDX_HEADS):
            sc = sc + head_w[h] * jnp.maximum(d[h * tq:(h + 1) * tq], 0.0)
        causal = col_minus_row <= (qt - j) * tq
        score_ref[j] = jnp.where(causal, sc, NEG)
        mx = jnp.maximum(mx, jnp.where(causal, sc, NEG))
        mn = jnp.minimum(mn, jnp.where(causal, sc, -NEG))
        return mx, mn

    mx, mn = lax.fori_loop(0, n_chunks, score_body,
                           (jnp.full((tq, tq), NEG, F32), jnp.full((tq, tq), -NEG, F32)))

    t_row = qt * tq + lax.broadcasted_iota(jnp.int32, (tq, 1), 0)
    thr_ref[...] = jnp.full((tq, LANES), NEG, F32)
    jcut_ref[...] = jnp.full((tq, LANES), -1.0, F32)

    def count(pred):
        def body(j, acc):
            return acc + pred(j, score_ref[j])
        acc = lax.fori_loop(0, n_chunks, body, jnp.zeros((tq, tq), F32))
        return jnp.sum(acc, axis=1, keepdims=True)

    def count_ge(x):
        xb = jnp.broadcast_to(x, (tq, tq))
        return count(lambda j, s: jnp.where(s >= xb, 1.0, 0.0))

    @pl.when((qt + 1) * tq > n_keys)
    def _search():
        need = (t_row >= n_keys).astype(F32)
        lo0 = jnp.min(mn, axis=1, keepdims=True)
        hi0 = jnp.max(mx, axis=1, keepdims=True)
        c_hi0 = count_ge(hi0)
        c_lo0 = (t_row + 1).astype(F32)
        hit0 = need * (c_hi0 == kf).astype(F32)
        tie0 = need * (c_hi0 > kf).astype(F32)
        done0 = jnp.maximum(1.0 - need, jnp.maximum(hit0, tie0))
        lo0 = jnp.where(tie0 > 0, hi0, lo0)
        c_hi0 = jnp.where(tie0 > 0, 0.0, c_hi0)
        state0 = (jnp.int32(0), lo0, hi0, c_lo0, c_hi0, done0, tie0, hi0,
                  jnp.max(1.0 - done0))

        def cond(st):
            return st[8] > 0.0

        def body(st):
            it, lo, hi, c_lo, c_hi, done, tie, x_fin, _ = st
            mid = 0.5 * lo + 0.5 * hi
            frac = (c_lo - kf) / (c_lo - c_hi)
            interp = lo + (hi - lo) * frac
            x = jnp.where(it % 2 == 0, interp, mid)
            x = jnp.where((x > lo) & (x < hi), x, mid)
            inside = ((x > lo) & (x < hi)).astype(F32)
            active = 1.0 - done
            probe = active * inside
            c = count_ge(jnp.where(probe > 0, x, x_fin))
            hit = probe * (c == kf).astype(F32)
            more = probe * (c > kf).astype(F32)
            less = probe * (c < kf).astype(F32)
            new_tie = active * (1.0 - inside)
            lo = jnp.where(more > 0, x, lo)
            c_lo = jnp.where(more > 0, c, c_lo)
            hi = jnp.where(less > 0, x, hi)
            c_hi = jnp.where(less > 0, c, c_hi)
            x_fin = jnp.where(hit > 0, x, x_fin)
            tie = jnp.maximum(tie, new_tie)
            done = jnp.maximum(done, jnp.maximum(hit, new_tie))
            return (it + 1, lo, hi, c_lo, c_hi, done, tie, x_fin, jnp.max(1.0 - done))

        st = lax.while_loop(cond, body, state0)
        _, lo, _, _, c_hi, _, tie, x_fin, _ = st
        thr = jnp.where(need > 0, jnp.where(tie > 0, lo, x_fin), NEG)
        jcut = jnp.where(need > 0, jnp.where(tie > 0, -1.0, float(seq)), -1.0)
        thr_ref[...] = jnp.broadcast_to(thr, (tq, LANES))
        jcut_ref[...] = jnp.broadcast_to(jcut, (tq, LANES))

        @pl.when(jnp.max(tie) > 0.0)
        def _ties():
            want = kf - c_hi
            thr_b = jnp.broadcast_to(thr, (tq, tq))
            colf = lax.broadcasted_iota(jnp.int32, (tq, tq), 1).astype(F32)

            def count_eq_upto(jm):
                jb = jnp.broadcast_to(jm, (tq, tq))

                def pred(j, s):
                    idx = colf + (j * tq).astype(F32)
                    return jnp.where(s == thr_b, jnp.where(idx <= jb, 1.0, 0.0), 0.0)
                return count(pred)

            j_lo = jnp.full((tq, 1), -1.0, F32)
            j_hi = jnp.full((tq, 1), float(seq - 1), F32)
            for _ in range(int(math.ceil(math.log2(seq))) + 1):
                j_mid = jnp.floor(0.5 * (j_lo + j_hi))
                enough = count_eq_upto(j_mid) >= want
                j_hi = jnp.where(enough, j_mid, j_hi)
                j_lo = jnp.where(enough, j_lo, j_mid)
            jcut_t = jnp.where(tie > 0, j_hi, jcut)
            jcut_ref[...] = jnp.broadcast_to(jcut_t, (tq, LANES))

    _split_heads(q_ref, qm_ref, DSA_HEADS)
    m_ref[...] = jnp.full(m_ref.shape, NEG, F32)
    acc_ref[...] = jnp.zeros(acc_ref.shape, F32)
    thr_b = jnp.concatenate([thr_ref[...]] * (tq // LANES), axis=1)
    jcut_b = jnp.concatenate([jcut_ref[...]] * (tq // LANES), axis=1)
    colf = lax.broadcasted_iota(jnp.int32, (tq, tq), 1).astype(F32)

    def attend(j, band_lo):
        rows = pl.ds(pl.multiple_of(j * tq, tq), tq)
        s_idx = score_ref[j]
        idx = colf + (j * tq).astype(F32)
        off = jnp.where(s_idx > thr_b, 0.0,
                        jnp.where(s_idx == thr_b, jnp.where(idx <= jcut_b, 0.0, NEG), NEG))
        kk = kv_ref[rows, 0:LANES]
        for h in range(DSA_HEADS):
            s = _dot_nt(qm_ref[h], kk) + off
            if band_lo is not None:
                s = s + band_ref[h, :, band_lo:band_lo + tq]
            v_aug = ve_ref[0, rows, :] if h % 2 == 0 else vo_ref[0, rows, :]
            _softmax_step(h, s, v_aug, m_ref, acc_ref)

    def far_body(j, carry):
        attend(j, None)
        return carry

    lax.fori_loop(0, jnp.maximum(qt - 1, 0), far_body, 0)

    @pl.when(qt >= 1)
    def _():
        attend(qt - 1, 0)

    attend(qt, tq)
    _finish_heads(acc_ref, o_ref, DSA_HEADS)


def _dsa(qb, qi, aux, kv, ki, band, bsz, seq, n_keys):
    tq = ATT_TILE
    nq = seq // tq
    qrow = lambda w: pl.BlockSpec((tq, w), lambda b, q: (b * nq + q, 0))
    seq_blk = lambda w: pl.BlockSpec((seq, w), lambda b, q: (b, 0))
    return pl.pallas_call(
        functools.partial(_dsa_kernel, n_keys=n_keys, seq=seq), grid=(bsz, nq),
        in_specs=[qrow(DSA_WIDTH), qrow(256), qrow(LANES), seq_blk(256), seq_blk(LANES),
                  _resident((DSA_HEADS, tq, 2 * tq))],
        out_specs=qrow(DSA_WIDTH),
        out_shape=jax.ShapeDtypeStruct((bsz * seq, DSA_WIDTH), BF16),
        scratch_shapes=[pltpu.VMEM((nq, tq, tq), F32),
                        pltpu.VMEM((1, seq, LANES), BF16), pltpu.VMEM((1, seq, LANES), BF16),
                        pltpu.VMEM((DSA_HEADS, tq, LANES), BF16),
                        pltpu.VMEM((DSA_HEADS, tq, LANES), F32),
                        pltpu.VMEM((DSA_HEADS, tq, LANES), F32),
                        pltpu.VMEM((tq, LANES), F32), pltpu.VMEM((tq, LANES), F32)],
        compiler_params=_cparams(("parallel", "arbitrary")), name="dsa",
    )(qb, qi, aux, kv, ki, band)


def _band_kernel(bucket_ref, rb_ref, o_ref):
    bucket = bucket_ref[...]
    for h in range(DSA_HEADS):
        acc = jnp.zeros(bucket.shape, F32)
        for k in range(REL_BUCKETS - 1):
            acc = jnp.where(bucket == k, rb_ref[k, h] - rb_ref[REL_BUCKETS - 1, h], acc)
        o_ref[h] = acc


def _t5_bucket_np(dist):
    d = np.maximum(dist, 0)
    df = np.maximum(d, 1).astype(np.float32)
    log_b = REL_MAX_EXACT + (np.log(df / REL_MAX_EXACT) / math.log(REL_MAX_DIST / REL_MAX_EXACT)
                             * (REL_BUCKETS - REL_MAX_EXACT)).astype(np.int32)
    log_b = np.minimum(log_b, REL_BUCKETS - 1)
    return np.where(d < REL_MAX_EXACT, d, log_b).astype(np.int32)


def _rel_band(rel_bias):
    tq = ATT_TILE
    i = np.arange(tq)[:, None]
    j = np.arange(2 * tq)[None, :]
    bucket = jnp.asarray(_t5_bucket_np(i + tq - j))
    return pl.pallas_call(
        _band_kernel,
        in_specs=[pl.BlockSpec(memory_space=pltpu.VMEM), pl.BlockSpec(memory_space=pltpu.SMEM)],
        out_specs=pl.BlockSpec(memory_space=pltpu.VMEM),
        out_shape=jax.ShapeDtypeStruct((DSA_HEADS, tq, 2 * tq), F32), name="rel_band",
    )(bucket, rel_bias)


def _cum_kernel(aux_ref, bf_ref, col_ref, row_ref, *, seq):
    x = aux_ref[...] + bf_ref[...]
    logf = -(jnp.maximum(-x, 0.0) + jnp.log1p(jnp.exp(-jnp.abs(x))))
    c = logf.T
    pos = lax.broadcasted_iota(jnp.int32, c.shape, 1)
    shift = 1
    while shift < seq:
        c = c + jnp.where(pos >= shift, pltpu.roll(c, shift, 1), 0.0)
        shift *= 2
    col_ref[...] = c.T
    for j in range(seq // ATT_TILE):
        row_ref[j] = c[0:8, j * ATT_TILE:(j + 1) * ATT_TILE]


def _fox_cum(aux, b_f, bsz, seq):
    nq = seq // ATT_TILE
    bf = jnp.zeros((1, LANES), F32).at[0, AUX_F:AUX_F + FOX_HEADS].set(b_f)
    return pl.pallas_call(
        functools.partial(_cum_kernel, seq=seq), grid=(bsz,),
        in_specs=[pl.BlockSpec((seq, LANES), lambda b: (b, 0)), _resident((1, LANES))],
        out_specs=[pl.BlockSpec((seq, LANES), lambda b: (b, 0)),
                   pl.BlockSpec((None, nq, 8, ATT_TILE), lambda b: (b, 0, 0, 0))],
        out_shape=[jax.ShapeDtypeStruct((bsz * seq, LANES), F32),
                   jax.ShapeDtypeStruct((bsz, nq, 8, ATT_TILE), F32)],
        compiler_params=_cparams(("parallel",)), name="fox_cum")(aux, bf)


def _fox_kernel(q_ref, k_ref, v_ref, ccol_ref, crow_ref, o_ref,
                ve_ref, vo_ref, qm_ref, m_ref, acc_ref):
    tq = ATT_TILE
    qt = pl.program_id(1)

    @pl.when(qt == 0)
    def _():
        for pr in range(FOX_HEADS // 2):
            _augment_values(v_ref[:, pr * LANES:(pr + 1) * LANES], ve_ref, vo_ref, pr)

    _split_heads(q_ref, qm_ref, FOX_HEADS)
    m_ref[...] = jnp.full(m_ref.shape, NEG, F32)
    acc_ref[...] = jnp.zeros(acc_ref.shape, F32)
    cc = ccol_ref[...]
    c_col = [jnp.broadcast_to(cc[:, AUX_F + h:AUX_F + h + 1], (tq, tq)) for h in range(FOX_HEADS)]
    causal_off = jnp.where(lax.broadcasted_iota(jnp.int32, (tq, tq), 1)
                           <= lax.broadcasted_iota(jnp.int32, (tq, tq), 0), 0.0, NEG)

    def attend(j, diagonal):
        rows = pl.ds(pl.multiple_of(j * tq, tq), tq)
        c_row = crow_ref[j]
        for h in range(FOX_HEADS):
            pr = h // 2
            kk = k_ref[rows, pr * LANES:(pr + 1) * LANES]
            s = _dot_nt(qm_ref[h], kk) + c_col[h] - c_row[AUX_F + h:AUX_F + h + 1, :]
            if diagonal:
                s = s + causal_off
            v_aug = ve_ref[pr, rows, :] if h % 2 == 0 else vo_ref[pr, rows, :]
            _softmax_step(h, s, v_aug, m_ref, acc_ref)

    def far_body(j, carry):
        attend(j, False)
        return carry

    lax.fori_loop(0, qt, far_body, 0)
    attend(qt, True)
    _finish_heads(acc_ref, o_ref, FOX_HEADS)


def _fox(qkv, c_col, c_row, bsz, seq):
    tq = ATT_TILE
    nq = seq // tq
    npair = FOX_HEADS // 2
    return pl.pallas_call(
        _fox_kernel, grid=(bsz, nq),
        in_specs=[pl.BlockSpec((tq, FOX_WIDTH), lambda b, q: (b * nq + q, 0)),
                  pl.BlockSpec((seq, FOX_WIDTH), lambda b, q: (b, 1)),
                  pl.BlockSpec((seq, FOX_WIDTH), lambda b, q: (b, 2)),
                  pl.BlockSpec((tq, LANES), lambda b, q: (b * nq + q, 0)),
                  pl.BlockSpec((None, nq, 8, tq), lambda b, q: (b, 0, 0, 0))],
        out_specs=pl.BlockSpec((tq, FOX_WIDTH), lambda b, q: (b * nq + q, 0)),
        out_shape=jax.ShapeDtypeStruct((bsz * seq, FOX_WIDTH), BF16),
        scratch_shapes=[pltpu.VMEM((npair, seq, LANES), BF16), pltpu.VMEM((npair, seq, LANES), BF16),
                        pltpu.VMEM((FOX_HEADS, tq, LANES), BF16),
                        pltpu.VMEM((FOX_HEADS, tq, LANES), F32),
                        pltpu.VMEM((FOX_HEADS, tq, LANES), F32)],
        compiler_params=_cparams(("parallel", "arbitrary")), name="fox",
    )(qkv, qkv, qkv, c_col, c_row)


def _gelu(y):
    return 0.5 * y * (1.0 + jnp.tanh(math.sqrt(2.0 / math.pi) * (y + 0.044715 * (y * y * y))))


def _merge_kernel(x_ref, yc_ref, u_ref, d_ref, wglu_ref, b_ref, c_ref, gate_ref,
                  wa_ref, wb_ref, wc_ref, wo_ref, o_ref):
    d = D_MODEL
    y = yc_ref[...] + d_ref[...] * u_ref[...]
    z = _dot(_gelu(y).astype(BF16), wglu_ref[...])
    a = z[:, :SSM_WIDTH] * _sigmoid(z[:, SSM_WIDTH:])
    merged = _sigmoid(gate_ref[:, 0:d].astype(F32)) * _dot(a.astype(BF16), wa_ref[...])
    merged += _sigmoid(gate_ref[:, d:2 * d].astype(F32)) * _dot(b_ref[...], wb_ref[...])
    merged += _sigmoid(gate_ref[:, 2 * d:3 * d].astype(F32)) * _dot(c_ref[...], wc_ref[...])
    o_ref[...] = x_ref[...] + _dot(merged.astype(BF16), wo_ref[...])


def _merge(xt, yc, u, d_skip, w_glu, b, c, gates, wa, wb, wc, wo, tm=512):
    m, d = xt.shape
    row = lambda w: pl.BlockSpec((tm, w), lambda i: (i, 0))
    return pl.pallas_call(
        _merge_kernel, grid=(m // tm,),
        in_specs=[row(d), row(SSM_WIDTH), row(SSM_WIDTH), _resident((1, SSM_WIDTH)),
                  _resident((SSM_WIDTH, 2 * SSM_WIDTH)), row(DSA_WIDTH), row(FOX_WIDTH), row(3 * d),
                  _resident((SSM_WIDTH, d)), _resident((DSA_WIDTH, d)), _resident((FOX_WIDTH, d)),
                  _resident((d, d))],
        out_specs=row(d), out_shape=jax.ShapeDtypeStruct((m, d), F32),
        compiler_params=_cparams(("parallel",)), name="merge",
    )(xt, yc, u, d_skip.reshape(1, -1), w_glu, b, c, gates, wa, wb, wc, wo)


def _pack_w_in(w_in):
    d = w_in.shape[0]
    splits = (SSM_WIDTH, DSA_WIDTH, DSA_LATENT, IDX_HEADS * IDX_DIM, IDX_DIM, IDX_HEADS,
              FOX_WIDTH, FOX_WIDTH, FOX_WIDTH, FOX_HEADS, D_MODEL, D_MODEL, D_MODEL)
    pts = np.cumsum(splits)[:-1]
    (w_u, w_qb, w_ckv, w_qi, w_ki, w_wi, w_qc, w_kc, w_vc, w_fc, w_ga, w_gb, w_gc) = jnp.split(
        w_in, pts, axis=1)
    scale = HEAD_DIM ** -0.5
    aux = jnp.zeros((d, LANES), F32)
    aux = aux.at[:, AUX_F:AUX_F + FOX_HEADS].set(w_fc).at[:, AUX_W:AUX_W + IDX_HEADS].set(w_wi)
    cols = [w_u, w_qb * scale, w_qi, w_ki, w_ki, w_ckv, aux, w_qc * scale, w_kc, w_vc,
            w_ga, w_gb, w_gc]
    return jnp.concatenate(cols, axis=1).astype(BF16)


def kernel(x, ffn1_norm, ffn1_w_gate, ffn1_w_up, ffn1_w_down, mix_norm, w_in, ssm_lambda_re, ssm_lambda_im, ssm_log_dt, ssm_b_re, ssm_b_im, ssm_c_re, ssm_c_im, ssm_d, ssm_w_glu, dsa_kv_norm, dsa_w_uk, dsa_w_uv, rel_bias, fox_b_f, w_branch_ssm, w_branch_dsa, w_branch_fox, w_out, ffn2_norm, ffn2_w_gate, ffn2_w_up, ffn2_w_down, final_norm):
    bsz, seq, d = x.shape
    depth = w_in.shape[0]
    n_keys = min(TOPK_MAX, seq // 4)
    assert d == D_MODEL and seq % ATT_TILE == 0 and (bsz * seq) % (8 * 512) == 0
    assert n_keys % LANES == 0 and bsz % 8 == 0
    m = bsz * seq
    n_chunks = seq // S5_CHUNK
    xt = x.reshape(m, d)
    band = _rel_band(rel_bias)
    bf = lambda w: w.astype(BF16)
    for l in range(depth):
        xt = _ffn(xt, ffn1_norm[l], bf(ffn1_w_gate[l]), bf(ffn1_w_up[l]), bf(ffn1_w_down[l]))
        w_kv = bf(jnp.concatenate([dsa_w_uk[l], dsa_w_uk[l], dsa_w_uv[l], dsa_w_uv[l]], axis=1))
        u, qb, qi, ki, kv, aux, qkv, gates = _inproj(
            xt, mix_norm[l], _pack_w_in(w_in[l]), dsa_kv_norm[l], w_kv)
        m_intra, m_sum, m_out, a_re, a_im = _s5_matrices(
            ssm_lambda_re[l], ssm_lambda_im[l], ssm_log_dt[l], ssm_b_re[l], ssm_b_im[l],
            ssm_c_re[l], ssm_c_im[l])
        u8 = u.reshape(m // S5_CHUNK, S5_COLS)
        y_intra, p = _s5a(u8, m_intra, m_sum)
        h_re, h_im = _s5b(p, a_re, a_im, n_chunks, nb=8)
        yc = _s5c(y_intra, h_re, h_im, m_out).reshape(m, SSM_WIDTH)
        b_out = _dsa(qb, qi, aux, kv, ki, band, bsz, seq, n_keys)
        c_col, c_row = _fox_cum(aux, fox_b_f[l], bsz, seq)
        c_out = _fox(qkv, c_col, c_row, bsz, seq)
        xt = _merge(xt, yc, u, ssm_d[l], bf(ssm_w_glu[l]), b_out, c_out, gates,
                    bf(w_branch_ssm[l]), bf(w_branch_dsa[l]), bf(w_branch_fox[l]), bf(w_out[l]))
        last = final_norm if l == depth - 1 else None
        xt = _ffn(xt, ffn2_norm[l], bf(ffn2_w_gate[l]), bf(ffn2_w_up[l]), bf(ffn2_w_down[l]),
                  final_norm=last)
    return xt.reshape(bsz, seq, d)
```

```python
import functools
import math

import numpy as np
import jax
import jax.numpy as jnp
from jax import lax
from jax.experimental import pallas as pl
from jax.experimental.pallas import tpu as pltpu

F32 = jnp.float32
BF16 = jnp.bfloat16

D_MODEL = 1024
SSM_WIDTH = D_MODEL // 4
SSM_GROUP = 16
SSM_GROUPS = SSM_WIDTH // SSM_GROUP
SSM_STATE = 64
HEAD_DIM = 64
DSA_HEADS = 6
DSA_WIDTH = DSA_HEADS * HEAD_DIM
DSA_LATENT = 2 * HEAD_DIM
IDX_HEADS = 4
IDX_DIM = 64
TOPK_MAX = 256
FOX_HEADS = 6
FOX_WIDTH = FOX_HEADS * HEAD_DIM
REL_BUCKETS = 32
REL_MAX_EXACT = 16
REL_MAX_DIST = 128
FFN_HIDDEN = 2816
EPS = 1e-6
NEG = -1e30

LANES = 128
S5_CHUNK = 8
S5_COLS = S5_CHUNK * SSM_WIDTH
S5_STATE = 2 * SSM_GROUPS * SSM_STATE
ATT_TILE = 256
VMEM_LIMIT = 56 * 1024 * 1024

_SEG = dict(u=256, qb=384, qi=256, ki=128, ckv=128, aux=128, qkv=1152, g=3072)
AUX_F = 0
AUX_W = 8


def _cparams(sem):
    return pltpu.CompilerParams(dimension_semantics=sem, vmem_limit_bytes=VMEM_LIMIT)


def _resident(shape):
    nd = len(shape)
    return pl.BlockSpec(shape, lambda *_: (0,) * nd, pipeline_mode=pl.Buffered(1))


def _rms(x, g):
    return x * lax.rsqrt(jnp.mean(x * x, axis=-1, keepdims=True) + EPS) * g


def _sigmoid(x):
    return 1.0 / (1.0 + jnp.exp(-x))


def _dot(a, b):
    return jnp.dot(a, b, preferred_element_type=F32)


def _dot_nt(a, b):
    return lax.dot_general(a, b, (((1,), (1,)), ((), ())), preferred_element_type=F32)


def _ffn_body(x, g_ref, wg_ref, wu_ref, wd_ref, fc):
    h = _rms(x, g_ref[...]).astype(BF16)
    acc = None
    for c in range(FFN_HIDDEN // fc):
        sl = slice(c * fc, (c + 1) * fc)
        gt = _dot(h, wg_ref[:, sl])
        up = _dot(h, wu_ref[:, sl])
        a = (gt * _sigmoid(gt) * up).astype(BF16)
        d = _dot(a, wd_ref[sl, :])
        acc = d if acc is None else acc + d
    return x + 0.5 * acc


def _ffn_kernel(x_ref, g_ref, wg_ref, wu_ref, wd_ref, o_ref, *, fc):
    o_ref[...] = _ffn_body(x_ref[...], g_ref, wg_ref, wu_ref, wd_ref, fc)


def _ffn_final_kernel(x_ref, g_ref, wg_ref, wu_ref, wd_ref, fn_ref, o_ref, *, fc):
    y = _ffn_body(x_ref[...], g_ref, wg_ref, wu_ref, wd_ref, fc)
    o_ref[...] = _rms(y, fn_ref[...])


def _ffn(xt, norm, wg, wu, wd, final_norm=None, tm=512, fc=1408):
    m, d = xt.shape
    f = wg.shape[1]
    row = pl.BlockSpec((tm, d), lambda i: (i, 0))
    vec = _resident((1, d))
    in_specs = [row, vec, _resident((d, f)), _resident((d, f)), _resident((f, d))]
    args = [xt, norm.reshape(1, d), wg, wu, wd]
    if final_norm is None:
        body = functools.partial(_ffn_kernel, fc=fc)
    else:
        body = functools.partial(_ffn_final_kernel, fc=fc)
        in_specs.append(vec)
        args.append(final_norm.reshape(1, d))
    return pl.pallas_call(
        body, grid=(m // tm,), in_specs=in_specs, out_specs=row,
        out_shape=jax.ShapeDtypeStruct((m, d), F32),
        compiler_params=_cparams(("parallel",)), name="ffn")(*args)


def _inproj_kernel(x_ref, g_ref, w_ref, kvn_ref, wkv_ref,
                   u_ref, qb_ref, qi_ref, ki_ref, kv_ref, aux_ref, qkv_ref, gate_ref):
    h = _rms(x_ref[...], g_ref[...]).astype(BF16)
    off = 0
    outs = dict(u=u_ref, qb=qb_ref, qi=qi_ref, ki=ki_ref, aux=aux_ref, qkv=qkv_ref, g=gate_ref)
    for name, width in _SEG.items():
        r = _dot(h, w_ref[:, off:off + width])
        off += width
        if name == "ckv":
            c = _rms(r, kvn_ref[...]).astype(BF16)
            kv_ref[...] = _dot(c, wkv_ref[...]).astype(BF16)
        else:
            outs[name][...] = r.astype(outs[name].dtype)


def _inproj(xt, norm, w_all, kv_norm, w_kv, tm=512):
    m, d = xt.shape
    wtot = w_all.shape[1]
    widths = dict(u=(256, F32), qb=(384, BF16), qi=(256, BF16), ki=(128, BF16),
                  kv=(256, BF16), aux=(128, F32), qkv=(1152, BF16), g=(3072, BF16))
    out_shape = [jax.ShapeDtypeStruct((m, w), dt) for w, dt in widths.values()]
    out_specs = [pl.BlockSpec((tm, w), lambda i: (i, 0)) for w, _ in widths.values()]
    return pl.pallas_call(
        _inproj_kernel, grid=(m // tm,),
        in_specs=[pl.BlockSpec((tm, d), lambda i: (i, 0)), _resident((1, d)),
                  _resident((d, wtot)), _resident((1, DSA_LATENT)),
                  _resident((DSA_LATENT, 256))],
        out_specs=out_specs, out_shape=out_shape,
        compiler_params=_cparams(("parallel",)), name="inproj",
    )(xt, norm.reshape(1, d), w_all, kv_norm.reshape(1, DSA_LATENT), w_kv)


def _s5a_kernel(u_ref, mi_ref, ms_ref, yi_ref, p_ref):
    u = u_ref[...].astype(BF16)
    yi_ref[...] = _dot(u, mi_ref[...])
    p_ref[...] = _dot(u, ms_ref[...])


def _s5a(u8, m_intra, m_sum, tr=512):
    r = u8.shape[0]
    row = pl.BlockSpec((tr, S5_COLS), lambda i: (i, 0))
    srow = pl.BlockSpec((tr, S5_STATE), lambda i: (i, 0))
    return pl.pallas_call(
        _s5a_kernel, grid=(r // tr,),
        in_specs=[row, _resident((S5_COLS, S5_COLS)), _resident((S5_COLS, S5_STATE))],
        out_specs=[row, srow],
        out_shape=[jax.ShapeDtypeStruct((r, S5_COLS), F32),
                   jax.ShapeDtypeStruct((r, S5_STATE), F32)],
        compiler_params=_cparams(("parallel",)), name="s5_chunk")(u8, m_intra, m_sum)


def _s5b_kernel(pr_ref, pi_ref, ar_ref, ai_ref, hr_ref, hi_ref, *, nb, nc):
    ar = ar_ref[...]
    ai = ai_ref[...]
    cols = ar.shape[1]

    def step(c, carry):
        hr, hi = carry
        rows = pl.ds(c, nb, stride=nc)
        hr_ref[rows, :] = hr.astype(hr_ref.dtype)
        hi_ref[rows, :] = hi.astype(hi_ref.dtype)
        pr = pr_ref[rows, :]
        pi = pi_ref[rows, :]
        return (ar * hr - ai * hi + pr, ar * hi + ai * hr + pi)

    z = jnp.zeros((nb, cols), F32)
    lax.fori_loop(0, nc, step, (z, z))


def _s5b(p, a_re, a_im, nc, nb, tcol=LANES):
    r = p.shape[0]
    half = S5_STATE // 2
    ncb = half // tcol
    blk = pl.BlockSpec((nb * nc, tcol), lambda i, j: (i, j))
    im_blk = pl.BlockSpec((nb * nc, tcol), lambda i, j: (i, j + ncb))
    a_blk = pl.BlockSpec((1, tcol), lambda i, j: (0, j))
    return pl.pallas_call(
        functools.partial(_s5b_kernel, nb=nb, nc=nc), grid=(r // (nb * nc), ncb),
        in_specs=[blk, im_blk, a_blk, a_blk],
        out_specs=[blk, blk],
        out_shape=[jax.ShapeDtypeStruct((r, half), F32)] * 2,
        compiler_params=_cparams(("parallel", "parallel")), name="s5_scan",
    )(p, p, a_re, a_im)


def _s5c_kernel(yi_ref, hr_ref, hi_ref, mo_ref, y_ref):
    half = S5_STATE // 2
    y = yi_ref[...]
    y = y + _dot(hr_ref[...].astype(BF16), mo_ref[:half, :])
    y = y + _dot(hi_ref[...].astype(BF16), mo_ref[half:, :])
    y_ref[...] = y


def _s5c(yi, h_re, h_im, m_out, tr=512):
    r = yi.shape[0]
    half = S5_STATE // 2
    row = pl.BlockSpec((tr, S5_COLS), lambda i: (i, 0))
    return pl.pallas_call(
        _s5c_kernel, grid=(r // tr,),
        in_specs=[row, pl.BlockSpec((tr, half), lambda i: (i, 0)),
                  pl.BlockSpec((tr, half), lambda i: (i, 0)),
                  _resident((S5_STATE, S5_COLS))],
        out_specs=row, out_shape=jax.ShapeDtypeStruct((r, S5_COLS), F32),
        compiler_params=_cparams(("parallel",)), name="s5_out")(yi, h_re, h_im, m_out)


def _s5_matrices(lam_re, lam_im, log_dt, b_re, b_im, c_re, c_im):
    hp = lax.Precision.HIGHEST
    t = S5_CHUNK
    dt = jnp.exp(log_dt)[:, None]
    lr, li = lam_re, lam_im
    mag = jnp.exp(lr * dt)
    ab_re, ab_im = mag * jnp.cos(li * dt), mag * jnp.sin(li * dt)
    den = lr * lr + li * li
    nr, ni = ab_re - 1.0, ab_im
    s_re = (nr * lr + ni * li) / den
    s_im = (ni * lr - nr * li) / den
    bb_re = s_re[..., None] * b_re - s_im[..., None] * b_im
    bb_im = s_re[..., None] * b_im + s_im[..., None] * b_re
    pr, pi = [jnp.ones_like(ab_re)], [jnp.zeros_like(ab_re)]
    for _ in range(t):
        pr.append(pr[-1] * ab_re - pi[-1] * ab_im)
        pi.append(pr[-2] * ab_im + pi[-1] * ab_re)
    pw_re, pw_im = jnp.stack(pr), jnp.stack(pi)
    ab_b_re = pw_re[..., None] * bb_re - pw_im[..., None] * bb_im
    ab_b_im = pw_re[..., None] * bb_im + pw_im[..., None] * bb_re
    kern = (jnp.einsum('gpn,dgnq->dgpq', c_re, ab_b_re[:t], precision=hp)
            - jnp.einsum('gpn,dgnq->dgpq', c_im, ab_b_im[:t], precision=hp))
    eye_g = jnp.eye(SSM_GROUPS, dtype=F32)
    lag = np.arange(t)[None, :] - np.arange(t)[:, None]
    sel = jnp.asarray((lag[None] == np.arange(t)[:, None, None]).astype(np.float32))
    m_intra = jnp.einsum('dji,dgpq,gh->jgqihp', sel, kern, eye_g, precision=hp)
    m_intra = m_intra.reshape(S5_COLS, S5_COLS)
    sb_re = ab_b_re[:t][::-1]
    sb_im = ab_b_im[:t][::-1]
    m_sum_re = jnp.einsum('jgnq,gh->jgqhn', sb_re, eye_g, precision=hp)
    m_sum_im = jnp.einsum('jgnq,gh->jgqhn', sb_im, eye_g, precision=hp)
    m_sum = jnp.concatenate([m_sum_re.reshape(S5_COLS, -1), m_sum_im.reshape(S5_COLS, -1)], axis=1)
    w_re = c_re[None] * pw_re[1:, :, None, :] - c_im[None] * pw_im[1:, :, None, :]
    w_im = c_re[None] * pw_im[1:, :, None, :] + c_im[None] * pw_re[1:, :, None, :]
    m_out_re = jnp.einsum('igpn,gh->gnihp', w_re, eye_g, precision=hp)
    m_out_im = jnp.einsum('igpn,gh->gnihp', -w_im, eye_g, precision=hp)
    m_out = jnp.concatenate([m_out_re.reshape(-1, S5_COLS), m_out_im.reshape(-1, S5_COLS)], axis=0)
    a_re = pw_re[t].reshape(1, -1)
    a_im = pw_im[t].reshape(1, -1)
    return m_intra.astype(BF16), m_sum.astype(BF16), m_out.astype(BF16), a_re, a_im


def _split_heads(q_ref, qm_ref, n_heads):
    tq = q_ref.shape[0]
    low = lax.broadcasted_iota(jnp.int32, (tq, LANES), 1) < HEAD_DIM
    for h in range(n_heads):
        pair = q_ref[:, (h // 2) * LANES:(h // 2 + 1) * LANES]
        keep = low if h % 2 == 0 else jnp.logical_not(low)
        qm_ref[h] = jnp.where(keep, pair, jnp.zeros_like(pair))


def _softmax_step(h, s, v_aug, m_ref, acc_ref):
    m_old = m_ref[h]
    m_new = jnp.maximum(m_old, jnp.max(s, axis=1, keepdims=True))
    alpha = jnp.exp(m_old - m_new)
    p = jnp.exp(s - jnp.concatenate([m_new] * (s.shape[1] // LANES), axis=1))
    acc_ref[h] = acc_ref[h] * alpha + _dot(p.astype(BF16), v_aug)
    m_ref[h] = m_new


def _finish_heads(acc_ref, o_ref, n_heads):
    tq = o_ref.shape[0]
    low = lax.broadcasted_iota(jnp.int32, (tq, LANES), 1) < HEAD_DIM
    for pr in range(n_heads // 2):
        ae = acc_ref[2 * pr]
        ao = acc_ref[2 * pr + 1]
        oe = ae / pltpu.roll(ae, HEAD_DIM, 1)
        oo = ao / pltpu.roll(ao, HEAD_DIM, 1)
        o_ref[:, pr * LANES:(pr + 1) * LANES] = jnp.where(low, oe, oo).astype(o_ref.dtype)


def _augment_values(vv, ve_ref, vo_ref, base):
    low = lax.broadcasted_iota(jnp.int32, vv.shape, 1) < HEAD_DIM
    one = jnp.ones_like(vv)
    ve_ref[base] = jnp.where(low, vv, one)
    vo_ref[base] = jnp.where(low, one, vv)


def _dsa_kernel(q_ref, qi_ref, aux_ref, kv_ref, ki_ref, band_ref, o_ref,
                score_ref, ve_ref, vo_ref, qm_ref, m_ref, acc_ref, thr_ref,
                *, n_keys, seq):
    tq = ATT_TILE
    qt = pl.program_id(1)
    n_chunks = qt + 1
    kf = float(n_keys)

    @pl.when(qt == 0)
    def _():
        _augment_values(kv_ref[:, LANES:2 * LANES], ve_ref, vo_ref, 0)

    qi = qi_ref[...]
    low = lax.broadcasted_iota(jnp.int32, (tq, LANES), 1) < HEAD_DIM
    parts = []
    for h in range(IDX_HEADS):
        pair = qi[:, (h // 2) * LANES:(h // 2 + 1) * LANES]
        keep = low if h % 2 == 0 else jnp.logical_not(low)
        parts.append(jnp.where(keep, pair, jnp.zeros_like(pair)))
    q_stack = jnp.concatenate(parts, axis=0)
    head_w = [aux_ref[:, AUX_W + h:AUX_W + h + 1] * (IDX_HEADS ** -0.5 * IDX_DIM ** -0.5)
              for h in range(IDX_HEADS)]
    col_minus_row = (lax.broadcasted_iota(jnp.int32, (tq, tq), 1)
                     - lax.broadcasted_iota(jnp.int32, (tq, tq), 0))

    def score_body(j, carry):
        mx, mn = carry
        kc = ki_ref[pl.ds(pl.multiple_of(j * tq, tq), tq), :]
        d = _dot_nt(q_stack, kc)
        sc = head_w[0] * jnp.maximum(d[0:tq], 0.0)
        for h in range(1, IDX_HEADS):
            sc = sc + head_w[h] * jnp.maximum(d[h * tq:(h + 1) * tq], 0.0)
        causal = col_minus_row <= (qt - j) * tq
        score_ref[j] = jnp.where(causal, sc, NEG)
        mx = jnp.maximum(mx, jnp.where(causal, sc, NEG))
        mn = jnp.minimum(mn, jnp.where(causal, sc, -NEG))
        return mx, mn

    mx, mn = lax.fori_loop(0, n_chunks, score_body,
                           (jnp.full((tq, tq), NEG, F32), jnp.full((tq, tq), -NEG, F32)))

    t_row = qt * tq + lax.broadcasted_iota(jnp.int32, (tq, 1), 0)
    thr_ref[...] = jnp.full((tq, LANES), 0.5 * NEG, F32)

    def count(pred):
        def body(j, acc):
            ind = pred(score_ref[j])
            return acc + (ind[:, :LANES] + ind[:, LANES:])
        acc = lax.fori_loop(0, n_chunks, body, jnp.zeros((tq, LANES), F32))
        return jnp.sum(acc, axis=1, keepdims=True)

    def count_ge(x):
        xb = jnp.broadcast_to(x, (tq, tq))
        return count(lambda s: jnp.where(s >= xb, 1.0, 0.0))

    @pl.when((qt + 1) * tq > n_keys)
    def _search():
        need = (t_row >= n_keys).astype(F32)
        lo0 = jnp.min(mn, axis=1, keepdims=True)
        hi0 = jnp.max(mx, axis=1, keepdims=True)
        c_top = count_ge(hi0)
        c_ge0 = count_ge(jnp.zeros((tq, 1), F32))
        c_gt0 = count(lambda s: jnp.where(s > 0.0, 1.0, 0.0))
        n_valid = (t_row + 1).astype(F32)
        top_hit = c_top == kf
        top_tie = c_top > kf
        zero_hit = c_ge0 == kf
        zero_tie = (c_gt0 < kf) & (c_ge0 > kf)
        positive = c_gt0 >= kf
        hi_neg = hi0 < 0.0
        lo1 = jnp.where(positive, jnp.maximum(lo0, 0.0), lo0)
        c_lo1 = jnp.where(positive, c_ge0, n_valid)
        hi1 = jnp.where(positive | hi_neg, hi0, 0.0)
        c_hi1 = jnp.where(positive | hi_neg, c_top, c_ge0)
        hit0 = top_hit | ((~top_tie) & zero_hit)
        tie0 = top_tie | ((~top_hit) & (~zero_hit) & zero_tie)
        x0 = jnp.where(top_hit, hi0, 0.0)
        tie_val = jnp.where(top_tie, hi0, 0.0)
        above0 = jnp.where(top_tie, 0.0, c_gt0)
        hit0 = need * hit0.astype(F32)
        tie0 = need * tie0.astype(F32) * (1.0 - hit0)
        done0 = jnp.maximum(1.0 - need, jnp.maximum(hit0, tie0))
        lo1 = jnp.where(tie0 > 0, tie_val, lo1)
        c_hi1 = jnp.where(tie0 > 0, above0, c_hi1)
        state0 = (jnp.int32(0), lo1, hi1, c_lo1, c_hi1, done0, tie0, x0,
                  jnp.max(1.0 - done0))

        def cond(st):
            return st[8] > 0.0

        def body(st):
            it, lo, hi, c_lo, c_hi, done, tie, x_fin, _ = st
            mid = 0.5 * lo + 0.5 * hi
            frac = (c_lo - kf) / (c_lo - c_hi)
            interp = lo + (hi - lo) * frac
            x = jnp.where(it % 2 == 0, interp, mid)
            x = jnp.where((x > lo) & (x < hi), x, mid)
            inside = ((x > lo) & (x < hi)).astype(F32)
            active = 1.0 - done
            probe = active * inside
            c = count_ge(jnp.where(probe > 0, x, x_fin))
            hit = probe * (c == kf).astype(F32)
            more = probe * (c > kf).astype(F32)
            less = probe * (c < kf).astype(F32)
            new_tie = active * (1.0 - inside)
            lo = jnp.where(more > 0, x, lo)
            c_lo = jnp.where(more > 0, c, c_lo)
            hi = jnp.where(less > 0, x, hi)
            c_hi = jnp.where(less > 0, c, c_hi)
            x_fin = jnp.where(hit > 0, x, x_fin)
            tie = jnp.maximum(tie, new_tie)
            done = jnp.maximum(done, jnp.maximum(hit, new_tie))
            return (it + 1, lo, hi, c_lo, c_hi, done, tie, x_fin, jnp.max(1.0 - done))

        st = lax.while_loop(cond, body, state0)
        _, lo, _, _, c_hi, _, tie, x_fin, _ = st
        thr = jnp.where(need > 0, jnp.where(tie > 0, lo, x_fin), 0.5 * NEG)
        thr_ref[...] = jnp.broadcast_to(thr, (tq, LANES))

        @pl.when(jnp.max(tie) > 0.0)
        def _ties():
            want = jnp.broadcast_to(kf - c_hi, (tq, tq))
            thr_t = jnp.broadcast_to(jnp.where(tie > 0, thr, -NEG), (tq, tq))
            upto = (lax.broadcasted_iota(jnp.int32, (tq, tq), 0)
                    <= lax.broadcasted_iota(jnp.int32, (tq, tq), 1))
            prefix = jnp.where(upto, 1.0, 0.0).astype(BF16)

            def body(j, seen):
                s = score_ref[j]
                eq = s == thr_t
                rank = _dot(jnp.where(eq, 1.0, 0.0).astype(BF16), prefix)
                rank = rank + jnp.broadcast_to(seen, (tq, tq))
                score_ref[j] = jnp.where(eq, jnp.where(rank > want, NEG, s), s)
                return rank[:, tq - 1:tq]

            lax.fori_loop(0, n_chunks, body, jnp.zeros((tq, 1), F32))

    _split_heads(q_ref, qm_ref, DSA_HEADS)
    m_ref[...] = jnp.full(m_ref.shape, NEG, F32)
    acc_ref[...] = jnp.zeros(acc_ref.shape, F32)
    thr_b = jnp.concatenate([thr_ref[...]] * (tq // LANES), axis=1)

    def attend(j, band_lo):
        rows = pl.ds(pl.multiple_of(j * tq, tq), tq)
        off = jnp.where(score_ref[j] >= thr_b, 0.0, NEG)
        kk = kv_ref[rows, 0:LANES]
        for h in range(DSA_HEADS):
            s = _dot_nt(qm_ref[h], kk) + off
            if band_lo is not None:
                s = s + band_ref[h, :, band_lo:band_lo + tq]
            v_aug = ve_ref[0, rows, :] if h % 2 == 0 else vo_ref[0, rows, :]
            _softmax_step(h, s, v_aug, m_ref, acc_ref)

    def far_body(j, carry):
        attend(j, None)
        return carry

    lax.fori_loop(0, jnp.maximum(qt - 1, 0), far_body, 0)

    @pl.when(qt >= 1)
    def _():
        attend(qt - 1, 0)

    attend(qt, tq)
    _finish_heads(acc_ref, o_ref, DSA_HEADS)


def _dsa(qb, qi, aux, kv, ki, band, bsz, seq, n_keys):
    tq = ATT_TILE
    nq = seq // tq
    qrow = lambda w: pl.BlockSpec((tq, w), lambda b, q: (b * nq + q, 0))
    seq_blk = lambda w: pl.BlockSpec((seq, w), lambda b, q: (b, 0))
    return pl.pallas_call(
        functools.partial(_dsa_kernel, n_keys=n_keys, seq=seq), grid=(bsz, nq),
        in_specs=[qrow(DSA_WIDTH), qrow(256), qrow(LANES), seq_blk(256), seq_blk(LANES),
                  _resident((DSA_HEADS, tq, 2 * tq))],
        out_specs=qrow(DSA_WIDTH),
        out_shape=jax.ShapeDtypeStruct((bsz * seq, DSA_WIDTH), BF16),
        scratch_shapes=[pltpu.VMEM((nq, tq, tq), F32),
                        pltpu.VMEM((1, seq, LANES), BF16), pltpu.VMEM((1, seq, LANES), BF16),
                        pltpu.VMEM((DSA_HEADS, tq, LANES), BF16),
                        pltpu.VMEM((DSA_HEADS, tq, LANES), F32),
                        pltpu.VMEM((DSA_HEADS, tq, LANES), F32),
                        pltpu.VMEM((tq, LANES), F32)],
        compiler_params=_cparams(("parallel", "arbitrary")), name="dsa",
    )(qb, qi, aux, kv, ki, band)


def _band_kernel(bucket_ref, rb_ref, o_ref):
    bucket = bucket_ref[...]
    for h in range(DSA_HEADS):
        acc = jnp.zeros(bucket.shape, F32)
        for k in range(REL_BUCKETS - 1):
            acc = jnp.where(bucket == k, rb_ref[k, h] - rb_ref[REL_BUCKETS - 1, h], acc)
        o_ref[h] = acc


def _t5_bucket_np(dist):
    d = np.maximum(dist, 0)
    df = np.maximum(d, 1).astype(np.float32)
    log_b = REL_MAX_EXACT + (np.log(df / REL_MAX_EXACT) / math.log(REL_MAX_DIST / REL_MAX_EXACT)
                             * (REL_BUCKETS - REL_MAX_EXACT)).astype(np.int32)
    log_b = np.minimum(log_b, REL_BUCKETS - 1)
    return np.where(d < REL_MAX_EXACT, d, log_b).astype(np.int32)


def _rel_band(rel_bias):
    tq = ATT_TILE
    i = np.arange(tq)[:, None]
    j = np.arange(2 * tq)[None, :]
    bucket = jnp.asarray(_t5_bucket_np(i + tq - j))
    return pl.pallas_call(
        _band_kernel,
        in_specs=[pl.BlockSpec(memory_space=pltpu.VMEM), pl.BlockSpec(memory_space=pltpu.SMEM)],
        out_specs=pl.BlockSpec(memory_space=pltpu.VMEM),
        out_shape=jax.ShapeDtypeStruct((DSA_HEADS, tq, 2 * tq), F32), name="rel_band",
    )(bucket, rel_bias)


def _cum_kernel(aux_ref, bf_ref, col_ref, row_ref, *, seq):
    x = aux_ref[...] + bf_ref[...]
    logf = -(jnp.maximum(-x, 0.0) + jnp.log1p(jnp.exp(-jnp.abs(x))))
    c = logf.T
    pos = lax.broadcasted_iota(jnp.int32, c.shape, 1)
    shift = 1
    while shift < seq:
        c = c + jnp.where(pos >= shift, pltpu.roll(c, shift, 1), 0.0)
        shift *= 2
    col_ref[...] = c.T
    for j in range(seq // ATT_TILE):
        row_ref[j] = c[0:8, j * ATT_TILE:(j + 1) * ATT_TILE]


def _fox_cum(aux, b_f, bsz, seq):
    nq = seq // ATT_TILE
    bf = jnp.zeros((1, LANES), F32).at[0, AUX_F:AUX_F + FOX_HEADS].set(b_f)
    return pl.pallas_call(
        functools.partial(_cum_kernel, seq=seq), grid=(bsz,),
        in_specs=[pl.BlockSpec((seq, LANES), lambda b: (b, 0)), _resident((1, LANES))],
        out_specs=[pl.BlockSpec((seq, LANES), lambda b: (b, 0)),
                   pl.BlockSpec((None, nq, 8, ATT_TILE), lambda b: (b, 0, 0, 0))],
        out_shape=[jax.ShapeDtypeStruct((bsz * seq, LANES), F32),
                   jax.ShapeDtypeStruct((bsz, nq, 8, ATT_TILE), F32)],
        compiler_params=_cparams(("parallel",)), name="fox_cum")(aux, bf)


def _fox_kernel(q_ref, k_ref, v_ref, ccol_ref, crow_ref, o_ref,
                ve_ref, vo_ref, qm_ref, m_ref, acc_ref):
    tq = ATT_TILE
    qt = pl.program_id(1)

    @pl.when(qt == 0)
    def _():
        for pr in range(FOX_HEADS // 2):
            _augment_values(v_ref[:, pr * LANES:(pr + 1) * LANES], ve_ref, vo_ref, pr)

    _split_heads(q_ref, qm_ref, FOX_HEADS)
    m_ref[...] = jnp.full(m_ref.shape, NEG, F32)
    acc_ref[...] = jnp.zeros(acc_ref.shape, F32)
    cc = ccol_ref[...]
    c_col = [jnp.broadcast_to(cc[:, AUX_F + h:AUX_F + h + 1], (tq, tq)) for h in range(FOX_HEADS)]
    causal_off = jnp.where(lax.broadcasted_iota(jnp.int32, (tq, tq), 1)
                           <= lax.broadcasted_iota(jnp.int32, (tq, tq), 0), 0.0, NEG)

    def attend(j, diagonal):
        rows = pl.ds(pl.multiple_of(j * tq, tq), tq)
        c_row = crow_ref[j]
        for h in range(FOX_HEADS):
            pr = h // 2
            kk = k_ref[rows, pr * LANES:(pr + 1) * LANES]
            s = _dot_nt(qm_ref[h], kk) + c_col[h] - c_row[AUX_F + h:AUX_F + h + 1, :]
            if diagonal:
                s = s + causal_off
            v_aug = ve_ref[pr, rows, :] if h % 2 == 0 else vo_ref[pr, rows, :]
            _softmax_step(h, s, v_aug, m_ref, acc_ref)

    def far_body(j, carry):
        attend(j, False)
        return carry

    lax.fori_loop(0, qt, far_body, 0)
    attend(qt, True)
    _finish_heads(acc_ref, o_ref, FOX_HEADS)


def _fox(qkv, c_col, c_row, bsz, seq):
    tq = ATT_TILE
    nq = seq // tq
    npair = FOX_HEADS // 2
    return pl.pallas_call(
        _fox_kernel, grid=(bsz, nq),
        in_specs=[pl.BlockSpec((tq, FOX_WIDTH), lambda b, q: (b * nq + q, 0)),
                  pl.BlockSpec((seq, FOX_WIDTH), lambda b, q: (b, 1)),
                  pl.BlockSpec((seq, FOX_WIDTH), lambda b, q: (b, 2)),
                  pl.BlockSpec((tq, LANES), lambda b, q: (b * nq + q, 0)),
                  pl.BlockSpec((None, nq, 8, tq), lambda b, q: (b, 0, 0, 0))],
        out_specs=pl.BlockSpec((tq, FOX_WIDTH), lambda b, q: (b * nq + q, 0)),
        out_shape=jax.ShapeDtypeStruct((bsz * seq, FOX_WIDTH), BF16),
        scratch_shapes=[pltpu.VMEM((npair, seq, LANES), BF16), pltpu.VMEM((npair, seq, LANES), BF16),
                        pltpu.VMEM((FOX_HEADS, tq, LANES), BF16),
                        pltpu.VMEM((FOX_HEADS, tq, LANES), F32),
                        pltpu.VMEM((FOX_HEADS, tq, LANES), F32)],
        compiler_params=_cparams(("parallel", "arbitrary")), name="fox",
    )(qkv, qkv, qkv, c_col, c_row)


def _gelu(y):
    return 0.5 * y * (1.0 + jnp.tanh(math.sqrt(2.0 / math.pi) * (y + 0.044715 * (y * y * y))))


def _merge_kernel(x_ref, yc_ref, u_ref, d_ref, wglu_ref, b_ref, c_ref, gate_ref,
                  wa_ref, wb_ref, wc_ref, wo_ref, o_ref):
    d = D_MODEL
    y = yc_ref[...] + d_ref[...] * u_ref[...]
    z = _dot(_gelu(y).astype(BF16), wglu_ref[...])
    a = z[:, :SSM_WIDTH] * _sigmoid(z[:, SSM_WIDTH:])
    merged = _sigmoid(gate_ref[:, 0:d].astype(F32)) * _dot(a.astype(BF16), wa_ref[...])
    merged += _sigmoid(gate_ref[:, d:2 * d].astype(F32)) * _dot(b_ref[...], wb_ref[...])
    merged += _sigmoid(gate_ref[:, 2 * d:3 * d].astype(F32)) * _dot(c_ref[...], wc_ref[...])
    o_ref[...] = x_ref[...] + _dot(merged.astype(BF16), wo_ref[...])


def _merge(xt, yc, u, d_skip, w_glu, b, c, gates, wa, wb, wc, wo, tm=512):
    m, d = xt.shape
    row = lambda w: pl.BlockSpec((tm, w), lambda i: (i, 0))
    return pl.pallas_call(
        _merge_kernel, grid=(m // tm,),
        in_specs=[row(d), row(SSM_WIDTH), row(SSM_WIDTH), _resident((1, SSM_WIDTH)),
                  _resident((SSM_WIDTH, 2 * SSM_WIDTH)), row(DSA_WIDTH), row(FOX_WIDTH), row(3 * d),
                  _resident((SSM_WIDTH, d)), _resident((DSA_WIDTH, d)), _resident((FOX_WIDTH, d)),
                  _resident((d, d))],
        out_specs=row(d), out_shape=jax.ShapeDtypeStruct((m, d), F32),
        compiler_params=_cparams(("parallel",)), name="merge",
    )(xt, yc, u, d_skip.reshape(1, -1), w_glu, b, c, gates, wa, wb, wc, wo)


def _pack_w_in(w_in):
    d = w_in.shape[0]
    splits = (SSM_WIDTH, DSA_WIDTH, DSA_LATENT, IDX_HEADS * IDX_DIM, IDX_DIM, IDX_HEADS,
              FOX_WIDTH, FOX_WIDTH, FOX_WIDTH, FOX_HEADS, D_MODEL, D_MODEL, D_MODEL)
    pts = np.cumsum(splits)[:-1]
    (w_u, w_qb, w_ckv, w_qi, w_ki, w_wi, w_qc, w_kc, w_vc, w_fc, w_ga, w_gb, w_gc) = jnp.split(
        w_in, pts, axis=1)
    scale = HEAD_DIM ** -0.5
    aux = jnp.zeros((d, LANES), F32)
    aux = aux.at[:, AUX_F:AUX_F + FOX_HEADS].set(w_fc).at[:, AUX_W:AUX_W + IDX_HEADS].set(w_wi)
    cols = [w_u, w_qb * scale, w_qi, w_ki, w_ki, w_ckv, aux, w_qc * scale, w_kc, w_vc,
            w_ga, w_gb, w_gc]
    return jnp.concatenate(cols, axis=1).astype(BF16)


def kernel(x, ffn1_norm, ffn1_w_gate, ffn1_w_up, ffn1_w_down, mix_norm, w_in, ssm_lambda_re, ssm_lambda_im, ssm_log_dt, ssm_b_re, ssm_b_im, ssm_c_re, ssm_c_im, ssm_d, ssm_w_glu, dsa_kv_norm, dsa_w_uk, dsa_w_uv, rel_bias, fox_b_f, w_branch_ssm, w_branch_dsa, w_branch_fox, w_out, ffn2_norm, ffn2_w_gate, ffn2_w_up, ffn2_w_down, final_norm):
    bsz, seq, d = x.shape
    depth = w_in.shape[0]
    n_keys = min(TOPK_MAX, seq // 4)
    assert d == D_MODEL and seq % ATT_TILE == 0 and (bsz * seq) % (8 * 512) == 0
    assert n_keys % LANES == 0 and bsz % 8 == 0
    m = bsz * seq
    n_chunks = seq // S5_CHUNK
    xt = x.reshape(m, d)
    band = _rel_band(rel_bias)
    bf = lambda w: w.astype(BF16)
    for l in range(depth):
        xt = _ffn(xt, ffn1_norm[l], bf(ffn1_w_gate[l]), bf(ffn1_w_up[l]), bf(ffn1_w_down[l]))
        w_kv = bf(jnp.concatenate([dsa_w_uk[l], dsa_w_uk[l], dsa_w_uv[l], dsa_w_uv[l]], axis=1))
        u, qb, qi, ki, kv, aux, qkv, gates = _inproj(
            xt, mix_norm[l], _pack_w_in(w_in[l]), dsa_kv_norm[l], w_kv)
        m_intra, m_sum, m_out, a_re, a_im = _s5_matrices(
            ssm_lambda_re[l], ssm_lambda_im[l], ssm_log_dt[l], ssm_b_re[l], ssm_b_im[l],
            ssm_c_re[l], ssm_c_im[l])
        u8 = u.reshape(m // S5_CHUNK, S5_COLS)
        y_intra, p = _s5a(u8, m_intra, m_sum)
        h_re, h_im = _s5b(p, a_re, a_im, n_chunks, nb=8)
        yc = _s5c(y_intra, h_re, h_im, m_out).reshape(m, SSM_WIDTH)
        b_out = _dsa(qb, qi, aux, kv, ki, band, bsz, seq, n_keys)
        c_col, c_row = _fox_cum(aux, fox_b_f[l], bsz, seq)
        c_out = _fox(qkv, c_col, c_row, bsz, seq)
        xt = _merge(xt, yc, u, ssm_d[l], bf(ssm_w_glu[l]), b_out, c_out, gates,
                    bf(w_branch_ssm[l]), bf(w_branch_dsa[l]), bf(w_branch_fox[l]), bf(w_out[l]))
        last = final_norm if l == depth - 1 else None
        xt = _ffn(xt, ffn2_norm[l], bf(ffn2_w_gate[l]), bf(ffn2_w_up[l]), bf(ffn2_w_down[l]),
                  final_norm=last)
    return xt.reshape(bsz, seq, d)
```

```python
import functools
import math

import numpy as np
import jax
import jax.numpy as jnp
from jax import lax
from jax.experimental import pallas as pl
from jax.experimental.pallas import tpu as pltpu

F32 = jnp.float32
BF16 = jnp.bfloat16

D_MODEL = 1024
SSM_WIDTH = D_MODEL // 4
SSM_GROUP = 16
SSM_GROUPS = SSM_WIDTH // SSM_GROUP
SSM_STATE = 64
HEAD_DIM = 64
DSA_HEADS = 6
DSA_WIDTH = DSA_HEADS * HEAD_DIM
DSA_LATENT = 2 * HEAD_DIM
IDX_HEADS = 4
IDX_DIM = 64
TOPK_MAX = 256
FOX_HEADS = 6
FOX_WIDTH = FOX_HEADS * HEAD_DIM
REL_BUCKETS = 32
REL_MAX_EXACT = 16
REL_MAX_DIST = 128
FFN_HIDDEN = 2816
EPS = 1e-6
NEG = -1e30

LANES = 128
S5_CHUNK = 8
S5_COLS = S5_CHUNK * SSM_WIDTH
S5_STATE = 2 * SSM_GROUPS * SSM_STATE
ATT_TILE = 256
VMEM_LIMIT = 56 * 1024 * 1024

_SEG = dict(u=256, qb=384, qi=256, ki=128, ckv=128, aux=128, qkv=1152, g=3072)
AUX_F = 0
AUX_W = 8


def _cparams(sem):
    return pltpu.CompilerParams(dimension_semantics=sem, vmem_limit_bytes=VMEM_LIMIT)


def _resident(shape):
    nd = len(shape)
    return pl.BlockSpec(shape, lambda *_: (0,) * nd, pipeline_mode=pl.Buffered(1))


def _rms(x, g):
    return x * lax.rsqrt(jnp.mean(x * x, axis=-1, keepdims=True) + EPS) * g


def _sigmoid(x):
    return 1.0 / (1.0 + jnp.exp(-x))


def _dot(a, b):
    return jnp.dot(a, b, preferred_element_type=F32)


def _ffn_body(x, g_ref, wg_ref, wu_ref, wd_ref, fc):
    h = _rms(x, g_ref[...]).astype(BF16)
    acc = None
    for c in range(FFN_HIDDEN // fc):
        sl = slice(c * fc, (c + 1) * fc)
        gt = _dot(h, wg_ref[:, sl])
        up = _dot(h, wu_ref[:, sl])
        a = (gt * _sigmoid(gt) * up).astype(BF16)
        d = _dot(a, wd_ref[sl, :])
        acc = d if acc is None else acc + d
    return x + 0.5 * acc


def _ffn_kernel(x_ref, g_ref, wg_ref, wu_ref, wd_ref, o_ref, *, fc):
    o_ref[...] = _ffn_body(x_ref[...], g_ref, wg_ref, wu_ref, wd_ref, fc)


def _ffn_final_kernel(x_ref, g_ref, wg_ref, wu_ref, wd_ref, fn_ref, o_ref, *, fc):
    y = _ffn_body(x_ref[...], g_ref, wg_ref, wu_ref, wd_ref, fc)
    o_ref[...] = _rms(y, fn_ref[...])


def _ffn(xt, norm, wg, wu, wd, final_norm=None, tm=512, fc=1408):
    m, d = xt.shape
    f = wg.shape[1]
    row = pl.BlockSpec((tm, d), lambda i: (i, 0))
    vec = _resident((1, d))
    in_specs = [row, vec, _resident((d, f)), _resident((d, f)), _resident((f, d))]
    args = [xt, norm.reshape(1, d), wg, wu, wd]
    if final_norm is None:
        body = functools.partial(_ffn_kernel, fc=fc)
    else:
        body = functools.partial(_ffn_final_kernel, fc=fc)
        in_specs.append(vec)
        args.append(final_norm.reshape(1, d))
    return pl.pallas_call(
        body, grid=(m // tm,), in_specs=in_specs, out_specs=row,
        out_shape=jax.ShapeDtypeStruct((m, d), F32),
        compiler_params=_cparams(("parallel",)), name="ffn")(*args)


def _inproj_kernel(x_ref, g_ref, w_ref, kvn_ref, wkv_ref,
                   u_ref, qb_ref, qi_ref, ki_ref, kv_ref, aux_ref, qkv_ref, gate_ref):
    h = _rms(x_ref[...], g_ref[...]).astype(BF16)
    off = 0
    outs = dict(u=u_ref, qb=qb_ref, qi=qi_ref, ki=ki_ref, aux=aux_ref, qkv=qkv_ref, g=gate_ref)
    for name, width in _SEG.items():
        r = _dot(h, w_ref[:, off:off + width])
        off += width
        if name == "ckv":
            c = _rms(r, kvn_ref[...]).astype(BF16)
            kv_ref[...] = _dot(c, wkv_ref[...]).astype(BF16)
        else:
            outs[name][...] = r.astype(outs[name].dtype)


def _inproj(xt, norm, w_all, kv_norm, w_kv, tm=512):
    m, d = xt.shape
    wtot = w_all.shape[1]
    widths = dict(u=(256, F32), qb=(384, BF16), qi=(256, BF16), ki=(128, BF16),
                  kv=(256, BF16), aux=(128, F32), qkv=(1152, BF16), g=(3072, BF16))
    out_shape = [jax.ShapeDtypeStruct((m, w), dt) for w, dt in widths.values()]
    out_specs = [pl.BlockSpec((tm, w), lambda i: (i, 0)) for w, _ in widths.values()]
    return pl.pallas_call(
        _inproj_kernel, grid=(m // tm,),
        in_specs=[pl.BlockSpec((tm, d), lambda i: (i, 0)), _resident((1, d)),
                  _resident((d, wtot)), _resident((1, DSA_LATENT)),
                  _resident((DSA_LATENT, 256))],
        out_specs=out_specs, out_shape=out_shape,
        compiler_params=_cparams(("parallel",)), name="inproj",
    )(xt, norm.reshape(1, d), w_all, kv_norm.reshape(1, DSA_LATENT), w_kv)


def _s5a_kernel(u_ref, mi_ref, ms_ref, yi_ref, p_ref):
    u = u_ref[...].astype(BF16)
    yi_ref[...] = _dot(u, mi_ref[...])
    p_ref[...] = _dot(u, ms_ref[...])


def _s5a(u8, m_intra, m_sum, tr=512):
    r = u8.shape[0]
    row = pl.BlockSpec((tr, S5_COLS), lambda i: (i, 0))
    srow = pl.BlockSpec((tr, S5_STATE), lambda i: (i, 0))
    return pl.pallas_call(
        _s5a_kernel, grid=(r // tr,),
        in_specs=[row, _resident((S5_COLS, S5_COLS)), _resident((S5_COLS, S5_STATE))],
        out_specs=[row, srow],
        out_shape=[jax.ShapeDtypeStruct((r, S5_COLS), F32),
                   jax.ShapeDtypeStruct((r, S5_STATE), F32)],
        compiler_params=_cparams(("parallel",)), name="s5_chunk")(u8, m_intra, m_sum)


def _s5b_kernel(pr_ref, pi_ref, ar_ref, ai_ref, hr_ref, hi_ref, *, nb, nc):
    ar = ar_ref[...]
    ai = ai_ref[...]
    cols = ar.shape[1]

    def step(c, carry):
        hr, hi = carry
        rows = pl.ds(c, nb, stride=nc)
        hr_ref[rows, :] = hr.astype(hr_ref.dtype)
        hi_ref[rows, :] = hi.astype(hi_ref.dtype)
        pr = pr_ref[rows, :]
        pi = pi_ref[rows, :]
        return (ar * hr - ai * hi + pr, ar * hi + ai * hr + pi)

    z = jnp.zeros((nb, cols), F32)
    lax.fori_loop(0, nc, step, (z, z))


def _s5b(p, a_re, a_im, nc, nb, tcol=LANES):
    r = p.shape[0]
    half = S5_STATE // 2
    ncb = half // tcol
    blk = pl.BlockSpec((nb * nc, tcol), lambda i, j: (i, j))
    im_blk = pl.BlockSpec((nb * nc, tcol), lambda i, j: (i, j + ncb))
    a_blk = pl.BlockSpec((1, tcol), lambda i, j: (0, j))
    return pl.pallas_call(
        functools.partial(_s5b_kernel, nb=nb, nc=nc), grid=(r // (nb * nc), ncb),
        in_specs=[blk, im_blk, a_blk, a_blk],
        out_specs=[blk, blk],
        out_shape=[jax.ShapeDtypeStruct((r, half), F32)] * 2,
        compiler_params=_cparams(("parallel", "parallel")), name="s5_scan",
    )(p, p, a_re, a_im)


def _s5c_kernel(yi_ref, hr_ref, hi_ref, mo_ref, y_ref):
    half = S5_STATE // 2
    y = yi_ref[...]
    y = y + _dot(hr_ref[...].astype(BF16), mo_ref[:half, :])
    y = y + _dot(hi_ref[...].astype(BF16), mo_ref[half:, :])
    y_ref[...] = y


def _s5c(yi, h_re, h_im, m_out, tr=512):
    r = yi.shape[0]
    half = S5_STATE // 2
    row = pl.BlockSpec((tr, S5_COLS), lambda i: (i, 0))
    return pl.pallas_call(
        _s5c_kernel, grid=(r // tr,),
        in_specs=[row, pl.BlockSpec((tr, half), lambda i: (i, 0)),
                  pl.BlockSpec((tr, half), lambda i: (i, 0)),
                  _resident((S5_STATE, S5_COLS))],
        out_specs=row, out_shape=jax.ShapeDtypeStruct((r, S5_COLS), F32),
        compiler_params=_cparams(("parallel",)), name="s5_out")(yi, h_re, h_im, m_out)


def _s5_matrices(lam_re, lam_im, log_dt, b_re, b_im, c_re, c_im):
    hp = lax.Precision.HIGHEST
    t = S5_CHUNK
    dt = jnp.exp(log_dt)[:, None]
    lr, li = lam_re, lam_im
    mag = jnp.exp(lr * dt)
    ab_re, ab_im = mag * jnp.cos(li * dt), mag * jnp.sin(li * dt)
    den = lr * lr + li * li
    nr, ni = ab_re - 1.0, ab_im
    s_re = (nr * lr + ni * li) / den
    s_im = (ni * lr - nr * li) / den
    bb_re = s_re[..., None] * b_re - s_im[..., None] * b_im
    bb_im = s_re[..., None] * b_im + s_im[..., None] * b_re
    pr, pi = [jnp.ones_like(ab_re)], [jnp.zeros_like(ab_re)]
    for _ in range(t):
        pr.append(pr[-1] * ab_re - pi[-1] * ab_im)
        pi.append(pr[-2] * ab_im + pi[-1] * ab_re)
    pw_re, pw_im = jnp.stack(pr), jnp.stack(pi)
    ab_b_re = pw_re[..., None] * bb_re - pw_im[..., None] * bb_im
    ab_b_im = pw_re[..., None] * bb_im + pw_im[..., None] * bb_re
    kern = (jnp.einsum('gpn,dgnq->dgpq', c_re, ab_b_re[:t], precision=hp)
            - jnp.einsum('gpn,dgnq->dgpq', c_im, ab_b_im[:t], precision=hp))
    eye_g = jnp.eye(SSM_GROUPS, dtype=F32)
    lag = np.arange(t)[None, :] - np.arange(t)[:, None]
    sel = jnp.asarray((lag[None] == np.arange(t)[:, None, None]).astype(np.float32))
    m_intra = jnp.einsum('dji,dgpq,gh->jgqihp', sel, kern, eye_g, precision=hp)
    m_intra = m_intra.reshape(S5_COLS, S5_COLS)
    sb_re = ab_b_re[:t][::-1]
    sb_im = ab_b_im[:t][::-1]
    m_sum_re = jnp.einsum('jgnq,gh->jgqhn', sb_re, eye_g, precision=hp)
    m_sum_im = jnp.einsum('jgnq,gh->jgqhn', sb_im, eye_g, precision=hp)
    m_sum = jnp.concatenate([m_sum_re.reshape(S5_COLS, -1), m_sum_im.reshape(S5_COLS, -1)], axis=1)
    w_re = c_re[None] * pw_re[1:, :, None, :] - c_im[None] * pw_im[1:, :, None, :]
    w_im = c_re[None] * pw_im[1:, :, None, :] + c_im[None] * pw_re[1:, :, None, :]
    m_out_re = jnp.einsum('igpn,gh->gnihp', w_re, eye_g, precision=hp)
    m_out_im = jnp.einsum('igpn,gh->gnihp', -w_im, eye_g, precision=hp)
    m_out = jnp.concatenate([m_out_re.reshape(-1, S5_COLS), m_out_im.reshape(-1, S5_COLS)], axis=0)
    a_re = pw_re[t].reshape(1, -1)
    a_im = pw_im[t].reshape(1, -1)
    return m_intra.astype(BF16), m_sum.astype(BF16), m_out.astype(BF16), a_re, a_im


def _fold8(x, op):
    parts = [x[r:r + 8] for r in range(0, x.shape[0], 8)]
    while len(parts) > 1:
        parts = [op(parts[i], parts[i + 1]) for i in range(0, len(parts), 2)]
    return parts[0]


def _split_heads_t(q_ref, qt_ref, n_heads):
    tq = q_ref.shape[0]
    top = lax.broadcasted_iota(jnp.int32, (LANES, tq), 0) < HEAD_DIM
    for pr in range(n_heads // 2):
        pair_t = q_ref[:, pr * LANES:(pr + 1) * LANES].astype(F32).T
        qt_ref[:, 2 * pr * tq:(2 * pr + 1) * tq] = jnp.where(top, pair_t, 0.0).astype(BF16)
        qt_ref[:, (2 * pr + 1) * tq:(2 * pr + 2) * tq] = jnp.where(top, 0.0, pair_t).astype(BF16)


def _softmax_update(s, pv, m_ref, acc_ref):
    m_old = m_ref[...]
    m_new = jnp.maximum(m_old, jnp.max(_fold8(s, jnp.maximum), axis=0, keepdims=True))
    alpha = jnp.exp(m_old - m_new)
    p = jnp.exp(s - jnp.broadcast_to(m_new[0:1], s.shape)).astype(BF16)
    acc = acc_ref[...]
    acc_ref[...] = acc * jnp.broadcast_to(alpha[0:1], acc.shape) + pv(p)
    m_ref[...] = m_new


def _finish_heads_t(acc_ref, o_ref, n_heads, odd_swapped):
    hd = HEAD_DIM
    tq = o_ref.shape[0]
    for pr in range(n_heads // 2):
        ae = acc_ref[:, 2 * pr * tq:(2 * pr + 1) * tq]
        ao = acc_ref[:, (2 * pr + 1) * tq:(2 * pr + 2) * tq]
        oe = ae[0:hd] / ae[hd:2 * hd]
        oo = ao[hd:2 * hd] / ao[0:hd] if odd_swapped else ao[0:hd] / ao[hd:2 * hd]
        pair = jnp.concatenate([oe, oo], axis=0)
        o_ref[:, pr * LANES:(pr + 1) * LANES] = pair.T.astype(o_ref.dtype)


def _dsa_kernel(q_ref, qi_ref, aux_ref, kv_ref, ki_ref, band_ref, o_ref,
                score_ref, vt_ref, qm_ref, m_ref, acc_ref, thr_ref, *, n_keys, seq):
    tq = tk = ATT_TILE
    qt = pl.program_id(1)
    n_chunks = qt + 1
    kf = float(n_keys)

    @pl.when(qt == 0)
    def _():
        top = lax.broadcasted_iota(jnp.int32, (LANES, tk), 0) < HEAD_DIM
        for j in range(seq // tk):
            vv = kv_ref[j * tk:(j + 1) * tk, LANES:2 * LANES].astype(F32)
            vt_ref[j] = jnp.where(top, vv.T, 1.0).astype(BF16)

    _split_heads_t(qi_ref, qm_ref, IDX_HEADS)
    aux_t = aux_ref[...].T
    head_w = jnp.concatenate(
        [aux_t[AUX_W + h:AUX_W + h + 1, :] for h in range(IDX_HEADS)], axis=1
    ) * (IDX_HEADS ** -0.5 * IDX_DIM ** -0.5)
    key_minus_query = (lax.broadcasted_iota(jnp.int32, (tk, tq), 0)
                       - lax.broadcasted_iota(jnp.int32, (tk, tq), 1))

    def score_body(j, carry):
        mx, mn = carry
        kc = ki_ref[pl.ds(pl.multiple_of(j * tk, tk), tk), :]
        d = _dot(kc, qm_ref[:, 0:IDX_HEADS * tq])
        d = jnp.broadcast_to(head_w, d.shape) * jnp.maximum(d, 0.0)
        sc = (d[:, 0:tq] + d[:, tq:2 * tq]) + (d[:, 2 * tq:3 * tq] + d[:, 3 * tq:4 * tq])
        causal = key_minus_query <= (qt - j) * tq
        score_ref[j] = jnp.where(causal, sc, NEG)
        mx = jnp.maximum(mx, _fold8(jnp.where(causal, sc, NEG), jnp.maximum))
        mn = jnp.minimum(mn, _fold8(jnp.where(causal, sc, -NEG), jnp.minimum))
        return mx, mn

    mx, mn = lax.fori_loop(0, n_chunks, score_body,
                           (jnp.full((8, tq), NEG, F32), jnp.full((8, tq), -NEG, F32)))

    t_q = qt * tq + lax.broadcasted_iota(jnp.int32, (1, tq), 1)
    thr_ref[...] = jnp.full((8, tq), 0.5 * NEG, F32)

    def count(pred):
        def body(j, acc):
            return acc + _fold8(pred(score_ref[j]), jnp.add)
        acc = lax.fori_loop(0, n_chunks, body, jnp.zeros((8, tq), F32))
        return jnp.sum(acc, axis=0, keepdims=True)

    def count_ge(x):
        xb = jnp.broadcast_to(x, (tk, tq))
        return count(lambda s: jnp.where(s >= xb, 1.0, 0.0))

    @pl.when((qt + 1) * tq > n_keys)
    def _search():
        need = (t_q >= n_keys).astype(F32)
        lo0 = jnp.min(mn, axis=0, keepdims=True)
        hi0 = jnp.max(mx, axis=0, keepdims=True)
        c_top = count_ge(hi0)
        c_ge0 = count_ge(jnp.zeros((1, tq), F32))
        c_gt0 = count(lambda s: jnp.where(s > 0.0, 1.0, 0.0))
        n_valid = (t_q + 1).astype(F32)
        top_hit = c_top == kf
        top_tie = c_top > kf
        zero_hit = c_ge0 == kf
        zero_tie = (c_gt0 < kf) & (c_ge0 > kf)
        positive = c_gt0 >= kf
        hi_neg = hi0 < 0.0
        lo1 = jnp.where(positive, jnp.maximum(lo0, 0.0), lo0)
        c_lo1 = jnp.where(positive, c_ge0, n_valid)
        hi1 = jnp.where(positive | hi_neg, hi0, 0.0)
        c_hi1 = jnp.where(positive | hi_neg, c_top, c_ge0)
        hit0 = top_hit | ((~top_tie) & zero_hit)
        tie0 = top_tie | ((~top_hit) & (~zero_hit) & zero_tie)
        x0 = jnp.where(top_hit, hi0, 0.0)
        tie_val = jnp.where(top_tie, hi0, 0.0)
        above0 = jnp.where(top_tie, 0.0, c_gt0)
        hit0 = need * hit0.astype(F32)
        tie0 = need * tie0.astype(F32) * (1.0 - hit0)
        done0 = jnp.maximum(1.0 - need, jnp.maximum(hit0, tie0))
        lo1 = jnp.where(tie0 > 0, tie_val, lo1)
        c_hi1 = jnp.where(tie0 > 0, above0, c_hi1)
        state0 = (lo1, hi1, c_lo1, c_hi1, done0, tie0, x0, jnp.max(1.0 - done0))

        def cond(st):
            return st[7] > 0.0

        def body(st):
            lo, hi, c_lo, c_hi, done, tie, x_fin, _ = st
            x = 0.5 * lo + 0.5 * hi
            inside = ((x > lo) & (x < hi)).astype(F32)
            active = 1.0 - done
            probe = active * inside
            c = count_ge(jnp.where(probe > 0, x, x_fin))
            hit = probe * (c == kf).astype(F32)
            more = probe * (c > kf).astype(F32)
            less = probe * (c < kf).astype(F32)
            new_tie = active * (1.0 - inside)
            lo = jnp.where(more > 0, x, lo)
            c_lo = jnp.where(more > 0, c, c_lo)
            hi = jnp.where(less > 0, x, hi)
            c_hi = jnp.where(less > 0, c, c_hi)
            x_fin = jnp.where(hit > 0, x, x_fin)
            tie = jnp.maximum(tie, new_tie)
            done = jnp.maximum(done, jnp.maximum(hit, new_tie))
            return (lo, hi, c_lo, c_hi, done, tie, x_fin, jnp.max(1.0 - done))

        st = lax.while_loop(cond, body, state0)
        lo, _, _, c_hi, _, tie, x_fin, _ = st
        thr = jnp.where(need > 0, jnp.where(tie > 0, lo, x_fin), 0.5 * NEG)
        thr_ref[...] = jnp.broadcast_to(thr, (8, tq))

        @pl.when(jnp.max(tie) > 0.0)
        def _ties():
            want = jnp.broadcast_to(kf - c_hi, (tk, tq))
            thr_t = jnp.broadcast_to(jnp.where(tie > 0, thr, -NEG), (tk, tq))
            upto = (lax.broadcasted_iota(jnp.int32, (tk, tk), 1)
                    <= lax.broadcasted_iota(jnp.int32, (tk, tk), 0))
            prefix = jnp.where(upto, 1.0, 0.0).astype(BF16)

            def body(j, seen):
                s = score_ref[j]
                eq = s == thr_t
                rank = _dot(prefix, jnp.where(eq, 1.0, 0.0).astype(BF16))
                rank = rank + jnp.broadcast_to(seen, (tk, tq))
                score_ref[j] = jnp.where(eq, jnp.where(rank > want, NEG, s), s)
                return rank[tk - 1:tk, :]

            lax.fori_loop(0, n_chunks, body, jnp.zeros((1, tq), F32))

    _split_heads_t(q_ref, qm_ref, DSA_HEADS)
    m_ref[...] = jnp.full(m_ref.shape, NEG, F32)
    acc_ref[...] = jnp.zeros(acc_ref.shape, F32)
    thr_b = jnp.broadcast_to(thr_ref[0:1, :], (tk, tq))

    def attend(j, band_lo):
        rows = pl.ds(pl.multiple_of(j * tk, tk), tk)
        off = jnp.where(score_ref[j] >= thr_b, 0.0, NEG)
        s = _dot(kv_ref[rows, 0:LANES], qm_ref[...])
        s = s + jnp.concatenate([off] * DSA_HEADS, axis=1)
        if band_lo is not None:
            s = s + band_ref[band_lo:band_lo + tk, :]
        v_t = vt_ref[j]
        _softmax_update(s, lambda p: _dot(v_t, p), m_ref, acc_ref)

    def far_body(j, carry):
        attend(j, None)
        return carry

    lax.fori_loop(0, jnp.maximum(qt - 1, 0), far_body, 0)

    @pl.when(qt >= 1)
    def _():
        attend(qt - 1, 0)

    attend(qt, tk)
    _finish_heads_t(acc_ref, o_ref, DSA_HEADS, odd_swapped=False)


def _dsa(qb, qi, aux, kv, ki, band, bsz, seq, n_keys):
    tq = ATT_TILE
    nq = seq // tq
    qrow = lambda w: pl.BlockSpec((tq, w), lambda b, q: (b * nq + q, 0))
    seq_blk = lambda w: pl.BlockSpec((seq, w), lambda b, q: (b, 0))
    return pl.pallas_call(
        functools.partial(_dsa_kernel, n_keys=n_keys, seq=seq), grid=(bsz, nq),
        in_specs=[qrow(DSA_WIDTH), qrow(256), qrow(LANES), seq_blk(256), seq_blk(LANES),
                  _resident((2 * tq, DSA_HEADS * tq))],
        out_specs=qrow(DSA_WIDTH),
        out_shape=jax.ShapeDtypeStruct((bsz * seq, DSA_WIDTH), BF16),
        scratch_shapes=[pltpu.VMEM((nq, tq, tq), F32),
                        pltpu.VMEM((nq, LANES, tq), BF16),
                        pltpu.VMEM((LANES, DSA_HEADS * tq), BF16),
                        pltpu.VMEM((8, DSA_HEADS * tq), F32),
                        pltpu.VMEM((LANES, DSA_HEADS * tq), F32),
                        pltpu.VMEM((8, tq), F32)],
        compiler_params=_cparams(("parallel", "arbitrary")), name="dsa",
    )(qb, qi, aux, kv, ki, band)


def _band_kernel(bucket_ref, rb_ref, o_ref):
    bucket = bucket_ref[...]
    tq = bucket.shape[1]
    for h in range(DSA_HEADS):
        acc = jnp.zeros(bucket.shape, F32)
        for k in range(REL_BUCKETS - 1):
            acc = jnp.where(bucket == k, rb_ref[k, h] - rb_ref[REL_BUCKETS - 1, h], acc)
        o_ref[:, h * tq:(h + 1) * tq] = acc


def _t5_bucket_np(dist):
    d = np.maximum(dist, 0)
    df = np.maximum(d, 1).astype(np.float32)
    log_b = REL_MAX_EXACT + (np.log(df / REL_MAX_EXACT) / math.log(REL_MAX_DIST / REL_MAX_EXACT)
                             * (REL_BUCKETS - REL_MAX_EXACT)).astype(np.int32)
    log_b = np.minimum(log_b, REL_BUCKETS - 1)
    return np.where(d < REL_MAX_EXACT, d, log_b).astype(np.int32)


def _rel_band(rel_bias):
    tq = ATT_TILE
    i = np.arange(tq)[None, :]
    j = np.arange(2 * tq)[:, None]
    bucket = jnp.asarray(_t5_bucket_np(i + tq - j))
    return pl.pallas_call(
        _band_kernel,
        in_specs=[pl.BlockSpec(memory_space=pltpu.VMEM), pl.BlockSpec(memory_space=pltpu.SMEM)],
        out_specs=pl.BlockSpec(memory_space=pltpu.VMEM),
        out_shape=jax.ShapeDtypeStruct((2 * tq, DSA_HEADS * tq), F32), name="rel_band",
    )(bucket, rel_bias)


def _cum_kernel(aux_ref, bf_ref, col_ref, row_ref, *, seq):
    x = aux_ref[...] + bf_ref[...]
    logf = -(jnp.maximum(-x, 0.0) + jnp.log1p(jnp.exp(-jnp.abs(x))))
    c = logf.T
    pos = lax.broadcasted_iota(jnp.int32, c.shape, 1)
    shift = 1
    while shift < seq:
        c = c + jnp.where(pos >= shift, pltpu.roll(c, shift, 1), 0.0)
        shift *= 2
    col_ref[...] = c.T
    for j in range(seq // ATT_TILE):
        row_ref[j] = c[0:8, j * ATT_TILE:(j + 1) * ATT_TILE]


def _fox_cum(aux, b_f, bsz, seq):
    nq = seq // ATT_TILE
    bf = jnp.zeros((1, LANES), F32).at[0, AUX_F:AUX_F + FOX_HEADS].set(b_f)
    return pl.pallas_call(
        functools.partial(_cum_kernel, seq=seq), grid=(bsz,),
        in_specs=[pl.BlockSpec((seq, LANES), lambda b: (b, 0)), _resident((1, LANES))],
        out_specs=[pl.BlockSpec((seq, LANES), lambda b: (b, 0)),
                   pl.BlockSpec((None, nq, 8, ATT_TILE), lambda b: (b, 0, 0, 0))],
        out_shape=[jax.ShapeDtypeStruct((bsz * seq, LANES), F32),
                   jax.ShapeDtypeStruct((bsz, nq, 8, ATT_TILE), F32)],
        compiler_params=_cparams(("parallel",)), name="fox_cum")(aux, bf)


def _fox_kernel(q_ref, k_ref, v_ref, ccol_ref, crow_ref, o_ref,
                vte_ref, vto_ref, ck_ref, qm_ref, m_ref, acc_ref, *, seq):
    tq = tk = ATT_TILE
    qt = pl.program_id(1)
    heads = range(FOX_HEADS)

    @pl.when(qt == 0)
    def _():
        top = lax.broadcasted_iota(jnp.int32, (LANES, tk), 0) < HEAD_DIM
        for j in range(seq // tk):
            rows = slice(j * tk, (j + 1) * tk)
            for pr in range(FOX_HEADS // 2):
                v_t = v_ref[rows, pr * LANES:(pr + 1) * LANES].astype(F32).T
                vte_ref[pr, j] = jnp.where(top, v_t, 1.0).astype(BF16)
                vto_ref[pr, j] = jnp.where(top, 1.0, v_t).astype(BF16)
            cc = ccol_ref[rows, :]
            for h in heads:
                ck_ref[h, rows, :] = jnp.broadcast_to(cc[:, AUX_F + h:AUX_F + h + 1], (tk, LANES))

    _split_heads_t(q_ref, qm_ref, FOX_HEADS)
    m_ref[...] = jnp.full(m_ref.shape, NEG, F32)
    acc_ref[...] = jnp.zeros(acc_ref.shape, F32)
    c_q = crow_ref[qt]
    c_q = jnp.concatenate([c_q[AUX_F + h:AUX_F + h + 1, :] for h in heads], axis=1)
    causal_off = jnp.where(lax.broadcasted_iota(jnp.int32, (tk, tq), 0)
                           <= lax.broadcasted_iota(jnp.int32, (tk, tq), 1), 0.0, NEG)

    def attend(j, diagonal):
        rows = pl.ds(pl.multiple_of(j * tk, tk), tk)
        s = jnp.concatenate(
            [_dot(k_ref[rows, (h // 2) * LANES:(h // 2 + 1) * LANES], qm_ref[:, h * tq:(h + 1) * tq])
             for h in heads], axis=1)
        c_k = jnp.concatenate([ck_ref[h, rows, :] for h in heads for _ in range(tq // LANES)], axis=1)
        s = s + (jnp.broadcast_to(c_q, s.shape) - c_k)
        if diagonal:
            s = s + jnp.concatenate([causal_off] * FOX_HEADS, axis=1)

        def pv(p):
            return jnp.concatenate(
                [_dot(vte_ref[h // 2, j] if h % 2 == 0 else vto_ref[h // 2, j],
                      p[:, h * tq:(h + 1) * tq]) for h in heads], axis=1)

        _softmax_update(s, pv, m_ref, acc_ref)

    def far_body(j, carry):
        attend(j, False)
        return carry

    lax.fori_loop(0, qt, far_body, 0)
    attend(qt, True)
    _finish_heads_t(acc_ref, o_ref, FOX_HEADS, odd_swapped=True)


def _fox(qkv, c_col, c_row, bsz, seq):
    tq = ATT_TILE
    nq = seq // tq
    npair = FOX_HEADS // 2
    return pl.pallas_call(
        functools.partial(_fox_kernel, seq=seq), grid=(bsz, nq),
        in_specs=[pl.BlockSpec((tq, FOX_WIDTH), lambda b, q: (b * nq + q, 0)),
                  pl.BlockSpec((seq, FOX_WIDTH), lambda b, q: (b, 1)),
                  pl.BlockSpec((seq, FOX_WIDTH), lambda b, q: (b, 2)),
                  pl.BlockSpec((seq, LANES), lambda b, q: (b, 0)),
                  pl.BlockSpec((None, nq, 8, tq), lambda b, q: (b, 0, 0, 0))],
        out_specs=pl.BlockSpec((tq, FOX_WIDTH), lambda b, q: (b * nq + q, 0)),
        out_shape=jax.ShapeDtypeStruct((bsz * seq, FOX_WIDTH), BF16),
        scratch_shapes=[pltpu.VMEM((npair, nq, LANES, tq), BF16),
                        pltpu.VMEM((npair, nq, LANES, tq), BF16),
                        pltpu.VMEM((FOX_HEADS, seq, LANES), F32),
                        pltpu.VMEM((LANES, FOX_HEADS * tq), BF16),
                        pltpu.VMEM((8, FOX_HEADS * tq), F32),
                        pltpu.VMEM((LANES, FOX_HEADS * tq), F32)],
        compiler_params=_cparams(("parallel", "arbitrary")), name="fox",
    )(qkv, qkv, qkv, c_col, c_row)


def _gelu(y):
    return 0.5 * y * (1.0 + jnp.tanh(math.sqrt(2.0 / math.pi) * (y + 0.044715 * (y * y * y))))


def _merge_kernel(x_ref, yc_ref, u_ref, d_ref, wglu_ref, b_ref, c_ref, gate_ref,
                  wa_ref, wb_ref, wc_ref, wo_ref, o_ref):
    d = D_MODEL
    y = yc_ref[...] + d_ref[...] * u_ref[...]
    z = _dot(_gelu(y).astype(BF16), wglu_ref[...])
    a = z[:, :SSM_WIDTH] * _sigmoid(z[:, SSM_WIDTH:])
    merged = _sigmoid(gate_ref[:, 0:d].astype(F32)) * _dot(a.astype(BF16), wa_ref[...])
    merged += _sigmoid(gate_ref[:, d:2 * d].astype(F32)) * _dot(b_ref[...], wb_ref[...])
    merged += _sigmoid(gate_ref[:, 2 * d:3 * d].astype(F32)) * _dot(c_ref[...], wc_ref[...])
    o_ref[...] = x_ref[...] + _dot(merged.astype(BF16), wo_ref[...])


def _merge(xt, yc, u, d_skip, w_glu, b, c, gates, wa, wb, wc, wo, tm=512):
    m, d = xt.shape
    row = lambda w: pl.BlockSpec((tm, w), lambda i: (i, 0))
    return pl.pallas_call(
        _merge_kernel, grid=(m // tm,),
        in_specs=[row(d), row(SSM_WIDTH), row(SSM_WIDTH), _resident((1, SSM_WIDTH)),
                  _resident((SSM_WIDTH, 2 * SSM_WIDTH)), row(DSA_WIDTH), row(FOX_WIDTH), row(3 * d),
                  _resident((SSM_WIDTH, d)), _resident((DSA_WIDTH, d)), _resident((FOX_WIDTH, d)),
                  _resident((d, d))],
        out_specs=row(d), out_shape=jax.ShapeDtypeStruct((m, d), F32),
        compiler_params=_cparams(("parallel",)), name="merge",
    )(xt, yc, u, d_skip.reshape(1, -1), w_glu, b, c, gates, wa, wb, wc, wo)


def _pack_w_in(w_in):
    d = w_in.shape[0]
    splits = (SSM_WIDTH, DSA_WIDTH, DSA_LATENT, IDX_HEADS * IDX_DIM, IDX_DIM, IDX_HEADS,
              FOX_WIDTH, FOX_WIDTH, FOX_WIDTH, FOX_HEADS, D_MODEL, D_MODEL, D_MODEL)
    pts = np.cumsum(splits)[:-1]
    (w_u, w_qb, w_ckv, w_qi, w_ki, w_wi, w_qc, w_kc, w_vc, w_fc, w_ga, w_gb, w_gc) = jnp.split(
        w_in, pts, axis=1)
    scale = HEAD_DIM ** -0.5
    aux = jnp.zeros((d, LANES), F32)
    aux = aux.at[:, AUX_F:AUX_F + FOX_HEADS].set(w_fc).at[:, AUX_W:AUX_W + IDX_HEADS].set(w_wi)
    cols = [w_u, w_qb * scale, w_qi, w_ki, w_ki, w_ckv, aux, w_qc * scale, w_kc, w_vc,
            w_ga, w_gb, w_gc]
    return jnp.concatenate(cols, axis=1).astype(BF16)


def kernel(x, ffn1_norm, ffn1_w_gate, ffn1_w_up, ffn1_w_down, mix_norm, w_in, ssm_lambda_re, ssm_lambda_im, ssm_log_dt, ssm_b_re, ssm_b_im, ssm_c_re, ssm_c_im, ssm_d, ssm_w_glu, dsa_kv_norm, dsa_w_uk, dsa_w_uv, rel_bias, fox_b_f, w_branch_ssm, w_branch_dsa, w_branch_fox, w_out, ffn2_norm, ffn2_w_gate, ffn2_w_up, ffn2_w_down, final_norm):
    bsz, seq, d = x.shape
    depth = w_in.shape[0]
    n_keys = min(TOPK_MAX, seq // 4)
    assert d == D_MODEL and seq % ATT_TILE == 0 and (bsz * seq) % (8 * 512) == 0
    assert n_keys % LANES == 0 and bsz % 8 == 0
    m = bsz * seq
    n_chunks = seq // S5_CHUNK
    xt = x.reshape(m, d)
    band = _rel_band(rel_bias)
    bf = lambda w: w.astype(BF16)
    for l in range(depth):
        xt = _ffn(xt, ffn1_norm[l], bf(ffn1_w_gate[l]), bf(ffn1_w_up[l]), bf(ffn1_w_down[l]))
        w_kv = bf(jnp.concatenate([dsa_w_uk[l], dsa_w_uk[l], dsa_w_uv[l], dsa_w_uv[l]], axis=1))
        u, qb, qi, ki, kv, aux, qkv, gates = _inproj(
            xt, mix_norm[l], _pack_w_in(w_in[l]), dsa_kv_norm[l], w_kv)
        m_intra, m_sum, m_out, a_re, a_im = _s5_matrices(
            ssm_lambda_re[l], ssm_lambda_im[l], ssm_log_dt[l], ssm_b_re[l], ssm_b_im[l],
            ssm_c_re[l], ssm_c_im[l])
        u8 = u.reshape(m // S5_CHUNK, S5_COLS)
        y_intra, p = _s5a(u8, m_intra, m_sum)
        h_re, h_im = _s5b(p, a_re, a_im, n_chunks, nb=8)
        yc = _s5c(y_intra, h_re, h_im, m_out).reshape(m, SSM_WIDTH)
        b_out = _dsa(qb, qi, aux, kv, ki, band, bsz, seq, n_keys)
        c_col, c_row = _fox_cum(aux, fox_b_f[l], bsz, seq)
        c_out = _fox(qkv, c_col, c_row, bsz, seq)
        xt = _merge(xt, yc, u, ssm_d[l], bf(ssm_w_glu[l]), b_out, c_out, gates,
                    bf(w_branch_ssm[l]), bf(w_branch_dsa[l]), bf(w_branch_fox[l]), bf(w_out[l]))
        last = final_norm if l == depth - 1 else None
        xt = _ffn(xt, ffn2_norm[l], bf(ffn2_w_gate[l]), bf(ffn2_w_up[l]), bf(ffn2_w_down[l]),
                  final_norm=last)
    return xt.reshape(bsz, seq, d)
```

```python
import functools
import math

import numpy as np
import jax
import jax.numpy as jnp
from jax import lax
from jax.experimental import pallas as pl
from jax.experimental.pallas import tpu as pltpu

F32 = jnp.float32
BF16 = jnp.bfloat16

D_MODEL = 1024
SSM_WIDTH = D_MODEL // 4
SSM_GROUP = 16
SSM_GROUPS = SSM_WIDTH // SSM_GROUP
SSM_STATE = 64
HEAD_DIM = 64
DSA_HEADS = 6
DSA_WIDTH = DSA_HEADS * HEAD_DIM
DSA_LATENT = 2 * HEAD_DIM
IDX_HEADS = 4
IDX_DIM = 64
TOPK_MAX = 256
FOX_HEADS = 6
FOX_WIDTH = FOX_HEADS * HEAD_DIM
REL_BUCKETS = 32
REL_MAX_EXACT = 16
REL_MAX_DIST = 128
FFN_HIDDEN = 2816
EPS = 1e-6
NEG = -1e30

LANES = 128
S5_CHUNK = 8
S5_COLS = S5_CHUNK * SSM_WIDTH
S5_STATE = 2 * SSM_GROUPS * SSM_STATE
ATT_TILE = 256
SEARCH_UNROLL = 4
VMEM_LIMIT = 56 * 1024 * 1024

_SEG = dict(u=256, qb=384, qi=256, ki=128, ckv=128, aux=128, qkv=1152, g=3072)
AUX_F = 0
AUX_W = 8


def _cparams(sem):
    return pltpu.CompilerParams(dimension_semantics=sem, vmem_limit_bytes=VMEM_LIMIT)


def _resident(shape):
    nd = len(shape)
    return pl.BlockSpec(shape, lambda *_: (0,) * nd, pipeline_mode=pl.Buffered(1))


def _rms(x, g):
    return x * lax.rsqrt(jnp.mean(x * x, axis=-1, keepdims=True) + EPS) * g


def _sigmoid(x):
    return 1.0 / (1.0 + jnp.exp(-x))


def _dot(a, b):
    return jnp.dot(a, b, preferred_element_type=F32)


def _ffn_body(x, g_ref, wg_ref, wu_ref, wd_ref, fc):
    h = _rms(x, g_ref[...]).astype(BF16)
    acc = None
    for c in range(FFN_HIDDEN // fc):
        sl = slice(c * fc, (c + 1) * fc)
        gt = _dot(h, wg_ref[:, sl])
        up = _dot(h, wu_ref[:, sl])
        a = (gt * _sigmoid(gt) * up).astype(BF16)
        d = _dot(a, wd_ref[sl, :])
        acc = d if acc is None else acc + d
    return x + 0.5 * acc


def _ffn_kernel(x_ref, g_ref, wg_ref, wu_ref, wd_ref, o_ref, *, fc):
    o_ref[...] = _ffn_body(x_ref[...], g_ref, wg_ref, wu_ref, wd_ref, fc)


def _ffn_final_kernel(x_ref, g_ref, wg_ref, wu_ref, wd_ref, fn_ref, o_ref, *, fc):
    y = _ffn_body(x_ref[...], g_ref, wg_ref, wu_ref, wd_ref, fc)
    o_ref[...] = _rms(y, fn_ref[...])


def _ffn(xt, norm, wg, wu, wd, final_norm=None, tm=1024, fc=256):
    m, d = xt.shape
    f = wg.shape[1]
    row = pl.BlockSpec((tm, d), lambda i: (i, 0))
    vec = _resident((1, d))
    in_specs = [row, vec, _resident((d, f)), _resident((d, f)), _resident((f, d))]
    args = [xt, norm.reshape(1, d), wg, wu, wd]
    if final_norm is None:
        body = functools.partial(_ffn_kernel, fc=fc)
    else:
        body = functools.partial(_ffn_final_kernel, fc=fc)
        in_specs.append(vec)
        args.append(final_norm.reshape(1, d))
    return pl.pallas_call(
        body, grid=(m // tm,), in_specs=in_specs, out_specs=row,
        out_shape=jax.ShapeDtypeStruct((m, d), F32),
        compiler_params=_cparams(("parallel",)), name="ffn")(*args)


def _rows_to_chunks(x, tmp_ref, o_ref):
    n, w = x.shape
    for hh in range(w // LANES):
        tmp_ref[hh] = x[:, hh * LANES:(hh + 1) * LANES]
    for i in range(S5_CHUNK):
        for hh in range(w // LANES):
            o_ref[:, i * w + hh * LANES:i * w + (hh + 1) * LANES] = (
                tmp_ref[hh, pl.ds(i, n // S5_CHUNK, stride=S5_CHUNK), :])


def _chunks_to_rows(x_ref, tmp_ref, w):
    n = x_ref.shape[0]
    for i in range(S5_CHUNK):
        for hh in range(w // LANES):
            tmp_ref[hh, pl.ds(i, n, stride=S5_CHUNK), :] = (
                x_ref[:, i * w + hh * LANES:i * w + (hh + 1) * LANES])
    return jnp.concatenate([tmp_ref[hh] for hh in range(w // LANES)], axis=1)


def _inproj_kernel(x_ref, g_ref, w_ref, kvn_ref, wkv_ref,
                   u_ref, qb_ref, qi_ref, ki_ref, kv_ref, aux_ref, qkv_ref, gate_ref, u8_ref,
                   tmp_ref):
    h = _rms(x_ref[...], g_ref[...]).astype(BF16)
    outs = dict(u=u_ref, qb=qb_ref, qi=qi_ref, ki=ki_ref, aux=aux_ref, qkv=qkv_ref)
    off = 0
    for group in (("u", "qb", "qi", "ki"), ("ckv", "aux", "qkv")):
        width = sum(_SEG[name] for name in group)
        r = _dot(h, w_ref[:, off:off + width])
        off += width
        lo = 0
        for name in group:
            piece = r[:, lo:lo + _SEG[name]]
            lo += _SEG[name]
            if name == "ckv":
                c = _rms(piece, kvn_ref[...]).astype(BF16)
                kv_ref[...] = _dot(c, wkv_ref[...]).astype(BF16)
            else:
                outs[name][...] = piece.astype(outs[name].dtype)
            if name == "u":
                _rows_to_chunks(piece, tmp_ref, u8_ref)
    for c in range(_SEG["g"] // D_MODEL):
        cols = slice(c * D_MODEL, (c + 1) * D_MODEL)
        gate_ref[:, cols] = _dot(h, w_ref[:, off + c * D_MODEL:off + (c + 1) * D_MODEL]).astype(BF16)


def _inproj(xt, norm, w_all, kv_norm, w_kv, tm=1024):
    m, d = xt.shape
    wtot = w_all.shape[1]
    widths = dict(u=(256, F32), qb=(384, BF16), qi=(256, BF16), ki=(128, BF16),
                  kv=(256, BF16), aux=(128, F32), qkv=(1152, BF16), g=(3072, BF16))
    out_shape = [jax.ShapeDtypeStruct((m, w), dt) for w, dt in widths.values()]
    out_specs = [pl.BlockSpec((tm, w), lambda i: (i, 0)) for w, _ in widths.values()]
    out_shape.append(jax.ShapeDtypeStruct((m // S5_CHUNK, S5_COLS), F32))
    out_specs.append(pl.BlockSpec((tm // S5_CHUNK, S5_COLS), lambda i: (i, 0)))
    return pl.pallas_call(
        _inproj_kernel, grid=(m // tm,),
        in_specs=[pl.BlockSpec((tm, d), lambda i: (i, 0)), _resident((1, d)),
                  _resident((d, wtot)), _resident((1, DSA_LATENT)),
                  _resident((DSA_LATENT, 256))],
        out_specs=out_specs, out_shape=out_shape,
        scratch_shapes=[pltpu.VMEM((SSM_WIDTH // LANES, tm, LANES), F32)],
        compiler_params=_cparams(("parallel",)), name="inproj",
    )(xt, norm.reshape(1, d), w_all, kv_norm.reshape(1, DSA_LATENT), w_kv)


def _s5a_kernel(u_ref, mi_ref, ms_ref, yi_ref, p_ref):
    u = u_ref[...].astype(BF16)
    yi_ref[...] = _dot(u, mi_ref[...])
    p_ref[...] = _dot(u, ms_ref[...])


def _s5a(u8, m_intra, m_sum, tr=512):
    r = u8.shape[0]
    row = pl.BlockSpec((tr, S5_COLS), lambda i: (i, 0))
    srow = pl.BlockSpec((tr, S5_STATE), lambda i: (i, 0))
    return pl.pallas_call(
        _s5a_kernel, grid=(r // tr,),
        in_specs=[row, _resident((S5_COLS, S5_COLS)), _resident((S5_COLS, S5_STATE))],
        out_specs=[row, srow],
        out_shape=[jax.ShapeDtypeStruct((r, S5_COLS), F32),
                   jax.ShapeDtypeStruct((r, S5_STATE), F32)],
        compiler_params=_cparams(("parallel",)), name="s5_chunk")(u8, m_intra, m_sum)


def _s5b_kernel(pr_ref, pi_ref, ar_ref, ai_ref, hr_ref, hi_ref, *, nb, nc):
    ar = ar_ref[...]
    ai = ai_ref[...]
    cols = ar.shape[1]

    def step(c, carry):
        hr, hi = carry
        rows = pl.ds(c, nb, stride=nc)
        hr_ref[rows, :] = hr.astype(hr_ref.dtype)
        hi_ref[rows, :] = hi.astype(hi_ref.dtype)
        pr = pr_ref[rows, :]
        pi = pi_ref[rows, :]
        return (ar * hr - ai * hi + pr, ar * hi + ai * hr + pi)

    z = jnp.zeros((nb, cols), F32)
    lax.fori_loop(0, nc, step, (z, z))


def _s5b(p, a_re, a_im, nc, nb, tcol=LANES):
    r = p.shape[0]
    half = S5_STATE // 2
    ncb = half // tcol
    blk = pl.BlockSpec((nb * nc, tcol), lambda i, j: (i, j))
    im_blk = pl.BlockSpec((nb * nc, tcol), lambda i, j: (i, j + ncb))
    a_blk = pl.BlockSpec((1, tcol), lambda i, j: (0, j))
    return pl.pallas_call(
        functools.partial(_s5b_kernel, nb=nb, nc=nc), grid=(r // (nb * nc), ncb),
        in_specs=[blk, im_blk, a_blk, a_blk],
        out_specs=[blk, blk],
        out_shape=[jax.ShapeDtypeStruct((r, half), F32)] * 2,
        compiler_params=_cparams(("parallel", "parallel")), name="s5_scan",
    )(p, p, a_re, a_im)


def _s5c_kernel(yi_ref, hr_ref, hi_ref, mo_ref, y_ref):
    half = S5_STATE // 2
    y = yi_ref[...]
    y = y + _dot(hr_ref[...].astype(BF16), mo_ref[:half, :])
    y = y + _dot(hi_ref[...].astype(BF16), mo_ref[half:, :])
    y_ref[...] = y


def _s5c(yi, h_re, h_im, m_out, tr=512):
    r = yi.shape[0]
    half = S5_STATE // 2
    row = pl.BlockSpec((tr, S5_COLS), lambda i: (i, 0))
    return pl.pallas_call(
        _s5c_kernel, grid=(r // tr,),
        in_specs=[row, pl.BlockSpec((tr, half), lambda i: (i, 0)),
                  pl.BlockSpec((tr, half), lambda i: (i, 0)),
                  _resident((S5_STATE, S5_COLS))],
        out_specs=row, out_shape=jax.ShapeDtypeStruct((r, S5_COLS), F32),
        compiler_params=_cparams(("parallel",)), name="s5_out")(yi, h_re, h_im, m_out)


def _s5_matrices(lam_re, lam_im, log_dt, b_re, b_im, c_re, c_im):
    hp = lax.Precision.HIGHEST
    t = S5_CHUNK
    dt = jnp.exp(log_dt)[:, None]
    lr, li = lam_re, lam_im
    mag = jnp.exp(lr * dt)
    ab_re, ab_im = mag * jnp.cos(li * dt), mag * jnp.sin(li * dt)
    den = lr * lr + li * li
    nr, ni = ab_re - 1.0, ab_im
    s_re = (nr * lr + ni * li) / den
    s_im = (ni * lr - nr * li) / den
    bb_re = s_re[..., None] * b_re - s_im[..., None] * b_im
    bb_im = s_re[..., None] * b_im + s_im[..., None] * b_re
    pr, pi = [jnp.ones_like(ab_re)], [jnp.zeros_like(ab_re)]
    for _ in range(t):
        pr.append(pr[-1] * ab_re - pi[-1] * ab_im)
        pi.append(pr[-2] * ab_im + pi[-1] * ab_re)
    pw_re, pw_im = jnp.stack(pr), jnp.stack(pi)
    ab_b_re = pw_re[..., None] * bb_re - pw_im[..., None] * bb_im
    ab_b_im = pw_re[..., None] * bb_im + pw_im[..., None] * bb_re
    kern = (jnp.einsum('gpn,dgnq->dgpq', c_re, ab_b_re[:t], precision=hp)
            - jnp.einsum('gpn,dgnq->dgpq', c_im, ab_b_im[:t], precision=hp))
    eye_g = jnp.eye(SSM_GROUPS, dtype=F32)
    lag = np.arange(t)[None, :] - np.arange(t)[:, None]
    sel = jnp.asarray((lag[None] == np.arange(t)[:, None, None]).astype(np.float32))
    m_intra = jnp.einsum('dji,dgpq,gh->jgqihp', sel, kern, eye_g, precision=hp)
    m_intra = m_intra.reshape(S5_COLS, S5_COLS)
    sb_re = ab_b_re[:t][::-1]
    sb_im = ab_b_im[:t][::-1]
    m_sum_re = jnp.einsum('jgnq,gh->jgqhn', sb_re, eye_g, precision=hp)
    m_sum_im = jnp.einsum('jgnq,gh->jgqhn', sb_im, eye_g, precision=hp)
    m_sum = jnp.concatenate([m_sum_re.reshape(S5_COLS, -1), m_sum_im.reshape(S5_COLS, -1)], axis=1)
    w_re = c_re[None] * pw_re[1:, :, None, :] - c_im[None] * pw_im[1:, :, None, :]
    w_im = c_re[None] * pw_im[1:, :, None, :] + c_im[None] * pw_re[1:, :, None, :]
    m_out_re = jnp.einsum('igpn,gh->gnihp', w_re, eye_g, precision=hp)
    m_out_im = jnp.einsum('igpn,gh->gnihp', -w_im, eye_g, precision=hp)
    m_out = jnp.concatenate([m_out_re.reshape(-1, S5_COLS), m_out_im.reshape(-1, S5_COLS)], axis=0)
    a_re = pw_re[t].reshape(1, -1)
    a_im = pw_im[t].reshape(1, -1)
    return m_intra.astype(BF16), m_sum.astype(BF16), m_out.astype(BF16), a_re, a_im


def _fold8(x, op):
    parts = [x[r:r + 8] for r in range(0, x.shape[0], 8)]
    while len(parts) > 1:
        parts = [op(parts[i], parts[i + 1]) for i in range(0, len(parts), 2)]
    return parts[0]


def _split_heads_t(q_ref, qt_ref, n_heads):
    tq = q_ref.shape[0]
    top = lax.broadcasted_iota(jnp.int32, (LANES, tq), 0) < HEAD_DIM
    for pr in range(n_heads // 2):
        pair_t = q_ref[:, pr * LANES:(pr + 1) * LANES].astype(F32).T
        qt_ref[:, 2 * pr * tq:(2 * pr + 1) * tq] = jnp.where(top, pair_t, 0.0).astype(BF16)
        qt_ref[:, (2 * pr + 1) * tq:(2 * pr + 2) * tq] = jnp.where(top, 0.0, pair_t).astype(BF16)


def _softmax_update(s, pv, m_ref, acc_ref):
    m_old = m_ref[...]
    m_new = jnp.maximum(m_old, jnp.max(_fold8(s, jnp.maximum), axis=0, keepdims=True))
    alpha = jnp.exp(m_old - m_new)
    p = jnp.exp(s - jnp.broadcast_to(m_new[0:1], s.shape)).astype(BF16)
    acc = acc_ref[...]
    acc_ref[...] = acc * jnp.broadcast_to(alpha[0:1], acc.shape) + pv(p)
    m_ref[...] = m_new


def _finish_heads_t(acc_ref, o_ref, n_heads, odd_swapped):
    hd = HEAD_DIM
    tq = o_ref.shape[0]
    for pr in range(n_heads // 2):
        ae = acc_ref[:, 2 * pr * tq:(2 * pr + 1) * tq]
        ao = acc_ref[:, (2 * pr + 1) * tq:(2 * pr + 2) * tq]
        oe = ae[0:hd] / ae[hd:2 * hd]
        oo = ao[hd:2 * hd] / ao[0:hd] if odd_swapped else ao[0:hd] / ao[hd:2 * hd]
        pair = jnp.concatenate([oe, oo], axis=0)
        o_ref[:, pr * LANES:(pr + 1) * LANES] = pair.T.astype(o_ref.dtype)


def _dsa_kernel(q_ref, qi_ref, aux_ref, kv_ref, ki_ref, band_ref, o_ref,
                score_ref, vt_ref, qm_ref, m_ref, acc_ref, thr_ref, *, n_keys, seq):
    tq = tk = ATT_TILE
    qt = pl.program_id(1)
    n_chunks = qt + 1
    kf = float(n_keys)

    @pl.when(qt == 0)
    def _():
        top = lax.broadcasted_iota(jnp.int32, (LANES, tk), 0) < HEAD_DIM
        for j in range(seq // tk):
            vv = kv_ref[j * tk:(j + 1) * tk, LANES:2 * LANES].astype(F32)
            vt_ref[j] = jnp.where(top, vv.T, 1.0).astype(BF16)

    _split_heads_t(qi_ref, qm_ref, IDX_HEADS)
    aux_t = aux_ref[...].T
    head_w = jnp.concatenate(
        [aux_t[AUX_W + h:AUX_W + h + 1, :] for h in range(IDX_HEADS)], axis=1
    ) * (IDX_HEADS ** -0.5 * IDX_DIM ** -0.5)
    key_minus_query = (lax.broadcasted_iota(jnp.int32, (tk, tq), 0)
                       - lax.broadcasted_iota(jnp.int32, (tk, tq), 1))

    def score_body(j, carry):
        mx, mn = carry
        kc = ki_ref[pl.ds(pl.multiple_of(j * tk, tk), tk), :]
        d = _dot(kc, qm_ref[:, 0:IDX_HEADS * tq])
        d = jnp.broadcast_to(head_w, d.shape) * jnp.maximum(d, 0.0)
        sc = (d[:, 0:tq] + d[:, tq:2 * tq]) + (d[:, 2 * tq:3 * tq] + d[:, 3 * tq:4 * tq])
        causal = key_minus_query <= (qt - j) * tq
        score_ref[j] = jnp.where(causal, sc, NEG)
        mx = jnp.maximum(mx, _fold8(jnp.where(causal, sc, NEG), jnp.maximum))
        mn = jnp.minimum(mn, _fold8(jnp.where(causal, sc, -NEG), jnp.minimum))
        return mx, mn

    mx, mn = lax.fori_loop(0, n_chunks, score_body,
                           (jnp.full((8, tq), NEG, F32), jnp.full((8, tq), -NEG, F32)))

    t_q = qt * tq + lax.broadcasted_iota(jnp.int32, (1, tq), 1)
    thr_ref[...] = jnp.full((8, tq), 0.5 * NEG, F32)

    def count(pred):
        def body(j, acc):
            return acc + _fold8(pred(score_ref[j]), jnp.add)
        acc = lax.fori_loop(0, n_chunks, body, jnp.zeros((8, tq), F32))
        return jnp.sum(acc, axis=0, keepdims=True)

    def count_ge(x):
        xb = jnp.broadcast_to(x, (tk, tq))
        return count(lambda s: jnp.where(s >= xb, 1.0, 0.0))

    @pl.when((qt + 1) * tq > n_keys)
    def _search():
        need = (t_q >= n_keys).astype(F32)
        lo0 = jnp.min(mn, axis=0, keepdims=True)
        hi0 = jnp.max(mx, axis=0, keepdims=True)
        c_top = count_ge(hi0)
        c_ge0 = count_ge(jnp.zeros((1, tq), F32))
        c_gt0 = count(lambda s: jnp.where(s > 0.0, 1.0, 0.0))
        n_valid = (t_q + 1).astype(F32)
        top_hit = c_top == kf
        top_tie = c_top > kf
        zero_hit = c_ge0 == kf
        zero_tie = (c_gt0 < kf) & (c_ge0 > kf)
        positive = c_gt0 >= kf
        hi_neg = hi0 < 0.0
        lo1 = jnp.where(positive, jnp.maximum(lo0, 0.0), lo0)
        c_lo1 = jnp.where(positive, c_ge0, n_valid)
        hi1 = jnp.where(positive | hi_neg, hi0, 0.0)
        c_hi1 = jnp.where(positive | hi_neg, c_top, c_ge0)
        hit0 = top_hit | ((~top_tie) & zero_hit)
        tie0 = top_tie | ((~top_hit) & (~zero_hit) & zero_tie)
        x0 = jnp.where(top_hit, hi0, 0.0)
        tie_val = jnp.where(top_tie, hi0, 0.0)
        above0 = jnp.where(top_tie, 0.0, c_gt0)
        hit0 = need * hit0.astype(F32)
        tie0 = need * tie0.astype(F32) * (1.0 - hit0)
        done0 = jnp.maximum(1.0 - need, jnp.maximum(hit0, tie0))
        lo1 = jnp.where(tie0 > 0, tie_val, lo1)
        c_hi1 = jnp.where(tie0 > 0, above0, c_hi1)
        state0 = (lo1, hi1, c_lo1, c_hi1, done0, tie0, x0, jnp.max(1.0 - done0))

        def cond(st):
            return st[7] > 0.0

        def body(st):
            for _ in range(SEARCH_UNROLL):
                st = step(st)
            return st

        def step(st):
            lo, hi, c_lo, c_hi, done, tie, x_fin, _ = st
            x = 0.5 * lo + 0.5 * hi
            inside = ((x > lo) & (x < hi)).astype(F32)
            active = 1.0 - done
            probe = active * inside
            c = count_ge(jnp.where(probe > 0, x, x_fin))
            hit = probe * (c == kf).astype(F32)
            more = probe * (c > kf).astype(F32)
            less = probe * (c < kf).astype(F32)
            new_tie = active * (1.0 - inside)
            lo = jnp.where(more > 0, x, lo)
            c_lo = jnp.where(more > 0, c, c_lo)
            hi = jnp.where(less > 0, x, hi)
            c_hi = jnp.where(less > 0, c, c_hi)
            x_fin = jnp.where(hit > 0, x, x_fin)
            tie = jnp.maximum(tie, new_tie)
            done = jnp.maximum(done, jnp.maximum(hit, new_tie))
            return (lo, hi, c_lo, c_hi, done, tie, x_fin, jnp.max(1.0 - done))

        st = lax.while_loop(cond, body, state0)
        lo, _, _, c_hi, _, tie, x_fin, _ = st
        thr = jnp.where(need > 0, jnp.where(tie > 0, lo, x_fin), 0.5 * NEG)
        thr_ref[...] = jnp.broadcast_to(thr, (8, tq))

        @pl.when(jnp.max(tie) > 0.0)
        def _ties():
            want = jnp.broadcast_to(kf - c_hi, (tk, tq))
            thr_t = jnp.broadcast_to(jnp.where(tie > 0, thr, -NEG), (tk, tq))
            upto = (lax.broadcasted_iota(jnp.int32, (tk, tk), 1)
                    <= lax.broadcasted_iota(jnp.int32, (tk, tk), 0))
            prefix = jnp.where(upto, 1.0, 0.0).astype(BF16)

            def body(j, seen):
                s = score_ref[j]
                eq = s == thr_t
                rank = _dot(prefix, jnp.where(eq, 1.0, 0.0).astype(BF16))
                rank = rank + jnp.broadcast_to(seen, (tk, tq))
                score_ref[j] = jnp.where(eq, jnp.where(rank > want, NEG, s), s)
                return rank[tk - 1:tk, :]

            lax.fori_loop(0, n_chunks, body, jnp.zeros((1, tq), F32))

    _split_heads_t(q_ref, qm_ref, DSA_HEADS)
    m_ref[...] = jnp.full(m_ref.shape, NEG, F32)
    acc_ref[...] = jnp.zeros(acc_ref.shape, F32)
    thr_b = jnp.broadcast_to(thr_ref[0:1, :], (tk, tq))

    def attend(j, band_lo):
        rows = pl.ds(pl.multiple_of(j * tk, tk), tk)
        off = jnp.where(score_ref[j] >= thr_b, 0.0, NEG)
        s = _dot(kv_ref[rows, 0:LANES], qm_ref[...])
        s = s + jnp.concatenate([off] * DSA_HEADS, axis=1)
        if band_lo is not None:
            s = s + band_ref[band_lo:band_lo + tk, :]
        v_t = vt_ref[j]
        _softmax_update(s, lambda p: _dot(v_t, p), m_ref, acc_ref)

    def far_body(j, carry):
        attend(j, None)
        return carry

    lax.fori_loop(0, jnp.maximum(qt - 1, 0), far_body, 0)

    @pl.when(qt >= 1)
    def _():
        attend(qt - 1, 0)

    attend(qt, tk)
    _finish_heads_t(acc_ref, o_ref, DSA_HEADS, odd_swapped=False)


def _dsa(qb, qi, aux, kv, ki, band, bsz, seq, n_keys):
    tq = ATT_TILE
    nq = seq // tq
    qrow = lambda w: pl.BlockSpec((tq, w), lambda b, q: (b * nq + q, 0))
    seq_blk = lambda w: pl.BlockSpec((seq, w), lambda b, q: (b, 0))
    return pl.pallas_call(
        functools.partial(_dsa_kernel, n_keys=n_keys, seq=seq), grid=(bsz, nq),
        in_specs=[qrow(DSA_WIDTH), qrow(256), qrow(LANES), seq_blk(256), seq_blk(LANES),
                  _resident((2 * tq, DSA_HEADS * tq))],
        out_specs=qrow(DSA_WIDTH),
        out_shape=jax.ShapeDtypeStruct((bsz * seq, DSA_WIDTH), BF16),
        scratch_shapes=[pltpu.VMEM((nq, tq, tq), F32),
                        pltpu.VMEM((nq, LANES, tq), BF16),
                        pltpu.VMEM((LANES, DSA_HEADS * tq), BF16),
                        pltpu.VMEM((8, DSA_HEADS * tq), F32),
                        pltpu.VMEM((LANES, DSA_HEADS * tq), F32),
                        pltpu.VMEM((8, tq), F32)],
        compiler_params=_cparams(("parallel", "arbitrary")), name="dsa",
    )(qb, qi, aux, kv, ki, band)


def _band_kernel(bucket_ref, rb_ref, o_ref):
    bucket = bucket_ref[...]
    tq = bucket.shape[1]
    for h in range(DSA_HEADS):
        acc = jnp.zeros(bucket.shape, F32)
        for k in range(REL_BUCKETS - 1):
            acc = jnp.where(bucket == k, rb_ref[k, h] - rb_ref[REL_BUCKETS - 1, h], acc)
        o_ref[:, h * tq:(h + 1) * tq] = acc


def _t5_bucket_np(dist):
    d = np.maximum(dist, 0)
    df = np.maximum(d, 1).astype(np.float32)
    log_b = REL_MAX_EXACT + (np.log(df / REL_MAX_EXACT) / math.log(REL_MAX_DIST / REL_MAX_EXACT)
                             * (REL_BUCKETS - REL_MAX_EXACT)).astype(np.int32)
    log_b = np.minimum(log_b, REL_BUCKETS - 1)
    return np.where(d < REL_MAX_EXACT, d, log_b).astype(np.int32)


def _rel_band(rel_bias):
    tq = ATT_TILE
    i = np.arange(tq)[None, :]
    j = np.arange(2 * tq)[:, None]
    bucket = jnp.asarray(_t5_bucket_np(i + tq - j))
    return pl.pallas_call(
        _band_kernel,
        in_specs=[pl.BlockSpec(memory_space=pltpu.VMEM), pl.BlockSpec(memory_space=pltpu.SMEM)],
        out_specs=pl.BlockSpec(memory_space=pltpu.VMEM),
        out_shape=jax.ShapeDtypeStruct((2 * tq, DSA_HEADS * tq), F32), name="rel_band",
    )(bucket, rel_bias)


def _cum_kernel(aux_ref, bf_ref, col_ref, row_ref, *, seq):
    x = aux_ref[...] + bf_ref[...]
    logf = -(jnp.maximum(-x, 0.0) + jnp.log1p(jnp.exp(-jnp.abs(x))))
    c = logf.T
    pos = lax.broadcasted_iota(jnp.int32, c.shape, 1)
    shift = 1
    while shift < seq:
        c = c + jnp.where(pos >= shift, pltpu.roll(c, shift, 1), 0.0)
        shift *= 2
    col_ref[...] = c.T
    for j in range(seq // ATT_TILE):
        row_ref[j] = c[0:8, j * ATT_TILE:(j + 1) * ATT_TILE]


def _fox_cum(aux, b_f, bsz, seq):
    nq = seq // ATT_TILE
    bf = jnp.zeros((1, LANES), F32).at[0, AUX_F:AUX_F + FOX_HEADS].set(b_f)
    return pl.pallas_call(
        functools.partial(_cum_kernel, seq=seq), grid=(bsz,),
        in_specs=[pl.BlockSpec((seq, LANES), lambda b: (b, 0)), _resident((1, LANES))],
        out_specs=[pl.BlockSpec((seq, LANES), lambda b: (b, 0)),
                   pl.BlockSpec((None, nq, 8, ATT_TILE), lambda b: (b, 0, 0, 0))],
        out_shape=[jax.ShapeDtypeStruct((bsz * seq, LANES), F32),
                   jax.ShapeDtypeStruct((bsz, nq, 8, ATT_TILE), F32)],
        compiler_params=_cparams(("parallel",)), name="fox_cum")(aux, bf)


def _fox_kernel(q_ref, k_ref, v_ref, ccol_ref, crow_ref, o_ref,
                vte_ref, vto_ref, ck_ref, qm_ref, m_ref, acc_ref, *, seq):
    tq = tk = ATT_TILE
    qt = pl.program_id(1)
    heads = range(FOX_HEADS)

    @pl.when(qt == 0)
    def _():
        top = lax.broadcasted_iota(jnp.int32, (LANES, tk), 0) < HEAD_DIM
        for j in range(seq // tk):
            rows = slice(j * tk, (j + 1) * tk)
            for pr in range(FOX_HEADS // 2):
                v_t = v_ref[rows, pr * LANES:(pr + 1) * LANES].astype(F32).T
                vte_ref[pr, j] = jnp.where(top, v_t, 1.0).astype(BF16)
                vto_ref[pr, j] = jnp.where(top, 1.0, v_t).astype(BF16)
            cc = ccol_ref[rows, :]
            for h in heads:
                ck_ref[h, rows, :] = jnp.broadcast_to(cc[:, AUX_F + h:AUX_F + h + 1], (tk, LANES))

    _split_heads_t(q_ref, qm_ref, FOX_HEADS)
    m_ref[...] = jnp.full(m_ref.shape, NEG, F32)
    acc_ref[...] = jnp.zeros(acc_ref.shape, F32)
    c_q = crow_ref[qt]
    c_q = jnp.concatenate([c_q[AUX_F + h:AUX_F + h + 1, :] for h in heads], axis=1)
    causal_off = jnp.where(lax.broadcasted_iota(jnp.int32, (tk, tq), 0)
                           <= lax.broadcasted_iota(jnp.int32, (tk, tq), 1), 0.0, NEG)

    def attend(j, diagonal):
        rows = pl.ds(pl.multiple_of(j * tk, tk), tk)
        s = jnp.concatenate(
            [_dot(k_ref[rows, (h // 2) * LANES:(h // 2 + 1) * LANES], qm_ref[:, h * tq:(h + 1) * tq])
             for h in heads], axis=1)
        c_k = jnp.concatenate([ck_ref[h, rows, :] for h in heads for _ in range(tq // LANES)], axis=1)
        s = s + (jnp.broadcast_to(c_q, s.shape) - c_k)
        if diagonal:
            s = s + jnp.concatenate([causal_off] * FOX_HEADS, axis=1)

        def pv(p):
            return jnp.concatenate(
                [_dot(vte_ref[h // 2, j] if h % 2 == 0 else vto_ref[h // 2, j],
                      p[:, h * tq:(h + 1) * tq]) for h in heads], axis=1)

        _softmax_update(s, pv, m_ref, acc_ref)

    def far_body(j, carry):
        attend(j, False)
        return carry

    lax.fori_loop(0, qt, far_body, 0)
    attend(qt, True)
    _finish_heads_t(acc_ref, o_ref, FOX_HEADS, odd_swapped=True)


def _fox(qkv, c_col, c_row, bsz, seq):
    tq = ATT_TILE
    nq = seq // tq
    npair = FOX_HEADS // 2
    return pl.pallas_call(
        functools.partial(_fox_kernel, seq=seq), grid=(bsz, nq),
        in_specs=[pl.BlockSpec((tq, FOX_WIDTH), lambda b, q: (b * nq + q, 0)),
                  pl.BlockSpec((seq, FOX_WIDTH), lambda b, q: (b, 1)),
                  pl.BlockSpec((seq, FOX_WIDTH), lambda b, q: (b, 2)),
                  pl.BlockSpec((seq, LANES), lambda b, q: (b, 0)),
                  pl.BlockSpec((None, nq, 8, tq), lambda b, q: (b, 0, 0, 0))],
        out_specs=pl.BlockSpec((tq, FOX_WIDTH), lambda b, q: (b * nq + q, 0)),
        out_shape=jax.ShapeDtypeStruct((bsz * seq, FOX_WIDTH), BF16),
        scratch_shapes=[pltpu.VMEM((npair, nq, LANES, tq), BF16),
                        pltpu.VMEM((npair, nq, LANES, tq), BF16),
                        pltpu.VMEM((FOX_HEADS, seq, LANES), F32),
                        pltpu.VMEM((LANES, FOX_HEADS * tq), BF16),
                        pltpu.VMEM((8, FOX_HEADS * tq), F32),
                        pltpu.VMEM((LANES, FOX_HEADS * tq), F32)],
        compiler_params=_cparams(("parallel", "arbitrary")), name="fox",
    )(qkv, qkv, qkv, c_col, c_row)


def _gelu(y):
    return 0.5 * y * (1.0 + jnp.tanh(math.sqrt(2.0 / math.pi) * (y + 0.044715 * (y * y * y))))


def _merge_kernel(x_ref, yc_ref, u_ref, d_ref, wglu_ref, b_ref, c_ref, gate_ref,
                  wa_ref, wb_ref, wc_ref, wo_ref, o_ref, tmp_ref):
    d = D_MODEL
    y = _chunks_to_rows(yc_ref, tmp_ref, SSM_WIDTH) + d_ref[...] * u_ref[...]
    z = _dot(_gelu(y).astype(BF16), wglu_ref[...])
    a = z[:, :SSM_WIDTH] * _sigmoid(z[:, SSM_WIDTH:])
    merged = _sigmoid(gate_ref[:, 0:d].astype(F32)) * _dot(a.astype(BF16), wa_ref[...])
    merged += _sigmoid(gate_ref[:, d:2 * d].astype(F32)) * _dot(b_ref[...], wb_ref[...])
    merged += _sigmoid(gate_ref[:, 2 * d:3 * d].astype(F32)) * _dot(c_ref[...], wc_ref[...])
    o_ref[...] = x_ref[...] + _dot(merged.astype(BF16), wo_ref[...])


def _merge(xt, yc, u, d_skip, w_glu, b, c, gates, wa, wb, wc, wo, tm=1024):
    m, d = xt.shape
    row = lambda w: pl.BlockSpec((tm, w), lambda i: (i, 0))
    return pl.pallas_call(
        _merge_kernel, grid=(m // tm,),
        in_specs=[row(d), pl.BlockSpec((tm // S5_CHUNK, S5_COLS), lambda i: (i, 0)),
                  row(SSM_WIDTH), _resident((1, SSM_WIDTH)),
                  _resident((SSM_WIDTH, 2 * SSM_WIDTH)), row(DSA_WIDTH), row(FOX_WIDTH), row(3 * d),
                  _resident((SSM_WIDTH, d)), _resident((DSA_WIDTH, d)), _resident((FOX_WIDTH, d)),
                  _resident((d, d))],
        out_specs=row(d), out_shape=jax.ShapeDtypeStruct((m, d), F32),
        scratch_shapes=[pltpu.VMEM((SSM_WIDTH // LANES, tm, LANES), F32)],
        compiler_params=_cparams(("parallel",)), name="merge",
    )(xt, yc, u, d_skip.reshape(1, -1), w_glu, b, c, gates, wa, wb, wc, wo)


def _pack_w_in(w_in):
    d = w_in.shape[0]
    splits = (SSM_WIDTH, DSA_WIDTH, DSA_LATENT, IDX_HEADS * IDX_DIM, IDX_DIM, IDX_HEADS,
              FOX_WIDTH, FOX_WIDTH, FOX_WIDTH, FOX_HEADS, D_MODEL, D_MODEL, D_MODEL)
    pts = np.cumsum(splits)[:-1]
    (w_u, w_qb, w_ckv, w_qi, w_ki, w_wi, w_qc, w_kc, w_vc, w_fc, w_ga, w_gb, w_gc) = jnp.split(
        w_in, pts, axis=1)
    scale = HEAD_DIM ** -0.5
    aux = jnp.zeros((d, LANES), F32)
    aux = aux.at[:, AUX_F:AUX_F + FOX_HEADS].set(w_fc).at[:, AUX_W:AUX_W + IDX_HEADS].set(w_wi)
    cols = [w_u, w_qb * scale, w_qi, w_ki, w_ki, w_ckv, aux, w_qc * scale, w_kc, w_vc,
            w_ga, w_gb, w_gc]
    return jnp.concatenate(cols, axis=1).astype(BF16)


def kernel(x, ffn1_norm, ffn1_w_gate, ffn1_w_up, ffn1_w_down, mix_norm, w_in, ssm_lambda_re, ssm_lambda_im, ssm_log_dt, ssm_b_re, ssm_b_im, ssm_c_re, ssm_c_im, ssm_d, ssm_w_glu, dsa_kv_norm, dsa_w_uk, dsa_w_uv, rel_bias, fox_b_f, w_branch_ssm, w_branch_dsa, w_branch_fox, w_out, ffn2_norm, ffn2_w_gate, ffn2_w_up, ffn2_w_down, final_norm):
    bsz, seq, d = x.shape
    depth = w_in.shape[0]
    n_keys = min(TOPK_MAX, seq // 4)
    assert d == D_MODEL and seq % ATT_TILE == 0 and (bsz * seq) % (8 * 512) == 0
    assert n_keys % LANES == 0 and bsz % 8 == 0
    m = bsz * seq
    n_chunks = seq // S5_CHUNK
    xt = x.reshape(m, d)
    band = _rel_band(rel_bias)
    bf = lambda w: w.astype(BF16)
    for l in range(depth):
        xt = _ffn(xt, ffn1_norm[l], bf(ffn1_w_gate[l]), bf(ffn1_w_up[l]), bf(ffn1_w_down[l]))
        w_kv = bf(jnp.concatenate([dsa_w_uk[l], dsa_w_uk[l], dsa_w_uv[l], dsa_w_uv[l]], axis=1))
        u, qb, qi, ki, kv, aux, qkv, gates, u8 = _inproj(
            xt, mix_norm[l], _pack_w_in(w_in[l]), dsa_kv_norm[l], w_kv)
        m_intra, m_sum, m_out, a_re, a_im = _s5_matrices(
            ssm_lambda_re[l], ssm_lambda_im[l], ssm_log_dt[l], ssm_b_re[l], ssm_b_im[l],
            ssm_c_re[l], ssm_c_im[l])
        y_intra, p = _s5a(u8, m_intra, m_sum)
        h_re, h_im = _s5b(p, a_re, a_im, n_chunks, nb=8)
        yc = _s5c(y_intra, h_re, h_im, m_out)
        b_out = _dsa(qb, qi, aux, kv, ki, band, bsz, seq, n_keys)
        c_col, c_row = _fox_cum(aux, fox_b_f[l], bsz, seq)
        c_out = _fox(qkv, c_col, c_row, bsz, seq)
        xt = _merge(xt, yc, u, ssm_d[l], bf(ssm_w_glu[l]), b_out, c_out, gates,
                    bf(w_branch_ssm[l]), bf(w_branch_dsa[l]), bf(w_branch_fox[l]), bf(w_out[l]))
        last = final_norm if l == depth - 1 else None
        xt = _ffn(xt, ffn2_norm[l], bf(ffn2_w_gate[l]), bf(ffn2_w_up[l]), bf(ffn2_w_down[l]),
                  final_norm=last)
    return xt.reshape(bsz, seq, d)
```

```python
import functools
import math

import numpy as np
import jax
import jax.numpy as jnp
from jax import lax
from jax.experimental import pallas as pl
from jax.experimental.pallas import tpu as pltpu

F32 = jnp.float32
BF16 = jnp.bfloat16

D_MODEL = 1024
SSM_WIDTH = D_MODEL // 4
SSM_GROUP = 16
SSM_GROUPS = SSM_WIDTH // SSM_GROUP
SSM_STATE = 64
HEAD_DIM = 64
DSA_HEADS = 6
DSA_WIDTH = DSA_HEADS * HEAD_DIM
DSA_LATENT = 2 * HEAD_DIM
IDX_HEADS = 4
IDX_DIM = 64
TOPK_MAX = 256
FOX_HEADS = 6
FOX_WIDTH = FOX_HEADS * HEAD_DIM
REL_BUCKETS = 32
REL_MAX_EXACT = 16
REL_MAX_DIST = 128
FFN_HIDDEN = 2816
EPS = 1e-6
NEG = -1e30

LANES = 128
S5_CHUNK = 8
S5_COLS = S5_CHUNK * SSM_WIDTH
S5_STATE = 2 * SSM_GROUPS * SSM_STATE
ATT_TILE = 256
SEARCH_UNROLL = 4
VMEM_LIMIT = 56 * 1024 * 1024

_SEG = dict(u=256, qb=384, qi=256, ki=128, ckv=128, aux=128, qkv=1152, g=3072)
AUX_F = 0
AUX_W = 8


def _cparams(sem):
    return pltpu.CompilerParams(dimension_semantics=sem, vmem_limit_bytes=VMEM_LIMIT)


def _resident(shape):
    nd = len(shape)
    return pl.BlockSpec(shape, lambda *_: (0,) * nd, pipeline_mode=pl.Buffered(1))


def _rms(x, g):
    return x * lax.rsqrt(jnp.mean(x * x, axis=-1, keepdims=True) + EPS) * g


def _sigmoid(x):
    return 1.0 / (1.0 + jnp.exp(-x))


def _dot(a, b):
    return jnp.dot(a, b, preferred_element_type=F32)


def _ffn_body(x, g_ref, wg_ref, wu_ref, wd_ref, fc):
    h = _rms(x, g_ref[...]).astype(BF16)
    acc = None
    for c in range(FFN_HIDDEN // fc):
        sl = slice(c * fc, (c + 1) * fc)
        gt = _dot(h, wg_ref[:, sl])
        up = _dot(h, wu_ref[:, sl])
        a = (gt * _sigmoid(gt) * up).astype(BF16)
        d = _dot(a, wd_ref[sl, :])
        acc = d if acc is None else acc + d
    return x + 0.5 * acc


def _ffn_kernel(x_ref, g_ref, wg_ref, wu_ref, wd_ref, o_ref, *, fc):
    o_ref[...] = _ffn_body(x_ref[...], g_ref, wg_ref, wu_ref, wd_ref, fc)


def _ffn_final_kernel(x_ref, g_ref, wg_ref, wu_ref, wd_ref, fn_ref, o_ref, *, fc):
    y = _ffn_body(x_ref[...], g_ref, wg_ref, wu_ref, wd_ref, fc)
    o_ref[...] = _rms(y, fn_ref[...])


def _ffn(xt, norm, wg, wu, wd, final_norm=None, tm=1024, fc=256):
    m, d = xt.shape
    f = wg.shape[1]
    row = pl.BlockSpec((tm, d), lambda i: (i, 0))
    vec = _resident((1, d))
    in_specs = [row, vec, _resident((d, f)), _resident((d, f)), _resident((f, d))]
    args = [xt, norm.reshape(1, d), wg, wu, wd]
    if final_norm is None:
        body = functools.partial(_ffn_kernel, fc=fc)
    else:
        body = functools.partial(_ffn_final_kernel, fc=fc)
        in_specs.append(vec)
        args.append(final_norm.reshape(1, d))
    return pl.pallas_call(
        body, grid=(m // tm,), in_specs=in_specs, out_specs=row,
        out_shape=jax.ShapeDtypeStruct((m, d), F32),
        compiler_params=_cparams(("parallel",)), name="ffn")(*args)


def _rows_to_chunks(x, tmp_ref, o_ref):
    n, w = x.shape
    for hh in range(w // LANES):
        tmp_ref[hh] = x[:, hh * LANES:(hh + 1) * LANES]
    for i in range(S5_CHUNK):
        for hh in range(w // LANES):
            o_ref[:, i * w + hh * LANES:i * w + (hh + 1) * LANES] = (
                tmp_ref[hh, pl.ds(i, n // S5_CHUNK, stride=S5_CHUNK), :])


def _chunks_to_rows(x_ref, tmp_ref, w):
    n = x_ref.shape[0]
    for i in range(S5_CHUNK):
        for hh in range(w // LANES):
            tmp_ref[hh, pl.ds(i, n, stride=S5_CHUNK), :] = (
                x_ref[:, i * w + hh * LANES:i * w + (hh + 1) * LANES])
    return jnp.concatenate([tmp_ref[hh] for hh in range(w // LANES)], axis=1)


def _inproj_kernel(x_ref, g_ref, w_ref, kvn_ref, wkv_ref,
                   u_ref, qb_ref, qi_ref, ki_ref, kv_ref, aux_ref, qkv_ref, gate_ref, u8_ref,
                   tmp_ref):
    h = _rms(x_ref[...], g_ref[...]).astype(BF16)
    outs = dict(u=u_ref, qb=qb_ref, qi=qi_ref, ki=ki_ref, aux=aux_ref, qkv=qkv_ref)
    off = 0
    for group in (("u", "qb", "qi", "ki"), ("ckv", "aux", "qkv")):
        width = sum(_SEG[name] for name in group)
        r = _dot(h, w_ref[:, off:off + width])
        off += width
        lo = 0
        for name in group:
            piece = r[:, lo:lo + _SEG[name]]
            lo += _SEG[name]
            if name == "ckv":
                c = _rms(piece, kvn_ref[...]).astype(BF16)
                kv_ref[...] = _dot(c, wkv_ref[...]).astype(BF16)
            else:
                outs[name][...] = piece.astype(outs[name].dtype)
            if name == "u":
                _rows_to_chunks(piece, tmp_ref, u8_ref)
    for c in range(_SEG["g"] // D_MODEL):
        cols = slice(c * D_MODEL, (c + 1) * D_MODEL)
        gate_ref[:, cols] = _dot(h, w_ref[:, off + c * D_MODEL:off + (c + 1) * D_MODEL]).astype(BF16)


def _inproj(xt, norm, w_all, kv_norm, w_kv, tm=1024):
    m, d = xt.shape
    wtot = w_all.shape[1]
    widths = dict(u=(256, F32), qb=(384, BF16), qi=(256, BF16), ki=(128, BF16),
                  kv=(256, BF16), aux=(128, F32), qkv=(1152, BF16), g=(3072, BF16))
    out_shape = [jax.ShapeDtypeStruct((m, w), dt) for w, dt in widths.values()]
    out_specs = [pl.BlockSpec((tm, w), lambda i: (i, 0)) for w, _ in widths.values()]
    out_shape.append(jax.ShapeDtypeStruct((m // S5_CHUNK, S5_COLS), F32))
    out_specs.append(pl.BlockSpec((tm // S5_CHUNK, S5_COLS), lambda i: (i, 0)))
    return pl.pallas_call(
        _inproj_kernel, grid=(m // tm,),
        in_specs=[pl.BlockSpec((tm, d), lambda i: (i, 0)), _resident((1, d)),
                  _resident((d, wtot)), _resident((1, DSA_LATENT)),
                  _resident((DSA_LATENT, 256))],
        out_specs=out_specs, out_shape=out_shape,
        scratch_shapes=[pltpu.VMEM((SSM_WIDTH // LANES, tm, LANES), F32)],
        compiler_params=_cparams(("parallel",)), name="inproj",
    )(xt, norm.reshape(1, d), w_all, kv_norm.reshape(1, DSA_LATENT), w_kv)


def _s5a_kernel(u_ref, mi_ref, ms_ref, yi_ref, p_ref):
    u = u_ref[...].astype(BF16)
    yi_ref[...] = _dot(u, mi_ref[...])
    p_ref[...] = _dot(u, ms_ref[...])


def _s5a(u8, m_intra, m_sum, tr=512):
    r = u8.shape[0]
    row = pl.BlockSpec((tr, S5_COLS), lambda i: (i, 0))
    srow = pl.BlockSpec((tr, S5_STATE), lambda i: (i, 0))
    return pl.pallas_call(
        _s5a_kernel, grid=(r // tr,),
        in_specs=[row, _resident((S5_COLS, S5_COLS)), _resident((S5_COLS, S5_STATE))],
        out_specs=[row, srow],
        out_shape=[jax.ShapeDtypeStruct((r, S5_COLS), F32),
                   jax.ShapeDtypeStruct((r, S5_STATE), F32)],
        compiler_params=_cparams(("parallel",)), name="s5_chunk")(u8, m_intra, m_sum)


def _s5b_kernel(pr_ref, pi_ref, ar_ref, ai_ref, hr_ref, hi_ref, *, nb, nc):
    ar = ar_ref[...]
    ai = ai_ref[...]
    cols = ar.shape[1]

    def step(c, carry):
        hr, hi = carry
        rows = pl.ds(c, nb, stride=nc)
        hr_ref[rows, :] = hr.astype(hr_ref.dtype)
        hi_ref[rows, :] = hi.astype(hi_ref.dtype)
        pr = pr_ref[rows, :]
        pi = pi_ref[rows, :]
        return (ar * hr - ai * hi + pr, ar * hi + ai * hr + pi)

    z = jnp.zeros((nb, cols), F32)
    lax.fori_loop(0, nc, step, (z, z))


def _s5b(p, a_re, a_im, nc, nb, tcol=LANES):
    r = p.shape[0]
    half = S5_STATE // 2
    ncb = half // tcol
    blk = pl.BlockSpec((nb * nc, tcol), lambda i, j: (i, j))
    im_blk = pl.BlockSpec((nb * nc, tcol), lambda i, j: (i, j + ncb))
    a_blk = pl.BlockSpec((1, tcol), lambda i, j: (0, j))
    return pl.pallas_call(
        functools.partial(_s5b_kernel, nb=nb, nc=nc), grid=(r // (nb * nc), ncb),
        in_specs=[blk, im_blk, a_blk, a_blk],
        out_specs=[blk, blk],
        out_shape=[jax.ShapeDtypeStruct((r, half), F32)] * 2,
        compiler_params=_cparams(("parallel", "parallel")), name="s5_scan",
    )(p, p, a_re, a_im)


def _s5c_kernel(yi_ref, hr_ref, hi_ref, mo_ref, y_ref):
    half = S5_STATE // 2
    y = yi_ref[...]
    y = y + _dot(hr_ref[...].astype(BF16), mo_ref[:half, :])
    y = y + _dot(hi_ref[...].astype(BF16), mo_ref[half:, :])
    y_ref[...] = y


def _s5c(yi, h_re, h_im, m_out, tr=512):
    r = yi.shape[0]
    half = S5_STATE // 2
    row = pl.BlockSpec((tr, S5_COLS), lambda i: (i, 0))
    return pl.pallas_call(
        _s5c_kernel, grid=(r // tr,),
        in_specs=[row, pl.BlockSpec((tr, half), lambda i: (i, 0)),
                  pl.BlockSpec((tr, half), lambda i: (i, 0)),
                  _resident((S5_STATE, S5_COLS))],
        out_specs=row, out_shape=jax.ShapeDtypeStruct((r, S5_COLS), F32),
        compiler_params=_cparams(("parallel",)), name="s5_out")(yi, h_re, h_im, m_out)


def _s5_matrices(lam_re, lam_im, log_dt, b_re, b_im, c_re, c_im):
    hp = lax.Precision.HIGHEST
    t = S5_CHUNK
    dt = jnp.exp(log_dt)[:, None]
    lr, li = lam_re, lam_im
    mag = jnp.exp(lr * dt)
    ab_re, ab_im = mag * jnp.cos(li * dt), mag * jnp.sin(li * dt)
    den = lr * lr + li * li
    nr, ni = ab_re - 1.0, ab_im
    s_re = (nr * lr + ni * li) / den
    s_im = (ni * lr - nr * li) / den
    bb_re = s_re[..., None] * b_re - s_im[..., None] * b_im
    bb_im = s_re[..., None] * b_im + s_im[..., None] * b_re
    pr, pi = [jnp.ones_like(ab_re)], [jnp.zeros_like(ab_re)]
    for _ in range(t):
        pr.append(pr[-1] * ab_re - pi[-1] * ab_im)
        pi.append(pr[-2] * ab_im + pi[-1] * ab_re)
    pw_re, pw_im = jnp.stack(pr), jnp.stack(pi)
    ab_b_re = pw_re[..., None] * bb_re - pw_im[..., None] * bb_im
    ab_b_im = pw_re[..., None] * bb_im + pw_im[..., None] * bb_re
    kern = (jnp.einsum('gpn,dgnq->dgpq', c_re, ab_b_re[:t], precision=hp)
            - jnp.einsum('gpn,dgnq->dgpq', c_im, ab_b_im[:t], precision=hp))
    eye_g = jnp.eye(SSM_GROUPS, dtype=F32)
    lag = np.arange(t)[None, :] - np.arange(t)[:, None]
    sel = jnp.asarray((lag[None] == np.arange(t)[:, None, None]).astype(np.float32))
    m_intra = jnp.einsum('dji,dgpq,gh->jgqihp', sel, kern, eye_g, precision=hp)
    m_intra = m_intra.reshape(S5_COLS, S5_COLS)
    sb_re = ab_b_re[:t][::-1]
    sb_im = ab_b_im[:t][::-1]
    m_sum_re = jnp.einsum('jgnq,gh->jgqhn', sb_re, eye_g, precision=hp)
    m_sum_im = jnp.einsum('jgnq,gh->jgqhn', sb_im, eye_g, precision=hp)
    m_sum = jnp.concatenate([m_sum_re.reshape(S5_COLS, -1), m_sum_im.reshape(S5_COLS, -1)], axis=1)
    w_re = c_re[None] * pw_re[1:, :, None, :] - c_im[None] * pw_im[1:, :, None, :]
    w_im = c_re[None] * pw_im[1:, :, None, :] + c_im[None] * pw_re[1:, :, None, :]
    m_out_re = jnp.einsum('igpn,gh->gnihp', w_re, eye_g, precision=hp)
    m_out_im = jnp.einsum('igpn,gh->gnihp', -w_im, eye_g, precision=hp)
    m_out = jnp.concatenate([m_out_re.reshape(-1, S5_COLS), m_out_im.reshape(-1, S5_COLS)], axis=0)
    a_re = pw_re[t].reshape(1, -1)
    a_im = pw_im[t].reshape(1, -1)
    return m_intra.astype(BF16), m_sum.astype(BF16), m_out.astype(BF16), a_re, a_im


def _fold8(x, op):
    parts = [x[r:r + 8] for r in range(0, x.shape[0], 8)]
    while len(parts) > 1:
        parts = [op(parts[i], parts[i + 1]) for i in range(0, len(parts), 2)]
    return parts[0]


def _split_heads_t(q_ref, qt_ref, n_heads):
    tq = q_ref.shape[0]
    top = lax.broadcasted_iota(jnp.int32, (LANES, tq), 0) < HEAD_DIM
    for pr in range(n_heads // 2):
        pair_t = q_ref[:, pr * LANES:(pr + 1) * LANES].astype(F32).T
        qt_ref[:, 2 * pr * tq:(2 * pr + 1) * tq] = jnp.where(top, pair_t, 0.0).astype(BF16)
        qt_ref[:, (2 * pr + 1) * tq:(2 * pr + 2) * tq] = jnp.where(top, 0.0, pair_t).astype(BF16)


def _softmax_stage(s, m_ref, alpha_ref, p_ref):
    m_old = m_ref[...]
    m_new = jnp.maximum(m_old, jnp.max(_fold8(s, jnp.maximum), axis=0, keepdims=True))
    alpha_ref[...] = jnp.exp(m_old - m_new)
    p_ref[...] = jnp.exp(s - jnp.broadcast_to(m_new[0:1], s.shape)).astype(BF16)
    m_ref[...] = m_new


def _value_stage(pv, alpha_ref, p_ref, acc_ref):
    acc = acc_ref[...]
    acc_ref[...] = acc * jnp.broadcast_to(alpha_ref[0:1, :], acc.shape) + pv(p_ref[...])


def _attend_chunks(qt, qk, logits, pv, bufs, m_ref, acc_ref):
    s_a, s_b, p_a, p_b, al_a, al_b = bufs

    def plain(j, kind):
        qk(j, s_a)
        _softmax_stage(logits(j, s_a[...], kind), m_ref, al_a, p_a)
        _value_stage(lambda p: pv(j, p), al_a, p_a, acc_ref)

    @pl.when(qt == 0)
    def _():
        plain(0, 'diag')

    @pl.when(qt >= 1)
    def _():
        n_far = qt - 1
        first = n_far % 2

        @pl.when(first == 1)
        def _():
            plain(0, 'far')

        qk(first, s_a)
        p_b[...] = jnp.zeros(p_b.shape, p_b.dtype)
        al_b[...] = jnp.ones(al_b.shape, al_b.dtype)

        def pair(c, kind0, kind1, last):
            qk(c + 1, s_b)
            _value_stage(lambda p: pv(jnp.maximum(c - 1, 0), p), al_b, p_b, acc_ref)
            _softmax_stage(logits(c, s_a[...], kind0), m_ref, al_a, p_a)
            if not last:
                qk(c + 2, s_a)
            _value_stage(lambda p: pv(c, p), al_a, p_a, acc_ref)
            _softmax_stage(logits(c + 1, s_b[...], kind1), m_ref, al_b, p_b)

        def body(i, carry):
            pair(first + 2 * i, 'far', 'far', False)
            return carry

        lax.fori_loop(0, (n_far - first) // 2, body, 0)
        pair(qt - 1, 'prev', 'diag', True)
        _value_stage(lambda p: pv(qt, p), al_b, p_b, acc_ref)


def _pipeline_buffers(width):
    return ([pltpu.VMEM((ATT_TILE, width), F32)] * 2 + [pltpu.VMEM((ATT_TILE, width), BF16)] * 2
            + [pltpu.VMEM((8, width), F32)] * 2)


def _finish_heads_t(acc_ref, o_ref, n_heads, odd_swapped):
    hd = HEAD_DIM
    tq = o_ref.shape[0]
    for pr in range(n_heads // 2):
        ae = acc_ref[:, 2 * pr * tq:(2 * pr + 1) * tq]
        ao = acc_ref[:, (2 * pr + 1) * tq:(2 * pr + 2) * tq]
        oe = ae[0:hd] / ae[hd:2 * hd]
        oo = ao[hd:2 * hd] / ao[0:hd] if odd_swapped else ao[0:hd] / ao[hd:2 * hd]
        pair = jnp.concatenate([oe, oo], axis=0)
        o_ref[:, pr * LANES:(pr + 1) * LANES] = pair.T.astype(o_ref.dtype)


def _dsa_kernel(q_ref, qi_ref, aux_ref, kv_ref, ki_ref, band_ref, o_ref,
                score_ref, vt_ref, qm_ref, m_ref, acc_ref, thr_ref, *bufs, n_keys, seq):
    tq = tk = ATT_TILE
    qt = pl.program_id(1)
    n_chunks = qt + 1
    kf = float(n_keys)

    @pl.when(qt == 0)
    def _():
        top = lax.broadcasted_iota(jnp.int32, (LANES, tk), 0) < HEAD_DIM
        for j in range(seq // tk):
            vv = kv_ref[j * tk:(j + 1) * tk, LANES:2 * LANES].astype(F32)
            vt_ref[j] = jnp.where(top, vv.T, 1.0).astype(BF16)

    _split_heads_t(qi_ref, qm_ref, IDX_HEADS)
    aux_t = aux_ref[...].T
    head_w = jnp.concatenate(
        [aux_t[AUX_W + h:AUX_W + h + 1, :] for h in range(IDX_HEADS)], axis=1
    ) * (IDX_HEADS ** -0.5 * IDX_DIM ** -0.5)
    key_minus_query = (lax.broadcasted_iota(jnp.int32, (tk, tq), 0)
                       - lax.broadcasted_iota(jnp.int32, (tk, tq), 1))

    def score_body(j, carry):
        mx, mn = carry
        kc = ki_ref[pl.ds(pl.multiple_of(j * tk, tk), tk), :]
        d = _dot(kc, qm_ref[:, 0:IDX_HEADS * tq])
        d = jnp.broadcast_to(head_w, d.shape) * jnp.maximum(d, 0.0)
        sc = (d[:, 0:tq] + d[:, tq:2 * tq]) + (d[:, 2 * tq:3 * tq] + d[:, 3 * tq:4 * tq])
        causal = key_minus_query <= (qt - j) * tq
        score_ref[j] = jnp.where(causal, sc, NEG)
        mx = jnp.maximum(mx, _fold8(jnp.where(causal, sc, NEG), jnp.maximum))
        mn = jnp.minimum(mn, _fold8(jnp.where(causal, sc, -NEG), jnp.minimum))
        return mx, mn

    mx, mn = lax.fori_loop(0, n_chunks, score_body,
                           (jnp.full((8, tq), NEG, F32), jnp.full((8, tq), -NEG, F32)))

    t_q = qt * tq + lax.broadcasted_iota(jnp.int32, (1, tq), 1)
    thr_ref[...] = jnp.full((8, tq), 0.5 * NEG, F32)

    def count(pred):
        def body(j, acc):
            return acc + _fold8(pred(score_ref[j]), jnp.add)
        acc = lax.fori_loop(0, n_chunks, body, jnp.zeros((8, tq), F32))
        return jnp.sum(acc, axis=0, keepdims=True)

    def count_ge(x):
        xb = jnp.broadcast_to(x, (tk, tq))
        return count(lambda s: jnp.where(s >= xb, 1.0, 0.0))

    @pl.when((qt + 1) * tq > n_keys)
    def _search():
        need = (t_q >= n_keys).astype(F32)
        lo0 = jnp.min(mn, axis=0, keepdims=True)
        hi0 = jnp.max(mx, axis=0, keepdims=True)
        c_top = count_ge(hi0)
        c_ge0 = count_ge(jnp.zeros((1, tq), F32))
        c_gt0 = count(lambda s: jnp.where(s > 0.0, 1.0, 0.0))
        n_valid = (t_q + 1).astype(F32)
        top_hit = c_top == kf
        top_tie = c_top > kf
        zero_hit = c_ge0 == kf
        zero_tie = (c_gt0 < kf) & (c_ge0 > kf)
        positive = c_gt0 >= kf
        hi_neg = hi0 < 0.0
        lo1 = jnp.where(positive, jnp.maximum(lo0, 0.0), lo0)
        c_lo1 = jnp.where(positive, c_ge0, n_valid)
        hi1 = jnp.where(positive | hi_neg, hi0, 0.0)
        c_hi1 = jnp.where(positive | hi_neg, c_top, c_ge0)
        hit0 = top_hit | ((~top_tie) & zero_hit)
        tie0 = top_tie | ((~top_hit) & (~zero_hit) & zero_tie)
        x0 = jnp.where(top_hit, hi0, 0.0)
        tie_val = jnp.where(top_tie, hi0, 0.0)
        above0 = jnp.where(top_tie, 0.0, c_gt0)
        hit0 = need * hit0.astype(F32)
        tie0 = need * tie0.astype(F32) * (1.0 - hit0)
        done0 = jnp.maximum(1.0 - need, jnp.maximum(hit0, tie0))
        lo1 = jnp.where(tie0 > 0, tie_val, lo1)
        c_hi1 = jnp.where(tie0 > 0, above0, c_hi1)
        state0 = (lo1, hi1, c_lo1, c_hi1, done0, tie0, x0, jnp.max(1.0 - done0))

        def cond(st):
            return st[7] > 0.0

        def body(st):
            for _ in range(SEARCH_UNROLL):
                st = step(st)
            return st

        def step(st):
            lo, hi, c_lo, c_hi, done, tie, x_fin, _ = st
            x = 0.5 * lo + 0.5 * hi
            inside = ((x > lo) & (x < hi)).astype(F32)
            active = 1.0 - done
            probe = active * inside
            c = count_ge(jnp.where(probe > 0, x, x_fin))
            hit = probe * (c == kf).astype(F32)
            more = probe * (c > kf).astype(F32)
            less = probe * (c < kf).astype(F32)
            new_tie = active * (1.0 - inside)
            lo = jnp.where(more > 0, x, lo)
            c_lo = jnp.where(more > 0, c, c_lo)
            hi = jnp.where(less > 0, x, hi)
            c_hi = jnp.where(less > 0, c, c_hi)
            x_fin = jnp.where(hit > 0, x, x_fin)
            tie = jnp.maximum(tie, new_tie)
            done = jnp.maximum(done, jnp.maximum(hit, new_tie))
            return (lo, hi, c_lo, c_hi, done, tie, x_fin, jnp.max(1.0 - done))

        st = lax.while_loop(cond, body, state0)
        lo, _, _, c_hi, _, tie, x_fin, _ = st
        thr = jnp.where(need > 0, jnp.where(tie > 0, lo, x_fin), 0.5 * NEG)
        thr_ref[...] = jnp.broadcast_to(thr, (8, tq))

        @pl.when(jnp.max(tie) > 0.0)
        def _ties():
            want = jnp.broadcast_to(kf - c_hi, (tk, tq))
            thr_t = jnp.broadcast_to(jnp.where(tie > 0, thr, -NEG), (tk, tq))
            upto = (lax.broadcasted_iota(jnp.int32, (tk, tk), 1)
                    <= lax.broadcasted_iota(jnp.int32, (tk, tk), 0))
            prefix = jnp.where(upto, 1.0, 0.0).astype(BF16)

            def body(j, seen):
                s = score_ref[j]
                eq = s == thr_t
                rank = _dot(prefix, jnp.where(eq, 1.0, 0.0).astype(BF16))
                rank = rank + jnp.broadcast_to(seen, (tk, tq))
                score_ref[j] = jnp.where(eq, jnp.where(rank > want, NEG, s), s)
                return rank[tk - 1:tk, :]

            lax.fori_loop(0, n_chunks, body, jnp.zeros((1, tq), F32))

    _split_heads_t(q_ref, qm_ref, DSA_HEADS)
    m_ref[...] = jnp.full(m_ref.shape, NEG, F32)
    acc_ref[...] = jnp.zeros(acc_ref.shape, F32)
    thr_b = jnp.broadcast_to(thr_ref[0:1, :], (tk, tq))

    def qk(j, s_ref):
        rows = pl.ds(pl.multiple_of(j * tk, tk), tk)
        s_ref[...] = _dot(kv_ref[rows, 0:LANES], qm_ref[...])

    def logits(j, s, kind):
        off = jnp.where(score_ref[j] >= thr_b, 0.0, NEG)
        s = s + jnp.concatenate([off] * DSA_HEADS, axis=1)
        if kind == 'prev':
            s = s + band_ref[0:tk, :]
        elif kind == 'diag':
            s = s + band_ref[tk:2 * tk, :]
        return s

    _attend_chunks(qt, qk, logits, lambda j, p: _dot(vt_ref[j], p), bufs, m_ref, acc_ref)
    _finish_heads_t(acc_ref, o_ref, DSA_HEADS, odd_swapped=False)


def _dsa(qb, qi, aux, kv, ki, band, bsz, seq, n_keys):
    tq = ATT_TILE
    nq = seq // tq
    qrow = lambda w: pl.BlockSpec((tq, w), lambda b, q: (b * nq + q, 0))
    seq_blk = lambda w: pl.BlockSpec((seq, w), lambda b, q: (b, 0))
    return pl.pallas_call(
        functools.partial(_dsa_kernel, n_keys=n_keys, seq=seq), grid=(bsz, nq),
        in_specs=[qrow(DSA_WIDTH), qrow(256), qrow(LANES), seq_blk(256), seq_blk(LANES),
                  _resident((2 * tq, DSA_HEADS * tq))],
        out_specs=qrow(DSA_WIDTH),
        out_shape=jax.ShapeDtypeStruct((bsz * seq, DSA_WIDTH), BF16),
        scratch_shapes=[pltpu.VMEM((nq, tq, tq), F32),
                        pltpu.VMEM((nq, LANES, tq), BF16),
                        pltpu.VMEM((LANES, DSA_HEADS * tq), BF16),
                        pltpu.VMEM((8, DSA_HEADS * tq), F32),
                        pltpu.VMEM((LANES, DSA_HEADS * tq), F32),
                        pltpu.VMEM((8, tq), F32)] + _pipeline_buffers(DSA_HEADS * tq),
        compiler_params=_cparams(("parallel", "arbitrary")), name="dsa",
    )(qb, qi, aux, kv, ki, band)


def _band_kernel(bucket_ref, rb_ref, o_ref):
    bucket = bucket_ref[...]
    tq = bucket.shape[1]
    for h in range(DSA_HEADS):
        acc = jnp.zeros(bucket.shape, F32)
        for k in range(REL_BUCKETS - 1):
            acc = jnp.where(bucket == k, rb_ref[k, h] - rb_ref[REL_BUCKETS - 1, h], acc)
        o_ref[:, h * tq:(h + 1) * tq] = acc


def _t5_bucket_np(dist):
    d = np.maximum(dist, 0)
    df = np.maximum(d, 1).astype(np.float32)
    log_b = REL_MAX_EXACT + (np.log(df / REL_MAX_EXACT) / math.log(REL_MAX_DIST / REL_MAX_EXACT)
                             * (REL_BUCKETS - REL_MAX_EXACT)).astype(np.int32)
    log_b = np.minimum(log_b, REL_BUCKETS - 1)
    return np.where(d < REL_MAX_EXACT, d, log_b).astype(np.int32)


def _rel_band(rel_bias):
    tq = ATT_TILE
    i = np.arange(tq)[None, :]
    j = np.arange(2 * tq)[:, None]
    bucket = jnp.asarray(_t5_bucket_np(i + tq - j))
    return pl.pallas_call(
        _band_kernel,
        in_specs=[pl.BlockSpec(memory_space=pltpu.VMEM), pl.BlockSpec(memory_space=pltpu.SMEM)],
        out_specs=pl.BlockSpec(memory_space=pltpu.VMEM),
        out_shape=jax.ShapeDtypeStruct((2 * tq, DSA_HEADS * tq), F32), name="rel_band",
    )(bucket, rel_bias)


def _cum_kernel(aux_ref, bf_ref, col_ref, row_ref, *, seq):
    x = aux_ref[...] + bf_ref[...]
    logf = -(jnp.maximum(-x, 0.0) + jnp.log1p(jnp.exp(-jnp.abs(x))))
    c = logf.T
    pos = lax.broadcasted_iota(jnp.int32, c.shape, 1)
    shift = 1
    while shift < seq:
        c = c + jnp.where(pos >= shift, pltpu.roll(c, shift, 1), 0.0)
        shift *= 2
    col_ref[...] = c.T
    for j in range(seq // ATT_TILE):
        row_ref[j] = c[0:8, j * ATT_TILE:(j + 1) * ATT_TILE]


def _fox_cum(aux, b_f, bsz, seq):
    nq = seq // ATT_TILE
    bf = jnp.zeros((1, LANES), F32).at[0, AUX_F:AUX_F + FOX_HEADS].set(b_f)
    return pl.pallas_call(
        functools.partial(_cum_kernel, seq=seq), grid=(bsz,),
        in_specs=[pl.BlockSpec((seq, LANES), lambda b: (b, 0)), _resident((1, LANES))],
        out_specs=[pl.BlockSpec((seq, LANES), lambda b: (b, 0)),
                   pl.BlockSpec((None, nq, 8, ATT_TILE), lambda b: (b, 0, 0, 0))],
        out_shape=[jax.ShapeDtypeStruct((bsz * seq, LANES), F32),
                   jax.ShapeDtypeStruct((bsz, nq, 8, ATT_TILE), F32)],
        compiler_params=_cparams(("parallel",)), name="fox_cum")(aux, bf)


def _fox_kernel(q_ref, k_ref, v_ref, ccol_ref, crow_ref, o_ref,
                vte_ref, vto_ref, ck_ref, qm_ref, m_ref, acc_ref, *bufs, seq):
    tq = tk = ATT_TILE
    qt = pl.program_id(1)
    heads = range(FOX_HEADS)

    @pl.when(qt == 0)
    def _():
        top = lax.broadcasted_iota(jnp.int32, (LANES, tk), 0) < HEAD_DIM
        for j in range(seq // tk):
            rows = slice(j * tk, (j + 1) * tk)
            for pr in range(FOX_HEADS // 2):
                v_t = v_ref[rows, pr * LANES:(pr + 1) * LANES].astype(F32).T
                vte_ref[pr, j] = jnp.where(top, v_t, 1.0).astype(BF16)
                vto_ref[pr, j] = jnp.where(top, 1.0, v_t).astype(BF16)
            cc = ccol_ref[rows, :]
            for h in heads:
                ck_ref[h, rows, :] = jnp.broadcast_to(cc[:, AUX_F + h:AUX_F + h + 1], (tk, LANES))

    _split_heads_t(q_ref, qm_ref, FOX_HEADS)
    m_ref[...] = jnp.full(m_ref.shape, NEG, F32)
    acc_ref[...] = jnp.zeros(acc_ref.shape, F32)
    c_q = crow_ref[qt]
    c_q = jnp.concatenate([c_q[AUX_F + h:AUX_F + h + 1, :] for h in heads], axis=1)
    causal_off = jnp.where(lax.broadcasted_iota(jnp.int32, (tk, tq), 0)
                           <= lax.broadcasted_iota(jnp.int32, (tk, tq), 1), 0.0, NEG)

    def qk(j, s_ref):
        rows = pl.ds(pl.multiple_of(j * tk, tk), tk)
        for h in heads:
            cols = slice(h * tq, (h + 1) * tq)
            s_ref[:, cols] = _dot(k_ref[rows, (h // 2) * LANES:(h // 2 + 1) * LANES], qm_ref[:, cols])

    def logits(j, s, kind):
        rows = pl.ds(pl.multiple_of(j * tk, tk), tk)
        c_k = jnp.concatenate([ck_ref[h, rows, :] for h in heads for _ in range(tq // LANES)], axis=1)
        s = s + (jnp.broadcast_to(c_q, s.shape) - c_k)
        if kind == 'diag':
            s = s + jnp.concatenate([causal_off] * FOX_HEADS, axis=1)
        return s

    def pv(j, p):
        return jnp.concatenate(
            [_dot(vte_ref[h // 2, j] if h % 2 == 0 else vto_ref[h // 2, j],
                  p[:, h * tq:(h + 1) * tq]) for h in heads], axis=1)

    _attend_chunks(qt, qk, logits, pv, bufs, m_ref, acc_ref)
    _finish_heads_t(acc_ref, o_ref, FOX_HEADS, odd_swapped=True)


def _fox(qkv, c_col, c_row, bsz, seq):
    tq = ATT_TILE
    nq = seq // tq
    npair = FOX_HEADS // 2
    return pl.pallas_call(
        functools.partial(_fox_kernel, seq=seq), grid=(bsz, nq),
        in_specs=[pl.BlockSpec((tq, FOX_WIDTH), lambda b, q: (b * nq + q, 0)),
                  pl.BlockSpec((seq, FOX_WIDTH), lambda b, q: (b, 1)),
                  pl.BlockSpec((seq, FOX_WIDTH), lambda b, q: (b, 2)),
                  pl.BlockSpec((seq, LANES), lambda b, q: (b, 0)),
                  pl.BlockSpec((None, nq, 8, tq), lambda b, q: (b, 0, 0, 0))],
        out_specs=pl.BlockSpec((tq, FOX_WIDTH), lambda b, q: (b * nq + q, 0)),
        out_shape=jax.ShapeDtypeStruct((bsz * seq, FOX_WIDTH), BF16),
        scratch_shapes=[pltpu.VMEM((npair, nq, LANES, tq), BF16),
                        pltpu.VMEM((npair, nq, LANES, tq), BF16),
                        pltpu.VMEM((FOX_HEADS, seq, LANES), F32),
                        pltpu.VMEM((LANES, FOX_HEADS * tq), BF16),
                        pltpu.VMEM((8, FOX_HEADS * tq), F32),
                        pltpu.VMEM((LANES, FOX_HEADS * tq), F32)] + _pipeline_buffers(FOX_HEADS * tq),
        compiler_params=_cparams(("parallel", "arbitrary")), name="fox",
    )(qkv, qkv, qkv, c_col, c_row)


def _gelu(y):
    return 0.5 * y * (1.0 + jnp.tanh(math.sqrt(2.0 / math.pi) * (y + 0.044715 * (y * y * y))))


def _merge_kernel(x_ref, yc_ref, u_ref, d_ref, wglu_ref, b_ref, c_ref, gate_ref,
                  wa_ref, wb_ref, wc_ref, wo_ref, o_ref, tmp_ref):
    d = D_MODEL
    y = _chunks_to_rows(yc_ref, tmp_ref, SSM_WIDTH) + d_ref[...] * u_ref[...]
    z = _dot(_gelu(y).astype(BF16), wglu_ref[...])
    a = z[:, :SSM_WIDTH] * _sigmoid(z[:, SSM_WIDTH:])
    merged = _sigmoid(gate_ref[:, 0:d].astype(F32)) * _dot(a.astype(BF16), wa_ref[...])
    merged += _sigmoid(gate_ref[:, d:2 * d].astype(F32)) * _dot(b_ref[...], wb_ref[...])
    merged += _sigmoid(gate_ref[:, 2 * d:3 * d].astype(F32)) * _dot(c_ref[...], wc_ref[...])
    o_ref[...] = x_ref[...] + _dot(merged.astype(BF16), wo_ref[...])


def _merge(xt, yc, u, d_skip, w_glu, b, c, gates, wa, wb, wc, wo, tm=1024):
    m, d = xt.shape
    row = lambda w: pl.BlockSpec((tm, w), lambda i: (i, 0))
    return pl.pallas_call(
        _merge_kernel, grid=(m // tm,),
        in_specs=[row(d), pl.BlockSpec((tm // S5_CHUNK, S5_COLS), lambda i: (i, 0)),
                  row(SSM_WIDTH), _resident((1, SSM_WIDTH)),
                  _resident((SSM_WIDTH, 2 * SSM_WIDTH)), row(DSA_WIDTH), row(FOX_WIDTH), row(3 * d),
                  _resident((SSM_WIDTH, d)), _resident((DSA_WIDTH, d)), _resident((FOX_WIDTH, d)),
                  _resident((d, d))],
        out_specs=row(d), out_shape=jax.ShapeDtypeStruct((m, d), F32),
        scratch_shapes=[pltpu.VMEM((SSM_WIDTH // LANES, tm, LANES), F32)],
        compiler_params=_cparams(("parallel",)), name="merge",
    )(xt, yc, u, d_skip.reshape(1, -1), w_glu, b, c, gates, wa, wb, wc, wo)


def _pack_w_in(w_in):
    d = w_in.shape[0]
    splits = (SSM_WIDTH, DSA_WIDTH, DSA_LATENT, IDX_HEADS * IDX_DIM, IDX_DIM, IDX_HEADS,
              FOX_WIDTH, FOX_WIDTH, FOX_WIDTH, FOX_HEADS, D_MODEL, D_MODEL, D_MODEL)
    pts = np.cumsum(splits)[:-1]
    (w_u, w_qb, w_ckv, w_qi, w_ki, w_wi, w_qc, w_kc, w_vc, w_fc, w_ga, w_gb, w_gc) = jnp.split(
        w_in, pts, axis=1)
    scale = HEAD_DIM ** -0.5
    aux = jnp.zeros((d, LANES), F32)
    aux = aux.at[:, AUX_F:AUX_F + FOX_HEADS].set(w_fc).at[:, AUX_W:AUX_W + IDX_HEADS].set(w_wi)
    cols = [w_u, w_qb * scale, w_qi, w_ki, w_ki, w_ckv, aux, w_qc * scale, w_kc, w_vc,
            w_ga, w_gb, w_gc]
    return jnp.concatenate(cols, axis=1).astype(BF16)


def kernel(x, ffn1_norm, ffn1_w_gate, ffn1_w_up, ffn1_w_down, mix_norm, w_in, ssm_lambda_re, ssm_lambda_im, ssm_log_dt, ssm_b_re, ssm_b_im, ssm_c_re, ssm_c_im, ssm_d, ssm_w_glu, dsa_kv_norm, dsa_w_uk, dsa_w_uv, rel_bias, fox_b_f, w_branch_ssm, w_branch_dsa, w_branch_fox, w_out, ffn2_norm, ffn2_w_gate, ffn2_w_up, ffn2_w_down, final_norm):
    bsz, seq, d = x.shape
    depth = w_in.shape[0]
    n_keys = min(TOPK_MAX, seq // 4)
    assert d == D_MODEL and seq % ATT_TILE == 0 and (bsz * seq) % (8 * 512) == 0
    assert n_keys % LANES == 0 and bsz % 8 == 0
    m = bsz * seq
    n_chunks = seq // S5_CHUNK
    xt = x.reshape(m, d)
    band = _rel_band(rel_bias)
    bf = lambda w: w.astype(BF16)
    for l in range(depth):
        xt = _ffn(xt, ffn1_norm[l], bf(ffn1_w_gate[l]), bf(ffn1_w_up[l]), bf(ffn1_w_down[l]))
        w_kv = bf(jnp.concatenate([dsa_w_uk[l], dsa_w_uk[l], dsa_w_uv[l], dsa_w_uv[l]], axis=1))
        u, qb, qi, ki, kv, aux, qkv, gates, u8 = _inproj(
            xt, mix_norm[l], _pack_w_in(w_in[l]), dsa_kv_norm[l], w_kv)
        m_intra, m_sum, m_out, a_re, a_im = _s5_matrices(
            ssm_lambda_re[l], ssm_lambda_im[l], ssm_log_dt[l], ssm_b_re[l], ssm_b_im[l],
            ssm_c_re[l], ssm_c_im[l])
        y_intra, p = _s5a(u8, m_intra, m_sum)
        h_re, h_im = _s5b(p, a_re, a_im, n_chunks, nb=8)
        yc = _s5c(y_intra, h_re, h_im, m_out)
        b_out = _dsa(qb, qi, aux, kv, ki, band, bsz, seq, n_keys)
        c_col, c_row = _fox_cum(aux, fox_b_f[l], bsz, seq)
        c_out = _fox(qkv, c_col, c_row, bsz, seq)
        xt = _merge(xt, yc, u, ssm_d[l], bf(ssm_w_glu[l]), b_out, c_out, gates,
                    bf(w_branch_ssm[l]), bf(w_branch_dsa[l]), bf(w_branch_fox[l]), bf(w_out[l]))
        last = final_norm if l == depth - 1 else None
        xt = _ffn(xt, ffn2_norm[l], bf(ffn2_w_gate[l]), bf(ffn2_w_up[l]), bf(ffn2_w_down[l]),
                  final_norm=last)
    return xt.reshape(bsz, seq, d)
```

```python
import functools
import math

import numpy as np
import jax
import jax.numpy as jnp
from jax import lax
from jax.experimental import pallas as pl
from jax.experimental.pallas import tpu as pltpu

F32 = jnp.float32
BF16 = jnp.bfloat16

D_MODEL = 1024
SSM_WIDTH = D_MODEL // 4
SSM_GROUP = 16
SSM_GROUPS = SSM_WIDTH // SSM_GROUP
SSM_STATE = 64
HEAD_DIM = 64
DSA_HEADS = 6
DSA_WIDTH = DSA_HEADS * HEAD_DIM
DSA_LATENT = 2 * HEAD_DIM
IDX_HEADS = 4
IDX_DIM = 64
TOPK_MAX = 256
FOX_HEADS = 6
FOX_WIDTH = FOX_HEADS * HEAD_DIM
REL_BUCKETS = 32
REL_MAX_EXACT = 16
REL_MAX_DIST = 128
FFN_HIDDEN = 2816
EPS = 1e-6
NEG = -1e30

LANES = 128
S5_CHUNK = 8
S5_COLS = S5_CHUNK * SSM_WIDTH
S5_STATE = 2 * SSM_GROUPS * SSM_STATE
ATT_TILE = 256
SEARCH_UNROLL = 4
VMEM_LIMIT = 56 * 1024 * 1024

_SEG = dict(u=256, qb=384, qi=256, ki=128, ckv=128, aux=128, qkv=1152, g=3072)
AUX_F = 0
AUX_W = 8


def _cparams(sem):
    return pltpu.CompilerParams(dimension_semantics=sem, vmem_limit_bytes=VMEM_LIMIT)


def _resident(shape):
    nd = len(shape)
    return pl.BlockSpec(shape, lambda *_: (0,) * nd, pipeline_mode=pl.Buffered(1))


def _rms(x, g):
    return x * lax.rsqrt(jnp.mean(x * x, axis=-1, keepdims=True) + EPS) * g


def _sigmoid(x):
    return 1.0 / (1.0 + jnp.exp(-x))


def _dot(a, b):
    return jnp.dot(a, b, preferred_element_type=F32)


def _ffn_body(x, g_ref, wg_ref, wu_ref, wd_ref, fc):
    h = _rms(x, g_ref[...]).astype(BF16)
    acc = None
    for c in range(FFN_HIDDEN // fc):
        sl = slice(c * fc, (c + 1) * fc)
        gt = _dot(h, wg_ref[:, sl])
        up = _dot(h, wu_ref[:, sl])
        a = (gt * _sigmoid(gt) * up).astype(BF16)
        d = _dot(a, wd_ref[sl, :])
        acc = d if acc is None else acc + d
    return x + 0.5 * acc


def _ffn_kernel(x_ref, g_ref, wg_ref, wu_ref, wd_ref, o_ref, *, fc):
    o_ref[...] = _ffn_body(x_ref[...], g_ref, wg_ref, wu_ref, wd_ref, fc)


def _ffn_final_kernel(x_ref, g_ref, wg_ref, wu_ref, wd_ref, fn_ref, o_ref, *, fc):
    y = _ffn_body(x_ref[...], g_ref, wg_ref, wu_ref, wd_ref, fc)
    o_ref[...] = _rms(y, fn_ref[...])


def _ffn(xt, norm, wg, wu, wd, final_norm=None, tm=1024, fc=256):
    m, d = xt.shape
    f = wg.shape[1]
    row = pl.BlockSpec((tm, d), lambda i: (i, 0))
    vec = _resident((1, d))
    in_specs = [row, vec, _resident((d, f)), _resident((d, f)), _resident((f, d))]
    args = [xt, norm.reshape(1, d), wg, wu, wd]
    if final_norm is None:
        body = functools.partial(_ffn_kernel, fc=fc)
    else:
        body = functools.partial(_ffn_final_kernel, fc=fc)
        in_specs.append(vec)
        args.append(final_norm.reshape(1, d))
    return pl.pallas_call(
        body, grid=(m // tm,), in_specs=in_specs, out_specs=row,
        out_shape=jax.ShapeDtypeStruct((m, d), F32),
        compiler_params=_cparams(("parallel",)), name="ffn")(*args)


def _rows_to_chunks(x, tmp_ref, o_ref):
    n, w = x.shape
    for hh in range(w // LANES):
        tmp_ref[hh] = x[:, hh * LANES:(hh + 1) * LANES]
    for i in range(S5_CHUNK):
        for hh in range(w // LANES):
            o_ref[:, i * w + hh * LANES:i * w + (hh + 1) * LANES] = (
                tmp_ref[hh, pl.ds(i, n // S5_CHUNK, stride=S5_CHUNK), :])


def _chunks_to_rows(x_ref, tmp_ref, w):
    n = x_ref.shape[0]
    for i in range(S5_CHUNK):
        for hh in range(w // LANES):
            tmp_ref[hh, pl.ds(i, n, stride=S5_CHUNK), :] = (
                x_ref[:, i * w + hh * LANES:i * w + (hh + 1) * LANES])
    return jnp.concatenate([tmp_ref[hh] for hh in range(w // LANES)], axis=1)


def _inproj_kernel(x_ref, g_ref, w_ref, kvn_ref, wkv_ref,
                   u_ref, qb_ref, qi_ref, ki_ref, kv_ref, aux_ref, qkv_ref, gate_ref, u8_ref,
                   tmp_ref):
    h = _rms(x_ref[...], g_ref[...]).astype(BF16)
    outs = dict(u=u_ref, qb=qb_ref, qi=qi_ref, ki=ki_ref, aux=aux_ref, qkv=qkv_ref)
    off = 0
    for group in (("u", "qb", "qi", "ki"), ("ckv", "aux", "qkv")):
        width = sum(_SEG[name] for name in group)
        r = _dot(h, w_ref[:, off:off + width])
        off += width
        lo = 0
        for name in group:
            piece = r[:, lo:lo + _SEG[name]]
            lo += _SEG[name]
            if name == "ckv":
                c = _rms(piece, kvn_ref[...]).astype(BF16)
                kv_ref[...] = _dot(c, wkv_ref[...]).astype(BF16)
            else:
                outs[name][...] = piece.astype(outs[name].dtype)
            if name == "u":
                _rows_to_chunks(piece, tmp_ref, u8_ref)
    for c in range(_SEG["g"] // D_MODEL):
        cols = slice(c * D_MODEL, (c + 1) * D_MODEL)
        gate_ref[:, cols] = _dot(h, w_ref[:, off + c * D_MODEL:off + (c + 1) * D_MODEL]).astype(BF16)


def _inproj(xt, norm, w_all, kv_norm, w_kv, tm=1024):
    m, d = xt.shape
    wtot = w_all.shape[1]
    widths = dict(u=(256, F32), qb=(384, BF16), qi=(256, BF16), ki=(128, BF16),
                  kv=(256, BF16), aux=(128, F32), qkv=(1152, BF16), g=(3072, BF16))
    out_shape = [jax.ShapeDtypeStruct((m, w), dt) for w, dt in widths.values()]
    out_specs = [pl.BlockSpec((tm, w), lambda i: (i, 0)) for w, _ in widths.values()]
    out_shape.append(jax.ShapeDtypeStruct((m // S5_CHUNK, S5_COLS), F32))
    out_specs.append(pl.BlockSpec((tm // S5_CHUNK, S5_COLS), lambda i: (i, 0)))
    return pl.pallas_call(
        _inproj_kernel, grid=(m // tm,),
        in_specs=[pl.BlockSpec((tm, d), lambda i: (i, 0)), _resident((1, d)),
                  _resident((d, wtot)), _resident((1, DSA_LATENT)),
                  _resident((DSA_LATENT, 256))],
        out_specs=out_specs, out_shape=out_shape,
        scratch_shapes=[pltpu.VMEM((SSM_WIDTH // LANES, tm, LANES), F32)],
        compiler_params=_cparams(("parallel",)), name="inproj",
    )(xt, norm.reshape(1, d), w_all, kv_norm.reshape(1, DSA_LATENT), w_kv)


def _s5a_kernel(u_ref, mi_ref, ms_ref, yi_ref, p_ref):
    u = u_ref[...].astype(BF16)
    yi_ref[...] = _dot(u, mi_ref[...])
    p_ref[...] = _dot(u, ms_ref[...])


def _s5a(u8, m_intra, m_sum, tr=512):
    r = u8.shape[0]
    row = pl.BlockSpec((tr, S5_COLS), lambda i: (i, 0))
    srow = pl.BlockSpec((tr, S5_STATE), lambda i: (i, 0))
    return pl.pallas_call(
        _s5a_kernel, grid=(r // tr,),
        in_specs=[row, _resident((S5_COLS, S5_COLS)), _resident((S5_COLS, S5_STATE))],
        out_specs=[row, srow],
        out_shape=[jax.ShapeDtypeStruct((r, S5_COLS), F32),
                   jax.ShapeDtypeStruct((r, S5_STATE), F32)],
        compiler_params=_cparams(("parallel",)), name="s5_chunk")(u8, m_intra, m_sum)


def _s5b_kernel(pr_ref, pi_ref, ar_ref, ai_ref, hr_ref, hi_ref, *, nb, nc):
    ar = ar_ref[...]
    ai = ai_ref[...]
    cols = ar.shape[1]

    def step(c, carry):
        hr, hi = carry
        rows = pl.ds(c, nb, stride=nc)
        hr_ref[rows, :] = hr.astype(hr_ref.dtype)
        hi_ref[rows, :] = hi.astype(hi_ref.dtype)
        pr = pr_ref[rows, :]
        pi = pi_ref[rows, :]
        return (ar * hr - ai * hi + pr, ar * hi + ai * hr + pi)

    z = jnp.zeros((nb, cols), F32)
    lax.fori_loop(0, nc, step, (z, z))


def _s5b(p, a_re, a_im, nc, nb, tcol=LANES):
    r = p.shape[0]
    half = S5_STATE // 2
    ncb = half // tcol
    blk = pl.BlockSpec((nb * nc, tcol), lambda i, j: (i, j))
    im_blk = pl.BlockSpec((nb * nc, tcol), lambda i, j: (i, j + ncb))
    a_blk = pl.BlockSpec((1, tcol), lambda i, j: (0, j))
    return pl.pallas_call(
        functools.partial(_s5b_kernel, nb=nb, nc=nc), grid=(r // (nb * nc), ncb),
        in_specs=[blk, im_blk, a_blk, a_blk],
        out_specs=[blk, blk],
        out_shape=[jax.ShapeDtypeStruct((r, half), F32)] * 2,
        compiler_params=_cparams(("parallel", "parallel")), name="s5_scan",
    )(p, p, a_re, a_im)


def _s5c_kernel(yi_ref, hr_ref, hi_ref, mo_ref, y_ref):
    half = S5_STATE // 2
    y = yi_ref[...]
    y = y + _dot(hr_ref[...].astype(BF16), mo_ref[:half, :])
    y = y + _dot(hi_ref[...].astype(BF16), mo_ref[half:, :])
    y_ref[...] = y


def _s5c(yi, h_re, h_im, m_out, tr=512):
    r = yi.shape[0]
    half = S5_STATE // 2
    row = pl.BlockSpec((tr, S5_COLS), lambda i: (i, 0))
    return pl.pallas_call(
        _s5c_kernel, grid=(r // tr,),
        in_specs=[row, pl.BlockSpec((tr, half), lambda i: (i, 0)),
                  pl.BlockSpec((tr, half), lambda i: (i, 0)),
                  _resident((S5_STATE, S5_COLS))],
        out_specs=row, out_shape=jax.ShapeDtypeStruct((r, S5_COLS), F32),
        compiler_params=_cparams(("parallel",)), name="s5_out")(yi, h_re, h_im, m_out)


def _toeplitz_kernel(lag_ref, o_ref):
    keep = pl.program_id(1) >= pl.program_id(0)
    o_ref[...] = jnp.where(keep, lag_ref[...], jnp.zeros_like(lag_ref))


def _block_toeplitz(lag_blocks):
    t, w, _ = lag_blocks.shape
    return pl.pallas_call(
        _toeplitz_kernel, grid=(t, t),
        in_specs=[pl.BlockSpec((None, w, w), lambda j, i: (jnp.maximum(i - j, 0), 0, 0))],
        out_specs=pl.BlockSpec((w, w), lambda j, i: (j, i)),
        out_shape=jax.ShapeDtypeStruct((t * w, t * w), lag_blocks.dtype), name="s5_toeplitz",
    )(lag_blocks)


def _s5_matrices(lam_re, lam_im, log_dt, b_re, b_im, c_re, c_im):
    hp = lax.Precision.HIGHEST
    t = S5_CHUNK
    dt = jnp.exp(log_dt)[:, None]
    lr, li = lam_re, lam_im
    mag = jnp.exp(lr * dt)
    ab_re, ab_im = mag * jnp.cos(li * dt), mag * jnp.sin(li * dt)
    den = lr * lr + li * li
    nr, ni = ab_re - 1.0, ab_im
    s_re = (nr * lr + ni * li) / den
    s_im = (ni * lr - nr * li) / den
    bb_re = s_re[..., None] * b_re - s_im[..., None] * b_im
    bb_im = s_re[..., None] * b_im + s_im[..., None] * b_re
    pr, pi = [jnp.ones_like(ab_re)], [jnp.zeros_like(ab_re)]
    for _ in range(t):
        pr.append(pr[-1] * ab_re - pi[-1] * ab_im)
        pi.append(pr[-2] * ab_im + pi[-1] * ab_re)
    pw_re, pw_im = jnp.stack(pr), jnp.stack(pi)
    lo_re, lo_im = jnp.stack(pr[:t]), jnp.stack(pi[:t])
    ab_b_re = lo_re[..., None] * bb_re - lo_im[..., None] * bb_im
    ab_b_im = lo_re[..., None] * bb_im + lo_im[..., None] * bb_re
    kern = (jnp.einsum('gpn,dgnq->dgpq', c_re, ab_b_re, precision=hp)
            - jnp.einsum('gpn,dgnq->dgpq', c_im, ab_b_im, precision=hp))
    eye_g = jnp.eye(SSM_GROUPS, dtype=F32)
    rows = np.arange(SSM_WIDTH)
    expand = jnp.asarray((rows[None, :] % SSM_GROUP == np.arange(SSM_GROUP)[:, None]).astype(np.float32))
    same_group = jnp.asarray(rows[:, None] // SSM_GROUP == rows[None, :] // SSM_GROUP)
    lag_rows = kern.transpose(0, 1, 3, 2).reshape(t, SSM_WIDTH, SSM_GROUP)
    lag_blocks = jnp.where(same_group, jnp.einsum('drp,pc->drc', lag_rows, expand, precision=hp), 0.0)
    m_intra = _block_toeplitz(lag_blocks.astype(BF16))
    hi_re, hi_im = jnp.stack(pr[t - 1::-1]), jnp.stack(pi[t - 1::-1])
    sb_re = hi_re[..., None] * bb_re - hi_im[..., None] * bb_im
    sb_im = hi_re[..., None] * bb_im + hi_im[..., None] * bb_re
    m_sum_re = jnp.einsum('jgnq,gh->jgqhn', sb_re, eye_g, precision=hp)
    m_sum_im = jnp.einsum('jgnq,gh->jgqhn', sb_im, eye_g, precision=hp)
    m_sum = jnp.concatenate([m_sum_re.reshape(S5_COLS, -1), m_sum_im.reshape(S5_COLS, -1)], axis=1)
    up_re, up_im = jnp.stack(pr[1:])[:, :, None, :], jnp.stack(pi[1:])[:, :, None, :]
    w_re = c_re[None] * up_re - c_im[None] * up_im
    w_im = c_re[None] * up_im + c_im[None] * up_re
    m_out_re = jnp.einsum('igpn,gh->gnihp', w_re, eye_g, precision=hp)
    m_out_im = jnp.einsum('igpn,gh->gnihp', -w_im, eye_g, precision=hp)
    m_out = jnp.concatenate([m_out_re.reshape(-1, S5_COLS), m_out_im.reshape(-1, S5_COLS)], axis=0)
    a_re = pw_re[t].reshape(1, -1)
    a_im = pw_im[t].reshape(1, -1)
    return m_intra.astype(BF16), m_sum.astype(BF16), m_out.astype(BF16), a_re, a_im


def _fold8(x, op):
    parts = [x[r:r + 8] for r in range(0, x.shape[0], 8)]
    while len(parts) > 1:
        parts = [op(parts[i], parts[i + 1]) for i in range(0, len(parts), 2)]
    return parts[0]


def _split_heads_t(q_ref, qt_ref, n_heads):
    tq = q_ref.shape[0]
    top = lax.broadcasted_iota(jnp.int32, (LANES, tq), 0) < HEAD_DIM
    for pr in range(n_heads // 2):
        pair_t = q_ref[:, pr * LANES:(pr + 1) * LANES].astype(F32).T
        qt_ref[:, 2 * pr * tq:(2 * pr + 1) * tq] = jnp.where(top, pair_t, 0.0).astype(BF16)
        qt_ref[:, (2 * pr + 1) * tq:(2 * pr + 2) * tq] = jnp.where(top, 0.0, pair_t).astype(BF16)


def _softmax_stage(s, m_ref, alpha_ref, p_ref):
    m_old = m_ref[...]
    m_new = jnp.maximum(m_old, jnp.max(_fold8(s, jnp.maximum), axis=0, keepdims=True))
    alpha_ref[...] = jnp.exp(m_old - m_new)
    p_ref[...] = jnp.exp(s - jnp.broadcast_to(m_new[0:1], s.shape)).astype(BF16)
    m_ref[...] = m_new


def _value_stage(pv, alpha_ref, p_ref, acc_ref):
    acc = acc_ref[...]
    acc_ref[...] = acc * jnp.broadcast_to(alpha_ref[0:1, :], acc.shape) + pv(p_ref[...])


def _attend_chunks(qt, qk, logits, pv, bufs, m_ref, acc_ref):
    s_a, s_b, p_a, p_b, al_a, al_b = bufs

    def plain(j, kind):
        qk(j, s_a)
        _softmax_stage(logits(j, s_a[...], kind), m_ref, al_a, p_a)
        _value_stage(lambda p: pv(j, p), al_a, p_a, acc_ref)

    @pl.when(qt == 0)
    def _():
        plain(0, 'diag')

    @pl.when(qt >= 1)
    def _():
        n_far = qt - 1
        first = n_far % 2

        @pl.when(first == 1)
        def _():
            plain(0, 'far')

        qk(first, s_a)
        p_b[...] = jnp.zeros(p_b.shape, p_b.dtype)
        al_b[...] = jnp.ones(al_b.shape, al_b.dtype)

        def pair(c, kind0, kind1, last):
            qk(c + 1, s_b)
            _value_stage(lambda p: pv(jnp.maximum(c - 1, 0), p), al_b, p_b, acc_ref)
            _softmax_stage(logits(c, s_a[...], kind0), m_ref, al_a, p_a)
            if not last:
                qk(c + 2, s_a)
            _value_stage(lambda p: pv(c, p), al_a, p_a, acc_ref)
            _softmax_stage(logits(c + 1, s_b[...], kind1), m_ref, al_b, p_b)

        def body(i, carry):
            pair(first + 2 * i, 'far', 'far', False)
            return carry

        lax.fori_loop(0, (n_far - first) // 2, body, 0)
        pair(qt - 1, 'prev', 'diag', True)
        _value_stage(lambda p: pv(qt, p), al_b, p_b, acc_ref)


def _pipeline_buffers(width):
    return ([pltpu.VMEM((ATT_TILE, width), F32)] * 2 + [pltpu.VMEM((ATT_TILE, width), BF16)] * 2
            + [pltpu.VMEM((8, width), F32)] * 2)


def _finish_heads_t(acc_ref, o_ref, n_heads, odd_swapped):
    hd = HEAD_DIM
    tq = o_ref.shape[0]
    for pr in range(n_heads // 2):
        ae = acc_ref[:, 2 * pr * tq:(2 * pr + 1) * tq]
        ao = acc_ref[:, (2 * pr + 1) * tq:(2 * pr + 2) * tq]
        oe = ae[0:hd] / ae[hd:2 * hd]
        oo = ao[hd:2 * hd] / ao[0:hd] if odd_swapped else ao[0:hd] / ao[hd:2 * hd]
        pair = jnp.concatenate([oe, oo], axis=0)
        o_ref[:, pr * LANES:(pr + 1) * LANES] = pair.T.astype(o_ref.dtype)


def _dsa_kernel(q_ref, qi_ref, aux_ref, kv_ref, ki_ref, band_ref, o_ref,
                score_ref, vt_ref, qm_ref, m_ref, acc_ref, thr_ref, *bufs, n_keys, seq):
    tq = tk = ATT_TILE
    qt = pl.program_id(1)
    n_chunks = qt + 1
    kf = float(n_keys)

    @pl.when(qt == 0)
    def _():
        top = lax.broadcasted_iota(jnp.int32, (LANES, tk), 0) < HEAD_DIM
        for j in range(seq // tk):
            vv = kv_ref[j * tk:(j + 1) * tk, LANES:2 * LANES].astype(F32)
            vt_ref[j] = jnp.where(top, vv.T, 1.0).astype(BF16)

    _split_heads_t(qi_ref, qm_ref, IDX_HEADS)
    aux_t = aux_ref[...].T
    head_w = jnp.concatenate(
        [aux_t[AUX_W + h:AUX_W + h + 1, :] for h in range(IDX_HEADS)], axis=1
    ) * (IDX_HEADS ** -0.5 * IDX_DIM ** -0.5)
    key_minus_query = (lax.broadcasted_iota(jnp.int32, (tk, tq), 0)
                       - lax.broadcasted_iota(jnp.int32, (tk, tq), 1))

    def score_body(j, carry):
        mx, mn = carry
        kc = ki_ref[pl.ds(pl.multiple_of(j * tk, tk), tk), :]
        d = _dot(kc, qm_ref[:, 0:IDX_HEADS * tq])
        d = jnp.broadcast_to(head_w, d.shape) * jnp.maximum(d, 0.0)
        sc = (d[:, 0:tq] + d[:, tq:2 * tq]) + (d[:, 2 * tq:3 * tq] + d[:, 3 * tq:4 * tq])
        causal = key_minus_query <= (qt - j) * tq
        score_ref[j] = jnp.where(causal, sc, NEG)
        mx = jnp.maximum(mx, _fold8(jnp.where(causal, sc, NEG), jnp.maximum))
        mn = jnp.minimum(mn, _fold8(jnp.where(causal, sc, -NEG), jnp.minimum))
        return mx, mn

    mx, mn = lax.fori_loop(0, n_chunks, score_body,
                           (jnp.full((8, tq), NEG, F32), jnp.full((8, tq), -NEG, F32)))

    t_q = qt * tq + lax.broadcasted_iota(jnp.int32, (1, tq), 1)
    thr_ref[...] = jnp.full((8, tq), 0.5 * NEG, F32)

    def count(pred):
        def body(j, acc):
            return acc + _fold8(pred(score_ref[j]), jnp.add)
        acc = lax.fori_loop(0, n_chunks, body, jnp.zeros((8, tq), F32))
        return jnp.sum(acc, axis=0, keepdims=True)

    def count_ge(x):
        xb = jnp.broadcast_to(x, (tk, tq))
        return count(lambda s: jnp.where(s >= xb, 1.0, 0.0))

    @pl.when((qt + 1) * tq > n_keys)
    def _search():
        need = (t_q >= n_keys).astype(F32)
        lo0 = jnp.min(mn, axis=0, keepdims=True)
        hi0 = jnp.max(mx, axis=0, keepdims=True)
        c_top = count_ge(hi0)
        c_ge0 = count_ge(jnp.zeros((1, tq), F32))
        c_gt0 = count(lambda s: jnp.where(s > 0.0, 1.0, 0.0))
        n_valid = (t_q + 1).astype(F32)
        top_hit = c_top == kf
        top_tie = c_top > kf
        zero_hit = c_ge0 == kf
        zero_tie = (c_gt0 < kf) & (c_ge0 > kf)
        positive = c_gt0 >= kf
        hi_neg = hi0 < 0.0
        lo1 = jnp.where(positive, jnp.maximum(lo0, 0.0), lo0)
        c_lo1 = jnp.where(positive, c_ge0, n_valid)
        hi1 = jnp.where(positive | hi_neg, hi0, 0.0)
        c_hi1 = jnp.where(positive | hi_neg, c_top, c_ge0)
        hit0 = top_hit | ((~top_tie) & zero_hit)
        tie0 = top_tie | ((~top_hit) & (~zero_hit) & zero_tie)
        x0 = jnp.where(top_hit, hi0, 0.0)
        tie_val = jnp.where(top_tie, hi0, 0.0)
        above0 = jnp.where(top_tie, 0.0, c_gt0)
        hit0 = need * hit0.astype(F32)
        tie0 = need * tie0.astype(F32) * (1.0 - hit0)
        done0 = jnp.maximum(1.0 - need, jnp.maximum(hit0, tie0))
        lo1 = jnp.where(tie0 > 0, tie_val, lo1)
        c_hi1 = jnp.where(tie0 > 0, above0, c_hi1)
        state0 = (lo1, hi1, c_lo1, c_hi1, done0, tie0, x0, jnp.max(1.0 - done0))

        def cond(st):
            return st[7] > 0.0

        def body(st):
            for _ in range(SEARCH_UNROLL):
                st = step(st)
            return st

        def step(st):
            lo, hi, c_lo, c_hi, done, tie, x_fin, _ = st
            x = 0.5 * lo + 0.5 * hi
            inside = ((x > lo) & (x < hi)).astype(F32)
            active = 1.0 - done
            probe = active * inside
            c = count_ge(jnp.where(probe > 0, x, x_fin))
            hit = probe * (c == kf).astype(F32)
            more = probe * (c > kf).astype(F32)
            less = probe * (c < kf).astype(F32)
            new_tie = active * (1.0 - inside)
            lo = jnp.where(more > 0, x, lo)
            c_lo = jnp.where(more > 0, c, c_lo)
            hi = jnp.where(less > 0, x, hi)
            c_hi = jnp.where(less > 0, c, c_hi)
            x_fin = jnp.where(hit > 0, x, x_fin)
            tie = jnp.maximum(tie, new_tie)
            done = jnp.maximum(done, jnp.maximum(hit, new_tie))
            return (lo, hi, c_lo, c_hi, done, tie, x_fin, jnp.max(1.0 - done))

        st = lax.while_loop(cond, body, state0)
        lo, _, _, c_hi, _, tie, x_fin, _ = st
        thr = jnp.where(need > 0, jnp.where(tie > 0, lo, x_fin), 0.5 * NEG)
        thr_ref[...] = jnp.broadcast_to(thr, (8, tq))

        @pl.when(jnp.max(tie) > 0.0)
        def _ties():
            want = jnp.broadcast_to(kf - c_hi, (tk, tq))
            thr_t = jnp.broadcast_to(jnp.where(tie > 0, thr, -NEG), (tk, tq))
            upto = (lax.broadcasted_iota(jnp.int32, (tk, tk), 1)
                    <= lax.broadcasted_iota(jnp.int32, (tk, tk), 0))
            prefix = jnp.where(upto, 1.0, 0.0).astype(BF16)

            def body(j, seen):
                s = score_ref[j]
                eq = s == thr_t
                rank = _dot(prefix, jnp.where(eq, 1.0, 0.0).astype(BF16))
                rank = rank + jnp.broadcast_to(seen, (tk, tq))
                score_ref[j] = jnp.where(eq, jnp.where(rank > want, NEG, s), s)
                return rank[tk - 1:tk, :]

            lax.fori_loop(0, n_chunks, body, jnp.zeros((1, tq), F32))

    _split_heads_t(q_ref, qm_ref, DSA_HEADS)
    m_ref[...] = jnp.full(m_ref.shape, NEG, F32)
    acc_ref[...] = jnp.zeros(acc_ref.shape, F32)
    thr_b = jnp.broadcast_to(thr_ref[0:1, :], (tk, tq))

    def qk(j, s_ref):
        rows = pl.ds(pl.multiple_of(j * tk, tk), tk)
        s_ref[...] = _dot(kv_ref[rows, 0:LANES], qm_ref[...])

    def logits(j, s, kind):
        off = jnp.where(score_ref[j] >= thr_b, 0.0, NEG)
        s = s + jnp.concatenate([off] * DSA_HEADS, axis=1)
        if kind == 'prev':
            s = s + band_ref[0:tk, :]
        elif kind == 'diag':
            s = s + band_ref[tk:2 * tk, :]
        return s

    _attend_chunks(qt, qk, logits, lambda j, p: _dot(vt_ref[j], p), bufs, m_ref, acc_ref)
    _finish_heads_t(acc_ref, o_ref, DSA_HEADS, odd_swapped=False)


def _dsa(qb, qi, aux, kv, ki, band, bsz, seq, n_keys):
    tq = ATT_TILE
    nq = seq // tq
    qrow = lambda w: pl.BlockSpec((tq, w), lambda b, q: (b * nq + q, 0))
    seq_blk = lambda w: pl.BlockSpec((seq, w), lambda b, q: (b, 0))
    return pl.pallas_call(
        functools.partial(_dsa_kernel, n_keys=n_keys, seq=seq), grid=(bsz, nq),
        in_specs=[qrow(DSA_WIDTH), qrow(256), qrow(LANES), seq_blk(256), seq_blk(LANES),
                  _resident((2 * tq, DSA_HEADS * tq))],
        out_specs=qrow(DSA_WIDTH),
        out_shape=jax.ShapeDtypeStruct((bsz * seq, DSA_WIDTH), BF16),
        scratch_shapes=[pltpu.VMEM((nq, tq, tq), F32),
                        pltpu.VMEM((nq, LANES, tq), BF16),
                        pltpu.VMEM((LANES, DSA_HEADS * tq), BF16),
                        pltpu.VMEM((8, DSA_HEADS * tq), F32),
                        pltpu.VMEM((LANES, DSA_HEADS * tq), F32),
                        pltpu.VMEM((8, tq), F32)] + _pipeline_buffers(DSA_HEADS * tq),
        compiler_params=_cparams(("parallel", "arbitrary")), name="dsa",
    )(qb, qi, aux, kv, ki, band)


def _band_kernel(bucket_ref, rb_ref, o_ref):
    bucket = bucket_ref[...]
    tq = bucket.shape[1]
    for h in range(DSA_HEADS):
        acc = jnp.zeros(bucket.shape, F32)
        for k in range(REL_BUCKETS - 1):
            acc = jnp.where(bucket == k, rb_ref[k, h] - rb_ref[REL_BUCKETS - 1, h], acc)
        o_ref[:, h * tq:(h + 1) * tq] = acc


def _t5_bucket_np(dist):
    d = np.maximum(dist, 0)
    df = np.maximum(d, 1).astype(np.float32)
    log_b = REL_MAX_EXACT + (np.log(df / REL_MAX_EXACT) / math.log(REL_MAX_DIST / REL_MAX_EXACT)
                             * (REL_BUCKETS - REL_MAX_EXACT)).astype(np.int32)
    log_b = np.minimum(log_b, REL_BUCKETS - 1)
    return np.where(d < REL_MAX_EXACT, d, log_b).astype(np.int32)


def _rel_band(rel_bias):
    tq = ATT_TILE
    i = np.arange(tq)[None, :]
    j = np.arange(2 * tq)[:, None]
    bucket = jnp.asarray(_t5_bucket_np(i + tq - j))
    return pl.pallas_call(
        _band_kernel,
        in_specs=[pl.BlockSpec(memory_space=pltpu.VMEM), pl.BlockSpec(memory_space=pltpu.SMEM)],
        out_specs=pl.BlockSpec(memory_space=pltpu.VMEM),
        out_shape=jax.ShapeDtypeStruct((2 * tq, DSA_HEADS * tq), F32), name="rel_band",
    )(bucket, rel_bias)


def _cum_kernel(aux_ref, bf_ref, col_ref, row_ref, *, seq):
    x = aux_ref[...] + bf_ref[...]
    logf = -(jnp.maximum(-x, 0.0) + jnp.log1p(jnp.exp(-jnp.abs(x))))
    c = logf.T
    pos = lax.broadcasted_iota(jnp.int32, c.shape, 1)
    shift = 1
    while shift < seq:
        c = c + jnp.where(pos >= shift, pltpu.roll(c, shift, 1), 0.0)
        shift *= 2
    col_ref[...] = c.T
    for j in range(seq // ATT_TILE):
        row_ref[j] = c[0:8, j * ATT_TILE:(j + 1) * ATT_TILE]


def _fox_cum(aux, b_f, bsz, seq):
    nq = seq // ATT_TILE
    bf = jnp.zeros((1, LANES), F32).at[0, AUX_F:AUX_F + FOX_HEADS].set(b_f)
    return pl.pallas_call(
        functools.partial(_cum_kernel, seq=seq), grid=(bsz,),
        in_specs=[pl.BlockSpec((seq, LANES), lambda b: (b, 0)), _resident((1, LANES))],
        out_specs=[pl.BlockSpec((seq, LANES), lambda b: (b, 0)),
                   pl.BlockSpec((None, nq, 8, ATT_TILE), lambda b: (b, 0, 0, 0))],
        out_shape=[jax.ShapeDtypeStruct((bsz * seq, LANES), F32),
                   jax.ShapeDtypeStruct((bsz, nq, 8, ATT_TILE), F32)],
        compiler_params=_cparams(("parallel",)), name="fox_cum")(aux, bf)


def _fox_kernel(q_ref, k_ref, v_ref, ccol_ref, crow_ref, o_ref,
                vte_ref, vto_ref, ck_ref, qm_ref, m_ref, acc_ref, *bufs, seq):
    tq = tk = ATT_TILE
    qt = pl.program_id(1)
    heads = range(FOX_HEADS)

    @pl.when(qt == 0)
    def _():
        top = lax.broadcasted_iota(jnp.int32, (LANES, tk), 0) < HEAD_DIM
        for j in range(seq // tk):
            rows = slice(j * tk, (j + 1) * tk)
            for pr in range(FOX_HEADS // 2):
                v_t = v_ref[rows, pr * LANES:(pr + 1) * LANES].astype(F32).T
                vte_ref[pr, j] = jnp.where(top, v_t, 1.0).astype(BF16)
                vto_ref[pr, j] = jnp.where(top, 1.0, v_t).astype(BF16)
            cc = ccol_ref[rows, :]
            for h in heads:
                ck_ref[h, rows, :] = jnp.broadcast_to(cc[:, AUX_F + h:AUX_F + h + 1], (tk, LANES))

    _split_heads_t(q_ref, qm_ref, FOX_HEADS)
    m_ref[...] = jnp.full(m_ref.shape, NEG, F32)
    acc_ref[...] = jnp.zeros(acc_ref.shape, F32)
    c_q = crow_ref[qt]
    c_q = jnp.concatenate([c_q[AUX_F + h:AUX_F + h + 1, :] for h in heads], axis=1)
    causal_off = jnp.where(lax.broadcasted_iota(jnp.int32, (tk, tq), 0)
                           <= lax.broadcasted_iota(jnp.int32, (tk, tq), 1), 0.0, NEG)

    def qk(j, s_ref):
        rows = pl.ds(pl.multiple_of(j * tk, tk), tk)
        for h in heads:
            cols = slice(h * tq, (h + 1) * tq)
            s_ref[:, cols] = _dot(k_ref[rows, (h // 2) * LANES:(h // 2 + 1) * LANES], qm_ref[:, cols])

    def logits(j, s, kind):
        rows = pl.ds(pl.multiple_of(j * tk, tk), tk)
        c_k = jnp.concatenate([ck_ref[h, rows, :] for h in heads for _ in range(tq // LANES)], axis=1)
        s = s + (jnp.broadcast_to(c_q, s.shape) - c_k)
        if kind == 'diag':
            s = s + jnp.concatenate([causal_off] * FOX_HEADS, axis=1)
        return s

    def pv(j, p):
        return jnp.concatenate(
            [_dot(vte_ref[h // 2, j] if h % 2 == 0 else vto_ref[h // 2, j],
                  p[:, h * tq:(h + 1) * tq]) for h in heads], axis=1)

    _attend_chunks(qt, qk, logits, pv, bufs, m_ref, acc_ref)
    _finish_heads_t(acc_ref, o_ref, FOX_HEADS, odd_swapped=True)


def _fox(qkv, c_col, c_row, bsz, seq):
    tq = ATT_TILE
    nq = seq // tq
    npair = FOX_HEADS // 2
    return pl.pallas_call(
        functools.partial(_fox_kernel, seq=seq), grid=(bsz, nq),
        in_specs=[pl.BlockSpec((tq, FOX_WIDTH), lambda b, q: (b * nq + q, 0)),
                  pl.BlockSpec((seq, FOX_WIDTH), lambda b, q: (b, 1)),
                  pl.BlockSpec((seq, FOX_WIDTH), lambda b, q: (b, 2)),
                  pl.BlockSpec((seq, LANES), lambda b, q: (b, 0)),
                  pl.BlockSpec((None, nq, 8, tq), lambda b, q: (b, 0, 0, 0))],
        out_specs=pl.BlockSpec((tq, FOX_WIDTH), lambda b, q: (b * nq + q, 0)),
        out_shape=jax.ShapeDtypeStruct((bsz * seq, FOX_WIDTH), BF16),
        scratch_shapes=[pltpu.VMEM((npair, nq, LANES, tq), BF16),
                        pltpu.VMEM((npair, nq, LANES, tq), BF16),
                        pltpu.VMEM((FOX_HEADS, seq, LANES), F32),
                        pltpu.VMEM((LANES, FOX_HEADS * tq), BF16),
                        pltpu.VMEM((8, FOX_HEADS * tq), F32),
                        pltpu.VMEM((LANES, FOX_HEADS * tq), F32)] + _pipeline_buffers(FOX_HEADS * tq),
        compiler_params=_cparams(("parallel", "arbitrary")), name="fox",
    )(qkv, qkv, qkv, c_col, c_row)


def _gelu(y):
    return 0.5 * y * (1.0 + jnp.tanh(math.sqrt(2.0 / math.pi) * (y + 0.044715 * (y * y * y))))


def _merge_kernel(x_ref, yc_ref, u_ref, d_ref, wglu_ref, b_ref, c_ref, gate_ref,
                  wa_ref, wb_ref, wc_ref, wo_ref, o_ref, tmp_ref):
    d = D_MODEL
    y = _chunks_to_rows(yc_ref, tmp_ref, SSM_WIDTH) + d_ref[...] * u_ref[...]
    z = _dot(_gelu(y).astype(BF16), wglu_ref[...])
    a = z[:, :SSM_WIDTH] * _sigmoid(z[:, SSM_WIDTH:])
    merged = _sigmoid(gate_ref[:, 0:d].astype(F32)) * _dot(a.astype(BF16), wa_ref[...])
    merged += _sigmoid(gate_ref[:, d:2 * d].astype(F32)) * _dot(b_ref[...], wb_ref[...])
    merged += _sigmoid(gate_ref[:, 2 * d:3 * d].astype(F32)) * _dot(c_ref[...], wc_ref[...])
    o_ref[...] = x_ref[...] + _dot(merged.astype(BF16), wo_ref[...])


def _merge(xt, yc, u, d_skip, w_glu, b, c, gates, wa, wb, wc, wo, tm=1024):
    m, d = xt.shape
    row = lambda w: pl.BlockSpec((tm, w), lambda i: (i, 0))
    return pl.pallas_call(
        _merge_kernel, grid=(m // tm,),
        in_specs=[row(d), pl.BlockSpec((tm // S5_CHUNK, S5_COLS), lambda i: (i, 0)),
                  row(SSM_WIDTH), _resident((1, SSM_WIDTH)),
                  _resident((SSM_WIDTH, 2 * SSM_WIDTH)), row(DSA_WIDTH), row(FOX_WIDTH), row(3 * d),
                  _resident((SSM_WIDTH, d)), _resident((DSA_WIDTH, d)), _resident((FOX_WIDTH, d)),
                  _resident((d, d))],
        out_specs=row(d), out_shape=jax.ShapeDtypeStruct((m, d), F32),
        scratch_shapes=[pltpu.VMEM((SSM_WIDTH // LANES, tm, LANES), F32)],
        compiler_params=_cparams(("parallel",)), name="merge",
    )(xt, yc, u, d_skip.reshape(1, -1), w_glu, b, c, gates, wa, wb, wc, wo)


def _pack_w_in(w_in):
    d = w_in.shape[0]
    splits = (SSM_WIDTH, DSA_WIDTH, DSA_LATENT, IDX_HEADS * IDX_DIM, IDX_DIM, IDX_HEADS,
              FOX_WIDTH, FOX_WIDTH, FOX_WIDTH, FOX_HEADS, D_MODEL, D_MODEL, D_MODEL)
    pts = np.cumsum(splits)[:-1]
    (w_u, w_qb, w_ckv, w_qi, w_ki, w_wi, w_qc, w_kc, w_vc, w_fc, w_ga, w_gb, w_gc) = jnp.split(
        w_in, pts, axis=1)
    scale = HEAD_DIM ** -0.5
    aux = jnp.zeros((d, LANES), F32)
    aux = aux.at[:, AUX_F:AUX_F + FOX_HEADS].set(w_fc).at[:, AUX_W:AUX_W + IDX_HEADS].set(w_wi)
    cols = [w_u, w_qb * scale, w_qi, w_ki, w_ki, w_ckv, aux, w_qc * scale, w_kc, w_vc,
            w_ga, w_gb, w_gc]
    return jnp.concatenate(cols, axis=1).astype(BF16)


def kernel(x, ffn1_norm, ffn1_w_gate, ffn1_w_up, ffn1_w_down, mix_norm, w_in, ssm_lambda_re, ssm_lambda_im, ssm_log_dt, ssm_b_re, ssm_b_im, ssm_c_re, ssm_c_im, ssm_d, ssm_w_glu, dsa_kv_norm, dsa_w_uk, dsa_w_uv, rel_bias, fox_b_f, w_branch_ssm, w_branch_dsa, w_branch_fox, w_out, ffn2_norm, ffn2_w_gate, ffn2_w_up, ffn2_w_down, final_norm):
    bsz, seq, d = x.shape
    depth = w_in.shape[0]
    n_keys = min(TOPK_MAX, seq // 4)
    assert d == D_MODEL and seq % ATT_TILE == 0 and (bsz * seq) % (8 * 512) == 0
    assert n_keys % LANES == 0 and bsz % 8 == 0
    m = bsz * seq
    n_chunks = seq // S5_CHUNK
    xt = x.reshape(m, d)
    band = _rel_band(rel_bias)
    bf = lambda w: w.astype(BF16)
    for l in range(depth):
        xt = _ffn(xt, ffn1_norm[l], bf(ffn1_w_gate[l]), bf(ffn1_w_up[l]), bf(ffn1_w_down[l]))
        w_kv = bf(jnp.concatenate([dsa_w_uk[l], dsa_w_uk[l], dsa_w_uv[l], dsa_w_uv[l]], axis=1))
        u, qb, qi, ki, kv, aux, qkv, gates, u8 = _inproj(
            xt, mix_norm[l], _pack_w_in(w_in[l]), dsa_kv_norm[l], w_kv)
        m_intra, m_sum, m_out, a_re, a_im = _s5_matrices(
            ssm_lambda_re[l], ssm_lambda_im[l], ssm_log_dt[l], ssm_b_re[l], ssm_b_im[l],
            ssm_c_re[l], ssm_c_im[l])
        y_intra, p = _s5a(u8, m_intra, m_sum)
        h_re, h_im = _s5b(p, a_re, a_im, n_chunks, nb=8)
        yc = _s5c(y_intra, h_re, h_im, m_out)
        b_out = _dsa(qb, qi, aux, kv, ki, band, bsz, seq, n_keys)
        c_col, c_row = _fox_cum(aux, fox_b_f[l], bsz, seq)
        c_out = _fox(qkv, c_col, c_row, bsz, seq)
        xt = _merge(xt, yc, u, ssm_d[l], bf(ssm_w_glu[l]), b_out, c_out, gates,
                    bf(w_branch_ssm[l]), bf(w_branch_dsa[l]), bf(w_branch_fox[l]), bf(w_out[l]))
        last = final_norm if l == depth - 1 else None
        xt = _ffn(xt, ffn2_norm[l], bf(ffn2_w_gate[l]), bf(ffn2_w_up[l]), bf(ffn2_w_down[l]),
                  final_norm=last)
    return xt.reshape(bsz, seq, d)
```

```python
import functools
import math

import numpy as np
import jax
import jax.numpy as jnp
from jax import lax
from jax.experimental import pallas as pl
from jax.experimental.pallas import tpu as pltpu

F32 = jnp.float32
BF16 = jnp.bfloat16

D_MODEL = 1024
SSM_WIDTH = D_MODEL // 4
SSM_GROUP = 16
SSM_GROUPS = SSM_WIDTH // SSM_GROUP
SSM_STATE = 64
HEAD_DIM = 64
DSA_HEADS = 6
DSA_WIDTH = DSA_HEADS * HEAD_DIM
DSA_LATENT = 2 * HEAD_DIM
IDX_HEADS = 4
IDX_DIM = 64
TOPK_MAX = 256
FOX_HEADS = 6
FOX_WIDTH = FOX_HEADS * HEAD_DIM
REL_BUCKETS = 32
REL_MAX_EXACT = 16
REL_MAX_DIST = 128
FFN_HIDDEN = 2816
EPS = 1e-6
NEG = -1e30

LANES = 128
S5_CHUNK = 8
S5_COLS = S5_CHUNK * SSM_WIDTH
S5_STATE = 2 * SSM_GROUPS * SSM_STATE
ATT_TILE = 256
SEARCH_UNROLL = 4
VMEM_LIMIT = 56 * 1024 * 1024

_SEG = dict(u=256, qb=384, qi=256, ki=128, ckv=128, aux=128, qkv=1152, g=3072)
AUX_F = 0
AUX_W = 8


def _cparams(sem):
    return pltpu.CompilerParams(dimension_semantics=sem, vmem_limit_bytes=VMEM_LIMIT)


def _resident(shape):
    nd = len(shape)
    return pl.BlockSpec(shape, lambda *_: (0,) * nd, pipeline_mode=pl.Buffered(1))


def _rms(x, g):
    return x * lax.rsqrt(jnp.mean(x * x, axis=-1, keepdims=True) + EPS) * g


def _sigmoid(x):
    return 1.0 / (1.0 + jnp.exp(-x))


def _dot(a, b):
    return jnp.dot(a, b, preferred_element_type=F32)


def _ffn_body(x, g_ref, wg_ref, wu_ref, wd_ref, fc):
    h = _rms(x, g_ref[...]).astype(BF16)
    acc = None
    for c in range(FFN_HIDDEN // fc):
        sl = slice(c * fc, (c + 1) * fc)
        gt = _dot(h, wg_ref[:, sl])
        up = _dot(h, wu_ref[:, sl])
        a = (gt * _sigmoid(gt) * up).astype(BF16)
        d = _dot(a, wd_ref[sl, :])
        acc = d if acc is None else acc + d
    return x + 0.5 * acc


def _ffn_kernel(x_ref, g_ref, wg_ref, wu_ref, wd_ref, o_ref, *, fc):
    o_ref[...] = _ffn_body(x_ref[...], g_ref, wg_ref, wu_ref, wd_ref, fc)


def _ffn_final_kernel(x_ref, g_ref, wg_ref, wu_ref, wd_ref, fn_ref, o_ref, *, fc):
    y = _ffn_body(x_ref[...], g_ref, wg_ref, wu_ref, wd_ref, fc)
    o_ref[...] = _rms(y, fn_ref[...])


def _ffn(xt, norm, wg, wu, wd, final_norm=None, tm=1024, fc=256):
    m, d = xt.shape
    f = wg.shape[1]
    row = pl.BlockSpec((tm, d), lambda i: (i, 0))
    vec = _resident((1, d))
    in_specs = [row, vec, _resident((d, f)), _resident((d, f)), _resident((f, d))]
    args = [xt, norm.reshape(1, d), wg, wu, wd]
    if final_norm is None:
        body = functools.partial(_ffn_kernel, fc=fc)
    else:
        body = functools.partial(_ffn_final_kernel, fc=fc)
        in_specs.append(vec)
        args.append(final_norm.reshape(1, d))
    return pl.pallas_call(
        body, grid=(m // tm,), in_specs=in_specs, out_specs=row,
        out_shape=jax.ShapeDtypeStruct((m, d), F32),
        compiler_params=_cparams(("parallel",)), name="ffn")(*args)


def _rows_to_chunks(x, tmp_ref, o_ref):
    n, w = x.shape
    for hh in range(w // LANES):
        tmp_ref[hh] = x[:, hh * LANES:(hh + 1) * LANES]
    for i in range(S5_CHUNK):
        for hh in range(w // LANES):
            o_ref[:, i * w + hh * LANES:i * w + (hh + 1) * LANES] = (
                tmp_ref[hh, pl.ds(i, n // S5_CHUNK, stride=S5_CHUNK), :])


def _chunks_to_rows(x_ref, tmp_ref, w):
    n = x_ref.shape[0]
    for i in range(S5_CHUNK):
        for hh in range(w // LANES):
            tmp_ref[hh, pl.ds(i, n, stride=S5_CHUNK), :] = (
                x_ref[:, i * w + hh * LANES:i * w + (hh + 1) * LANES])
    return jnp.concatenate([tmp_ref[hh] for hh in range(w // LANES)], axis=1)


def _inproj_kernel(x_ref, g_ref, w_ref, kvn_ref, wkv_ref,
                   u_ref, qb_ref, qi_ref, ki_ref, kv_ref, aux_ref, qkv_ref, gate_ref, u8_ref,
                   tmp_ref):
    h = _rms(x_ref[...], g_ref[...]).astype(BF16)
    outs = dict(u=u_ref, qb=qb_ref, qi=qi_ref, ki=ki_ref, aux=aux_ref, qkv=qkv_ref)
    off = 0
    for group in (("u", "qb", "qi", "ki"), ("ckv", "aux", "qkv")):
        width = sum(_SEG[name] for name in group)
        r = _dot(h, w_ref[:, off:off + width])
        off += width
        lo = 0
        for name in group:
            piece = r[:, lo:lo + _SEG[name]]
            lo += _SEG[name]
            if name == "ckv":
                c = _rms(piece, kvn_ref[...]).astype(BF16)
                kv_ref[...] = _dot(c, wkv_ref[...]).astype(BF16)
            else:
                outs[name][...] = piece.astype(outs[name].dtype)
            if name == "u":
                _rows_to_chunks(piece, tmp_ref, u8_ref)
    for c in range(_SEG["g"] // D_MODEL):
        cols = slice(c * D_MODEL, (c + 1) * D_MODEL)
        gate_ref[:, cols] = _dot(h, w_ref[:, off + c * D_MODEL:off + (c + 1) * D_MODEL]).astype(BF16)


def _inproj(xt, norm, w_all, kv_norm, w_kv, tm=1024):
    m, d = xt.shape
    wtot = w_all.shape[1]
    widths = dict(u=(256, F32), qb=(384, BF16), qi=(256, BF16), ki=(128, BF16),
                  kv=(256, BF16), aux=(128, F32), qkv=(1152, BF16), g=(3072, BF16))
    out_shape = [jax.ShapeDtypeStruct((m, w), dt) for w, dt in widths.values()]
    out_specs = [pl.BlockSpec((tm, w), lambda i: (i, 0)) for w, _ in widths.values()]
    out_shape.append(jax.ShapeDtypeStruct((m // S5_CHUNK, S5_COLS), F32))
    out_specs.append(pl.BlockSpec((tm // S5_CHUNK, S5_COLS), lambda i: (i, 0)))
    return pl.pallas_call(
        _inproj_kernel, grid=(m // tm,),
        in_specs=[pl.BlockSpec((tm, d), lambda i: (i, 0)), _resident((1, d)),
                  _resident((d, wtot)), _resident((1, DSA_LATENT)),
                  _resident((DSA_LATENT, 256))],
        out_specs=out_specs, out_shape=out_shape,
        scratch_shapes=[pltpu.VMEM((SSM_WIDTH // LANES, tm, LANES), F32)],
        compiler_params=_cparams(("parallel",)), name="inproj",
    )(xt, norm.reshape(1, d), w_all, kv_norm.reshape(1, DSA_LATENT), w_kv)


S5_SLABS = S5_STATE // LANES
S5_SCAN_SLABS = 4


def _s5a_kernel(u_ref, mi_ref, ms_ref, yi_ref, p_ref):
    u = u_ref[...].astype(BF16)
    yi_ref[...] = _dot(u, mi_ref[...])
    p = _dot(u, ms_ref[...])
    for k in range(S5_SLABS):
        p_ref[k] = p[:, k * LANES:(k + 1) * LANES]


def _s5a(u8, m_intra, m_sum, tr=512):
    r = u8.shape[0]
    row = pl.BlockSpec((tr, S5_COLS), lambda i: (i, 0))
    slabs = pl.BlockSpec((S5_SLABS, tr, LANES), lambda i: (0, i, 0))
    return pl.pallas_call(
        _s5a_kernel, grid=(r // tr,),
        in_specs=[row, _resident((S5_COLS, S5_COLS)), _resident((S5_COLS, S5_STATE))],
        out_specs=[row, slabs],
        out_shape=[jax.ShapeDtypeStruct((r, S5_COLS), F32),
                   jax.ShapeDtypeStruct((S5_SLABS, r, LANES), F32)],
        compiler_params=_cparams(("parallel",)), name="s5_chunk")(u8, m_intra, m_sum)


def _s5b_kernel(pr_ref, pi_ref, ar_ref, ai_ref, hr_ref, hi_ref, *, nb, nc):
    ar = ar_ref[...]
    ai = ai_ref[...]
    ns = S5_SCAN_SLABS

    def step(c, carry):
        hr, hi = carry
        rows = pl.ds(c, nb, stride=nc)
        for k in range(ns):
            hr_ref[k, rows, :] = hr[:, k * LANES:(k + 1) * LANES]
            hi_ref[k, rows, :] = hi[:, k * LANES:(k + 1) * LANES]
        pr = jnp.concatenate([pr_ref[k, rows, :] for k in range(ns)], axis=1)
        pi = jnp.concatenate([pi_ref[k, rows, :] for k in range(ns)], axis=1)
        return (ar * hr - ai * hi + pr, ar * hi + ai * hr + pi)

    z = jnp.zeros((nb, ns * LANES), F32)
    lax.fori_loop(0, nc, step, (z, z))


def _s5b(p, a_re, a_im, nc, nb):
    r = p.shape[1]
    ns = S5_SCAN_SLABS
    half = S5_SLABS // 2
    re_blk = pl.BlockSpec((ns, nb * nc, LANES), lambda i, j: (j, i, 0))
    im_blk = pl.BlockSpec((ns, nb * nc, LANES), lambda i, j: (j + half // ns, i, 0))
    a_blk = pl.BlockSpec((1, ns * LANES), lambda i, j: (0, j))
    return pl.pallas_call(
        functools.partial(_s5b_kernel, nb=nb, nc=nc), grid=(r // (nb * nc), half // ns),
        in_specs=[re_blk, im_blk, a_blk, a_blk],
        out_specs=[re_blk, re_blk],
        out_shape=[jax.ShapeDtypeStruct((half, r, LANES), F32)] * 2,
        compiler_params=_cparams(("parallel", "parallel")), name="s5_scan",
    )(p, p, a_re, a_im)


def _s5c_kernel(yi_ref, hr_ref, hi_ref, mo_ref, y_ref):
    half = S5_STATE // 2
    hr = jnp.concatenate([hr_ref[k] for k in range(S5_SLABS // 2)], axis=1)
    hi = jnp.concatenate([hi_ref[k] for k in range(S5_SLABS // 2)], axis=1)
    y = yi_ref[...]
    y = y + _dot(hr.astype(BF16), mo_ref[:half, :])
    y = y + _dot(hi.astype(BF16), mo_ref[half:, :])
    y_ref[...] = y


def _s5c(yi, h_re, h_im, m_out, tr=512):
    r = yi.shape[0]
    row = pl.BlockSpec((tr, S5_COLS), lambda i: (i, 0))
    slabs = pl.BlockSpec((S5_SLABS // 2, tr, LANES), lambda i: (0, i, 0))
    return pl.pallas_call(
        _s5c_kernel, grid=(r // tr,),
        in_specs=[row, slabs, slabs, _resident((S5_STATE, S5_COLS))],
        out_specs=row, out_shape=jax.ShapeDtypeStruct((r, S5_COLS), F32),
        compiler_params=_cparams(("parallel",)), name="s5_out")(yi, h_re, h_im, m_out)


def _toeplitz_kernel(lag_ref, o_ref):
    keep = pl.program_id(1) >= pl.program_id(0)
    o_ref[...] = jnp.where(keep, lag_ref[...], jnp.zeros_like(lag_ref))


def _block_toeplitz(lag_blocks):
    t, w, _ = lag_blocks.shape
    return pl.pallas_call(
        _toeplitz_kernel, grid=(t, t),
        in_specs=[pl.BlockSpec((None, w, w), lambda j, i: (jnp.maximum(i - j, 0), 0, 0))],
        out_specs=pl.BlockSpec((w, w), lambda j, i: (j, i)),
        out_shape=jax.ShapeDtypeStruct((t * w, t * w), lag_blocks.dtype), name="s5_toeplitz",
    )(lag_blocks)


def _s5_matrices(lam_re, lam_im, log_dt, b_re, b_im, c_re, c_im):
    hp = lax.Precision.HIGHEST
    t = S5_CHUNK
    dt = jnp.exp(log_dt)[:, None]
    lr, li = lam_re, lam_im
    mag = jnp.exp(lr * dt)
    ab_re, ab_im = mag * jnp.cos(li * dt), mag * jnp.sin(li * dt)
    den = lr * lr + li * li
    nr, ni = ab_re - 1.0, ab_im
    s_re = (nr * lr + ni * li) / den
    s_im = (ni * lr - nr * li) / den
    bb_re = s_re[..., None] * b_re - s_im[..., None] * b_im
    bb_im = s_re[..., None] * b_im + s_im[..., None] * b_re
    pr, pi = [jnp.ones_like(ab_re)], [jnp.zeros_like(ab_re)]
    for _ in range(t):
        pr.append(pr[-1] * ab_re - pi[-1] * ab_im)
        pi.append(pr[-2] * ab_im + pi[-1] * ab_re)
    pw_re, pw_im = jnp.stack(pr), jnp.stack(pi)
    lo_re, lo_im = jnp.stack(pr[:t]), jnp.stack(pi[:t])
    ab_b_re = lo_re[..., None] * bb_re - lo_im[..., None] * bb_im
    ab_b_im = lo_re[..., None] * bb_im + lo_im[..., None] * bb_re
    kern = (jnp.einsum('gpn,dgnq->dgpq', c_re, ab_b_re, precision=hp)
            - jnp.einsum('gpn,dgnq->dgpq', c_im, ab_b_im, precision=hp))
    eye_g = jnp.eye(SSM_GROUPS, dtype=F32)
    rows = np.arange(SSM_WIDTH)
    expand = jnp.asarray((rows[None, :] % SSM_GROUP == np.arange(SSM_GROUP)[:, None]).astype(np.float32))
    same_group = jnp.asarray(rows[:, None] // SSM_GROUP == rows[None, :] // SSM_GROUP)
    lag_rows = kern.transpose(0, 1, 3, 2).reshape(t, SSM_WIDTH, SSM_GROUP)
    lag_blocks = jnp.where(same_group, jnp.einsum('drp,pc->drc', lag_rows, expand, precision=hp), 0.0)
    m_intra = _block_toeplitz(lag_blocks.astype(BF16))
    hi_re, hi_im = jnp.stack(pr[t - 1::-1]), jnp.stack(pi[t - 1::-1])
    sb_re = hi_re[..., None] * bb_re - hi_im[..., None] * bb_im
    sb_im = hi_re[..., None] * bb_im + hi_im[..., None] * bb_re
    m_sum_re = jnp.einsum('jgnq,gh->jgqhn', sb_re, eye_g, precision=hp)
    m_sum_im = jnp.einsum('jgnq,gh->jgqhn', sb_im, eye_g, precision=hp)
    m_sum = jnp.concatenate([m_sum_re.reshape(S5_COLS, -1), m_sum_im.reshape(S5_COLS, -1)], axis=1)
    up_re, up_im = jnp.stack(pr[1:])[:, :, None, :], jnp.stack(pi[1:])[:, :, None, :]
    w_re = c_re[None] * up_re - c_im[None] * up_im
    w_im = c_re[None] * up_im + c_im[None] * up_re
    m_out_re = jnp.einsum('igpn,gh->gnihp', w_re, eye_g, precision=hp)
    m_out_im = jnp.einsum('igpn,gh->gnihp', -w_im, eye_g, precision=hp)
    m_out = jnp.concatenate([m_out_re.reshape(-1, S5_COLS), m_out_im.reshape(-1, S5_COLS)], axis=0)
    a_re = pw_re[t].reshape(1, -1)
    a_im = pw_im[t].reshape(1, -1)
    return m_intra.astype(BF16), m_sum.astype(BF16), m_out.astype(BF16), a_re, a_im


def _fold8(x, op):
    parts = [x[r:r + 8] for r in range(0, x.shape[0], 8)]
    while len(parts) > 1:
        parts = [op(parts[i], parts[i + 1]) for i in range(0, len(parts), 2)]
    return parts[0]


def _split_heads_t(q_ref, qt_ref, n_heads):
    tq = q_ref.shape[0]
    top = lax.broadcasted_iota(jnp.int32, (LANES, tq), 0) < HEAD_DIM
    for pr in range(n_heads // 2):
        pair_t = q_ref[:, pr * LANES:(pr + 1) * LANES].astype(F32).T
        qt_ref[:, 2 * pr * tq:(2 * pr + 1) * tq] = jnp.where(top, pair_t, 0.0).astype(BF16)
        qt_ref[:, (2 * pr + 1) * tq:(2 * pr + 2) * tq] = jnp.where(top, 0.0, pair_t).astype(BF16)


def _softmax_stage(s, m_ref, alpha_ref, p_ref):
    m_old = m_ref[...]
    m_new = jnp.maximum(m_old, jnp.max(_fold8(s, jnp.maximum), axis=0, keepdims=True))
    alpha_ref[...] = jnp.exp(m_old - m_new)
    p_ref[...] = jnp.exp(s - jnp.broadcast_to(m_new[0:1], s.shape)).astype(BF16)
    m_ref[...] = m_new


def _value_stage(pv, alpha_ref, p_ref, acc_ref):
    acc = acc_ref[...]
    acc_ref[...] = acc * jnp.broadcast_to(alpha_ref[0:1, :], acc.shape) + pv(p_ref[...])


def _attend_chunks(qt, qk, logits, pv, bufs, m_ref, acc_ref):
    s_a, s_b, p_a, p_b, al_a, al_b = bufs

    def plain(j, kind):
        qk(j, s_a)
        _softmax_stage(logits(j, s_a[...], kind), m_ref, al_a, p_a)
        _value_stage(lambda p: pv(j, p), al_a, p_a, acc_ref)

    @pl.when(qt == 0)
    def _():
        plain(0, 'diag')

    @pl.when(qt >= 1)
    def _():
        n_far = qt - 1
        first = n_far % 2

        @pl.when(first == 1)
        def _():
            plain(0, 'far')

        qk(first, s_a)
        p_b[...] = jnp.zeros(p_b.shape, p_b.dtype)
        al_b[...] = jnp.ones(al_b.shape, al_b.dtype)

        def pair(c, kind0, kind1, last):
            qk(c + 1, s_b)
            _value_stage(lambda p: pv(jnp.maximum(c - 1, 0), p), al_b, p_b, acc_ref)
            _softmax_stage(logits(c, s_a[...], kind0), m_ref, al_a, p_a)
            if not last:
                qk(c + 2, s_a)
            _value_stage(lambda p: pv(c, p), al_a, p_a, acc_ref)
            _softmax_stage(logits(c + 1, s_b[...], kind1), m_ref, al_b, p_b)

        def body(i, carry):
            pair(first + 2 * i, 'far', 'far', False)
            return carry

        lax.fori_loop(0, (n_far - first) // 2, body, 0)
        pair(qt - 1, 'prev', 'diag', True)
        _value_stage(lambda p: pv(qt, p), al_b, p_b, acc_ref)


def _pipeline_buffers(width):
    return ([pltpu.VMEM((ATT_TILE, width), F32)] * 2 + [pltpu.VMEM((ATT_TILE, width), BF16)] * 2
            + [pltpu.VMEM((8, width), F32)] * 2)


def _finish_heads_t(acc_ref, o_ref, n_heads, odd_swapped):
    hd = HEAD_DIM
    tq = o_ref.shape[0]
    for pr in range(n_heads // 2):
        ae = acc_ref[:, 2 * pr * tq:(2 * pr + 1) * tq]
        ao = acc_ref[:, (2 * pr + 1) * tq:(2 * pr + 2) * tq]
        oe = ae[0:hd] / ae[hd:2 * hd]
        oo = ao[hd:2 * hd] / ao[0:hd] if odd_swapped else ao[0:hd] / ao[hd:2 * hd]
        pair = jnp.concatenate([oe, oo], axis=0)
        o_ref[:, pr * LANES:(pr + 1) * LANES] = pair.T.astype(o_ref.dtype)


def _dsa_kernel(q_ref, qi_ref, aux_ref, kv_ref, ki_ref, band_ref, o_ref,
                score_ref, vt_ref, qm_ref, m_ref, acc_ref, thr_ref, *bufs, n_keys, seq):
    tq = tk = ATT_TILE
    qt = pl.program_id(1)
    n_chunks = qt + 1
    kf = float(n_keys)

    @pl.when(qt == 0)
    def _():
        top = lax.broadcasted_iota(jnp.int32, (LANES, tk), 0) < HEAD_DIM
        for j in range(seq // tk):
            vv = kv_ref[j * tk:(j + 1) * tk, LANES:2 * LANES].astype(F32)
            vt_ref[j] = jnp.where(top, vv.T, 1.0).astype(BF16)

    _split_heads_t(qi_ref, qm_ref, IDX_HEADS)
    aux_t = aux_ref[...].T
    head_w = jnp.concatenate(
        [aux_t[AUX_W + h:AUX_W + h + 1, :] for h in range(IDX_HEADS)], axis=1
    ) * (IDX_HEADS ** -0.5 * IDX_DIM ** -0.5)
    key_minus_query = (lax.broadcasted_iota(jnp.int32, (tk, tq), 0)
                       - lax.broadcasted_iota(jnp.int32, (tk, tq), 1))

    def score_body(j, carry):
        mx, mn = carry
        kc = ki_ref[pl.ds(pl.multiple_of(j * tk, tk), tk), :]
        d = _dot(kc, qm_ref[:, 0:IDX_HEADS * tq])
        d = jnp.broadcast_to(head_w, d.shape) * jnp.maximum(d, 0.0)
        sc = (d[:, 0:tq] + d[:, tq:2 * tq]) + (d[:, 2 * tq:3 * tq] + d[:, 3 * tq:4 * tq])
        causal = key_minus_query <= (qt - j) * tq
        score_ref[j] = jnp.where(causal, sc, NEG)
        mx = jnp.maximum(mx, _fold8(jnp.where(causal, sc, NEG), jnp.maximum))
        mn = jnp.minimum(mn, _fold8(jnp.where(causal, sc, -NEG), jnp.minimum))
        return mx, mn

    mx, mn = lax.fori_loop(0, n_chunks, score_body,
                           (jnp.full((8, tq), NEG, F32), jnp.full((8, tq), -NEG, F32)))

    t_q = qt * tq + lax.broadcasted_iota(jnp.int32, (1, tq), 1)
    thr_ref[...] = jnp.full((8, tq), 0.5 * NEG, F32)

    def count(pred):
        def body(j, acc):
            return acc + _fold8(pred(score_ref[j]), jnp.add)
        acc = lax.fori_loop(0, n_chunks, body, jnp.zeros((8, tq), F32))
        return jnp.sum(acc, axis=0, keepdims=True)

    def count_ge(x):
        xb = jnp.broadcast_to(x, (tk, tq))
        return count(lambda s: jnp.where(s >= xb, 1.0, 0.0))

    @pl.when((qt + 1) * tq > n_keys)
    def _search():
        need = (t_q >= n_keys).astype(F32)
        lo0 = jnp.min(mn, axis=0, keepdims=True)
        hi0 = jnp.max(mx, axis=0, keepdims=True)
        c_top = count_ge(hi0)
        c_ge0 = count_ge(jnp.zeros((1, tq), F32))
        c_gt0 = count(lambda s: jnp.where(s > 0.0, 1.0, 0.0))
        n_valid = (t_q + 1).astype(F32)
        top_hit = c_top == kf
        top_tie = c_top > kf
        zero_hit = c_ge0 == kf
        zero_tie = (c_gt0 < kf) & (c_ge0 > kf)
        positive = c_gt0 >= kf
        hi_neg = hi0 < 0.0
        lo1 = jnp.where(positive, jnp.maximum(lo0, 0.0), lo0)
        c_lo1 = jnp.where(positive, c_ge0, n_valid)
        hi1 = jnp.where(positive | hi_neg, hi0, 0.0)
        c_hi1 = jnp.where(positive | hi_neg, c_top, c_ge0)
        hit0 = top_hit | ((~top_tie) & zero_hit)
        tie0 = top_tie | ((~top_hit) & (~zero_hit) & zero_tie)
        x0 = jnp.where(top_hit, hi0, 0.0)
        tie_val = jnp.where(top_tie, hi0, 0.0)
        above0 = jnp.where(top_tie, 0.0, c_gt0)
        hit0 = need * hit0.astype(F32)
        tie0 = need * tie0.astype(F32) * (1.0 - hit0)
        done0 = jnp.maximum(1.0 - need, jnp.maximum(hit0, tie0))
        lo1 = jnp.where(tie0 > 0, tie_val, lo1)
        c_hi1 = jnp.where(tie0 > 0, above0, c_hi1)
        state0 = (lo1, hi1, c_lo1, c_hi1, done0, tie0, x0, jnp.max(1.0 - done0))

        def cond(st):
            return st[7] > 0.0

        def body(st):
            for _ in range(SEARCH_UNROLL):
                st = step(st)
            return st

        def step(st):
            lo, hi, c_lo, c_hi, done, tie, x_fin, _ = st
            x = 0.5 * lo + 0.5 * hi
            inside = ((x > lo) & (x < hi)).astype(F32)
            active = 1.0 - done
            probe = active * inside
            c = count_ge(jnp.where(probe > 0, x, x_fin))
            hit = probe * (c == kf).astype(F32)
            more = probe * (c > kf).astype(F32)
            less = probe * (c < kf).astype(F32)
            new_tie = active * (1.0 - inside)
            lo = jnp.where(more > 0, x, lo)
            c_lo = jnp.where(more > 0, c, c_lo)
            hi = jnp.where(less > 0, x, hi)
            c_hi = jnp.where(less > 0, c, c_hi)
            x_fin = jnp.where(hit > 0, x, x_fin)
            tie = jnp.maximum(tie, new_tie)
            done = jnp.maximum(done, jnp.maximum(hit, new_tie))
            return (lo, hi, c_lo, c_hi, done, tie, x_fin, jnp.max(1.0 - done))

        st = lax.while_loop(cond, body, state0)
        lo, _, _, c_hi, _, tie, x_fin, _ = st
        thr = jnp.where(need > 0, jnp.where(tie > 0, lo, x_fin), 0.5 * NEG)
        thr_ref[...] = jnp.broadcast_to(thr, (8, tq))

        @pl.when(jnp.max(tie) > 0.0)
        def _ties():
            want = jnp.broadcast_to(kf - c_hi, (tk, tq))
            thr_t = jnp.broadcast_to(jnp.where(tie > 0, thr, -NEG), (tk, tq))
            upto = (lax.broadcasted_iota(jnp.int32, (tk, tk), 1)
                    <= lax.broadcasted_iota(jnp.int32, (tk, tk), 0))
            prefix = jnp.where(upto, 1.0, 0.0).astype(BF16)

            def body(j, seen):
                s = score_ref[j]
                eq = s == thr_t
                rank = _dot(prefix, jnp.where(eq, 1.0, 0.0).astype(BF16))
                rank = rank + jnp.broadcast_to(seen, (tk, tq))
                score_ref[j] = jnp.where(eq, jnp.where(rank > want, NEG, s), s)
                return rank[tk - 1:tk, :]

            lax.fori_loop(0, n_chunks, body, jnp.zeros((1, tq), F32))

    _split_heads_t(q_ref, qm_ref, DSA_HEADS)
    m_ref[...] = jnp.full(m_ref.shape, NEG, F32)
    acc_ref[...] = jnp.zeros(acc_ref.shape, F32)
    thr_b = jnp.broadcast_to(thr_ref[0:1, :], (tk, tq))

    def qk(j, s_ref):
        rows = pl.ds(pl.multiple_of(j * tk, tk), tk)
        s_ref[...] = _dot(kv_ref[rows, 0:LANES], qm_ref[...])

    def logits(j, s, kind):
        off = jnp.where(score_ref[j] >= thr_b, 0.0, NEG)
        s = s + jnp.concatenate([off] * DSA_HEADS, axis=1)
        if kind == 'prev':
            s = s + band_ref[0:tk, :]
        elif kind == 'diag':
            s = s + band_ref[tk:2 * tk, :]
        return s

    _attend_chunks(qt, qk, logits, lambda j, p: _dot(vt_ref[j], p), bufs, m_ref, acc_ref)
    _finish_heads_t(acc_ref, o_ref, DSA_HEADS, odd_swapped=False)


def _dsa(qb, qi, aux, kv, ki, band, bsz, seq, n_keys):
    tq = ATT_TILE
    nq = seq // tq
    qrow = lambda w: pl.BlockSpec((tq, w), lambda b, q: (b * nq + q, 0))
    seq_blk = lambda w: pl.BlockSpec((seq, w), lambda b, q: (b, 0))
    return pl.pallas_call(
        functools.partial(_dsa_kernel, n_keys=n_keys, seq=seq), grid=(bsz, nq),
        in_specs=[qrow(DSA_WIDTH), qrow(256), qrow(LANES), seq_blk(256), seq_blk(LANES),
                  _resident((2 * tq, DSA_HEADS * tq))],
        out_specs=qrow(DSA_WIDTH),
        out_shape=jax.ShapeDtypeStruct((bsz * seq, DSA_WIDTH), BF16),
        scratch_shapes=[pltpu.VMEM((nq, tq, tq), F32),
                        pltpu.VMEM((nq, LANES, tq), BF16),
                        pltpu.VMEM((LANES, DSA_HEADS * tq), BF16),
                        pltpu.VMEM((8, DSA_HEADS * tq), F32),
                        pltpu.VMEM((LANES, DSA_HEADS * tq), F32),
                        pltpu.VMEM((8, tq), F32)] + _pipeline_buffers(DSA_HEADS * tq),
        compiler_params=_cparams(("parallel", "arbitrary")), name="dsa",
    )(qb, qi, aux, kv, ki, band)


def _band_kernel(bucket_ref, rb_ref, o_ref):
    bucket = bucket_ref[...]
    tq = bucket.shape[1]
    for h in range(DSA_HEADS):
        acc = jnp.zeros(bucket.shape, F32)
        for k in range(REL_BUCKETS - 1):
            acc = jnp.where(bucket == k, rb_ref[k, h] - rb_ref[REL_BUCKETS - 1, h], acc)
        o_ref[:, h * tq:(h + 1) * tq] = acc


def _t5_bucket_np(dist):
    d = np.maximum(dist, 0)
    df = np.maximum(d, 1).astype(np.float32)
    log_b = REL_MAX_EXACT + (np.log(df / REL_MAX_EXACT) / math.log(REL_MAX_DIST / REL_MAX_EXACT)
                             * (REL_BUCKETS - REL_MAX_EXACT)).astype(np.int32)
    log_b = np.minimum(log_b, REL_BUCKETS - 1)
    return np.where(d < REL_MAX_EXACT, d, log_b).astype(np.int32)


def _rel_band(rel_bias):
    tq = ATT_TILE
    i = np.arange(tq)[None, :]
    j = np.arange(2 * tq)[:, None]
    bucket = jnp.asarray(_t5_bucket_np(i + tq - j))
    return pl.pallas_call(
        _band_kernel,
        in_specs=[pl.BlockSpec(memory_space=pltpu.VMEM), pl.BlockSpec(memory_space=pltpu.SMEM)],
        out_specs=pl.BlockSpec(memory_space=pltpu.VMEM),
        out_shape=jax.ShapeDtypeStruct((2 * tq, DSA_HEADS * tq), F32), name="rel_band",
    )(bucket, rel_bias)


def _cum_kernel(aux_ref, bf_ref, col_ref, row_ref, *, seq):
    x = aux_ref[...] + bf_ref[...]
    logf = -(jnp.maximum(-x, 0.0) + jnp.log1p(jnp.exp(-jnp.abs(x))))
    c = logf.T[0:8]
    pos = lax.broadcasted_iota(jnp.int32, c.shape, 1)
    shift = 1
    while shift < seq:
        c = c + jnp.where(pos >= shift, pltpu.roll(c, shift, 1), 0.0)
        shift *= 2
    col_ref[...] = jnp.concatenate([c, jnp.zeros((LANES - 8, seq), F32)], axis=0).T
    for j in range(seq // ATT_TILE):
        row_ref[j] = c[:, j * ATT_TILE:(j + 1) * ATT_TILE]


def _fox_cum(aux, b_f, bsz, seq):
    nq = seq // ATT_TILE
    bf = jnp.zeros((1, LANES), F32).at[0, AUX_F:AUX_F + FOX_HEADS].set(b_f)
    return pl.pallas_call(
        functools.partial(_cum_kernel, seq=seq), grid=(bsz,),
        in_specs=[pl.BlockSpec((seq, LANES), lambda b: (b, 0)), _resident((1, LANES))],
        out_specs=[pl.BlockSpec((seq, LANES), lambda b: (b, 0)),
                   pl.BlockSpec((None, nq, 8, ATT_TILE), lambda b: (b, 0, 0, 0))],
        out_shape=[jax.ShapeDtypeStruct((bsz * seq, LANES), F32),
                   jax.ShapeDtypeStruct((bsz, nq, 8, ATT_TILE), F32)],
        compiler_params=_cparams(("parallel",)), name="fox_cum")(aux, bf)


def _fox_kernel(q_ref, k_ref, v_ref, ccol_ref, crow_ref, o_ref,
                vte_ref, vto_ref, ck_ref, qm_ref, m_ref, acc_ref, *bufs, seq):
    tq = tk = ATT_TILE
    qt = pl.program_id(1)
    heads = range(FOX_HEADS)

    @pl.when(qt == 0)
    def _():
        top = lax.broadcasted_iota(jnp.int32, (LANES, tk), 0) < HEAD_DIM
        for j in range(seq // tk):
            rows = slice(j * tk, (j + 1) * tk)
            for pr in range(FOX_HEADS // 2):
                v_t = v_ref[rows, pr * LANES:(pr + 1) * LANES].astype(F32).T
                vte_ref[pr, j] = jnp.where(top, v_t, 1.0).astype(BF16)
                vto_ref[pr, j] = jnp.where(top, 1.0, v_t).astype(BF16)
            cc = ccol_ref[rows, :]
            for h in heads:
                ck_ref[h, rows, :] = jnp.broadcast_to(cc[:, AUX_F + h:AUX_F + h + 1], (tk, LANES))

    _split_heads_t(q_ref, qm_ref, FOX_HEADS)
    m_ref[...] = jnp.full(m_ref.shape, NEG, F32)
    acc_ref[...] = jnp.zeros(acc_ref.shape, F32)
    c_q = crow_ref[qt]
    c_q = jnp.concatenate([c_q[AUX_F + h:AUX_F + h + 1, :] for h in heads], axis=1)
    causal_off = jnp.where(lax.broadcasted_iota(jnp.int32, (tk, tq), 0)
                           <= lax.broadcasted_iota(jnp.int32, (tk, tq), 1), 0.0, NEG)

    def qk(j, s_ref):
        rows = pl.ds(pl.multiple_of(j * tk, tk), tk)
        for h in heads:
            cols = slice(h * tq, (h + 1) * tq)
            s_ref[:, cols] = _dot(k_ref[rows, (h // 2) * LANES:(h // 2 + 1) * LANES], qm_ref[:, cols])

    def logits(j, s, kind):
        rows = pl.ds(pl.multiple_of(j * tk, tk), tk)
        c_k = jnp.concatenate([ck_ref[h, rows, :] for h in heads for _ in range(tq // LANES)], axis=1)
        s = s + (jnp.broadcast_to(c_q, s.shape) - c_k)
        if kind == 'diag':
            s = s + jnp.concatenate([causal_off] * FOX_HEADS, axis=1)
        return s

    def pv(j, p):
        return jnp.concatenate(
            [_dot(vte_ref[h // 2, j] if h % 2 == 0 else vto_ref[h // 2, j],
                  p[:, h * tq:(h + 1) * tq]) for h in heads], axis=1)

    _attend_chunks(qt, qk, logits, pv, bufs, m_ref, acc_ref)
    _finish_heads_t(acc_ref, o_ref, FOX_HEADS, odd_swapped=True)


def _fox(qkv, c_col, c_row, bsz, seq):
    tq = ATT_TILE
    nq = seq // tq
    npair = FOX_HEADS // 2
    return pl.pallas_call(
        functools.partial(_fox_kernel, seq=seq), grid=(bsz, nq),
        in_specs=[pl.BlockSpec((tq, FOX_WIDTH), lambda b, q: (b * nq + q, 0)),
                  pl.BlockSpec((seq, FOX_WIDTH), lambda b, q: (b, 1)),
                  pl.BlockSpec((seq, FOX_WIDTH), lambda b, q: (b, 2)),
                  pl.BlockSpec((seq, LANES), lambda b, q: (b, 0)),
                  pl.BlockSpec((None, nq, 8, tq), lambda b, q: (b, 0, 0, 0))],
        out_specs=pl.BlockSpec((tq, FOX_WIDTH), lambda b, q: (b * nq + q, 0)),
        out_shape=jax.ShapeDtypeStruct((bsz * seq, FOX_WIDTH), BF16),
        scratch_shapes=[pltpu.VMEM((npair, nq, LANES, tq), BF16),
                        pltpu.VMEM((npair, nq, LANES, tq), BF16),
                        pltpu.VMEM((FOX_HEADS, seq, LANES), F32),
                        pltpu.VMEM((LANES, FOX_HEADS * tq), BF16),
                        pltpu.VMEM((8, FOX_HEADS * tq), F32),
                        pltpu.VMEM((LANES, FOX_HEADS * tq), F32)] + _pipeline_buffers(FOX_HEADS * tq),
        compiler_params=_cparams(("parallel", "arbitrary")), name="fox",
    )(qkv, qkv, qkv, c_col, c_row)


def _gelu(y):
    return 0.5 * y * (1.0 + jnp.tanh(math.sqrt(2.0 / math.pi) * (y + 0.044715 * (y * y * y))))


def _merge_kernel(x_ref, yc_ref, u_ref, d_ref, wglu_ref, b_ref, c_ref, gate_ref,
                  wa_ref, wb_ref, wc_ref, wo_ref, o_ref, tmp_ref):
    d = D_MODEL
    y = _chunks_to_rows(yc_ref, tmp_ref, SSM_WIDTH) + d_ref[...] * u_ref[...]
    z = _dot(_gelu(y).astype(BF16), wglu_ref[...])
    a = z[:, :SSM_WIDTH] * _sigmoid(z[:, SSM_WIDTH:])
    merged = _sigmoid(gate_ref[:, 0:d].astype(F32)) * _dot(a.astype(BF16), wa_ref[...])
    merged += _sigmoid(gate_ref[:, d:2 * d].astype(F32)) * _dot(b_ref[...], wb_ref[...])
    merged += _sigmoid(gate_ref[:, 2 * d:3 * d].astype(F32)) * _dot(c_ref[...], wc_ref[...])
    o_ref[...] = x_ref[...] + _dot(merged.astype(BF16), wo_ref[...])


def _merge(xt, yc, u, d_skip, w_glu, b, c, gates, wa, wb, wc, wo, tm=1024):
    m, d = xt.shape
    row = lambda w: pl.BlockSpec((tm, w), lambda i: (i, 0))
    return pl.pallas_call(
        _merge_kernel, grid=(m // tm,),
        in_specs=[row(d), pl.BlockSpec((tm // S5_CHUNK, S5_COLS), lambda i: (i, 0)),
                  row(SSM_WIDTH), _resident((1, SSM_WIDTH)),
                  _resident((SSM_WIDTH, 2 * SSM_WIDTH)), row(DSA_WIDTH), row(FOX_WIDTH), row(3 * d),
                  _resident((SSM_WIDTH, d)), _resident((DSA_WIDTH, d)), _resident((FOX_WIDTH, d)),
                  _resident((d, d))],
        out_specs=row(d), out_shape=jax.ShapeDtypeStruct((m, d), F32),
        scratch_shapes=[pltpu.VMEM((SSM_WIDTH // LANES, tm, LANES), F32)],
        compiler_params=_cparams(("parallel",)), name="merge",
    )(xt, yc, u, d_skip.reshape(1, -1), w_glu, b, c, gates, wa, wb, wc, wo)


def _pack_w_in(w_in):
    d = w_in.shape[0]
    splits = (SSM_WIDTH, DSA_WIDTH, DSA_LATENT, IDX_HEADS * IDX_DIM, IDX_DIM, IDX_HEADS,
              FOX_WIDTH, FOX_WIDTH, FOX_WIDTH, FOX_HEADS, D_MODEL, D_MODEL, D_MODEL)
    pts = np.cumsum(splits)[:-1]
    (w_u, w_qb, w_ckv, w_qi, w_ki, w_wi, w_qc, w_kc, w_vc, w_fc, w_ga, w_gb, w_gc) = jnp.split(
        w_in, pts, axis=1)
    scale = HEAD_DIM ** -0.5
    aux = jnp.zeros((d, LANES), F32)
    aux = aux.at[:, AUX_F:AUX_F + FOX_HEADS].set(w_fc).at[:, AUX_W:AUX_W + IDX_HEADS].set(w_wi)
    cols = [w_u, w_qb * scale, w_qi, w_ki, w_ki, w_ckv, aux, w_qc * scale, w_kc, w_vc,
            w_ga, w_gb, w_gc]
    return jnp.concatenate(cols, axis=1).astype(BF16)


def kernel(x, ffn1_norm, ffn1_w_gate, ffn1_w_up, ffn1_w_down, mix_norm, w_in, ssm_lambda_re, ssm_lambda_im, ssm_log_dt, ssm_b_re, ssm_b_im, ssm_c_re, ssm_c_im, ssm_d, ssm_w_glu, dsa_kv_norm, dsa_w_uk, dsa_w_uv, rel_bias, fox_b_f, w_branch_ssm, w_branch_dsa, w_branch_fox, w_out, ffn2_norm, ffn2_w_gate, ffn2_w_up, ffn2_w_down, final_norm):
    bsz, seq, d = x.shape
    depth = w_in.shape[0]
    n_keys = min(TOPK_MAX, seq // 4)
    assert d == D_MODEL and seq % ATT_TILE == 0 and (bsz * seq) % (8 * 512) == 0
    assert n_keys % LANES == 0 and bsz % 8 == 0
    m = bsz * seq
    n_chunks = seq // S5_CHUNK
    xt = x.reshape(m, d)
    band = _rel_band(rel_bias)
    bf = lambda w: w.astype(BF16)
    for l in range(depth):
        xt = _ffn(xt, ffn1_norm[l], bf(ffn1_w_gate[l]), bf(ffn1_w_up[l]), bf(ffn1_w_down[l]))
        w_kv = bf(jnp.concatenate([dsa_w_uk[l], dsa_w_uk[l], dsa_w_uv[l], dsa_w_uv[l]], axis=1))
        u, qb, qi, ki, kv, aux, qkv, gates, u8 = _inproj(
            xt, mix_norm[l], _pack_w_in(w_in[l]), dsa_kv_norm[l], w_kv)
        m_intra, m_sum, m_out, a_re, a_im = _s5_matrices(
            ssm_lambda_re[l], ssm_lambda_im[l], ssm_log_dt[l], ssm_b_re[l], ssm_b_im[l],
            ssm_c_re[l], ssm_c_im[l])
        y_intra, p = _s5a(u8, m_intra, m_sum)
        h_re, h_im = _s5b(p, a_re, a_im, n_chunks, nb=8)
        yc = _s5c(y_intra, h_re, h_im, m_out)
        b_out = _dsa(qb, qi, aux, kv, ki, band, bsz, seq, n_keys)
        c_col, c_row = _fox_cum(aux, fox_b_f[l], bsz, seq)
        c_out = _fox(qkv, c_col, c_row, bsz, seq)
        xt = _merge(xt, yc, u, ssm_d[l], bf(ssm_w_glu[l]), b_out, c_out, gates,
                    bf(w_branch_ssm[l]), bf(w_branch_dsa[l]), bf(w_branch_fox[l]), bf(w_out[l]))
        last = final_norm if l == depth - 1 else None
        xt = _ffn(xt, ffn2_norm[l], bf(ffn2_w_gate[l]), bf(ffn2_w_up[l]), bf(ffn2_w_down[l]),
                  final_norm=last)
    return xt.reshape(bsz, seq, d)
```

```python
import functools
import math

import numpy as np
import jax
import jax.numpy as jnp
from jax import lax
from jax.experimental import pallas as pl
from jax.experimental.pallas import tpu as pltpu

F32 = jnp.float32
BF16 = jnp.bfloat16

D_MODEL = 1024
SSM_WIDTH = D_MODEL // 4
SSM_GROUP = 16
SSM_GROUPS = SSM_WIDTH // SSM_GROUP
SSM_STATE = 64
HEAD_DIM = 64
DSA_HEADS = 6
DSA_WIDTH = DSA_HEADS * HEAD_DIM
DSA_LATENT = 2 * HEAD_DIM
IDX_HEADS = 4
IDX_DIM = 64
TOPK_MAX = 256
FOX_HEADS = 6
FOX_WIDTH = FOX_HEADS * HEAD_DIM
REL_BUCKETS = 32
REL_MAX_EXACT = 16
REL_MAX_DIST = 128
FFN_HIDDEN = 2816
EPS = 1e-6
NEG = -1e30

LANES = 128
S5_CHUNK = 8
S5_COLS = S5_CHUNK * SSM_WIDTH
S5_STATE = 2 * SSM_GROUPS * SSM_STATE
ATT_TILE = 256
SEARCH_FIXED = 12
SEARCH_UNROLL = 2
VMEM_LIMIT = 56 * 1024 * 1024

_SEG = dict(u=256, qb=384, qi=256, ki=128, ckv=128, aux=128, qkv=1152, g=3072)
AUX_F = 0
AUX_W = 8


def _cparams(sem):
    return pltpu.CompilerParams(dimension_semantics=sem, vmem_limit_bytes=VMEM_LIMIT)


def _resident(shape):
    nd = len(shape)
    return pl.BlockSpec(shape, lambda *_: (0,) * nd, pipeline_mode=pl.Buffered(1))


def _rms(x, g):
    return x * lax.rsqrt(jnp.mean(x * x, axis=-1, keepdims=True) + EPS) * g


def _sigmoid(x):
    return 1.0 / (1.0 + jnp.exp(-x))


def _dot(a, b):
    return jnp.dot(a, b, preferred_element_type=F32)


def _ffn_body(x, g_ref, wg_ref, wu_ref, wd_ref, fc):
    h = _rms(x, g_ref[...]).astype(BF16)
    acc = None
    for c in range(FFN_HIDDEN // fc):
        sl = slice(c * fc, (c + 1) * fc)
        gt = _dot(h, wg_ref[:, sl])
        up = _dot(h, wu_ref[:, sl])
        a = (gt * _sigmoid(gt) * up).astype(BF16)
        d = _dot(a, wd_ref[sl, :])
        acc = d if acc is None else acc + d
    return x + 0.5 * acc


def _ffn_kernel(x_ref, g_ref, wg_ref, wu_ref, wd_ref, o_ref, *, fc):
    o_ref[...] = _ffn_body(x_ref[...], g_ref, wg_ref, wu_ref, wd_ref, fc)


def _ffn_final_kernel(x_ref, g_ref, wg_ref, wu_ref, wd_ref, fn_ref, o_ref, *, fc):
    y = _ffn_body(x_ref[...], g_ref, wg_ref, wu_ref, wd_ref, fc)
    o_ref[...] = _rms(y, fn_ref[...])


def _ffn(xt, norm, wg, wu, wd, final_norm=None, tm=1024, fc=256):
    m, d = xt.shape
    f = wg.shape[1]
    row = pl.BlockSpec((tm, d), lambda i: (i, 0))
    vec = _resident((1, d))
    in_specs = [row, vec, _resident((d, f)), _resident((d, f)), _resident((f, d))]
    args = [xt, norm.reshape(1, d), wg, wu, wd]
    if final_norm is None:
        body = functools.partial(_ffn_kernel, fc=fc)
    else:
        body = functools.partial(_ffn_final_kernel, fc=fc)
        in_specs.append(vec)
        args.append(final_norm.reshape(1, d))
    return pl.pallas_call(
        body, grid=(m // tm,), in_specs=in_specs, out_specs=row,
        out_shape=jax.ShapeDtypeStruct((m, d), F32),
        compiler_params=_cparams(("parallel",)), name="ffn")(*args)


def _rows_to_chunks(x, tmp_ref, o_ref):
    n, w = x.shape
    for hh in range(w // LANES):
        tmp_ref[hh] = x[:, hh * LANES:(hh + 1) * LANES]
    for i in range(S5_CHUNK):
        for hh in range(w // LANES):
            o_ref[:, i * w + hh * LANES:i * w + (hh + 1) * LANES] = (
                tmp_ref[hh, pl.ds(i, n // S5_CHUNK, stride=S5_CHUNK), :])


def _chunks_to_rows(x_ref, tmp_ref, w):
    n = x_ref.shape[0]
    for i in range(S5_CHUNK):
        for hh in range(w // LANES):
            tmp_ref[hh, pl.ds(i, n, stride=S5_CHUNK), :] = (
                x_ref[:, i * w + hh * LANES:i * w + (hh + 1) * LANES])
    return jnp.concatenate([tmp_ref[hh] for hh in range(w // LANES)], axis=1)


def _inproj_kernel(x_ref, g_ref, w_ref, kvn_ref, wkv_ref,
                   u_ref, qb_ref, qi_ref, ki_ref, kv_ref, aux_ref, qkv_ref, gate_ref, u8_ref,
                   tmp_ref):
    h = _rms(x_ref[...], g_ref[...]).astype(BF16)
    outs = dict(u=u_ref, qb=qb_ref, qi=qi_ref, ki=ki_ref, aux=aux_ref, qkv=qkv_ref)
    off = 0
    for group in (("u", "qb", "qi", "ki"), ("ckv", "aux", "qkv")):
        width = sum(_SEG[name] for name in group)
        r = _dot(h, w_ref[:, off:off + width])
        off += width
        lo = 0
        for name in group:
            piece = r[:, lo:lo + _SEG[name]]
            lo += _SEG[name]
            if name == "ckv":
                c = _rms(piece, kvn_ref[...]).astype(BF16)
                kv_ref[...] = _dot(c, wkv_ref[...]).astype(BF16)
            else:
                outs[name][...] = piece.astype(outs[name].dtype)
            if name == "u":
                _rows_to_chunks(piece, tmp_ref, u8_ref)
    for c in range(_SEG["g"] // D_MODEL):
        cols = slice(c * D_MODEL, (c + 1) * D_MODEL)
        gate_ref[:, cols] = _dot(h, w_ref[:, off + c * D_MODEL:off + (c + 1) * D_MODEL]).astype(BF16)


def _inproj(xt, norm, w_all, kv_norm, w_kv, tm=1024):
    m, d = xt.shape
    wtot = w_all.shape[1]
    widths = dict(u=(256, F32), qb=(384, BF16), qi=(256, BF16), ki=(128, BF16),
                  kv=(256, BF16), aux=(128, F32), qkv=(1152, BF16), g=(3072, BF16))
    out_shape = [jax.ShapeDtypeStruct((m, w), dt) for w, dt in widths.values()]
    out_specs = [pl.BlockSpec((tm, w), lambda i: (i, 0)) for w, _ in widths.values()]
    out_shape.append(jax.ShapeDtypeStruct((m // S5_CHUNK, S5_COLS), F32))
    out_specs.append(pl.BlockSpec((tm // S5_CHUNK, S5_COLS), lambda i: (i, 0)))
    return pl.pallas_call(
        _inproj_kernel, grid=(m // tm,),
        in_specs=[pl.BlockSpec((tm, d), lambda i: (i, 0)), _resident((1, d)),
                  _resident((d, wtot)), _resident((1, DSA_LATENT)),
                  _resident((DSA_LATENT, 256))],
        out_specs=out_specs, out_shape=out_shape,
        scratch_shapes=[pltpu.VMEM((SSM_WIDTH // LANES, tm, LANES), F32)],
        compiler_params=_cparams(("parallel",)), name="inproj",
    )(xt, norm.reshape(1, d), w_all, kv_norm.reshape(1, DSA_LATENT), w_kv)


S5_SLABS = S5_STATE // LANES
S5_SCAN_SLABS = 4


def _s5a_kernel(u_ref, mi_ref, ms_ref, yi_ref, p_ref):
    u = u_ref[...].astype(BF16)
    yi_ref[...] = _dot(u, mi_ref[...])
    p = _dot(u, ms_ref[...])
    for k in range(S5_SLABS):
        p_ref[k] = p[:, k * LANES:(k + 1) * LANES]


def _s5a(u8, m_intra, m_sum, tr=512):
    r = u8.shape[0]
    row = pl.BlockSpec((tr, S5_COLS), lambda i: (i, 0))
    slabs = pl.BlockSpec((S5_SLABS, tr, LANES), lambda i: (0, i, 0))
    return pl.pallas_call(
        _s5a_kernel, grid=(r // tr,),
        in_specs=[row, _resident((S5_COLS, S5_COLS)), _resident((S5_COLS, S5_STATE))],
        out_specs=[row, slabs],
        out_shape=[jax.ShapeDtypeStruct((r, S5_COLS), F32),
                   jax.ShapeDtypeStruct((S5_SLABS, r, LANES), F32)],
        compiler_params=_cparams(("parallel",)), name="s5_chunk")(u8, m_intra, m_sum)


def _s5b_kernel(pr_ref, pi_ref, ar_ref, ai_ref, hr_ref, hi_ref, *, nb, nc):
    ar = ar_ref[...]
    ai = ai_ref[...]
    ns = S5_SCAN_SLABS

    def step(c, carry):
        hr, hi = carry
        rows = pl.ds(c, nb, stride=nc)
        for k in range(ns):
            hr_ref[k, rows, :] = hr[:, k * LANES:(k + 1) * LANES]
            hi_ref[k, rows, :] = hi[:, k * LANES:(k + 1) * LANES]
        pr = jnp.concatenate([pr_ref[k, rows, :] for k in range(ns)], axis=1)
        pi = jnp.concatenate([pi_ref[k, rows, :] for k in range(ns)], axis=1)
        return (ar * hr - ai * hi + pr, ar * hi + ai * hr + pi)

    z = jnp.zeros((nb, ns * LANES), F32)
    lax.fori_loop(0, nc, step, (z, z))


def _s5b(p, a_re, a_im, nc, nb):
    r = p.shape[1]
    ns = S5_SCAN_SLABS
    half = S5_SLABS // 2
    re_blk = pl.BlockSpec((ns, nb * nc, LANES), lambda i, j: (j, i, 0))
    im_blk = pl.BlockSpec((ns, nb * nc, LANES), lambda i, j: (j + half // ns, i, 0))
    a_blk = pl.BlockSpec((1, ns * LANES), lambda i, j: (0, j))
    return pl.pallas_call(
        functools.partial(_s5b_kernel, nb=nb, nc=nc), grid=(r // (nb * nc), half // ns),
        in_specs=[re_blk, im_blk, a_blk, a_blk],
        out_specs=[re_blk, re_blk],
        out_shape=[jax.ShapeDtypeStruct((half, r, LANES), F32)] * 2,
        compiler_params=_cparams(("parallel", "parallel")), name="s5_scan",
    )(p, p, a_re, a_im)


def _s5c_kernel(yi_ref, hr_ref, hi_ref, mo_ref, y_ref):
    half = S5_STATE // 2
    hr = jnp.concatenate([hr_ref[k] for k in range(S5_SLABS // 2)], axis=1)
    hi = jnp.concatenate([hi_ref[k] for k in range(S5_SLABS // 2)], axis=1)
    y = yi_ref[...]
    y = y + _dot(hr.astype(BF16), mo_ref[:half, :])
    y = y + _dot(hi.astype(BF16), mo_ref[half:, :])
    y_ref[...] = y


def _s5c(yi, h_re, h_im, m_out, tr=512):
    r = yi.shape[0]
    row = pl.BlockSpec((tr, S5_COLS), lambda i: (i, 0))
    slabs = pl.BlockSpec((S5_SLABS // 2, tr, LANES), lambda i: (0, i, 0))
    return pl.pallas_call(
        _s5c_kernel, grid=(r // tr,),
        in_specs=[row, slabs, slabs, _resident((S5_STATE, S5_COLS))],
        out_specs=row, out_shape=jax.ShapeDtypeStruct((r, S5_COLS), F32),
        compiler_params=_cparams(("parallel",)), name="s5_out")(yi, h_re, h_im, m_out)


def _toeplitz_kernel(lag_ref, o_ref):
    keep = pl.program_id(1) >= pl.program_id(0)
    o_ref[...] = jnp.where(keep, lag_ref[...], jnp.zeros_like(lag_ref))


def _block_toeplitz(lag_blocks):
    t, w, _ = lag_blocks.shape
    return pl.pallas_call(
        _toeplitz_kernel, grid=(t, t),
        in_specs=[pl.BlockSpec((None, w, w), lambda j, i: (jnp.maximum(i - j, 0), 0, 0))],
        out_specs=pl.BlockSpec((w, w), lambda j, i: (j, i)),
        out_shape=jax.ShapeDtypeStruct((t * w, t * w), lag_blocks.dtype), name="s5_toeplitz",
    )(lag_blocks)


def _s5_matrices(lam_re, lam_im, log_dt, b_re, b_im, c_re, c_im):
    hp = lax.Precision.HIGHEST
    t = S5_CHUNK
    dt = jnp.exp(log_dt)[:, None]
    lr, li = lam_re, lam_im
    mag = jnp.exp(lr * dt)
    ab_re, ab_im = mag * jnp.cos(li * dt), mag * jnp.sin(li * dt)
    den = lr * lr + li * li
    nr, ni = ab_re - 1.0, ab_im
    s_re = (nr * lr + ni * li) / den
    s_im = (ni * lr - nr * li) / den
    bb_re = s_re[..., None] * b_re - s_im[..., None] * b_im
    bb_im = s_re[..., None] * b_im + s_im[..., None] * b_re
    pr, pi = [jnp.ones_like(ab_re)], [jnp.zeros_like(ab_re)]
    for _ in range(t):
        pr.append(pr[-1] * ab_re - pi[-1] * ab_im)
        pi.append(pr[-2] * ab_im + pi[-1] * ab_re)
    pw_re, pw_im = jnp.stack(pr), jnp.stack(pi)
    lo_re, lo_im = jnp.stack(pr[:t]), jnp.stack(pi[:t])
    ab_b_re = lo_re[..., None] * bb_re - lo_im[..., None] * bb_im
    ab_b_im = lo_re[..., None] * bb_im + lo_im[..., None] * bb_re
    kern = (jnp.einsum('gpn,dgnq->dgpq', c_re, ab_b_re, precision=hp)
            - jnp.einsum('gpn,dgnq->dgpq', c_im, ab_b_im, precision=hp))
    eye_g = jnp.eye(SSM_GROUPS, dtype=F32)
    rows = np.arange(SSM_WIDTH)
    expand = jnp.asarray((rows[None, :] % SSM_GROUP == np.arange(SSM_GROUP)[:, None]).astype(np.float32))
    same_group = jnp.asarray(rows[:, None] // SSM_GROUP == rows[None, :] // SSM_GROUP)
    lag_rows = kern.transpose(0, 1, 3, 2).reshape(t, SSM_WIDTH, SSM_GROUP)
    lag_blocks = jnp.where(same_group, jnp.einsum('drp,pc->drc', lag_rows, expand, precision=hp), 0.0)
    m_intra = _block_toeplitz(lag_blocks.astype(BF16))
    hi_re, hi_im = jnp.stack(pr[t - 1::-1]), jnp.stack(pi[t - 1::-1])
    sb_re = hi_re[..., None] * bb_re - hi_im[..., None] * bb_im
    sb_im = hi_re[..., None] * bb_im + hi_im[..., None] * bb_re
    m_sum_re = jnp.einsum('jgnq,gh->jgqhn', sb_re, eye_g, precision=hp)
    m_sum_im = jnp.einsum('jgnq,gh->jgqhn', sb_im, eye_g, precision=hp)
    m_sum = jnp.concatenate([m_sum_re.reshape(S5_COLS, -1), m_sum_im.reshape(S5_COLS, -1)], axis=1)
    up_re, up_im = jnp.stack(pr[1:])[:, :, None, :], jnp.stack(pi[1:])[:, :, None, :]
    w_re = c_re[None] * up_re - c_im[None] * up_im
    w_im = c_re[None] * up_im + c_im[None] * up_re
    m_out_re = jnp.einsum('igpn,gh->gnihp', w_re, eye_g, precision=hp)
    m_out_im = jnp.einsum('igpn,gh->gnihp', -w_im, eye_g, precision=hp)
    m_out = jnp.concatenate([m_out_re.reshape(-1, S5_COLS), m_out_im.reshape(-1, S5_COLS)], axis=0)
    a_re = pw_re[t].reshape(1, -1)
    a_im = pw_im[t].reshape(1, -1)
    return m_intra.astype(BF16), m_sum.astype(BF16), m_out.astype(BF16), a_re, a_im


def _fold8(x, op):
    parts = [x[r:r + 8] for r in range(0, x.shape[0], 8)]
    while len(parts) > 1:
        parts = [op(parts[i], parts[i + 1]) for i in range(0, len(parts), 2)]
    return parts[0]


def _split_heads_t(q_ref, qt_ref, n_heads):
    tq = q_ref.shape[0]
    top = lax.broadcasted_iota(jnp.int32, (LANES, tq), 0) < HEAD_DIM
    for pr in range(n_heads // 2):
        pair_t = q_ref[:, pr * LANES:(pr + 1) * LANES].astype(F32).T
        qt_ref[:, 2 * pr * tq:(2 * pr + 1) * tq] = jnp.where(top, pair_t, 0.0).astype(BF16)
        qt_ref[:, (2 * pr + 1) * tq:(2 * pr + 2) * tq] = jnp.where(top, 0.0, pair_t).astype(BF16)


def _softmax_stage(s, m_ref, alpha_ref, p_ref):
    m_old = m_ref[...]
    m_new = jnp.maximum(m_old, jnp.max(_fold8(s, jnp.maximum), axis=0, keepdims=True))
    alpha_ref[...] = jnp.exp(m_old - m_new)
    p_ref[...] = jnp.exp(s - jnp.broadcast_to(m_new[0:1], s.shape)).astype(BF16)
    m_ref[...] = m_new


def _value_stage(pv, alpha_ref, p_ref, acc_ref):
    acc = acc_ref[...]
    acc_ref[...] = acc * jnp.broadcast_to(alpha_ref[0:1, :], acc.shape) + pv(p_ref[...])


def _attend_chunks(qt, qk, logits, pv, bufs, m_ref, acc_ref):
    s_a, s_b, p_a, p_b, al_a, al_b = bufs

    def plain(j, kind):
        qk(j, s_a)
        _softmax_stage(logits(j, s_a[...], kind), m_ref, al_a, p_a)
        _value_stage(lambda p: pv(j, p), al_a, p_a, acc_ref)

    @pl.when(qt == 0)
    def _():
        plain(0, 'diag')

    @pl.when(qt >= 1)
    def _():
        n_far = qt - 1
        first = n_far % 2

        @pl.when(first == 1)
        def _():
            plain(0, 'far')

        qk(first, s_a)
        p_b[...] = jnp.zeros(p_b.shape, p_b.dtype)
        al_b[...] = jnp.ones(al_b.shape, al_b.dtype)

        def pair(c, kind0, kind1, last):
            qk(c + 1, s_b)
            _value_stage(lambda p: pv(jnp.maximum(c - 1, 0), p), al_b, p_b, acc_ref)
            _softmax_stage(logits(c, s_a[...], kind0), m_ref, al_a, p_a)
            if not last:
                qk(c + 2, s_a)
            _value_stage(lambda p: pv(c, p), al_a, p_a, acc_ref)
            _softmax_stage(logits(c + 1, s_b[...], kind1), m_ref, al_b, p_b)

        def body(i, carry):
            pair(first + 2 * i, 'far', 'far', False)
            return carry

        lax.fori_loop(0, (n_far - first) // 2, body, 0)
        pair(qt - 1, 'prev', 'diag', True)
        _value_stage(lambda p: pv(qt, p), al_b, p_b, acc_ref)


def _pipeline_buffers(width):
    return ([pltpu.VMEM((ATT_TILE, width), F32)] * 2 + [pltpu.VMEM((ATT_TILE, width), BF16)] * 2
            + [pltpu.VMEM((8, width), F32)] * 2)


def _finish_heads_t(acc_ref, o_ref, n_heads, odd_swapped):
    hd = HEAD_DIM
    tq = o_ref.shape[0]
    for pr in range(n_heads // 2):
        ae = acc_ref[:, 2 * pr * tq:(2 * pr + 1) * tq]
        ao = acc_ref[:, (2 * pr + 1) * tq:(2 * pr + 2) * tq]
        oe = ae[0:hd] / ae[hd:2 * hd]
        oo = ao[hd:2 * hd] / ao[0:hd] if odd_swapped else ao[0:hd] / ao[hd:2 * hd]
        pair = jnp.concatenate([oe, oo], axis=0)
        o_ref[:, pr * LANES:(pr + 1) * LANES] = pair.T.astype(o_ref.dtype)


def _dsa_kernel(q_ref, qi_ref, aux_ref, kv_ref, ki_ref, band_ref, o_ref,
                score_ref, vt_ref, qm_ref, m_ref, acc_ref, thr_ref, *bufs, n_keys, seq):
    tq = tk = ATT_TILE
    qt = pl.program_id(1)
    n_chunks = qt + 1
    kf = float(n_keys)

    @pl.when(qt == 0)
    def _():
        top = lax.broadcasted_iota(jnp.int32, (LANES, tk), 0) < HEAD_DIM
        for j in range(seq // tk):
            vv = kv_ref[j * tk:(j + 1) * tk, LANES:2 * LANES].astype(F32)
            vt_ref[j] = jnp.where(top, vv.T, 1.0).astype(BF16)

    _split_heads_t(qi_ref, qm_ref, IDX_HEADS)
    aux_t = aux_ref[...].T
    head_w = jnp.concatenate(
        [aux_t[AUX_W + h:AUX_W + h + 1, :] for h in range(IDX_HEADS)], axis=1
    ) * (IDX_HEADS ** -0.5 * IDX_DIM ** -0.5)
    key_minus_query = (lax.broadcasted_iota(jnp.int32, (tk, tq), 0)
                       - lax.broadcasted_iota(jnp.int32, (tk, tq), 1))

    def score_body(j, carry):
        mx, mn = carry
        kc = ki_ref[pl.ds(pl.multiple_of(j * tk, tk), tk), :]
        d = _dot(kc, qm_ref[:, 0:IDX_HEADS * tq])
        d = jnp.broadcast_to(head_w, d.shape) * jnp.maximum(d, 0.0)
        sc = (d[:, 0:tq] + d[:, tq:2 * tq]) + (d[:, 2 * tq:3 * tq] + d[:, 3 * tq:4 * tq])
        causal = key_minus_query <= (qt - j) * tq
        score_ref[j] = jnp.where(causal, sc, NEG)
        mx = jnp.maximum(mx, _fold8(jnp.where(causal, sc, NEG), jnp.maximum))
        mn = jnp.minimum(mn, _fold8(jnp.where(causal, sc, -NEG), jnp.minimum))
        return mx, mn

    mx, mn = lax.fori_loop(0, n_chunks, score_body,
                           (jnp.full((8, tq), NEG, F32), jnp.full((8, tq), -NEG, F32)))

    t_q = qt * tq + lax.broadcasted_iota(jnp.int32, (1, tq), 1)
    thr_ref[...] = jnp.full((8, tq), 0.5 * NEG, F32)

    def count(pred):
        def body(j, acc):
            return acc + _fold8(pred(score_ref[j]), jnp.add)
        acc = lax.fori_loop(0, n_chunks, body, jnp.zeros((8, tq), F32))
        return jnp.sum(acc, axis=0, keepdims=True)

    def count_ge(x):
        xb = jnp.broadcast_to(x, (tk, tq))
        return count(lambda s: jnp.where(s >= xb, 1.0, 0.0))

    @pl.when((qt + 1) * tq > n_keys)
    def _search():
        need = (t_q >= n_keys).astype(F32)
        lo0 = jnp.min(mn, axis=0, keepdims=True)
        hi0 = jnp.max(mx, axis=0, keepdims=True)
        hi_top = hi0 + jnp.maximum(jnp.abs(hi0) * 2.0 ** -22, 1e-30)
        c_ge0 = count_ge(jnp.zeros((1, tq), F32))
        c_gt0 = count(lambda s: jnp.where(s > 0.0, 1.0, 0.0))
        n_valid = (t_q + 1).astype(F32)
        zero_hit = c_ge0 == kf
        zero_tie = (c_gt0 < kf) & (c_ge0 > kf)
        positive = c_gt0 >= kf
        hi_neg = hi0 < 0.0
        lo1 = jnp.where(positive, jnp.maximum(lo0, 0.0), lo0)
        c_lo1 = jnp.where(positive, c_ge0, n_valid)
        hi1 = jnp.where(positive | hi_neg, hi_top, 0.0)
        c_hi1 = jnp.where(positive | hi_neg, 0.0, c_ge0)
        hit0 = need * zero_hit.astype(F32)
        tie0 = need * zero_tie.astype(F32)
        done0 = jnp.maximum(1.0 - need, jnp.maximum(hit0, tie0))
        lo1 = jnp.where(tie0 > 0, 0.0, lo1)
        c_hi1 = jnp.where(tie0 > 0, c_gt0, c_hi1)
        x0 = jnp.zeros((1, tq), F32)
        state0 = (lo1, hi1, c_lo1, c_hi1, done0, tie0, x0, jnp.max(1.0 - done0))

        def cond(st):
            return st[7] > 0.0

        def body(st):
            for _ in range(SEARCH_UNROLL):
                st = step(st)
            return st

        def step(st):
            lo, hi, c_lo, c_hi, done, tie, x_fin, _ = st
            x = 0.5 * lo + 0.5 * hi
            inside = ((x > lo) & (x < hi)).astype(F32)
            active = 1.0 - done
            probe = active * inside
            c = count_ge(jnp.where(probe > 0, x, x_fin))
            hit = probe * (c == kf).astype(F32)
            more = probe * (c > kf).astype(F32)
            less = probe * (c < kf).astype(F32)
            new_tie = active * (1.0 - inside)
            lo = jnp.where(more > 0, x, lo)
            c_lo = jnp.where(more > 0, c, c_lo)
            hi = jnp.where(less > 0, x, hi)
            c_hi = jnp.where(less > 0, c, c_hi)
            x_fin = jnp.where(hit > 0, x, x_fin)
            tie = jnp.maximum(tie, new_tie)
            done = jnp.maximum(done, jnp.maximum(hit, new_tie))
            return (lo, hi, c_lo, c_hi, done, tie, x_fin, jnp.max(1.0 - done))

        st = state0
        for _ in range(SEARCH_FIXED):
            st = step(st)
        st = lax.while_loop(cond, body, st)
        lo, _, _, c_hi, _, tie, x_fin, _ = st
        thr = jnp.where(need > 0, jnp.where(tie > 0, lo, x_fin), 0.5 * NEG)
        thr_ref[...] = jnp.broadcast_to(thr, (8, tq))

        @pl.when(jnp.max(tie) > 0.0)
        def _ties():
            want = jnp.broadcast_to(kf - c_hi, (tk, tq))
            thr_t = jnp.broadcast_to(jnp.where(tie > 0, thr, -NEG), (tk, tq))
            upto = (lax.broadcasted_iota(jnp.int32, (tk, tk), 1)
                    <= lax.broadcasted_iota(jnp.int32, (tk, tk), 0))
            prefix = jnp.where(upto, 1.0, 0.0).astype(BF16)

            def body(j, seen):
                s = score_ref[j]
                eq = s == thr_t
                rank = _dot(prefix, jnp.where(eq, 1.0, 0.0).astype(BF16))
                rank = rank + jnp.broadcast_to(seen, (tk, tq))
                score_ref[j] = jnp.where(eq, jnp.where(rank > want, NEG, s), s)
                return rank[tk - 1:tk, :]

            lax.fori_loop(0, n_chunks, body, jnp.zeros((1, tq), F32))

    _split_heads_t(q_ref, qm_ref, DSA_HEADS)
    m_ref[...] = jnp.full(m_ref.shape, NEG, F32)
    acc_ref[...] = jnp.zeros(acc_ref.shape, F32)
    thr_b = jnp.broadcast_to(thr_ref[0:1, :], (tk, tq))

    def qk(j, s_ref):
        rows = pl.ds(pl.multiple_of(j * tk, tk), tk)
        s_ref[...] = _dot(kv_ref[rows, 0:LANES], qm_ref[...])

    def logits(j, s, kind):
        off = jnp.where(score_ref[j] >= thr_b, 0.0, NEG)
        s = s + jnp.concatenate([off] * DSA_HEADS, axis=1)
        if kind == 'prev':
            s = s + band_ref[0:tk, :]
        elif kind == 'diag':
            s = s + band_ref[tk:2 * tk, :]
        return s

    _attend_chunks(qt, qk, logits, lambda j, p: _dot(vt_ref[j], p), bufs, m_ref, acc_ref)
    _finish_heads_t(acc_ref, o_ref, DSA_HEADS, odd_swapped=False)


def _dsa(qb, qi, aux, kv, ki, band, bsz, seq, n_keys):
    tq = ATT_TILE
    nq = seq // tq
    qrow = lambda w: pl.BlockSpec((tq, w), lambda b, q: (b * nq + q, 0))
    seq_blk = lambda w: pl.BlockSpec((seq, w), lambda b, q: (b, 0))
    return pl.pallas_call(
        functools.partial(_dsa_kernel, n_keys=n_keys, seq=seq), grid=(bsz, nq),
        in_specs=[qrow(DSA_WIDTH), qrow(256), qrow(LANES), seq_blk(256), seq_blk(LANES),
                  _resident((2 * tq, DSA_HEADS * tq))],
        out_specs=qrow(DSA_WIDTH),
        out_shape=jax.ShapeDtypeStruct((bsz * seq, DSA_WIDTH), BF16),
        scratch_shapes=[pltpu.VMEM((nq, tq, tq), F32),
                        pltpu.VMEM((nq, LANES, tq), BF16),
                        pltpu.VMEM((LANES, DSA_HEADS * tq), BF16),
                        pltpu.VMEM((8, DSA_HEADS * tq), F32),
                        pltpu.VMEM((LANES, DSA_HEADS * tq), F32),
                        pltpu.VMEM((8, tq), F32)] + _pipeline_buffers(DSA_HEADS * tq),
        compiler_params=_cparams(("parallel", "arbitrary")), name="dsa",
    )(qb, qi, aux, kv, ki, band)


def _band_kernel(bucket_ref, rb_ref, o_ref):
    bucket = bucket_ref[...]
    tq = bucket.shape[1]
    for h in range(DSA_HEADS):
        acc = jnp.zeros(bucket.shape, F32)
        for k in range(REL_BUCKETS - 1):
            acc = jnp.where(bucket == k, rb_ref[k, h] - rb_ref[REL_BUCKETS - 1, h], acc)
        o_ref[:, h * tq:(h + 1) * tq] = acc


def _t5_bucket_np(dist):
    d = np.maximum(dist, 0)
    df = np.maximum(d, 1).astype(np.float32)
    log_b = REL_MAX_EXACT + (np.log(df / REL_MAX_EXACT) / math.log(REL_MAX_DIST / REL_MAX_EXACT)
                             * (REL_BUCKETS - REL_MAX_EXACT)).astype(np.int32)
    log_b = np.minimum(log_b, REL_BUCKETS - 1)
    return np.where(d < REL_MAX_EXACT, d, log_b).astype(np.int32)


def _rel_band(rel_bias):
    tq = ATT_TILE
    i = np.arange(tq)[None, :]
    j = np.arange(2 * tq)[:, None]
    bucket = jnp.asarray(_t5_bucket_np(i + tq - j))
    return pl.pallas_call(
        _band_kernel,
        in_specs=[pl.BlockSpec(memory_space=pltpu.VMEM), pl.BlockSpec(memory_space=pltpu.SMEM)],
        out_specs=pl.BlockSpec(memory_space=pltpu.VMEM),
        out_shape=jax.ShapeDtypeStruct((2 * tq, DSA_HEADS * tq), F32), name="rel_band",
    )(bucket, rel_bias)


def _cum_kernel(aux_ref, bf_ref, col_ref, row_ref, *, seq):
    x = aux_ref[...] + bf_ref[...]
    logf = -(jnp.maximum(-x, 0.0) + jnp.log1p(jnp.exp(-jnp.abs(x))))
    c = logf.T[0:8]
    pos = lax.broadcasted_iota(jnp.int32, c.shape, 1)
    shift = 1
    while shift < seq:
        c = c + jnp.where(pos >= shift, pltpu.roll(c, shift, 1), 0.0)
        shift *= 2
    col_ref[...] = jnp.concatenate([c, jnp.zeros((LANES - 8, seq), F32)], axis=0).T
    for j in range(seq // ATT_TILE):
        row_ref[j] = c[:, j * ATT_TILE:(j + 1) * ATT_TILE]


def _fox_cum(aux, b_f, bsz, seq):
    nq = seq // ATT_TILE
    bf = jnp.zeros((1, LANES), F32).at[0, AUX_F:AUX_F + FOX_HEADS].set(b_f)
    return pl.pallas_call(
        functools.partial(_cum_kernel, seq=seq), grid=(bsz,),
        in_specs=[pl.BlockSpec((seq, LANES), lambda b: (b, 0)), _resident((1, LANES))],
        out_specs=[pl.BlockSpec((seq, LANES), lambda b: (b, 0)),
                   pl.BlockSpec((None, nq, 8, ATT_TILE), lambda b: (b, 0, 0, 0))],
        out_shape=[jax.ShapeDtypeStruct((bsz * seq, LANES), F32),
                   jax.ShapeDtypeStruct((bsz, nq, 8, ATT_TILE), F32)],
        compiler_params=_cparams(("parallel",)), name="fox_cum")(aux, bf)


def _fox_kernel(q_ref, k_ref, v_ref, ccol_ref, crow_ref, o_ref,
                vte_ref, vto_ref, ck_ref, qm_ref, m_ref, acc_ref, *bufs, seq):
    tq = tk = ATT_TILE
    qt = pl.program_id(1)
    heads = range(FOX_HEADS)

    @pl.when(qt == 0)
    def _():
        top = lax.broadcasted_iota(jnp.int32, (LANES, tk), 0) < HEAD_DIM
        for j in range(seq // tk):
            rows = slice(j * tk, (j + 1) * tk)
            for pr in range(FOX_HEADS // 2):
                v_t = v_ref[rows, pr * LANES:(pr + 1) * LANES].astype(F32).T
                vte_ref[pr, j] = jnp.where(top, v_t, 1.0).astype(BF16)
                vto_ref[pr, j] = jnp.where(top, 1.0, v_t).astype(BF16)
            cc = ccol_ref[rows, :]
            for h in heads:
                ck_ref[h, rows, :] = jnp.broadcast_to(cc[:, AUX_F + h:AUX_F + h + 1], (tk, LANES))

    _split_heads_t(q_ref, qm_ref, FOX_HEADS)
    m_ref[...] = jnp.full(m_ref.shape, NEG, F32)
    acc_ref[...] = jnp.zeros(acc_ref.shape, F32)
    c_q = crow_ref[qt]
    c_q = jnp.concatenate([c_q[AUX_F + h:AUX_F + h + 1, :] for h in heads], axis=1)
    causal_off = jnp.where(lax.broadcasted_iota(jnp.int32, (tk, tq), 0)
                           <= lax.broadcasted_iota(jnp.int32, (tk, tq), 1), 0.0, NEG)

    def qk(j, s_ref):
        rows = pl.ds(pl.multiple_of(j * tk, tk), tk)
        for h in heads:
            cols = slice(h * tq, (h + 1) * tq)
            s_ref[:, cols] = _dot(k_ref[rows, (h // 2) * LANES:(h // 2 + 1) * LANES], qm_ref[:, cols])

    def logits(j, s, kind):
        rows = pl.ds(pl.multiple_of(j * tk, tk), tk)
        c_k = jnp.concatenate([ck_ref[h, rows, :] for h in heads for _ in range(tq // LANES)], axis=1)
        s = s + (jnp.broadcast_to(c_q, s.shape) - c_k)
        if kind == 'diag':
            s = s + jnp.concatenate([causal_off] * FOX_HEADS, axis=1)
        return s

    def pv(j, p):
        return jnp.concatenate(
            [_dot(vte_ref[h // 2, j] if h % 2 == 0 else vto_ref[h // 2, j],
                  p[:, h * tq:(h + 1) * tq]) for h in heads], axis=1)

    _attend_chunks(qt, qk, logits, pv, bufs, m_ref, acc_ref)
    _finish_heads_t(acc_ref, o_ref, FOX_HEADS, odd_swapped=True)


def _fox(qkv, c_col, c_row, bsz, seq):
    tq = ATT_TILE
    nq = seq // tq
    npair = FOX_HEADS // 2
    return pl.pallas_call(
        functools.partial(_fox_kernel, seq=seq), grid=(bsz, nq),
        in_specs=[pl.BlockSpec((tq, FOX_WIDTH), lambda b, q: (b * nq + q, 0)),
                  pl.BlockSpec((seq, FOX_WIDTH), lambda b, q: (b, 1)),
                  pl.BlockSpec((seq, FOX_WIDTH), lambda b, q: (b, 2)),
                  pl.BlockSpec((seq, LANES), lambda b, q: (b, 0)),
                  pl.BlockSpec((None, nq, 8, tq), lambda b, q: (b, 0, 0, 0))],
        out_specs=pl.BlockSpec((tq, FOX_WIDTH), lambda b, q: (b * nq + q, 0)),
        out_shape=jax.ShapeDtypeStruct((bsz * seq, FOX_WIDTH), BF16),
        scratch_shapes=[pltpu.VMEM((npair, nq, LANES, tq), BF16),
                        pltpu.VMEM((npair, nq, LANES, tq), BF16),
                        pltpu.VMEM((FOX_HEADS, seq, LANES), F32),
                        pltpu.VMEM((LANES, FOX_HEADS * tq), BF16),
                        pltpu.VMEM((8, FOX_HEADS * tq), F32),
                        pltpu.VMEM((LANES, FOX_HEADS * tq), F32)] + _pipeline_buffers(FOX_HEADS * tq),
        compiler_params=_cparams(("parallel", "arbitrary")), name="fox",
    )(qkv, qkv, qkv, c_col, c_row)


def _gelu(y):
    return 0.5 * y * (1.0 + jnp.tanh(math.sqrt(2.0 / math.pi) * (y + 0.044715 * (y * y * y))))


def _merge_kernel(x_ref, yc_ref, u_ref, d_ref, wglu_ref, b_ref, c_ref, gate_ref,
                  wa_ref, wb_ref, wc_ref, wo_ref, o_ref, tmp_ref):
    d = D_MODEL
    y = _chunks_to_rows(yc_ref, tmp_ref, SSM_WIDTH) + d_ref[...] * u_ref[...]
    z = _dot(_gelu(y).astype(BF16), wglu_ref[...])
    a = z[:, :SSM_WIDTH] * _sigmoid(z[:, SSM_WIDTH:])
    merged = _sigmoid(gate_ref[:, 0:d].astype(F32)) * _dot(a.astype(BF16), wa_ref[...])
    merged += _sigmoid(gate_ref[:, d:2 * d].astype(F32)) * _dot(b_ref[...], wb_ref[...])
    merged += _sigmoid(gate_ref[:, 2 * d:3 * d].astype(F32)) * _dot(c_ref[...], wc_ref[...])
    o_ref[...] = x_ref[...] + _dot(merged.astype(BF16), wo_ref[...])


def _merge(xt, yc, u, d_skip, w_glu, b, c, gates, wa, wb, wc, wo, tm=1024):
    m, d = xt.shape
    row = lambda w: pl.BlockSpec((tm, w), lambda i: (i, 0))
    return pl.pallas_call(
        _merge_kernel, grid=(m // tm,),
        in_specs=[row(d), pl.BlockSpec((tm // S5_CHUNK, S5_COLS), lambda i: (i, 0)),
                  row(SSM_WIDTH), _resident((1, SSM_WIDTH)),
                  _resident((SSM_WIDTH, 2 * SSM_WIDTH)), row(DSA_WIDTH), row(FOX_WIDTH), row(3 * d),
                  _resident((SSM_WIDTH, d)), _resident((DSA_WIDTH, d)), _resident((FOX_WIDTH, d)),
                  _resident((d, d))],
        out_specs=row(d), out_shape=jax.ShapeDtypeStruct((m, d), F32),
        scratch_shapes=[pltpu.VMEM((SSM_WIDTH // LANES, tm, LANES), F32)],
        compiler_params=_cparams(("parallel",)), name="merge",
    )(xt, yc, u, d_skip.reshape(1, -1), w_glu, b, c, gates, wa, wb, wc, wo)


def _pack_w_in(w_in):
    d = w_in.shape[0]
    splits = (SSM_WIDTH, DSA_WIDTH, DSA_LATENT, IDX_HEADS * IDX_DIM, IDX_DIM, IDX_HEADS,
              FOX_WIDTH, FOX_WIDTH, FOX_WIDTH, FOX_HEADS, D_MODEL, D_MODEL, D_MODEL)
    pts = np.cumsum(splits)[:-1]
    (w_u, w_qb, w_ckv, w_qi, w_ki, w_wi, w_qc, w_kc, w_vc, w_fc, w_ga, w_gb, w_gc) = jnp.split(
        w_in, pts, axis=1)
    scale = HEAD_DIM ** -0.5
    aux = jnp.zeros((d, LANES), F32)
    aux = aux.at[:, AUX_F:AUX_F + FOX_HEADS].set(w_fc).at[:, AUX_W:AUX_W + IDX_HEADS].set(w_wi)
    cols = [w_u, w_qb * scale, w_qi, w_ki, w_ki, w_ckv, aux, w_qc * scale, w_kc, w_vc,
            w_ga, w_gb, w_gc]
    return jnp.concatenate(cols, axis=1).astype(BF16)


def kernel(x, ffn1_norm, ffn1_w_gate, ffn1_w_up, ffn1_w_down, mix_norm, w_in, ssm_lambda_re, ssm_lambda_im, ssm_log_dt, ssm_b_re, ssm_b_im, ssm_c_re, ssm_c_im, ssm_d, ssm_w_glu, dsa_kv_norm, dsa_w_uk, dsa_w_uv, rel_bias, fox_b_f, w_branch_ssm, w_branch_dsa, w_branch_fox, w_out, ffn2_norm, ffn2_w_gate, ffn2_w_up, ffn2_w_down, final_norm):
    bsz, seq, d = x.shape
    depth = w_in.shape[0]
    n_keys = min(TOPK_MAX, seq // 4)
    assert d == D_MODEL and seq % ATT_TILE == 0 and (bsz * seq) % (8 * 512) == 0
    assert n_keys % LANES == 0 and bsz % 8 == 0
    m = bsz * seq
    n_chunks = seq // S5_CHUNK
    xt = x.reshape(m, d)
    band = _rel_band(rel_bias)
    bf = lambda w: w.astype(BF16)
    for l in range(depth):
        xt = _ffn(xt, ffn1_norm[l], bf(ffn1_w_gate[l]), bf(ffn1_w_up[l]), bf(ffn1_w_down[l]))
        w_kv = bf(jnp.concatenate([dsa_w_uk[l], dsa_w_uk[l], dsa_w_uv[l], dsa_w_uv[l]], axis=1))
        u, qb, qi, ki, kv, aux, qkv, gates, u8 = _inproj(
            xt, mix_norm[l], _pack_w_in(w_in[l]), dsa_kv_norm[l], w_kv)
        m_intra, m_sum, m_out, a_re, a_im = _s5_matrices(
            ssm_lambda_re[l], ssm_lambda_im[l], ssm_log_dt[l], ssm_b_re[l], ssm_b_im[l],
            ssm_c_re[l], ssm_c_im[l])
        y_intra, p = _s5a(u8, m_intra, m_sum)
        h_re, h_im = _s5b(p, a_re, a_im, n_chunks, nb=8)
        yc = _s5c(y_intra, h_re, h_im, m_out)
        b_out = _dsa(qb, qi, aux, kv, ki, band, bsz, seq, n_keys)
        c_col, c_row = _fox_cum(aux, fox_b_f[l], bsz, seq)
        c_out = _fox(qkv, c_col, c_row, bsz, seq)
        xt = _merge(xt, yc, u, ssm_d[l], bf(ssm_w_glu[l]), b_out, c_out, gates,
                    bf(w_branch_ssm[l]), bf(w_branch_dsa[l]), bf(w_branch_fox[l]), bf(w_out[l]))
        last = final_norm if l == depth - 1 else None
        xt = _ffn(xt, ffn2_norm[l], bf(ffn2_w_gate[l]), bf(ffn2_w_up[l]), bf(ffn2_w_down[l]),
                  final_norm=last)
    return xt.reshape(bsz, seq, d)
```

```python
import functools
import math

import numpy as np
import jax
import jax.numpy as jnp
from jax import lax
from jax.experimental import pallas as pl
from jax.experimental.pallas import tpu as pltpu

F32 = jnp.float32
BF16 = jnp.bfloat16

D_MODEL = 1024
SSM_WIDTH = D_MODEL // 4
SSM_GROUP = 16
SSM_GROUPS = SSM_WIDTH // SSM_GROUP
SSM_STATE = 64
HEAD_DIM = 64
DSA_HEADS = 6
DSA_WIDTH = DSA_HEADS * HEAD_DIM
DSA_LATENT = 2 * HEAD_DIM
IDX_HEADS = 4
IDX_DIM = 64
TOPK_MAX = 256
FOX_HEADS = 6
FOX_WIDTH = FOX_HEADS * HEAD_DIM
REL_BUCKETS = 32
REL_MAX_EXACT = 16
REL_MAX_DIST = 128
FFN_HIDDEN = 2816
EPS = 1e-6
NEG = -1e30
LOG2E = math.log2(math.e)

LANES = 128
S5_CHUNK = 8
S5_COLS = S5_CHUNK * SSM_WIDTH
S5_STATE = 2 * SSM_GROUPS * SSM_STATE
ATT_TILE = 256
SEARCH_FIXED = 12
SEARCH_UNROLL = 2
VMEM_LIMIT = 56 * 1024 * 1024

_SEG = dict(u=256, qb=384, qi=256, ki=128, ckv=128, aux=128, qkv=1152, g=3072)
AUX_F = 0
AUX_W = 8


def _cparams(sem):
    return pltpu.CompilerParams(dimension_semantics=sem, vmem_limit_bytes=VMEM_LIMIT)


def _resident(shape):
    nd = len(shape)
    return pl.BlockSpec(shape, lambda *_: (0,) * nd, pipeline_mode=pl.Buffered(1))


def _rms(x, g):
    return x * lax.rsqrt(jnp.mean(x * x, axis=-1, keepdims=True) + EPS) * g


def _sigmoid(x):
    return 1.0 / (1.0 + jnp.exp(-x))


def _dot(a, b):
    return jnp.dot(a, b, preferred_element_type=F32)


def _ffn_body(x, g_ref, wg_ref, wu_ref, wd_ref, fc):
    h = _rms(x, g_ref[...]).astype(BF16)
    acc = None
    for c in range(FFN_HIDDEN // fc):
        sl = slice(c * fc, (c + 1) * fc)
        gt = _dot(h, wg_ref[:, sl])
        up = _dot(h, wu_ref[:, sl])
        a = (gt * _sigmoid(gt) * up).astype(BF16)
        d = _dot(a, wd_ref[sl, :])
        acc = d if acc is None else acc + d
    return x + 0.5 * acc


def _ffn_kernel(x_ref, g_ref, wg_ref, wu_ref, wd_ref, o_ref, *, fc):
    o_ref[...] = _ffn_body(x_ref[...], g_ref, wg_ref, wu_ref, wd_ref, fc)


def _ffn_final_kernel(x_ref, g_ref, wg_ref, wu_ref, wd_ref, fn_ref, o_ref, *, fc):
    y = _ffn_body(x_ref[...], g_ref, wg_ref, wu_ref, wd_ref, fc)
    o_ref[...] = _rms(y, fn_ref[...])


def _ffn(xt, norm, wg, wu, wd, final_norm=None, tm=1024, fc=256):
    m, d = xt.shape
    f = wg.shape[1]
    row = pl.BlockSpec((tm, d), lambda i: (i, 0))
    vec = _resident((1, d))
    in_specs = [row, vec, _resident((d, f)), _resident((d, f)), _resident((f, d))]
    args = [xt, norm.reshape(1, d), wg, wu, wd]
    if final_norm is None:
        body = functools.partial(_ffn_kernel, fc=fc)
    else:
        body = functools.partial(_ffn_final_kernel, fc=fc)
        in_specs.append(vec)
        args.append(final_norm.reshape(1, d))
    return pl.pallas_call(
        body, grid=(m // tm,), in_specs=in_specs, out_specs=row,
        out_shape=jax.ShapeDtypeStruct((m, d), F32),
        compiler_params=_cparams(("parallel",)), name="ffn")(*args)


def _rows_to_chunks(x, tmp_ref, o_ref):
    n, w = x.shape
    for hh in range(w // LANES):
        tmp_ref[hh] = x[:, hh * LANES:(hh + 1) * LANES]
    for i in range(S5_CHUNK):
        for hh in range(w // LANES):
            o_ref[:, i * w + hh * LANES:i * w + (hh + 1) * LANES] = (
                tmp_ref[hh, pl.ds(i, n // S5_CHUNK, stride=S5_CHUNK), :])


def _chunks_to_rows(x_ref, tmp_ref, w):
    n = x_ref.shape[0]
    for i in range(S5_CHUNK):
        for hh in range(w // LANES):
            tmp_ref[hh, pl.ds(i, n, stride=S5_CHUNK), :] = (
                x_ref[:, i * w + hh * LANES:i * w + (hh + 1) * LANES])
    return jnp.concatenate([tmp_ref[hh] for hh in range(w // LANES)], axis=1)


def _inproj_kernel(x_ref, g_ref, w_ref, kvn_ref, wkv_ref,
                   u_ref, qb_ref, qi_ref, ki_ref, kv_ref, aux_ref, qkv_ref, gate_ref, u8_ref,
                   tmp_ref):
    h = _rms(x_ref[...], g_ref[...]).astype(BF16)
    outs = dict(u=u_ref, qb=qb_ref, qi=qi_ref, ki=ki_ref, aux=aux_ref, qkv=qkv_ref)
    off = 0
    for group in (("u", "qb", "qi", "ki"), ("ckv", "aux", "qkv")):
        width = sum(_SEG[name] for name in group)
        r = _dot(h, w_ref[:, off:off + width])
        off += width
        lo = 0
        for name in group:
            piece = r[:, lo:lo + _SEG[name]]
            lo += _SEG[name]
            if name == "ckv":
                c = _rms(piece, kvn_ref[...]).astype(BF16)
                kv_ref[...] = _dot(c, wkv_ref[...]).astype(BF16)
            else:
                outs[name][...] = piece.astype(outs[name].dtype)
            if name == "u":
                _rows_to_chunks(piece, tmp_ref, u8_ref)
    for c in range(_SEG["g"] // D_MODEL):
        cols = slice(c * D_MODEL, (c + 1) * D_MODEL)
        gate_ref[:, cols] = _dot(h, w_ref[:, off + c * D_MODEL:off + (c + 1) * D_MODEL]).astype(BF16)


def _inproj(xt, norm, w_all, kv_norm, w_kv, tm=1024):
    m, d = xt.shape
    wtot = w_all.shape[1]
    widths = dict(u=(256, F32), qb=(384, BF16), qi=(256, BF16), ki=(128, BF16),
                  kv=(256, BF16), aux=(128, F32), qkv=(1152, BF16), g=(3072, BF16))
    out_shape = [jax.ShapeDtypeStruct((m, w), dt) for w, dt in widths.values()]
    out_specs = [pl.BlockSpec((tm, w), lambda i: (i, 0)) for w, _ in widths.values()]
    out_shape.append(jax.ShapeDtypeStruct((m // S5_CHUNK, S5_COLS), F32))
    out_specs.append(pl.BlockSpec((tm // S5_CHUNK, S5_COLS), lambda i: (i, 0)))
    return pl.pallas_call(
        _inproj_kernel, grid=(m // tm,),
        in_specs=[pl.BlockSpec((tm, d), lambda i: (i, 0)), _resident((1, d)),
                  _resident((d, wtot)), _resident((1, DSA_LATENT)),
                  _resident((DSA_LATENT, 256))],
        out_specs=out_specs, out_shape=out_shape,
        scratch_shapes=[pltpu.VMEM((SSM_WIDTH // LANES, tm, LANES), F32)],
        compiler_params=_cparams(("parallel",)), name="inproj",
    )(xt, norm.reshape(1, d), w_all, kv_norm.reshape(1, DSA_LATENT), w_kv)


S5_SLABS = S5_STATE // LANES
S5_SCAN_SLABS = 4


def _s5a_kernel(u_ref, mi_ref, ms_ref, yi_ref, p_ref):
    u = u_ref[...].astype(BF16)
    yi_ref[...] = _dot(u, mi_ref[...])
    p = _dot(u, ms_ref[...])
    for k in range(S5_SLABS):
        p_ref[k] = p[:, k * LANES:(k + 1) * LANES]


def _s5a(u8, m_intra, m_sum, tr=512):
    r = u8.shape[0]
    row = pl.BlockSpec((tr, S5_COLS), lambda i: (i, 0))
    slabs = pl.BlockSpec((S5_SLABS, tr, LANES), lambda i: (0, i, 0))
    return pl.pallas_call(
        _s5a_kernel, grid=(r // tr,),
        in_specs=[row, _resident((S5_COLS, S5_COLS)), _resident((S5_COLS, S5_STATE))],
        out_specs=[row, slabs],
        out_shape=[jax.ShapeDtypeStruct((r, S5_COLS), F32),
                   jax.ShapeDtypeStruct((S5_SLABS, r, LANES), F32)],
        compiler_params=_cparams(("parallel",)), name="s5_chunk")(u8, m_intra, m_sum)


def _s5b_kernel(pr_ref, pi_ref, ar_ref, ai_ref, hr_ref, hi_ref, *, nb, nc):
    ar = ar_ref[...]
    ai = ai_ref[...]
    ns = S5_SCAN_SLABS

    def step(c, carry):
        hr, hi = carry
        rows = pl.ds(c, nb, stride=nc)
        for k in range(ns):
            hr_ref[k, rows, :] = hr[:, k * LANES:(k + 1) * LANES]
            hi_ref[k, rows, :] = hi[:, k * LANES:(k + 1) * LANES]
        pr = jnp.concatenate([pr_ref[k, rows, :] for k in range(ns)], axis=1)
        pi = jnp.concatenate([pi_ref[k, rows, :] for k in range(ns)], axis=1)
        return (ar * hr - ai * hi + pr, ar * hi + ai * hr + pi)

    z = jnp.zeros((nb, ns * LANES), F32)
    lax.fori_loop(0, nc, step, (z, z))


def _s5b(p, a_re, a_im, nc, nb):
    r = p.shape[1]
    ns = S5_SCAN_SLABS
    half = S5_SLABS // 2
    re_blk = pl.BlockSpec((ns, nb * nc, LANES), lambda i, j: (j, i, 0))
    im_blk = pl.BlockSpec((ns, nb * nc, LANES), lambda i, j: (j + half // ns, i, 0))
    a_blk = pl.BlockSpec((1, ns * LANES), lambda i, j: (0, j))
    return pl.pallas_call(
        functools.partial(_s5b_kernel, nb=nb, nc=nc), grid=(r // (nb * nc), half // ns),
        in_specs=[re_blk, im_blk, a_blk, a_blk],
        out_specs=[re_blk, re_blk],
        out_shape=[jax.ShapeDtypeStruct((half, r, LANES), F32)] * 2,
        compiler_params=_cparams(("parallel", "parallel")), name="s5_scan",
    )(p, p, a_re, a_im)


def _s5c_kernel(yi_ref, hr_ref, hi_ref, mo_ref, y_ref):
    half = S5_STATE // 2
    hr = jnp.concatenate([hr_ref[k] for k in range(S5_SLABS // 2)], axis=1)
    hi = jnp.concatenate([hi_ref[k] for k in range(S5_SLABS // 2)], axis=1)
    y = yi_ref[...]
    y = y + _dot(hr.astype(BF16), mo_ref[:half, :])
    y = y + _dot(hi.astype(BF16), mo_ref[half:, :])
    y_ref[...] = y


def _s5c(yi, h_re, h_im, m_out, tr=512):
    r = yi.shape[0]
    row = pl.BlockSpec((tr, S5_COLS), lambda i: (i, 0))
    slabs = pl.BlockSpec((S5_SLABS // 2, tr, LANES), lambda i: (0, i, 0))
    return pl.pallas_call(
        _s5c_kernel, grid=(r // tr,),
        in_specs=[row, slabs, slabs, _resident((S5_STATE, S5_COLS))],
        out_specs=row, out_shape=jax.ShapeDtypeStruct((r, S5_COLS), F32),
        compiler_params=_cparams(("parallel",)), name="s5_out")(yi, h_re, h_im, m_out)


def _toeplitz_kernel(lag_ref, o_ref):
    keep = pl.program_id(1) >= pl.program_id(0)
    o_ref[...] = jnp.where(keep, lag_ref[...], jnp.zeros_like(lag_ref))


def _block_toeplitz(lag_blocks):
    t, w, _ = lag_blocks.shape
    return pl.pallas_call(
        _toeplitz_kernel, grid=(t, t),
        in_specs=[pl.BlockSpec((None, w, w), lambda j, i: (jnp.maximum(i - j, 0), 0, 0))],
        out_specs=pl.BlockSpec((w, w), lambda j, i: (j, i)),
        out_shape=jax.ShapeDtypeStruct((t * w, t * w), lag_blocks.dtype), name="s5_toeplitz",
    )(lag_blocks)


def _s5_matrices(lam_re, lam_im, log_dt, b_re, b_im, c_re, c_im):
    hp = lax.Precision.HIGHEST
    t = S5_CHUNK
    dt = jnp.exp(log_dt)[:, None]
    lr, li = lam_re, lam_im
    mag = jnp.exp(lr * dt)
    ab_re, ab_im = mag * jnp.cos(li * dt), mag * jnp.sin(li * dt)
    den = lr * lr + li * li
    nr, ni = ab_re - 1.0, ab_im
    s_re = (nr * lr + ni * li) / den
    s_im = (ni * lr - nr * li) / den
    bb_re = s_re[..., None] * b_re - s_im[..., None] * b_im
    bb_im = s_re[..., None] * b_im + s_im[..., None] * b_re
    pr, pi = [jnp.ones_like(ab_re)], [jnp.zeros_like(ab_re)]
    for _ in range(t):
        pr.append(pr[-1] * ab_re - pi[-1] * ab_im)
        pi.append(pr[-2] * ab_im + pi[-1] * ab_re)
    pw_re, pw_im = jnp.stack(pr), jnp.stack(pi)
    lo_re, lo_im = jnp.stack(pr[:t]), jnp.stack(pi[:t])
    ab_b_re = lo_re[..., None] * bb_re - lo_im[..., None] * bb_im
    ab_b_im = lo_re[..., None] * bb_im + lo_im[..., None] * bb_re
    kern = (jnp.einsum('gpn,dgnq->dgpq', c_re, ab_b_re, precision=hp)
            - jnp.einsum('gpn,dgnq->dgpq', c_im, ab_b_im, precision=hp))
    eye_g = jnp.eye(SSM_GROUPS, dtype=F32)
    rows = np.arange(SSM_WIDTH)
    expand = jnp.asarray((rows[None, :] % SSM_GROUP == np.arange(SSM_GROUP)[:, None]).astype(np.float32))
    same_group = jnp.asarray(rows[:, None] // SSM_GROUP == rows[None, :] // SSM_GROUP)
    lag_rows = kern.transpose(0, 1, 3, 2).reshape(t, SSM_WIDTH, SSM_GROUP)
    lag_blocks = jnp.where(same_group, jnp.einsum('drp,pc->drc', lag_rows, expand, precision=hp), 0.0)
    m_intra = _block_toeplitz(lag_blocks.astype(BF16))
    hi_re, hi_im = jnp.stack(pr[t - 1::-1]), jnp.stack(pi[t - 1::-1])
    sb_re = hi_re[..., None] * bb_re - hi_im[..., None] * bb_im
    sb_im = hi_re[..., None] * bb_im + hi_im[..., None] * bb_re
    m_sum_re = jnp.einsum('jgnq,gh->jgqhn', sb_re, eye_g, precision=hp)
    m_sum_im = jnp.einsum('jgnq,gh->jgqhn', sb_im, eye_g, precision=hp)
    m_sum = jnp.concatenate([m_sum_re.reshape(S5_COLS, -1), m_sum_im.reshape(S5_COLS, -1)], axis=1)
    up_re, up_im = jnp.stack(pr[1:])[:, :, None, :], jnp.stack(pi[1:])[:, :, None, :]
    w_re = c_re[None] * up_re - c_im[None] * up_im
    w_im = c_re[None] * up_im + c_im[None] * up_re
    m_out_re = jnp.einsum('igpn,gh->gnihp', w_re, eye_g, precision=hp)
    m_out_im = jnp.einsum('igpn,gh->gnihp', -w_im, eye_g, precision=hp)
    m_out = jnp.concatenate([m_out_re.reshape(-1, S5_COLS), m_out_im.reshape(-1, S5_COLS)], axis=0)
    a_re = pw_re[t].reshape(1, -1)
    a_im = pw_im[t].reshape(1, -1)
    return m_intra.astype(BF16), m_sum.astype(BF16), m_out.astype(BF16), a_re, a_im


def _fold8(x, op):
    parts = [x[r:r + 8] for r in range(0, x.shape[0], 8)]
    while len(parts) > 1:
        parts = [op(parts[i], parts[i + 1]) for i in range(0, len(parts), 2)]
    return parts[0]


def _split_heads_t(q_ref, qt_ref, n_heads):
    tq = q_ref.shape[0]
    top = lax.broadcasted_iota(jnp.int32, (LANES, tq), 0) < HEAD_DIM
    for pr in range(n_heads // 2):
        pair_t = q_ref[:, pr * LANES:(pr + 1) * LANES].astype(F32).T
        qt_ref[:, 2 * pr * tq:(2 * pr + 1) * tq] = jnp.where(top, pair_t, 0.0).astype(BF16)
        qt_ref[:, (2 * pr + 1) * tq:(2 * pr + 2) * tq] = jnp.where(top, 0.0, pair_t).astype(BF16)


def _softmax_stage(s, m_ref, alpha_ref, p_ref):
    m_old = m_ref[...]
    m_new = jnp.maximum(m_old, jnp.max(_fold8(s, jnp.maximum), axis=0, keepdims=True))
    alpha_ref[...] = jnp.exp2(m_old - m_new)
    p_ref[...] = jnp.exp2(s - jnp.broadcast_to(m_new[0:1], s.shape)).astype(BF16)
    m_ref[...] = m_new


def _value_stage(pv, alpha_ref, p_ref, acc_ref):
    acc = acc_ref[...]
    acc_ref[...] = acc * jnp.broadcast_to(alpha_ref[0:1, :], acc.shape) + pv(p_ref[...])


def _attend_chunks(qt, qk, logits, pv, bufs, m_ref, acc_ref):
    s_a, s_b, p_a, p_b, al_a, al_b = bufs

    def plain(j, kind):
        qk(j, s_a)
        _softmax_stage(logits(j, s_a[...], kind), m_ref, al_a, p_a)
        _value_stage(lambda p: pv(j, p), al_a, p_a, acc_ref)

    @pl.when(qt == 0)
    def _():
        plain(0, 'diag')

    @pl.when(qt >= 1)
    def _():
        n_far = qt - 1
        first = n_far % 2

        @pl.when(first == 1)
        def _():
            plain(0, 'far')

        qk(first, s_a)
        p_b[...] = jnp.zeros(p_b.shape, p_b.dtype)
        al_b[...] = jnp.ones(al_b.shape, al_b.dtype)

        def pair(c, kind0, kind1, last):
            qk(c + 1, s_b)
            _value_stage(lambda p: pv(jnp.maximum(c - 1, 0), p), al_b, p_b, acc_ref)
            _softmax_stage(logits(c, s_a[...], kind0), m_ref, al_a, p_a)
            if not last:
                qk(c + 2, s_a)
            _value_stage(lambda p: pv(c, p), al_a, p_a, acc_ref)
            _softmax_stage(logits(c + 1, s_b[...], kind1), m_ref, al_b, p_b)

        def body(i, carry):
            pair(first + 2 * i, 'far', 'far', False)
            return carry

        lax.fori_loop(0, (n_far - first) // 2, body, 0)
        pair(qt - 1, 'prev', 'diag', True)
        _value_stage(lambda p: pv(qt, p), al_b, p_b, acc_ref)


def _pipeline_buffers(width):
    return ([pltpu.VMEM((ATT_TILE, width), F32)] * 2 + [pltpu.VMEM((ATT_TILE, width), BF16)] * 2
            + [pltpu.VMEM((8, width), F32)] * 2)


def _finish_heads_t(acc_ref, o_ref, n_heads, odd_swapped):
    hd = HEAD_DIM
    tq = o_ref.shape[0]
    for pr in range(n_heads // 2):
        ae = acc_ref[:, 2 * pr * tq:(2 * pr + 1) * tq]
        ao = acc_ref[:, (2 * pr + 1) * tq:(2 * pr + 2) * tq]
        oe = ae[0:hd] / ae[hd:2 * hd]
        oo = ao[hd:2 * hd] / ao[0:hd] if odd_swapped else ao[0:hd] / ao[hd:2 * hd]
        pair = jnp.concatenate([oe, oo], axis=0)
        o_ref[:, pr * LANES:(pr + 1) * LANES] = pair.T.astype(o_ref.dtype)


def _dsa_kernel(q_ref, qi_ref, aux_ref, kv_ref, ki_ref, band_ref, o_ref,
                score_ref, vt_ref, qm_ref, m_ref, acc_ref, thr_ref, *bufs, n_keys, seq):
    tq = tk = ATT_TILE
    qt = pl.program_id(1)
    n_chunks = qt + 1
    kf = float(n_keys)

    @pl.when(qt == 0)
    def _():
        top = lax.broadcasted_iota(jnp.int32, (LANES, tk), 0) < HEAD_DIM
        for j in range(seq // tk):
            vv = kv_ref[j * tk:(j + 1) * tk, LANES:2 * LANES].astype(F32)
            vt_ref[j] = jnp.where(top, vv.T, 1.0).astype(BF16)

    _split_heads_t(qi_ref, qm_ref, IDX_HEADS)
    aux_t = aux_ref[...].T
    head_w = jnp.concatenate(
        [aux_t[AUX_W + h:AUX_W + h + 1, :] for h in range(IDX_HEADS)], axis=1
    ) * (IDX_HEADS ** -0.5 * IDX_DIM ** -0.5)
    key_minus_query = (lax.broadcasted_iota(jnp.int32, (tk, tq), 0)
                       - lax.broadcasted_iota(jnp.int32, (tk, tq), 1))

    def score_body(j, carry):
        mx, mn = carry
        kc = ki_ref[pl.ds(pl.multiple_of(j * tk, tk), tk), :]
        d = _dot(kc, qm_ref[:, 0:IDX_HEADS * tq])
        d = jnp.broadcast_to(head_w, d.shape) * jnp.maximum(d, 0.0)
        sc = (d[:, 0:tq] + d[:, tq:2 * tq]) + (d[:, 2 * tq:3 * tq] + d[:, 3 * tq:4 * tq])
        causal = key_minus_query <= (qt - j) * tq
        score_ref[j] = jnp.where(causal, sc, NEG)
        mx = jnp.maximum(mx, _fold8(jnp.where(causal, sc, NEG), jnp.maximum))
        mn = jnp.minimum(mn, _fold8(jnp.where(causal, sc, -NEG), jnp.minimum))
        return mx, mn

    mx, mn = lax.fori_loop(0, n_chunks, score_body,
                           (jnp.full((8, tq), NEG, F32), jnp.full((8, tq), -NEG, F32)))

    t_q = qt * tq + lax.broadcasted_iota(jnp.int32, (1, tq), 1)
    thr_ref[...] = jnp.full((8, tq), 0.5 * NEG, F32)

    def count(pred):
        def body(j, acc):
            return acc + _fold8(pred(score_ref[j]), jnp.add)
        acc = lax.fori_loop(0, n_chunks, body, jnp.zeros((8, tq), F32))
        return jnp.sum(acc, axis=0, keepdims=True)

    def count_ge(x):
        xb = jnp.broadcast_to(x, (tk, tq))
        return count(lambda s: jnp.where(s >= xb, 1.0, 0.0))

    @pl.when((qt + 1) * tq > n_keys)
    def _search():
        need = (t_q >= n_keys).astype(F32)
        lo0 = jnp.min(mn, axis=0, keepdims=True)
        hi0 = jnp.max(mx, axis=0, keepdims=True)
        hi_top = hi0 + jnp.maximum(jnp.abs(hi0) * 2.0 ** -22, 1e-30)
        c_ge0 = count_ge(jnp.zeros((1, tq), F32))
        c_gt0 = count(lambda s: jnp.where(s > 0.0, 1.0, 0.0))
        n_valid = (t_q + 1).astype(F32)
        zero_hit = c_ge0 == kf
        zero_tie = (c_gt0 < kf) & (c_ge0 > kf)
        positive = c_gt0 >= kf
        hi_neg = hi0 < 0.0
        lo1 = jnp.where(positive, jnp.maximum(lo0, 0.0), lo0)
        c_lo1 = jnp.where(positive, c_ge0, n_valid)
        hi1 = jnp.where(positive | hi_neg, hi_top, 0.0)
        c_hi1 = jnp.where(positive | hi_neg, 0.0, c_ge0)
        hit0 = need * zero_hit.astype(F32)
        tie0 = need * zero_tie.astype(F32)
        done0 = jnp.maximum(1.0 - need, jnp.maximum(hit0, tie0))
        lo1 = jnp.where(tie0 > 0, 0.0, lo1)
        c_hi1 = jnp.where(tie0 > 0, c_gt0, c_hi1)
        x0 = jnp.zeros((1, tq), F32)
        state0 = (lo1, hi1, c_lo1, c_hi1, done0, tie0, x0, jnp.max(1.0 - done0))

        def cond(st):
            return st[7] > 0.0

        def body(st):
            for _ in range(SEARCH_UNROLL):
                st = step(st)
            return st

        def step(st):
            lo, hi, c_lo, c_hi, done, tie, x_fin, _ = st
            x = 0.5 * lo + 0.5 * hi
            inside = ((x > lo) & (x < hi)).astype(F32)
            active = 1.0 - done
            probe = active * inside
            c = count_ge(jnp.where(probe > 0, x, x_fin))
            hit = probe * (c == kf).astype(F32)
            more = probe * (c > kf).astype(F32)
            less = probe * (c < kf).astype(F32)
            new_tie = active * (1.0 - inside)
            lo = jnp.where(more > 0, x, lo)
            c_lo = jnp.where(more > 0, c, c_lo)
            hi = jnp.where(less > 0, x, hi)
            c_hi = jnp.where(less > 0, c, c_hi)
            x_fin = jnp.where(hit > 0, x, x_fin)
            tie = jnp.maximum(tie, new_tie)
            done = jnp.maximum(done, jnp.maximum(hit, new_tie))
            return (lo, hi, c_lo, c_hi, done, tie, x_fin, jnp.max(1.0 - done))

        st = state0
        for _ in range(SEARCH_FIXED):
            st = step(st)
        st = lax.while_loop(cond, body, st)
        lo, _, _, c_hi, _, tie, x_fin, _ = st
        thr = jnp.where(need > 0, jnp.where(tie > 0, lo, x_fin), 0.5 * NEG)
        thr_ref[...] = jnp.broadcast_to(thr, (8, tq))

        @pl.when(jnp.max(tie) > 0.0)
        def _ties():
            want = jnp.broadcast_to(kf - c_hi, (tk, tq))
            thr_t = jnp.broadcast_to(jnp.where(tie > 0, thr, -NEG), (tk, tq))
            upto = (lax.broadcasted_iota(jnp.int32, (tk, tk), 1)
                    <= lax.broadcasted_iota(jnp.int32, (tk, tk), 0))
            prefix = jnp.where(upto, 1.0, 0.0).astype(BF16)

            def body(j, seen):
                s = score_ref[j]
                eq = s == thr_t
                rank = _dot(prefix, jnp.where(eq, 1.0, 0.0).astype(BF16))
                rank = rank + jnp.broadcast_to(seen, (tk, tq))
                score_ref[j] = jnp.where(eq, jnp.where(rank > want, NEG, s), s)
                return rank[tk - 1:tk, :]

            lax.fori_loop(0, n_chunks, body, jnp.zeros((1, tq), F32))

    _split_heads_t(q_ref, qm_ref, DSA_HEADS)
    m_ref[...] = jnp.full(m_ref.shape, NEG, F32)
    acc_ref[...] = jnp.zeros(acc_ref.shape, F32)
    thr_b = jnp.broadcast_to(thr_ref[0:1, :], (tk, tq))

    def qk(j, s_ref):
        rows = pl.ds(pl.multiple_of(j * tk, tk), tk)
        s_ref[...] = _dot(kv_ref[rows, 0:LANES], qm_ref[...])

    def logits(j, s, kind):
        off = jnp.where(score_ref[j] >= thr_b, 0.0, NEG)
        s = s + jnp.concatenate([off] * DSA_HEADS, axis=1)
        if kind == 'prev':
            s = s + band_ref[0:tk, :]
        elif kind == 'diag':
            s = s + band_ref[tk:2 * tk, :]
        return s

    _attend_chunks(qt, qk, logits, lambda j, p: _dot(vt_ref[j], p), bufs, m_ref, acc_ref)
    _finish_heads_t(acc_ref, o_ref, DSA_HEADS, odd_swapped=False)


def _dsa(qb, qi, aux, kv, ki, band, bsz, seq, n_keys):
    tq = ATT_TILE
    nq = seq // tq
    qrow = lambda w: pl.BlockSpec((tq, w), lambda b, q: (b * nq + q, 0))
    seq_blk = lambda w: pl.BlockSpec((seq, w), lambda b, q: (b, 0))
    return pl.pallas_call(
        functools.partial(_dsa_kernel, n_keys=n_keys, seq=seq), grid=(bsz, nq),
        in_specs=[qrow(DSA_WIDTH), qrow(256), qrow(LANES), seq_blk(256), seq_blk(LANES),
                  _resident((2 * tq, DSA_HEADS * tq))],
        out_specs=qrow(DSA_WIDTH),
        out_shape=jax.ShapeDtypeStruct((bsz * seq, DSA_WIDTH), BF16),
        scratch_shapes=[pltpu.VMEM((nq, tq, tq), F32),
                        pltpu.VMEM((nq, LANES, tq), BF16),
                        pltpu.VMEM((LANES, DSA_HEADS * tq), BF16),
                        pltpu.VMEM((8, DSA_HEADS * tq), F32),
                        pltpu.VMEM((LANES, DSA_HEADS * tq), F32),
                        pltpu.VMEM((8, tq), F32)] + _pipeline_buffers(DSA_HEADS * tq),
        compiler_params=_cparams(("parallel", "arbitrary")), name="dsa",
    )(qb, qi, aux, kv, ki, band)


def _band_kernel(bucket_ref, rb_ref, o_ref):
    bucket = bucket_ref[...]
    tq = bucket.shape[1]
    for h in range(DSA_HEADS):
        acc = jnp.zeros(bucket.shape, F32)
        for k in range(REL_BUCKETS - 1):
            acc = jnp.where(bucket == k, (rb_ref[k, h] - rb_ref[REL_BUCKETS - 1, h]) * LOG2E, acc)
        o_ref[:, h * tq:(h + 1) * tq] = acc


def _t5_bucket_np(dist):
    d = np.maximum(dist, 0)
    df = np.maximum(d, 1).astype(np.float32)
    log_b = REL_MAX_EXACT + (np.log(df / REL_MAX_EXACT) / math.log(REL_MAX_DIST / REL_MAX_EXACT)
                             * (REL_BUCKETS - REL_MAX_EXACT)).astype(np.int32)
    log_b = np.minimum(log_b, REL_BUCKETS - 1)
    return np.where(d < REL_MAX_EXACT, d, log_b).astype(np.int32)


def _rel_band(rel_bias):
    tq = ATT_TILE
    i = np.arange(tq)[None, :]
    j = np.arange(2 * tq)[:, None]
    bucket = jnp.asarray(_t5_bucket_np(i + tq - j))
    return pl.pallas_call(
        _band_kernel,
        in_specs=[pl.BlockSpec(memory_space=pltpu.VMEM), pl.BlockSpec(memory_space=pltpu.SMEM)],
        out_specs=pl.BlockSpec(memory_space=pltpu.VMEM),
        out_shape=jax.ShapeDtypeStruct((2 * tq, DSA_HEADS * tq), F32), name="rel_band",
    )(bucket, rel_bias)


def _cum_kernel(aux_ref, bf_ref, col_ref, row_ref, *, seq):
    x = aux_ref[...] + bf_ref[...]
    logf = -(jnp.maximum(-x, 0.0) + jnp.log1p(jnp.exp(-jnp.abs(x))))
    c = logf.T[0:8]
    pos = lax.broadcasted_iota(jnp.int32, c.shape, 1)
    shift = 1
    while shift < seq:
        c = c + jnp.where(pos >= shift, pltpu.roll(c, shift, 1), 0.0)
        shift *= 2
    c = c * LOG2E
    col_ref[...] = jnp.concatenate([c, jnp.zeros((LANES - 8, seq), F32)], axis=0).T
    for j in range(seq // ATT_TILE):
        row_ref[j] = c[:, j * ATT_TILE:(j + 1) * ATT_TILE]


def _fox_cum(aux, b_f, bsz, seq):
    nq = seq // ATT_TILE
    bf = jnp.zeros((1, LANES), F32).at[0, AUX_F:AUX_F + FOX_HEADS].set(b_f)
    return pl.pallas_call(
        functools.partial(_cum_kernel, seq=seq), grid=(bsz,),
        in_specs=[pl.BlockSpec((seq, LANES), lambda b: (b, 0)), _resident((1, LANES))],
        out_specs=[pl.BlockSpec((seq, LANES), lambda b: (b, 0)),
                   pl.BlockSpec((None, nq, 8, ATT_TILE), lambda b: (b, 0, 0, 0))],
        out_shape=[jax.ShapeDtypeStruct((bsz * seq, LANES), F32),
                   jax.ShapeDtypeStruct((bsz, nq, 8, ATT_TILE), F32)],
        compiler_params=_cparams(("parallel",)), name="fox_cum")(aux, bf)


def _bf16_pieces(x):
    hi = x.astype(BF16).astype(F32)
    rest = x - hi
    mid = rest.astype(BF16).astype(F32)
    return [hi, mid, (rest - mid).astype(BF16).astype(F32)]


def _fox_kernel(q_ref, k_ref, v_ref, ccol_ref, crow_ref, o_ref,
                vte_ref, vto_ref, ka_ref, qm_ref, m_ref, acc_ref, *bufs, seq):
    tq = tk = ATT_TILE
    qt = pl.program_id(1)
    heads = range(FOX_HEADS)

    @pl.when(qt == 0)
    def _():
        top = lax.broadcasted_iota(jnp.int32, (LANES, tk), 0) < HEAD_DIM
        for j in range(seq // tk):
            rows = slice(j * tk, (j + 1) * tk)
            for pr in range(FOX_HEADS // 2):
                v_t = v_ref[rows, pr * LANES:(pr + 1) * LANES].astype(F32).T
                vte_ref[pr, j] = jnp.where(top, v_t, 1.0).astype(BF16)
                vto_ref[pr, j] = jnp.where(top, 1.0, v_t).astype(BF16)
            cc = ccol_ref[rows, :]
            lane = lax.broadcasted_iota(jnp.int32, (tk, LANES), 1)
            for h in heads:
                k_h = k_ref[rows, (h // 2) * LANES:(h // 2 + 1) * LANES].astype(F32)
                if h % 2 == 1:
                    k_h = pltpu.roll(k_h, HEAD_DIM, 1)
                pieces = _bf16_pieces(jnp.broadcast_to(cc[:, AUX_F + h:AUX_F + h + 1], (tk, LANES)))
                ext = jnp.where(lane < HEAD_DIM + 2 * len(pieces), 1.0, 0.0)
                for i, piece in enumerate(pieces):
                    ext = jnp.where(lane == HEAD_DIM + i, piece, ext)
                ka_ref[h, rows, :] = jnp.where(lane < HEAD_DIM, k_h, ext).astype(BF16)

    c_q = crow_ref[qt]
    row = lax.broadcasted_iota(jnp.int32, (LANES - HEAD_DIM, tq), 0)
    for pr in range(FOX_HEADS // 2):
        pair_t = q_ref[:, pr * LANES:(pr + 1) * LANES].astype(F32).T
        for h in (2 * pr, 2 * pr + 1):
            pieces = _bf16_pieces(c_q[AUX_F + h:AUX_F + h + 1, :])
            ext = jnp.where(row < len(pieces), -1.0, 0.0)
            for i, piece in enumerate(pieces):
                ext = jnp.where(row == len(pieces) + i, jnp.broadcast_to(piece, ext.shape), ext)
            q_h = pair_t[(h % 2) * HEAD_DIM:(h % 2 + 1) * HEAD_DIM]
            qm_ref[:, h * tq:(h + 1) * tq] = jnp.concatenate([q_h, ext], axis=0).astype(BF16)
    m_ref[...] = jnp.full(m_ref.shape, NEG, F32)
    acc_ref[...] = jnp.zeros(acc_ref.shape, F32)
    causal_off = jnp.where(lax.broadcasted_iota(jnp.int32, (tk, tq), 0)
                           <= lax.broadcasted_iota(jnp.int32, (tk, tq), 1), 0.0, NEG)

    def qk(j, s_ref):
        rows = pl.ds(pl.multiple_of(j * tk, tk), tk)
        for h in heads:
            cols = slice(h * tq, (h + 1) * tq)
            s_ref[:, cols] = _dot(ka_ref[h, rows, :], qm_ref[:, cols])

    def logits(j, s, kind):
        if kind == 'diag':
            s = s + jnp.concatenate([causal_off] * FOX_HEADS, axis=1)
        return s

    def pv(j, p):
        return jnp.concatenate(
            [_dot(vte_ref[h // 2, j] if h % 2 == 0 else vto_ref[h // 2, j],
                  p[:, h * tq:(h + 1) * tq]) for h in heads], axis=1)

    _attend_chunks(qt, qk, logits, pv, bufs, m_ref, acc_ref)
    _finish_heads_t(acc_ref, o_ref, FOX_HEADS, odd_swapped=True)


def _fox(qkv, c_col, c_row, bsz, seq):
    tq = ATT_TILE
    nq = seq // tq
    npair = FOX_HEADS // 2
    return pl.pallas_call(
        functools.partial(_fox_kernel, seq=seq), grid=(bsz, nq),
        in_specs=[pl.BlockSpec((tq, FOX_WIDTH), lambda b, q: (b * nq + q, 0)),
                  pl.BlockSpec((seq, FOX_WIDTH), lambda b, q: (b, 1)),
                  pl.BlockSpec((seq, FOX_WIDTH), lambda b, q: (b, 2)),
                  pl.BlockSpec((seq, LANES), lambda b, q: (b, 0)),
                  pl.BlockSpec((None, nq, 8, tq), lambda b, q: (b, 0, 0, 0))],
        out_specs=pl.BlockSpec((tq, FOX_WIDTH), lambda b, q: (b * nq + q, 0)),
        out_shape=jax.ShapeDtypeStruct((bsz * seq, FOX_WIDTH), BF16),
        scratch_shapes=[pltpu.VMEM((npair, nq, LANES, tq), BF16),
                        pltpu.VMEM((npair, nq, LANES, tq), BF16),
                        pltpu.VMEM((FOX_HEADS, seq, LANES), BF16),
                        pltpu.VMEM((LANES, FOX_HEADS * tq), BF16),
                        pltpu.VMEM((8, FOX_HEADS * tq), F32),
                        pltpu.VMEM((LANES, FOX_HEADS * tq), F32)] + _pipeline_buffers(FOX_HEADS * tq),
        compiler_params=_cparams(("parallel", "arbitrary")), name="fox",
    )(qkv, qkv, qkv, c_col, c_row)


def _gelu(y):
    return 0.5 * y * (1.0 + jnp.tanh(math.sqrt(2.0 / math.pi) * (y + 0.044715 * (y * y * y))))


def _merge_kernel(x_ref, yc_ref, u_ref, d_ref, wglu_ref, b_ref, c_ref, gate_ref,
                  wa_ref, wb_ref, wc_ref, wo_ref, o_ref, tmp_ref):
    d = D_MODEL
    y = _chunks_to_rows(yc_ref, tmp_ref, SSM_WIDTH) + d_ref[...] * u_ref[...]
    z = _dot(_gelu(y).astype(BF16), wglu_ref[...])
    a = z[:, :SSM_WIDTH] * _sigmoid(z[:, SSM_WIDTH:])
    merged = _sigmoid(gate_ref[:, 0:d].astype(F32)) * _dot(a.astype(BF16), wa_ref[...])
    merged += _sigmoid(gate_ref[:, d:2 * d].astype(F32)) * _dot(b_ref[...], wb_ref[...])
    merged += _sigmoid(gate_ref[:, 2 * d:3 * d].astype(F32)) * _dot(c_ref[...], wc_ref[...])
    o_ref[...] = x_ref[...] + _dot(merged.astype(BF16), wo_ref[...])


def _merge(xt, yc, u, d_skip, w_glu, b, c, gates, wa, wb, wc, wo, tm=1024):
    m, d = xt.shape
    row = lambda w: pl.BlockSpec((tm, w), lambda i: (i, 0))
    return pl.pallas_call(
        _merge_kernel, grid=(m // tm,),
        in_specs=[row(d), pl.BlockSpec((tm // S5_CHUNK, S5_COLS), lambda i: (i, 0)),
                  row(SSM_WIDTH), _resident((1, SSM_WIDTH)),
                  _resident((SSM_WIDTH, 2 * SSM_WIDTH)), row(DSA_WIDTH), row(FOX_WIDTH), row(3 * d),
                  _resident((SSM_WIDTH, d)), _resident((DSA_WIDTH, d)), _resident((FOX_WIDTH, d)),
                  _resident((d, d))],
        out_specs=row(d), out_shape=jax.ShapeDtypeStruct((m, d), F32),
        scratch_shapes=[pltpu.VMEM((SSM_WIDTH // LANES, tm, LANES), F32)],
        compiler_params=_cparams(("parallel",)), name="merge",
    )(xt, yc, u, d_skip.reshape(1, -1), w_glu, b, c, gates, wa, wb, wc, wo)


def _pack_w_in(w_in):
    d = w_in.shape[0]
    splits = (SSM_WIDTH, DSA_WIDTH, DSA_LATENT, IDX_HEADS * IDX_DIM, IDX_DIM, IDX_HEADS,
              FOX_WIDTH, FOX_WIDTH, FOX_WIDTH, FOX_HEADS, D_MODEL, D_MODEL, D_MODEL)
    pts = np.cumsum(splits)[:-1]
    (w_u, w_qb, w_ckv, w_qi, w_ki, w_wi, w_qc, w_kc, w_vc, w_fc, w_ga, w_gb, w_gc) = jnp.split(
        w_in, pts, axis=1)
    scale = HEAD_DIM ** -0.5 * LOG2E
    aux = jnp.zeros((d, LANES), F32)
    aux = aux.at[:, AUX_F:AUX_F + FOX_HEADS].set(w_fc).at[:, AUX_W:AUX_W + IDX_HEADS].set(w_wi)
    cols = [w_u, w_qb * scale, w_qi, w_ki, w_ki, w_ckv, aux, w_qc * scale, w_kc, w_vc,
            w_ga, w_gb, w_gc]
    return jnp.concatenate(cols, axis=1).astype(BF16)


def kernel(x, ffn1_norm, ffn1_w_gate, ffn1_w_up, ffn1_w_down, mix_norm, w_in, ssm_lambda_re, ssm_lambda_im, ssm_log_dt, ssm_b_re, ssm_b_im, ssm_c_re, ssm_c_im, ssm_d, ssm_w_glu, dsa_kv_norm, dsa_w_uk, dsa_w_uv, rel_bias, fox_b_f, w_branch_ssm, w_branch_dsa, w_branch_fox, w_out, ffn2_norm, ffn2_w_gate, ffn2_w_up, ffn2_w_down, final_norm):
    bsz, seq, d = x.shape
    depth = w_in.shape[0]
    n_keys = min(TOPK_MAX, seq // 4)
    assert d == D_MODEL and seq % ATT_TILE == 0 and (bsz * seq) % (8 * 512) == 0
    assert n_keys % LANES == 0 and bsz % 8 == 0
    m = bsz * seq
    n_chunks = seq // S5_CHUNK
    xt = x.reshape(m, d)
    band = _rel_band(rel_bias)
    bf = lambda w: w.astype(BF16)
    for l in range(depth):
        xt = _ffn(xt, ffn1_norm[l], bf(ffn1_w_gate[l]), bf(ffn1_w_up[l]), bf(ffn1_w_down[l]))
        w_kv = bf(jnp.concatenate([dsa_w_uk[l], dsa_w_uk[l], dsa_w_uv[l], dsa_w_uv[l]], axis=1))
        u, qb, qi, ki, kv, aux, qkv, gates, u8 = _inproj(
            xt, mix_norm[l], _pack_w_in(w_in[l]), dsa_kv_norm[l], w_kv)
        m_intra, m_sum, m_out, a_re, a_im = _s5_matrices(
            ssm_lambda_re[l], ssm_lambda_im[l], ssm_log_dt[l], ssm_b_re[l], ssm_b_im[l],
            ssm_c_re[l], ssm_c_im[l])
        y_intra, p = _s5a(u8, m_intra, m_sum)
        h_re, h_im = _s5b(p, a_re, a_im, n_chunks, nb=8)
        yc = _s5c(y_intra, h_re, h_im, m_out)
        b_out = _dsa(qb, qi, aux, kv, ki, band, bsz, seq, n_keys)
        c_col, c_row = _fox_cum(aux, fox_b_f[l], bsz, seq)
        c_out = _fox(qkv, c_col, c_row, bsz, seq)
        xt = _merge(xt, yc, u, ssm_d[l], bf(ssm_w_glu[l]), b_out, c_out, gates,
                    bf(w_branch_ssm[l]), bf(w_branch_dsa[l]), bf(w_branch_fox[l]), bf(w_out[l]))
        last = final_norm if l == depth - 1 else None
        xt = _ffn(xt, ffn2_norm[l], bf(ffn2_w_gate[l]), bf(ffn2_w_up[l]), bf(ffn2_w_down[l]),
                  final_norm=last)
    return xt.reshape(bsz, seq, d)
```

```python
import functools
import math

import numpy as np
import jax
import jax.numpy as jnp
from jax import lax
from jax.experimental import pallas as pl
from jax.experimental.pallas import tpu as pltpu

F32 = jnp.float32
BF16 = jnp.bfloat16

D_MODEL = 1024
SSM_WIDTH = D_MODEL // 4
SSM_GROUP = 16
SSM_GROUPS = SSM_WIDTH // SSM_GROUP
SSM_STATE = 64
HEAD_DIM = 64
DSA_HEADS = 6
DSA_WIDTH = DSA_HEADS * HEAD_DIM
DSA_LATENT = 2 * HEAD_DIM
IDX_HEADS = 4
IDX_DIM = 64
TOPK_MAX = 256
FOX_HEADS = 6
FOX_WIDTH = FOX_HEADS * HEAD_DIM
REL_BUCKETS = 32
REL_MAX_EXACT = 16
REL_MAX_DIST = 128
FFN_HIDDEN = 2816
EPS = 1e-6
NEG = -1e30
LOG2E = math.log2(math.e)

LANES = 128
S5_CHUNK = 8
S5_COLS = S5_CHUNK * SSM_WIDTH
S5_STATE = 2 * SSM_GROUPS * SSM_STATE
ATT_TILE = 256
SEARCH_FIXED = 12
SEARCH_UNROLL = 2
VMEM_LIMIT = 56 * 1024 * 1024

_SEG = dict(u=256, qb=384, qi=256, ki=128, ckv=128, aux=128, qkv=1152, g=3072)
AUX_F = 0
AUX_W = 8


def _cparams(sem):
    return pltpu.CompilerParams(dimension_semantics=sem, vmem_limit_bytes=VMEM_LIMIT)


def _resident(shape):
    nd = len(shape)
    return pl.BlockSpec(shape, lambda *_: (0,) * nd, pipeline_mode=pl.Buffered(1))


def _rms(x, g):
    return x * lax.rsqrt(jnp.mean(x * x, axis=-1, keepdims=True) + EPS) * g


def _sigmoid(x):
    return 1.0 / (1.0 + jnp.exp(-x))


def _dot(a, b):
    return jnp.dot(a, b, preferred_element_type=F32)


def _ffn_body(x, g_ref, wg_ref, wu_ref, wd_ref, fc):
    h = _rms(x, g_ref[...]).astype(BF16)
    acc = None
    for c in range(FFN_HIDDEN // fc):
        sl = slice(c * fc, (c + 1) * fc)
        gt = _dot(h, wg_ref[:, sl])
        up = _dot(h, wu_ref[:, sl])
        a = (gt * _sigmoid(gt) * up).astype(BF16)
        d = _dot(a, wd_ref[sl, :])
        acc = d if acc is None else acc + d
    return x + 0.5 * acc


def _ffn_kernel(x_ref, g_ref, wg_ref, wu_ref, wd_ref, o_ref, *, fc):
    o_ref[...] = _ffn_body(x_ref[...], g_ref, wg_ref, wu_ref, wd_ref, fc)


def _ffn_final_kernel(x_ref, g_ref, wg_ref, wu_ref, wd_ref, fn_ref, o_ref, *, fc):
    y = _ffn_body(x_ref[...], g_ref, wg_ref, wu_ref, wd_ref, fc)
    o_ref[...] = _rms(y, fn_ref[...])


def _ffn(xt, norm, wg, wu, wd, final_norm=None, tm=1024, fc=256):
    m, d = xt.shape
    f = wg.shape[1]
    row = pl.BlockSpec((tm, d), lambda i: (i, 0))
    vec = _resident((1, d))
    in_specs = [row, vec, _resident((d, f)), _resident((d, f)), _resident((f, d))]
    args = [xt, norm.reshape(1, d), wg, wu, wd]
    if final_norm is None:
        body = functools.partial(_ffn_kernel, fc=fc)
    else:
        body = functools.partial(_ffn_final_kernel, fc=fc)
        in_specs.append(vec)
        args.append(final_norm.reshape(1, d))
    return pl.pallas_call(
        body, grid=(m // tm,), in_specs=in_specs, out_specs=row,
        out_shape=jax.ShapeDtypeStruct((m, d), F32),
        compiler_params=_cparams(("parallel",)), name="ffn")(*args)


def _rows_to_chunks(x, tmp_ref, o_ref):
    n, w = x.shape
    for hh in range(w // LANES):
        tmp_ref[hh] = x[:, hh * LANES:(hh + 1) * LANES]
    for i in range(S5_CHUNK):
        for hh in range(w // LANES):
            o_ref[:, i * w + hh * LANES:i * w + (hh + 1) * LANES] = (
                tmp_ref[hh, pl.ds(i, n // S5_CHUNK, stride=S5_CHUNK), :])


def _chunks_to_rows(x_ref, tmp_ref, w):
    n = x_ref.shape[0]
    for i in range(S5_CHUNK):
        for hh in range(w // LANES):
            tmp_ref[hh, pl.ds(i, n, stride=S5_CHUNK), :] = (
                x_ref[:, i * w + hh * LANES:i * w + (hh + 1) * LANES])
    return jnp.concatenate([tmp_ref[hh] for hh in range(w // LANES)], axis=1)


def _emit_transposed(piece, o_ref, rows=None):
    for t in range(piece.shape[0] // ATT_TILE):
        slab_t = piece[t * ATT_TILE:(t + 1) * ATT_TILE, :].T
        o_ref[t] = (slab_t if rows is None else slab_t[rows]).astype(o_ref.dtype)


def _inproj_kernel(x_ref, g_ref, w_ref, kvn_ref, wkv_ref,
                   u_ref, qbt_ref, qit_ref, ki_ref, kv_ref, aux_ref, wit_ref, qct_ref, kvc_ref,
                   gate_ref, u8_ref, tmp_ref):
    h = _rms(x_ref[...], g_ref[...]).astype(BF16)
    outs = dict(u=u_ref, ki=ki_ref, aux=aux_ref)
    off = 0
    for group in (("u", "qb", "qi", "ki"), ("ckv", "aux", "qkv")):
        width = sum(_SEG[name] for name in group)
        r = _dot(h, w_ref[:, off:off + width])
        off += width
        lo = 0
        for name in group:
            piece = r[:, lo:lo + _SEG[name]]
            lo += _SEG[name]
            if name == "ckv":
                c = _rms(piece, kvn_ref[...]).astype(BF16)
                kv_ref[...] = _dot(c, wkv_ref[...]).astype(BF16)
            elif name == "qb":
                _emit_transposed(piece, qbt_ref)
            elif name == "qi":
                _emit_transposed(piece, qit_ref)
            elif name == "qkv":
                _emit_transposed(piece[:, :FOX_WIDTH], qct_ref)
                kvc_ref[...] = piece[:, FOX_WIDTH:].astype(BF16)
            else:
                outs[name][...] = piece.astype(outs[name].dtype)
            if name == "u":
                _rows_to_chunks(piece, tmp_ref, u8_ref)
            if name == "aux":
                _emit_transposed(piece, wit_ref, rows=slice(AUX_W, AUX_W + 8))
    for c in range(_SEG["g"] // D_MODEL):
        cols = slice(c * D_MODEL, (c + 1) * D_MODEL)
        gate_ref[:, cols] = _dot(h, w_ref[:, off + c * D_MODEL:off + (c + 1) * D_MODEL]).astype(BF16)


def _inproj(xt, norm, w_all, kv_norm, w_kv, tm=1024):
    m, d = xt.shape
    wtot = w_all.shape[1]
    tq = ATT_TILE

    def rows(w, dt):
        return jax.ShapeDtypeStruct((m, w), dt), pl.BlockSpec((tm, w), lambda i: (i, 0))

    def tiles_t(w, dt):
        return (jax.ShapeDtypeStruct((m // tq, w, tq), dt),
                pl.BlockSpec((tm // tq, w, tq), lambda i: (i, 0, 0)))

    outs = [rows(SSM_WIDTH, F32), tiles_t(DSA_WIDTH, BF16), tiles_t(IDX_HEADS * IDX_DIM, BF16),
            rows(LANES, BF16), rows(2 * LANES, BF16), rows(LANES, F32), tiles_t(8, F32),
            tiles_t(FOX_WIDTH, BF16), rows(2 * FOX_WIDTH, BF16), rows(3 * D_MODEL, BF16),
            (jax.ShapeDtypeStruct((m // S5_CHUNK, S5_COLS), F32),
             pl.BlockSpec((tm // S5_CHUNK, S5_COLS), lambda i: (i, 0)))]
    out_shape = [o[0] for o in outs]
    out_specs = [o[1] for o in outs]
    return pl.pallas_call(
        _inproj_kernel, grid=(m // tm,),
        in_specs=[pl.BlockSpec((tm, d), lambda i: (i, 0)), _resident((1, d)),
                  _resident((d, wtot)), _resident((1, DSA_LATENT)),
                  _resident((DSA_LATENT, 256))],
        out_specs=out_specs, out_shape=out_shape,
        scratch_shapes=[pltpu.VMEM((SSM_WIDTH // LANES, tm, LANES), F32)],
        compiler_params=_cparams(("parallel",)), name="inproj",
    )(xt, norm.reshape(1, d), w_all, kv_norm.reshape(1, DSA_LATENT), w_kv)


S5_SLABS = S5_STATE // LANES
S5_SCAN_SLABS = 4


def _s5a_kernel(u_ref, mi_ref, ms_ref, yi_ref, p_ref):
    u = u_ref[...].astype(BF16)
    yi_ref[...] = _dot(u, mi_ref[...])
    p = _dot(u, ms_ref[...])
    for k in range(S5_SLABS):
        p_ref[k] = p[:, k * LANES:(k + 1) * LANES]


def _s5a(u8, m_intra, m_sum, tr=512):
    r = u8.shape[0]
    row = pl.BlockSpec((tr, S5_COLS), lambda i: (i, 0))
    slabs = pl.BlockSpec((S5_SLABS, tr, LANES), lambda i: (0, i, 0))
    return pl.pallas_call(
        _s5a_kernel, grid=(r // tr,),
        in_specs=[row, _resident((S5_COLS, S5_COLS)), _resident((S5_COLS, S5_STATE))],
        out_specs=[row, slabs],
        out_shape=[jax.ShapeDtypeStruct((r, S5_COLS), F32),
                   jax.ShapeDtypeStruct((S5_SLABS, r, LANES), F32)],
        compiler_params=_cparams(("parallel",)), name="s5_chunk")(u8, m_intra, m_sum)


def _s5b_kernel(pr_ref, pi_ref, ar_ref, ai_ref, hr_ref, hi_ref, *, nb, nc):
    ar = ar_ref[...]
    ai = ai_ref[...]
    ns = S5_SCAN_SLABS

    def step(c, carry):
        hr, hi = carry
        rows = pl.ds(c, nb, stride=nc)
        for k in range(ns):
            hr_ref[k, rows, :] = hr[:, k * LANES:(k + 1) * LANES]
            hi_ref[k, rows, :] = hi[:, k * LANES:(k + 1) * LANES]
        pr = jnp.concatenate([pr_ref[k, rows, :] for k in range(ns)], axis=1)
        pi = jnp.concatenate([pi_ref[k, rows, :] for k in range(ns)], axis=1)
        return (ar * hr - ai * hi + pr, ar * hi + ai * hr + pi)

    z = jnp.zeros((nb, ns * LANES), F32)
    lax.fori_loop(0, nc, step, (z, z))


def _s5b(p, a_re, a_im, nc, nb):
    r = p.shape[1]
    ns = S5_SCAN_SLABS
    half = S5_SLABS // 2
    re_blk = pl.BlockSpec((ns, nb * nc, LANES), lambda i, j: (j, i, 0))
    im_blk = pl.BlockSpec((ns, nb * nc, LANES), lambda i, j: (j + half // ns, i, 0))
    a_blk = pl.BlockSpec((1, ns * LANES), lambda i, j: (0, j))
    return pl.pallas_call(
        functools.partial(_s5b_kernel, nb=nb, nc=nc), grid=(r // (nb * nc), half // ns),
        in_specs=[re_blk, im_blk, a_blk, a_blk],
        out_specs=[re_blk, re_blk],
        out_shape=[jax.ShapeDtypeStruct((half, r, LANES), F32)] * 2,
        compiler_params=_cparams(("parallel", "parallel")), name="s5_scan",
    )(p, p, a_re, a_im)


def _s5c_kernel(yi_ref, hr_ref, hi_ref, mo_ref, y_ref):
    half = S5_STATE // 2
    hr = jnp.concatenate([hr_ref[k] for k in range(S5_SLABS // 2)], axis=1)
    hi = jnp.concatenate([hi_ref[k] for k in range(S5_SLABS // 2)], axis=1)
    y = yi_ref[...]
    y = y + _dot(hr.astype(BF16), mo_ref[:half, :])
    y = y + _dot(hi.astype(BF16), mo_ref[half:, :])
    y_ref[...] = y


def _s5c(yi, h_re, h_im, m_out, tr=512):
    r = yi.shape[0]
    row = pl.BlockSpec((tr, S5_COLS), lambda i: (i, 0))
    slabs = pl.BlockSpec((S5_SLABS // 2, tr, LANES), lambda i: (0, i, 0))
    return pl.pallas_call(
        _s5c_kernel, grid=(r // tr,),
        in_specs=[row, slabs, slabs, _resident((S5_STATE, S5_COLS))],
        out_specs=row, out_shape=jax.ShapeDtypeStruct((r, S5_COLS), F32),
        compiler_params=_cparams(("parallel",)), name="s5_out")(yi, h_re, h_im, m_out)


def _toeplitz_kernel(lag_ref, o_ref):
    keep = pl.program_id(1) >= pl.program_id(0)
    o_ref[...] = jnp.where(keep, lag_ref[...], jnp.zeros_like(lag_ref))


def _block_toeplitz(lag_blocks):
    t, w, _ = lag_blocks.shape
    return pl.pallas_call(
        _toeplitz_kernel, grid=(t, t),
        in_specs=[pl.BlockSpec((None, w, w), lambda j, i: (jnp.maximum(i - j, 0), 0, 0))],
        out_specs=pl.BlockSpec((w, w), lambda j, i: (j, i)),
        out_shape=jax.ShapeDtypeStruct((t * w, t * w), lag_blocks.dtype), name="s5_toeplitz",
    )(lag_blocks)


def _s5_matrices(lam_re, lam_im, log_dt, b_re, b_im, c_re, c_im):
    hp = lax.Precision.HIGHEST
    t = S5_CHUNK
    dt = jnp.exp(log_dt)[:, None]
    lr, li = lam_re, lam_im
    mag = jnp.exp(lr * dt)
    ab_re, ab_im = mag * jnp.cos(li * dt), mag * jnp.sin(li * dt)
    den = lr * lr + li * li
    nr, ni = ab_re - 1.0, ab_im
    s_re = (nr * lr + ni * li) / den
    s_im = (ni * lr - nr * li) / den
    bb_re = s_re[..., None] * b_re - s_im[..., None] * b_im
    bb_im = s_re[..., None] * b_im + s_im[..., None] * b_re
    pr, pi = [jnp.ones_like(ab_re)], [jnp.zeros_like(ab_re)]
    for _ in range(t):
        pr.append(pr[-1] * ab_re - pi[-1] * ab_im)
        pi.append(pr[-2] * ab_im + pi[-1] * ab_re)
    pw_re, pw_im = jnp.stack(pr), jnp.stack(pi)
    lo_re, lo_im = jnp.stack(pr[:t]), jnp.stack(pi[:t])
    ab_b_re = lo_re[..., None] * bb_re - lo_im[..., None] * bb_im
    ab_b_im = lo_re[..., None] * bb_im + lo_im[..., None] * bb_re
    kern = (jnp.einsum('gpn,dgnq->dgpq', c_re, ab_b_re, precision=hp)
            - jnp.einsum('gpn,dgnq->dgpq', c_im, ab_b_im, precision=hp))
    eye_g = jnp.eye(SSM_GROUPS, dtype=F32)
    rows = np.arange(SSM_WIDTH)
    expand = jnp.asarray((rows[None, :] % SSM_GROUP == np.arange(SSM_GROUP)[:, None]).astype(np.float32))
    same_group = jnp.asarray(rows[:, None] // SSM_GROUP == rows[None, :] // SSM_GROUP)
    lag_rows = kern.transpose(0, 1, 3, 2).reshape(t, SSM_WIDTH, SSM_GROUP)
    lag_blocks = jnp.where(same_group, jnp.einsum('drp,pc->drc', lag_rows, expand, precision=hp), 0.0)
    m_intra = _block_toeplitz(lag_blocks.astype(BF16))
    hi_re, hi_im = jnp.stack(pr[t - 1::-1]), jnp.stack(pi[t - 1::-1])
    sb_re = hi_re[..., None] * bb_re - hi_im[..., None] * bb_im
    sb_im = hi_re[..., None] * bb_im + hi_im[..., None] * bb_re
    m_sum_re = jnp.einsum('jgnq,gh->jgqhn', sb_re, eye_g, precision=hp)
    m_sum_im = jnp.einsum('jgnq,gh->jgqhn', sb_im, eye_g, precision=hp)
    m_sum = jnp.concatenate([m_sum_re.reshape(S5_COLS, -1), m_sum_im.reshape(S5_COLS, -1)], axis=1)
    up_re, up_im = jnp.stack(pr[1:])[:, :, None, :], jnp.stack(pi[1:])[:, :, None, :]
    w_re = c_re[None] * up_re - c_im[None] * up_im
    w_im = c_re[None] * up_im + c_im[None] * up_re
    m_out_re = jnp.einsum('igpn,gh->gnihp', w_re, eye_g, precision=hp)
    m_out_im = jnp.einsum('igpn,gh->gnihp', -w_im, eye_g, precision=hp)
    m_out = jnp.concatenate([m_out_re.reshape(-1, S5_COLS), m_out_im.reshape(-1, S5_COLS)], axis=0)
    a_re = pw_re[t].reshape(1, -1)
    a_im = pw_im[t].reshape(1, -1)
    return m_intra.astype(BF16), m_sum.astype(BF16), m_out.astype(BF16), a_re, a_im


def _fold8(x, op):
    parts = [x[r:r + 8] for r in range(0, x.shape[0], 8)]
    while len(parts) > 1:
        parts = [op(parts[i], parts[i + 1]) for i in range(0, len(parts), 2)]
    return parts[0]


def _softmax_stage(s, m_ref, alpha_ref, p_ref):
    m_old = m_ref[...]
    m_new = jnp.maximum(m_old, jnp.max(_fold8(s, jnp.maximum), axis=0, keepdims=True))
    alpha_ref[...] = jnp.exp2(m_old - m_new)
    p_ref[...] = jnp.exp2(s - jnp.broadcast_to(m_new[0:1], s.shape)).astype(BF16)
    m_ref[...] = m_new


def _value_stage(pv, alpha_ref, p_ref, acc_ref):
    acc = acc_ref[...]
    acc_ref[...] = acc * jnp.broadcast_to(alpha_ref[0:1, :], acc.shape) + pv(p_ref[...])


def _attend_chunks(qt, qk, logits, pv, bufs, m_ref, acc_ref):
    s_a, s_b, p_a, p_b, al_a, al_b = bufs

    def plain(j, kind):
        qk(j, s_a)
        _softmax_stage(logits(j, s_a[...], kind), m_ref, al_a, p_a)
        _value_stage(lambda p: pv(j, p), al_a, p_a, acc_ref)

    @pl.when(qt == 0)
    def _():
        plain(0, 'diag')

    @pl.when(qt >= 1)
    def _():
        n_far = qt - 1
        first = n_far % 2

        @pl.when(first == 1)
        def _():
            plain(0, 'far')

        qk(first, s_a)
        p_b[...] = jnp.zeros(p_b.shape, p_b.dtype)
        al_b[...] = jnp.ones(al_b.shape, al_b.dtype)

        def pair(c, kind0, kind1, last):
            qk(c + 1, s_b)
            _value_stage(lambda p: pv(jnp.maximum(c - 1, 0), p), al_b, p_b, acc_ref)
            _softmax_stage(logits(c, s_a[...], kind0), m_ref, al_a, p_a)
            if not last:
                qk(c + 2, s_a)
            _value_stage(lambda p: pv(c, p), al_a, p_a, acc_ref)
            _softmax_stage(logits(c + 1, s_b[...], kind1), m_ref, al_b, p_b)

        def body(i, carry):
            pair(first + 2 * i, 'far', 'far', False)
            return carry

        lax.fori_loop(0, (n_far - first) // 2, body, 0)
        pair(qt - 1, 'prev', 'diag', True)
        _value_stage(lambda p: pv(qt, p), al_b, p_b, acc_ref)


def _pipeline_buffers(width):
    return ([pltpu.VMEM((ATT_TILE, width), F32)] * 2 + [pltpu.VMEM((ATT_TILE, width), BF16)] * 2
            + [pltpu.VMEM((8, width), F32)] * 2)


def _finish_heads_t(acc_ref, o_ref, n_heads, odd_swapped):
    hd = HEAD_DIM
    tq = o_ref.shape[0]
    for pr in range(n_heads // 2):
        ae = acc_ref[:, 2 * pr * tq:(2 * pr + 1) * tq]
        ao = acc_ref[:, (2 * pr + 1) * tq:(2 * pr + 2) * tq]
        oe = ae[0:hd] / ae[hd:2 * hd]
        oo = ao[hd:2 * hd] / ao[0:hd] if odd_swapped else ao[0:hd] / ao[hd:2 * hd]
        pair = jnp.concatenate([oe, oo], axis=0)
        o_ref[:, pr * LANES:(pr + 1) * LANES] = pair.T.astype(o_ref.dtype)


def _heads_side_by_side(qt_ref, qm_ref, n_heads):
    tq = qt_ref.shape[1]
    for h in range(n_heads):
        qm_ref[:, h * tq:(h + 1) * tq] = qt_ref[h * HEAD_DIM:(h + 1) * HEAD_DIM, :]


def _dsa_kernel(q_ref, qi_ref, wi_ref, kv_ref, ki_ref, band_ref, o_ref,
                score_ref, vt_ref, qm_ref, m_ref, acc_ref, thr_ref, *bufs, n_keys, seq):
    tq = tk = ATT_TILE
    qt = pl.program_id(1)
    n_chunks = qt + 1
    kf = float(n_keys)

    @pl.when(qt == 0)
    def _():
        top = lax.broadcasted_iota(jnp.int32, (LANES, tk), 0) < HEAD_DIM
        for j in range(seq // tk):
            vv = kv_ref[j * tk:(j + 1) * tk, LANES:2 * LANES].astype(F32)
            vt_ref[j] = jnp.where(top, vv.T, 1.0).astype(BF16)

    _heads_side_by_side(qi_ref, qm_ref, IDX_HEADS)
    head_w = jnp.concatenate(
        [wi_ref[h:h + 1, :] for h in range(IDX_HEADS)], axis=1
    ) * (IDX_HEADS ** -0.5 * IDX_DIM ** -0.5)
    key_minus_query = (lax.broadcasted_iota(jnp.int32, (tk, tq), 0)
                       - lax.broadcasted_iota(jnp.int32, (tk, tq), 1))

    def score_body(j, carry):
        mx, mn = carry
        kc = ki_ref[pl.ds(pl.multiple_of(j * tk, tk), tk), 0:IDX_DIM]
        d = _dot(kc, qm_ref[:, 0:IDX_HEADS * tq])
        d = jnp.broadcast_to(head_w, d.shape) * jnp.maximum(d, 0.0)
        sc = (d[:, 0:tq] + d[:, tq:2 * tq]) + (d[:, 2 * tq:3 * tq] + d[:, 3 * tq:4 * tq])
        causal = key_minus_query <= (qt - j) * tq
        score_ref[j] = jnp.where(causal, sc, NEG)
        mx = jnp.maximum(mx, _fold8(jnp.where(causal, sc, NEG), jnp.maximum))
        mn = jnp.minimum(mn, _fold8(jnp.where(causal, sc, -NEG), jnp.minimum))
        return mx, mn

    mx, mn = lax.fori_loop(0, n_chunks, score_body,
                           (jnp.full((8, tq), NEG, F32), jnp.full((8, tq), -NEG, F32)))

    t_q = qt * tq + lax.broadcasted_iota(jnp.int32, (1, tq), 1)
    thr_ref[...] = jnp.full((8, tq), 0.5 * NEG, F32)

    def count(pred):
        def body(j, acc):
            return acc + _fold8(pred(score_ref[j]), jnp.add)
        acc = lax.fori_loop(0, n_chunks, body, jnp.zeros((8, tq), F32))
        return jnp.sum(acc, axis=0, keepdims=True)

    def count_ge(x):
        xb = jnp.broadcast_to(x, (tk, tq))
        return count(lambda s: jnp.where(s >= xb, 1.0, 0.0))

    @pl.when((qt + 1) * tq > n_keys)
    def _search():
        need = (t_q >= n_keys).astype(F32)
        lo0 = jnp.min(mn, axis=0, keepdims=True)
        hi0 = jnp.max(mx, axis=0, keepdims=True)
        hi_top = hi0 + jnp.maximum(jnp.abs(hi0) * 2.0 ** -22, 1e-30)
        c_ge0 = count_ge(jnp.zeros((1, tq), F32))
        c_gt0 = count(lambda s: jnp.where(s > 0.0, 1.0, 0.0))
        n_valid = (t_q + 1).astype(F32)
        zero_hit = c_ge0 == kf
        zero_tie = (c_gt0 < kf) & (c_ge0 > kf)
        positive = c_gt0 >= kf
        hi_neg = hi0 < 0.0
        lo1 = jnp.where(positive, jnp.maximum(lo0, 0.0), lo0)
        c_lo1 = jnp.where(positive, c_ge0, n_valid)
        hi1 = jnp.where(positive | hi_neg, hi_top, 0.0)
        c_hi1 = jnp.where(positive | hi_neg, 0.0, c_ge0)
        hit0 = need * zero_hit.astype(F32)
        tie0 = need * zero_tie.astype(F32)
        done0 = jnp.maximum(1.0 - need, jnp.maximum(hit0, tie0))
        lo1 = jnp.where(tie0 > 0, 0.0, lo1)
        c_hi1 = jnp.where(tie0 > 0, c_gt0, c_hi1)
        x0 = jnp.zeros((1, tq), F32)
        state0 = (lo1, hi1, c_lo1, c_hi1, done0, tie0, x0, jnp.max(1.0 - done0))

        def cond(st):
            return st[7] > 0.0

        def body(st):
            for _ in range(SEARCH_UNROLL):
                st = step(st)
            return st

        def step(st):
            lo, hi, c_lo, c_hi, done, tie, x_fin, _ = st
            x = 0.5 * lo + 0.5 * hi
            inside = ((x > lo) & (x < hi)).astype(F32)
            active = 1.0 - done
            probe = active * inside
            c = count_ge(jnp.where(probe > 0, x, x_fin))
            hit = probe * (c == kf).astype(F32)
            more = probe * (c > kf).astype(F32)
            less = probe * (c < kf).astype(F32)
            new_tie = active * (1.0 - inside)
            lo = jnp.where(more > 0, x, lo)
            c_lo = jnp.where(more > 0, c, c_lo)
            hi = jnp.where(less > 0, x, hi)
            c_hi = jnp.where(less > 0, c, c_hi)
            x_fin = jnp.where(hit > 0, x, x_fin)
            tie = jnp.maximum(tie, new_tie)
            done = jnp.maximum(done, jnp.maximum(hit, new_tie))
            return (lo, hi, c_lo, c_hi, done, tie, x_fin, jnp.max(1.0 - done))

        st = state0
        for _ in range(SEARCH_FIXED):
            st = step(st)
        st = lax.while_loop(cond, body, st)
        lo, _, _, c_hi, _, tie, x_fin, _ = st
        thr = jnp.where(need > 0, jnp.where(tie > 0, lo, x_fin), 0.5 * NEG)
        thr_ref[...] = jnp.broadcast_to(thr, (8, tq))

        @pl.when(jnp.max(tie) > 0.0)
        def _ties():
            want = jnp.broadcast_to(kf - c_hi, (tk, tq))
            thr_t = jnp.broadcast_to(jnp.where(tie > 0, thr, -NEG), (tk, tq))
            upto = (lax.broadcasted_iota(jnp.int32, (tk, tk), 1)
                    <= lax.broadcasted_iota(jnp.int32, (tk, tk), 0))
            prefix = jnp.where(upto, 1.0, 0.0).astype(BF16)

            def body(j, seen):
                s = score_ref[j]
                eq = s == thr_t
                rank = _dot(prefix, jnp.where(eq, 1.0, 0.0).astype(BF16))
                rank = rank + jnp.broadcast_to(seen, (tk, tq))
                score_ref[j] = jnp.where(eq, jnp.where(rank > want, NEG, s), s)
                return rank[tk - 1:tk, :]

            lax.fori_loop(0, n_chunks, body, jnp.zeros((1, tq), F32))

    _heads_side_by_side(q_ref, qm_ref, DSA_HEADS)
    m_ref[...] = jnp.full(m_ref.shape, NEG, F32)
    acc_ref[...] = jnp.zeros(acc_ref.shape, F32)
    thr_b = jnp.broadcast_to(thr_ref[0:1, :], (tk, tq))

    def qk(j, s_ref):
        rows = pl.ds(pl.multiple_of(j * tk, tk), tk)
        s_ref[...] = _dot(kv_ref[rows, 0:HEAD_DIM], qm_ref[...])

    def logits(j, s, kind):
        off = jnp.where(score_ref[j] >= thr_b, 0.0, NEG)
        s = s + jnp.concatenate([off] * DSA_HEADS, axis=1)
        if kind == 'prev':
            s = s + band_ref[0:tk, :]
        elif kind == 'diag':
            s = s + band_ref[tk:2 * tk, :]
        return s

    _attend_chunks(qt, qk, logits, lambda j, p: _dot(vt_ref[j], p), bufs, m_ref, acc_ref)
    _finish_heads_t(acc_ref, o_ref, DSA_HEADS, odd_swapped=False)


def _dsa(qb_t, qi_t, wi_t, kv, ki, band, bsz, seq, n_keys):
    tq = ATT_TILE
    nq = seq // tq
    qrow = lambda w: pl.BlockSpec((tq, w), lambda b, q: (b * nq + q, 0))
    qtile = lambda w: pl.BlockSpec((None, w, tq), lambda b, q: (b * nq + q, 0, 0))
    seq_blk = lambda w: pl.BlockSpec((seq, w), lambda b, q: (b, 0))
    return pl.pallas_call(
        functools.partial(_dsa_kernel, n_keys=n_keys, seq=seq), grid=(bsz, nq),
        in_specs=[qtile(DSA_WIDTH), qtile(IDX_HEADS * IDX_DIM), qtile(8), seq_blk(2 * LANES),
                  seq_blk(LANES), _resident((2 * tq, DSA_HEADS * tq))],
        out_specs=qrow(DSA_WIDTH),
        out_shape=jax.ShapeDtypeStruct((bsz * seq, DSA_WIDTH), BF16),
        scratch_shapes=[pltpu.VMEM((nq, tq, tq), F32),
                        pltpu.VMEM((nq, LANES, tq), BF16),
                        pltpu.VMEM((HEAD_DIM, DSA_HEADS * tq), BF16),
                        pltpu.VMEM((8, DSA_HEADS * tq), F32),
                        pltpu.VMEM((LANES, DSA_HEADS * tq), F32),
                        pltpu.VMEM((8, tq), F32)] + _pipeline_buffers(DSA_HEADS * tq),
        compiler_params=_cparams(("parallel", "arbitrary")), name="dsa",
    )(qb_t, qi_t, wi_t, kv, ki, band)


def _band_kernel(bucket_ref, rb_ref, o_ref):
    bucket = bucket_ref[...]
    tq = bucket.shape[1]
    for h in range(DSA_HEADS):
        acc = jnp.zeros(bucket.shape, F32)
        for k in range(REL_BUCKETS - 1):
            acc = jnp.where(bucket == k, (rb_ref[k, h] - rb_ref[REL_BUCKETS - 1, h]) * LOG2E, acc)
        o_ref[:, h * tq:(h + 1) * tq] = acc


def _t5_bucket_np(dist):
    d = np.maximum(dist, 0)
    df = np.maximum(d, 1).astype(np.float32)
    log_b = REL_MAX_EXACT + (np.log(df / REL_MAX_EXACT) / math.log(REL_MAX_DIST / REL_MAX_EXACT)
                             * (REL_BUCKETS - REL_MAX_EXACT)).astype(np.int32)
    log_b = np.minimum(log_b, REL_BUCKETS - 1)
    return np.where(d < REL_MAX_EXACT, d, log_b).astype(np.int32)


def _rel_band(rel_bias):
    tq = ATT_TILE
    i = np.arange(tq)[None, :]
    j = np.arange(2 * tq)[:, None]
    bucket = jnp.asarray(_t5_bucket_np(i + tq - j))
    return pl.pallas_call(
        _band_kernel,
        in_specs=[pl.BlockSpec(memory_space=pltpu.VMEM), pl.BlockSpec(memory_space=pltpu.SMEM)],
        out_specs=pl.BlockSpec(memory_space=pltpu.VMEM),
        out_shape=jax.ShapeDtypeStruct((2 * tq, DSA_HEADS * tq), F32), name="rel_band",
    )(bucket, rel_bias)


def _cum_kernel(aux_ref, bf_ref, col_ref, row_ref, *, seq):
    x = aux_ref[...] + bf_ref[...]
    logf = -(jnp.maximum(-x, 0.0) + jnp.log1p(jnp.exp(-jnp.abs(x))))
    c = logf.T[0:8]
    pos = lax.broadcasted_iota(jnp.int32, c.shape, 1)
    shift = 1
    while shift < seq:
        c = c + jnp.where(pos >= shift, pltpu.roll(c, shift, 1), 0.0)
        shift *= 2
    c = c * LOG2E
    col_ref[...] = jnp.concatenate([c, jnp.zeros((LANES - 8, seq), F32)], axis=0).T
    for j in range(seq // ATT_TILE):
        row_ref[j] = c[:, j * ATT_TILE:(j + 1) * ATT_TILE]


def _fox_cum(aux, b_f, bsz, seq):
    nq = seq // ATT_TILE
    bf = jnp.zeros((1, LANES), F32).at[0, AUX_F:AUX_F + FOX_HEADS].set(b_f)
    return pl.pallas_call(
        functools.partial(_cum_kernel, seq=seq), grid=(bsz,),
        in_specs=[pl.BlockSpec((seq, LANES), lambda b: (b, 0)), _resident((1, LANES))],
        out_specs=[pl.BlockSpec((seq, LANES), lambda b: (b, 0)),
                   pl.BlockSpec((None, nq, 8, ATT_TILE), lambda b: (b, 0, 0, 0))],
        out_shape=[jax.ShapeDtypeStruct((bsz * seq, LANES), F32),
                   jax.ShapeDtypeStruct((bsz, nq, 8, ATT_TILE), F32)],
        compiler_params=_cparams(("parallel",)), name="fox_cum")(aux, bf)


def _bf16_pieces(x):
    hi = x.astype(BF16).astype(F32)
    rest = x - hi
    mid = rest.astype(BF16).astype(F32)
    return [hi, mid, (rest - mid).astype(BF16).astype(F32)]


def _fox_kernel(q_ref, k_ref, v_ref, ccol_ref, crow_ref, o_ref,
                vte_ref, vto_ref, ka_ref, qm_ref, m_ref, acc_ref, *bufs, seq):
    tq = tk = ATT_TILE
    qt = pl.program_id(1)
    heads = range(FOX_HEADS)

    @pl.when(qt == 0)
    def _():
        top = lax.broadcasted_iota(jnp.int32, (LANES, tk), 0) < HEAD_DIM
        for j in range(seq // tk):
            rows = slice(j * tk, (j + 1) * tk)
            for pr in range(FOX_HEADS // 2):
                v_t = v_ref[rows, pr * LANES:(pr + 1) * LANES].astype(F32).T
                vte_ref[pr, j] = jnp.where(top, v_t, 1.0).astype(BF16)
                vto_ref[pr, j] = jnp.where(top, 1.0, v_t).astype(BF16)
            cc = ccol_ref[rows, :]
            lane = lax.broadcasted_iota(jnp.int32, (tk, LANES), 1)
            for h in heads:
                k_h = k_ref[rows, (h // 2) * LANES:(h // 2 + 1) * LANES].astype(F32)
                if h % 2 == 1:
                    k_h = pltpu.roll(k_h, HEAD_DIM, 1)
                pieces = _bf16_pieces(jnp.broadcast_to(cc[:, AUX_F + h:AUX_F + h + 1], (tk, LANES)))
                ext = jnp.where(lane < HEAD_DIM + 2 * len(pieces), 1.0, 0.0)
                for i, piece in enumerate(pieces):
                    ext = jnp.where(lane == HEAD_DIM + i, piece, ext)
                ka_ref[h, rows, :] = jnp.where(lane < HEAD_DIM, k_h, ext).astype(BF16)

    c_q = crow_ref[qt]
    row = lax.broadcasted_iota(jnp.int32, (LANES - HEAD_DIM, tq), 0)
    for h in heads:
        pieces = _bf16_pieces(c_q[AUX_F + h:AUX_F + h + 1, :])
        ext = jnp.where(row < len(pieces), -1.0, 0.0)
        for i, piece in enumerate(pieces):
            ext = jnp.where(row == len(pieces) + i, jnp.broadcast_to(piece, ext.shape), ext)
        q_h = q_ref[h * HEAD_DIM:(h + 1) * HEAD_DIM, :]
        qm_ref[:, h * tq:(h + 1) * tq] = jnp.concatenate([q_h, ext.astype(BF16)], axis=0)
    m_ref[...] = jnp.full(m_ref.shape, NEG, F32)
    acc_ref[...] = jnp.zeros(acc_ref.shape, F32)
    causal_off = jnp.where(lax.broadcasted_iota(jnp.int32, (tk, tq), 0)
                           <= lax.broadcasted_iota(jnp.int32, (tk, tq), 1), 0.0, NEG)

    def qk(j, s_ref):
        rows = pl.ds(pl.multiple_of(j * tk, tk), tk)
        for h in heads:
            cols = slice(h * tq, (h + 1) * tq)
            s_ref[:, cols] = _dot(ka_ref[h, rows, :], qm_ref[:, cols])

    def logits(j, s, kind):
        if kind == 'diag':
            s = s + jnp.concatenate([causal_off] * FOX_HEADS, axis=1)
        return s

    def pv(j, p):
        return jnp.concatenate(
            [_dot(vte_ref[h // 2, j] if h % 2 == 0 else vto_ref[h // 2, j],
                  p[:, h * tq:(h + 1) * tq]) for h in heads], axis=1)

    _attend_chunks(qt, qk, logits, pv, bufs, m_ref, acc_ref)
    _finish_heads_t(acc_ref, o_ref, FOX_HEADS, odd_swapped=True)


def _fox(q_t, kv, c_col, c_row, bsz, seq):
    tq = ATT_TILE
    nq = seq // tq
    npair = FOX_HEADS // 2
    return pl.pallas_call(
        functools.partial(_fox_kernel, seq=seq), grid=(bsz, nq),
        in_specs=[pl.BlockSpec((None, FOX_WIDTH, tq), lambda b, q: (b * nq + q, 0, 0)),
                  pl.BlockSpec((seq, FOX_WIDTH), lambda b, q: (b, 0)),
                  pl.BlockSpec((seq, FOX_WIDTH), lambda b, q: (b, 1)),
                  pl.BlockSpec((seq, LANES), lambda b, q: (b, 0)),
                  pl.BlockSpec((None, nq, 8, tq), lambda b, q: (b, 0, 0, 0))],
        out_specs=pl.BlockSpec((tq, FOX_WIDTH), lambda b, q: (b * nq + q, 0)),
        out_shape=jax.ShapeDtypeStruct((bsz * seq, FOX_WIDTH), BF16),
        scratch_shapes=[pltpu.VMEM((npair, nq, LANES, tq), BF16),
                        pltpu.VMEM((npair, nq, LANES, tq), BF16),
                        pltpu.VMEM((FOX_HEADS, seq, LANES), BF16),
                        pltpu.VMEM((LANES, FOX_HEADS * tq), BF16),
                        pltpu.VMEM((8, FOX_HEADS * tq), F32),
                        pltpu.VMEM((LANES, FOX_HEADS * tq), F32)] + _pipeline_buffers(FOX_HEADS * tq),
        compiler_params=_cparams(("parallel", "arbitrary")), name="fox",
    )(q_t, kv, kv, c_col, c_row)


def _gelu(y):
    return 0.5 * y * (1.0 + jnp.tanh(math.sqrt(2.0 / math.pi) * (y + 0.044715 * (y * y * y))))


def _merge_kernel(x_ref, yc_ref, u_ref, d_ref, wglu_ref, b_ref, c_ref, gate_ref,
                  wa_ref, wb_ref, wc_ref, wo_ref, o_ref, tmp_ref):
    d = D_MODEL
    y = _chunks_to_rows(yc_ref, tmp_ref, SSM_WIDTH) + d_ref[...] * u_ref[...]
    z = _dot(_gelu(y).astype(BF16), wglu_ref[...])
    a = z[:, :SSM_WIDTH] * _sigmoid(z[:, SSM_WIDTH:])
    merged = _sigmoid(gate_ref[:, 0:d].astype(F32)) * _dot(a.astype(BF16), wa_ref[...])
    merged += _sigmoid(gate_ref[:, d:2 * d].astype(F32)) * _dot(b_ref[...], wb_ref[...])
    merged += _sigmoid(gate_ref[:, 2 * d:3 * d].astype(F32)) * _dot(c_ref[...], wc_ref[...])
    o_ref[...] = x_ref[...] + _dot(merged.astype(BF16), wo_ref[...])


def _merge(xt, yc, u, d_skip, w_glu, b, c, gates, wa, wb, wc, wo, tm=1024):
    m, d = xt.shape
    row = lambda w: pl.BlockSpec((tm, w), lambda i: (i, 0))
    return pl.pallas_call(
        _merge_kernel, grid=(m // tm,),
        in_specs=[row(d), pl.BlockSpec((tm // S5_CHUNK, S5_COLS), lambda i: (i, 0)),
                  row(SSM_WIDTH), _resident((1, SSM_WIDTH)),
                  _resident((SSM_WIDTH, 2 * SSM_WIDTH)), row(DSA_WIDTH), row(FOX_WIDTH), row(3 * d),
                  _resident((SSM_WIDTH, d)), _resident((DSA_WIDTH, d)), _resident((FOX_WIDTH, d)),
                  _resident((d, d))],
        out_specs=row(d), out_shape=jax.ShapeDtypeStruct((m, d), F32),
        scratch_shapes=[pltpu.VMEM((SSM_WIDTH // LANES, tm, LANES), F32)],
        compiler_params=_cparams(("parallel",)), name="merge",
    )(xt, yc, u, d_skip.reshape(1, -1), w_glu, b, c, gates, wa, wb, wc, wo)


def _pack_w_in(w_in):
    d = w_in.shape[0]
    splits = (SSM_WIDTH, DSA_WIDTH, DSA_LATENT, IDX_HEADS * IDX_DIM, IDX_DIM, IDX_HEADS,
              FOX_WIDTH, FOX_WIDTH, FOX_WIDTH, FOX_HEADS, D_MODEL, D_MODEL, D_MODEL)
    pts = np.cumsum(splits)[:-1]
    (w_u, w_qb, w_ckv, w_qi, w_ki, w_wi, w_qc, w_kc, w_vc, w_fc, w_ga, w_gb, w_gc) = jnp.split(
        w_in, pts, axis=1)
    scale = HEAD_DIM ** -0.5 * LOG2E
    aux = jnp.zeros((d, LANES), F32)
    aux = aux.at[:, AUX_F:AUX_F + FOX_HEADS].set(w_fc).at[:, AUX_W:AUX_W + IDX_HEADS].set(w_wi)
    cols = [w_u, w_qb * scale, w_qi, w_ki, w_ki, w_ckv, aux, w_qc * scale, w_kc, w_vc,
            w_ga, w_gb, w_gc]
    return jnp.concatenate(cols, axis=1).astype(BF16)


def kernel(x, ffn1_norm, ffn1_w_gate, ffn1_w_up, ffn1_w_down, mix_norm, w_in, ssm_lambda_re, ssm_lambda_im, ssm_log_dt, ssm_b_re, ssm_b_im, ssm_c_re, ssm_c_im, ssm_d, ssm_w_glu, dsa_kv_norm, dsa_w_uk, dsa_w_uv, rel_bias, fox_b_f, w_branch_ssm, w_branch_dsa, w_branch_fox, w_out, ffn2_norm, ffn2_w_gate, ffn2_w_up, ffn2_w_down, final_norm):
    bsz, seq, d = x.shape
    depth = w_in.shape[0]
    n_keys = min(TOPK_MAX, seq // 4)
    assert d == D_MODEL and seq % ATT_TILE == 0 and (bsz * seq) % (8 * 512) == 0
    assert n_keys % LANES == 0 and bsz % 8 == 0
    m = bsz * seq
    n_chunks = seq // S5_CHUNK
    xt = x.reshape(m, d)
    band = _rel_band(rel_bias)
    bf = lambda w: w.astype(BF16)
    for l in range(depth):
        xt = _ffn(xt, ffn1_norm[l], bf(ffn1_w_gate[l]), bf(ffn1_w_up[l]), bf(ffn1_w_down[l]))
        w_kv = bf(jnp.concatenate([dsa_w_uk[l], dsa_w_uk[l], dsa_w_uv[l], dsa_w_uv[l]], axis=1))
        u, qb_t, qi_t, ki, kv, aux, wi_t, qc_t, kv_fox, gates, u8 = _inproj(
            xt, mix_norm[l], _pack_w_in(w_in[l]), dsa_kv_norm[l], w_kv)
        m_intra, m_sum, m_out, a_re, a_im = _s5_matrices(
            ssm_lambda_re[l], ssm_lambda_im[l], ssm_log_dt[l], ssm_b_re[l], ssm_b_im[l],
            ssm_c_re[l], ssm_c_im[l])
        y_intra, p = _s5a(u8, m_intra, m_sum)
        h_re, h_im = _s5b(p, a_re, a_im, n_chunks, nb=8)
        yc = _s5c(y_intra, h_re, h_im, m_out)
        b_out = _dsa(qb_t, qi_t, wi_t, kv, ki, band, bsz, seq, n_keys)
        c_col, c_row = _fox_cum(aux, fox_b_f[l], bsz, seq)
        c_out = _fox(qc_t, kv_fox, c_col, c_row, bsz, seq)
        xt = _merge(xt, yc, u, ssm_d[l], bf(ssm_w_glu[l]), b_out, c_out, gates,
                    bf(w_branch_ssm[l]), bf(w_branch_dsa[l]), bf(w_branch_fox[l]), bf(w_out[l]))
        last = final_norm if l == depth - 1 else None
        xt = _ffn(xt, ffn2_norm[l], bf(ffn2_w_gate[l]), bf(ffn2_w_up[l]), bf(ffn2_w_down[l]),
                  final_norm=last)
    return xt.reshape(bsz, seq, d)
```

```python
import functools
import math

import numpy as np
import jax
import jax.numpy as jnp
from jax import lax
from jax.experimental import pallas as pl
from jax.experimental.pallas import tpu as pltpu

F32 = jnp.float32
BF16 = jnp.bfloat16

D_MODEL = 1024
SSM_WIDTH = D_MODEL // 4
SSM_GROUP = 16
SSM_GROUPS = SSM_WIDTH // SSM_GROUP
SSM_STATE = 64
HEAD_DIM = 64
DSA_HEADS = 6
DSA_WIDTH = DSA_HEADS * HEAD_DIM
DSA_LATENT = 2 * HEAD_DIM
IDX_HEADS = 4
IDX_DIM = 64
TOPK_MAX = 256
FOX_HEADS = 6
FOX_WIDTH = FOX_HEADS * HEAD_DIM
REL_BUCKETS = 32
REL_MAX_EXACT = 16
REL_MAX_DIST = 128
FFN_HIDDEN = 2816
EPS = 1e-6
NEG = -1e30
LOG2E = math.log2(math.e)

LANES = 128
SUBLANES = 8
S5_CHUNK = 8
S5_COLS = S5_CHUNK * SSM_WIDTH
S5_STATE = 2 * SSM_GROUPS * SSM_STATE
ATT_TILE = 256
SEARCH_FIXED = 12
SEARCH_UNROLL = 2
VMEM_LIMIT = 56 * 1024 * 1024

_SEG = dict(u=SSM_WIDTH, qb=DSA_WIDTH, qi=IDX_HEADS * IDX_DIM, ki=LANES, ckv=DSA_LATENT, aux=LANES,
            qkv=3 * FOX_WIDTH, g=3 * D_MODEL)
AUX_F = 0
AUX_W = 8


def _cparams(sem):
    return pltpu.CompilerParams(dimension_semantics=sem, vmem_limit_bytes=VMEM_LIMIT)


def _resident(shape):
    nd = len(shape)
    return pl.BlockSpec(shape, lambda *_: (0,) * nd, pipeline_mode=pl.Buffered(1))


def _rms(x, g):
    return x * lax.rsqrt(jnp.mean(x * x, axis=-1, keepdims=True) + EPS) * g


def _sigmoid(x):
    return 1.0 / (1.0 + jnp.exp(-x))


def _dot(a, b):
    return jnp.dot(a, b, preferred_element_type=F32)


def _ffn_body(x, g_ref, wg_ref, wu_ref, wd_ref, fc):
    h = _rms(x, g_ref[...]).astype(BF16)
    acc = None
    for c in range(FFN_HIDDEN // fc):
        sl = slice(c * fc, (c + 1) * fc)
        gt = _dot(h, wg_ref[:, sl])
        up = _dot(h, wu_ref[:, sl])
        a = (gt * _sigmoid(gt) * up).astype(BF16)
        d = _dot(a, wd_ref[sl, :])
        acc = d if acc is None else acc + d
    return x + 0.5 * acc


def _ffn_kernel(x_ref, g_ref, wg_ref, wu_ref, wd_ref, o_ref, *, fc):
    o_ref[...] = _ffn_body(x_ref[...], g_ref, wg_ref, wu_ref, wd_ref, fc)


def _ffn_final_kernel(x_ref, g_ref, wg_ref, wu_ref, wd_ref, fn_ref, o_ref, *, fc):
    y = _ffn_body(x_ref[...], g_ref, wg_ref, wu_ref, wd_ref, fc)
    o_ref[...] = _rms(y, fn_ref[...])


def _ffn(xt, norm, wg, wu, wd, final_norm=None, tm=1024, fc=256):
    m, d = xt.shape
    f = wg.shape[1]
    row = pl.BlockSpec((tm, d), lambda i: (i, 0))
    vec = _resident((1, d))
    in_specs = [row, vec, _resident((d, f)), _resident((d, f)), _resident((f, d))]
    args = [xt, norm.reshape(1, d), wg, wu, wd]
    if final_norm is None:
        body = functools.partial(_ffn_kernel, fc=fc)
    else:
        body = functools.partial(_ffn_final_kernel, fc=fc)
        in_specs.append(vec)
        args.append(final_norm.reshape(1, d))
    return pl.pallas_call(
        body, grid=(m // tm,), in_specs=in_specs, out_specs=row,
        out_shape=jax.ShapeDtypeStruct((m, d), F32),
        compiler_params=_cparams(("parallel",)), name="ffn")(*args)


def _rows_to_chunks(x, tmp_ref, o_ref):
    n, w = x.shape
    for hh in range(w // LANES):
        tmp_ref[hh] = x[:, hh * LANES:(hh + 1) * LANES]
    for i in range(S5_CHUNK):
        for hh in range(w // LANES):
            o_ref[:, i * w + hh * LANES:i * w + (hh + 1) * LANES] = (
                tmp_ref[hh, pl.ds(i, n // S5_CHUNK, stride=S5_CHUNK), :])


def _chunks_to_rows(x_ref, tmp_ref, w):
    n = x_ref.shape[0]
    for i in range(S5_CHUNK):
        for hh in range(w // LANES):
            tmp_ref[hh, pl.ds(i, n, stride=S5_CHUNK), :] = (
                x_ref[:, i * w + hh * LANES:i * w + (hh + 1) * LANES])
    return jnp.concatenate([tmp_ref[hh] for hh in range(w // LANES)], axis=1)


def _emit_transposed(piece, o_ref, rows=None):
    for t in range(piece.shape[0] // ATT_TILE):
        slab_t = piece[t * ATT_TILE:(t + 1) * ATT_TILE, :].T
        o_ref[t] = (slab_t if rows is None else slab_t[rows]).astype(o_ref.dtype)


def _inproj_kernel(x_ref, g_ref, w_ref, kvn_ref, wkv_ref,
                   u_ref, qbt_ref, qit_ref, ki_ref, kv_ref, aux_ref, wit_ref, qct_ref, kvc_ref,
                   gate_ref, u8_ref, tmp_ref):
    h = _rms(x_ref[...], g_ref[...]).astype(BF16)
    outs = dict(u=u_ref, ki=ki_ref, aux=aux_ref)
    off = 0
    for group in (("u", "qb", "qi", "ki"), ("ckv", "aux", "qkv")):
        width = sum(_SEG[name] for name in group)
        r = _dot(h, w_ref[:, off:off + width])
        off += width
        lo = 0
        for name in group:
            piece = r[:, lo:lo + _SEG[name]]
            lo += _SEG[name]
            if name == "ckv":
                c = _rms(piece, kvn_ref[...]).astype(BF16)
                kv_ref[...] = _dot(c, wkv_ref[...]).astype(BF16)
            elif name == "qb":
                _emit_transposed(piece, qbt_ref)
            elif name == "qi":
                _emit_transposed(piece, qit_ref)
            elif name == "qkv":
                _emit_transposed(piece[:, :FOX_WIDTH], qct_ref)
                kvc_ref[...] = piece[:, FOX_WIDTH:].astype(BF16)
            else:
                outs[name][...] = piece.astype(outs[name].dtype)
            if name == "u":
                _rows_to_chunks(piece, tmp_ref, u8_ref)
            if name == "aux":
                _emit_transposed(piece, wit_ref, rows=slice(AUX_W, AUX_W + SUBLANES))
    for c in range(_SEG["g"] // D_MODEL):
        cols = slice(c * D_MODEL, (c + 1) * D_MODEL)
        gate_ref[:, cols] = _dot(h, w_ref[:, off + c * D_MODEL:off + (c + 1) * D_MODEL]).astype(BF16)


def _inproj(xt, norm, w_all, kv_norm, w_kv, tm=1024):
    m, d = xt.shape
    wtot = w_all.shape[1]
    tq = ATT_TILE

    def rows(w, dt):
        return jax.ShapeDtypeStruct((m, w), dt), pl.BlockSpec((tm, w), lambda i: (i, 0))

    def tiles_t(w, dt):
        return (jax.ShapeDtypeStruct((m // tq, w, tq), dt),
                pl.BlockSpec((tm // tq, w, tq), lambda i: (i, 0, 0)))

    outs = [rows(SSM_WIDTH, F32), tiles_t(DSA_WIDTH, BF16), tiles_t(IDX_HEADS * IDX_DIM, BF16),
            rows(LANES, BF16), rows(2 * LANES, BF16), rows(LANES, F32), tiles_t(SUBLANES, F32),
            tiles_t(FOX_WIDTH, BF16), rows(2 * FOX_WIDTH, BF16), rows(3 * D_MODEL, BF16),
            (jax.ShapeDtypeStruct((m // S5_CHUNK, S5_COLS), F32),
             pl.BlockSpec((tm // S5_CHUNK, S5_COLS), lambda i: (i, 0)))]
    out_shape = [o[0] for o in outs]
    out_specs = [o[1] for o in outs]
    return pl.pallas_call(
        _inproj_kernel, grid=(m // tm,),
        in_specs=[pl.BlockSpec((tm, d), lambda i: (i, 0)), _resident((1, d)),
                  _resident((d, wtot)), _resident((1, DSA_LATENT)),
                  _resident((DSA_LATENT, 2 * LANES))],
        out_specs=out_specs, out_shape=out_shape,
        scratch_shapes=[pltpu.VMEM((SSM_WIDTH // LANES, tm, LANES), F32)],
        compiler_params=_cparams(("parallel",)), name="inproj",
    )(xt, norm.reshape(1, d), w_all, kv_norm.reshape(1, DSA_LATENT), w_kv)


S5_SLABS = S5_STATE // LANES
S5_SCAN_SLABS = 4


def _s5a_kernel(u_ref, mi_ref, ms_ref, yi_ref, p_ref):
    u = u_ref[...].astype(BF16)
    yi_ref[...] = _dot(u, mi_ref[...])
    p = _dot(u, ms_ref[...])
    for k in range(S5_SLABS):
        p_ref[k] = p[:, k * LANES:(k + 1) * LANES]


def _s5a(u8, m_intra, m_sum, tr=512):
    r = u8.shape[0]
    row = pl.BlockSpec((tr, S5_COLS), lambda i: (i, 0))
    slabs = pl.BlockSpec((S5_SLABS, tr, LANES), lambda i: (0, i, 0))
    return pl.pallas_call(
        _s5a_kernel, grid=(r // tr,),
        in_specs=[row, _resident((S5_COLS, S5_COLS)), _resident((S5_COLS, S5_STATE))],
        out_specs=[row, slabs],
        out_shape=[jax.ShapeDtypeStruct((r, S5_COLS), F32),
                   jax.ShapeDtypeStruct((S5_SLABS, r, LANES), F32)],
        compiler_params=_cparams(("parallel",)), name="s5_chunk")(u8, m_intra, m_sum)


def _s5b_kernel(pr_ref, pi_ref, ar_ref, ai_ref, hr_ref, hi_ref, *, nb, nc):
    ar = ar_ref[...]
    ai = ai_ref[...]
    ns = S5_SCAN_SLABS

    def step(c, carry):
        hr, hi = carry
        rows = pl.ds(c, nb, stride=nc)
        for k in range(ns):
            hr_ref[k, rows, :] = hr[:, k * LANES:(k + 1) * LANES]
            hi_ref[k, rows, :] = hi[:, k * LANES:(k + 1) * LANES]
        pr = jnp.concatenate([pr_ref[k, rows, :] for k in range(ns)], axis=1)
        pi = jnp.concatenate([pi_ref[k, rows, :] for k in range(ns)], axis=1)
        return (ar * hr - ai * hi + pr, ar * hi + ai * hr + pi)

    z = jnp.zeros((nb, ns * LANES), F32)
    lax.fori_loop(0, nc, step, (z, z))


def _s5b(p, a_re, a_im, nc, nb):
    r = p.shape[1]
    ns = S5_SCAN_SLABS
    half = S5_SLABS // 2
    re_blk = pl.BlockSpec((ns, nb * nc, LANES), lambda i, j: (j, i, 0))
    im_blk = pl.BlockSpec((ns, nb * nc, LANES), lambda i, j: (j + half // ns, i, 0))
    a_blk = pl.BlockSpec((1, ns * LANES), lambda i, j: (0, j))
    return pl.pallas_call(
        functools.partial(_s5b_kernel, nb=nb, nc=nc), grid=(r // (nb * nc), half // ns),
        in_specs=[re_blk, im_blk, a_blk, a_blk],
        out_specs=[re_blk, re_blk],
        out_shape=[jax.ShapeDtypeStruct((half, r, LANES), F32)] * 2,
        compiler_params=_cparams(("parallel", "parallel")), name="s5_scan",
    )(p, p, a_re, a_im)


def _s5c_kernel(yi_ref, hr_ref, hi_ref, mo_ref, y_ref):
    half = S5_STATE // 2
    hr = jnp.concatenate([hr_ref[k] for k in range(S5_SLABS // 2)], axis=1)
    hi = jnp.concatenate([hi_ref[k] for k in range(S5_SLABS // 2)], axis=1)
    y = yi_ref[...]
    y = y + _dot(hr.astype(BF16), mo_ref[:half, :])
    y = y + _dot(hi.astype(BF16), mo_ref[half:, :])
    y_ref[...] = y


def _s5c(yi, h_re, h_im, m_out, tr=512):
    r = yi.shape[0]
    row = pl.BlockSpec((tr, S5_COLS), lambda i: (i, 0))
    slabs = pl.BlockSpec((S5_SLABS // 2, tr, LANES), lambda i: (0, i, 0))
    return pl.pallas_call(
        _s5c_kernel, grid=(r // tr,),
        in_specs=[row, slabs, slabs, _resident((S5_STATE, S5_COLS))],
        out_specs=row, out_shape=jax.ShapeDtypeStruct((r, S5_COLS), F32),
        compiler_params=_cparams(("parallel",)), name="s5_out")(yi, h_re, h_im, m_out)


def _toeplitz_kernel(lag_ref, o_ref):
    j = pl.program_id(0)
    t, w, _ = lag_ref.shape
    for i in range(t):
        blk = lag_ref[jnp.maximum(i - j, 0)]
        o_ref[:, i * w:(i + 1) * w] = jnp.where(i >= j, blk, jnp.zeros_like(blk))


def _block_toeplitz(lag_blocks):
    t, w, _ = lag_blocks.shape
    return pl.pallas_call(
        _toeplitz_kernel, grid=(t,),
        in_specs=[_resident((t, w, w))],
        out_specs=pl.BlockSpec((w, t * w), lambda j: (j, 0)),
        out_shape=jax.ShapeDtypeStruct((t * w, t * w), lag_blocks.dtype), name="s5_toeplitz",
    )(lag_blocks)


def _s5_matrices(lam_re, lam_im, log_dt, b_re, b_im, c_re, c_im):
    hp = lax.Precision.HIGHEST
    t = S5_CHUNK
    dt = jnp.exp(log_dt)[:, None]
    lr, li = lam_re, lam_im
    mag = jnp.exp(lr * dt)
    ab_re, ab_im = mag * jnp.cos(li * dt), mag * jnp.sin(li * dt)
    den = lr * lr + li * li
    nr, ni = ab_re - 1.0, ab_im
    s_re = (nr * lr + ni * li) / den
    s_im = (ni * lr - nr * li) / den
    bb_re = s_re[..., None] * b_re - s_im[..., None] * b_im
    bb_im = s_re[..., None] * b_im + s_im[..., None] * b_re
    pr, pi = [jnp.ones_like(ab_re)], [jnp.zeros_like(ab_re)]
    for _ in range(t):
        pr.append(pr[-1] * ab_re - pi[-1] * ab_im)
        pi.append(pr[-2] * ab_im + pi[-1] * ab_re)
    pw_re, pw_im = jnp.stack(pr), jnp.stack(pi)
    lo_re, lo_im = jnp.stack(pr[:t]), jnp.stack(pi[:t])
    ab_b_re = lo_re[..., None] * bb_re - lo_im[..., None] * bb_im
    ab_b_im = lo_re[..., None] * bb_im + lo_im[..., None] * bb_re
    kern = (jnp.einsum('gpn,dgnq->dgpq', c_re, ab_b_re, precision=hp)
            - jnp.einsum('gpn,dgnq->dgpq', c_im, ab_b_im, precision=hp))
    eye_g = jnp.eye(SSM_GROUPS, dtype=F32)
    rows = np.arange(SSM_WIDTH)
    expand = jnp.asarray((rows[None, :] % SSM_GROUP == np.arange(SSM_GROUP)[:, None]).astype(np.float32))
    same_group = jnp.asarray(rows[:, None] // SSM_GROUP == rows[None, :] // SSM_GROUP)
    lag_rows = kern.transpose(0, 1, 3, 2).reshape(t, SSM_WIDTH, SSM_GROUP)
    lag_blocks = jnp.where(same_group, jnp.einsum('drp,pc->drc', lag_rows, expand, precision=hp), 0.0)
    m_intra = _block_toeplitz(lag_blocks.astype(BF16))
    hi_re, hi_im = jnp.stack(pr[t - 1::-1]), jnp.stack(pi[t - 1::-1])
    sb_re = hi_re[..., None] * bb_re - hi_im[..., None] * bb_im
    sb_im = hi_re[..., None] * bb_im + hi_im[..., None] * bb_re
    m_sum_re = jnp.einsum('jgnq,gh->jgqhn', sb_re, eye_g, precision=hp)
    m_sum_im = jnp.einsum('jgnq,gh->jgqhn', sb_im, eye_g, precision=hp)
    m_sum = jnp.concatenate([m_sum_re.reshape(S5_COLS, -1), m_sum_im.reshape(S5_COLS, -1)], axis=1)
    up_re, up_im = jnp.stack(pr[1:])[:, :, None, :], jnp.stack(pi[1:])[:, :, None, :]
    w_re = c_re[None] * up_re - c_im[None] * up_im
    w_im = c_re[None] * up_im + c_im[None] * up_re
    m_out_re = jnp.einsum('igpn,gh->gnihp', w_re, eye_g, precision=hp)
    m_out_im = jnp.einsum('igpn,gh->gnihp', -w_im, eye_g, precision=hp)
    m_out = jnp.concatenate([m_out_re.reshape(-1, S5_COLS), m_out_im.reshape(-1, S5_COLS)], axis=0)
    a_re = pw_re[t].reshape(1, -1)
    a_im = pw_im[t].reshape(1, -1)
    return m_intra.astype(BF16), m_sum.astype(BF16), m_out.astype(BF16), a_re, a_im


def _fold_rows(x, op):
    parts = [x[r:r + SUBLANES] for r in range(0, x.shape[0], SUBLANES)]
    while len(parts) > 1:
        parts = [op(parts[i], parts[i + 1]) for i in range(0, len(parts), 2)]
    return parts[0]


def _softmax_stage(s, m_ref, alpha_ref, p_ref):
    m_old = m_ref[...]
    m_new = jnp.maximum(m_old, jnp.max(_fold_rows(s, jnp.maximum), axis=0, keepdims=True))
    alpha_ref[...] = jnp.exp2(m_old - m_new)
    p_ref[...] = jnp.exp2(s - jnp.broadcast_to(m_new[0:1], s.shape)).astype(BF16)
    m_ref[...] = m_new


def _value_stage(pv, alpha_ref, p_ref, acc_ref):
    acc = acc_ref[...]
    acc_ref[...] = acc * jnp.broadcast_to(alpha_ref[0:1, :], acc.shape) + pv(p_ref[...])


def _attend_chunks(qt, qk, logits, pv, bufs, m_ref, acc_ref):
    s_a, s_b, p_a, p_b, al_a, al_b = bufs

    def plain(j, kind):
        qk(j, s_a)
        _softmax_stage(logits(j, s_a[...], kind), m_ref, al_a, p_a)
        _value_stage(lambda p: pv(j, p), al_a, p_a, acc_ref)

    @pl.when(qt == 0)
    def _():
        plain(0, 'diag')

    @pl.when(qt >= 1)
    def _():
        n_far = qt - 1
        first = n_far % 2

        @pl.when(first == 1)
        def _():
            plain(0, 'far')

        qk(first, s_a)
        p_b[...] = jnp.zeros(p_b.shape, p_b.dtype)
        al_b[...] = jnp.ones(al_b.shape, al_b.dtype)

        def pair(c, kind0, kind1, last):
            qk(c + 1, s_b)
            _value_stage(lambda p: pv(jnp.maximum(c - 1, 0), p), al_b, p_b, acc_ref)
            _softmax_stage(logits(c, s_a[...], kind0), m_ref, al_a, p_a)
            if not last:
                qk(c + 2, s_a)
            _value_stage(lambda p: pv(c, p), al_a, p_a, acc_ref)
            _softmax_stage(logits(c + 1, s_b[...], kind1), m_ref, al_b, p_b)

        def body(i, carry):
            pair(first + 2 * i, 'far', 'far', False)
            return carry

        lax.fori_loop(0, (n_far - first) // 2, body, 0)
        pair(qt - 1, 'prev', 'diag', True)
        _value_stage(lambda p: pv(qt, p), al_b, p_b, acc_ref)


def _pipeline_buffers(width):
    return ([pltpu.VMEM((ATT_TILE, width), F32)] * 2 + [pltpu.VMEM((ATT_TILE, width), BF16)] * 2
            + [pltpu.VMEM((SUBLANES, width), F32)] * 2)


def _finish_heads_t(acc_ref, o_ref, n_heads, odd_swapped):
    hd = HEAD_DIM
    tq = o_ref.shape[0]
    for pr in range(n_heads // 2):
        ae = acc_ref[:, 2 * pr * tq:(2 * pr + 1) * tq]
        ao = acc_ref[:, (2 * pr + 1) * tq:(2 * pr + 2) * tq]
        oe = ae[0:hd] / ae[hd:2 * hd]
        oo = ao[hd:2 * hd] / ao[0:hd] if odd_swapped else ao[0:hd] / ao[hd:2 * hd]
        pair = jnp.concatenate([oe, oo], axis=0)
        o_ref[:, pr * LANES:(pr + 1) * LANES] = pair.T.astype(o_ref.dtype)


def _heads_side_by_side(qt_ref, qm_ref, n_heads):
    tq = qt_ref.shape[1]
    for h in range(n_heads):
        qm_ref[:, h * tq:(h + 1) * tq] = qt_ref[h * HEAD_DIM:(h + 1) * HEAD_DIM, :]


def _dsa_kernel(q_ref, qi_ref, wi_ref, kv_ref, ki_ref, band_ref, o_ref,
                score_ref, vt_ref, qm_ref, m_ref, acc_ref, thr_ref, *bufs, n_keys, seq):
    tq = tk = ATT_TILE
    qt = pl.program_id(1)
    n_chunks = qt + 1
    kf = float(n_keys)

    @pl.when(qt == 0)
    def _():
        top = lax.broadcasted_iota(jnp.int32, (LANES, tk), 0) < HEAD_DIM
        for j in range(seq // tk):
            vv = kv_ref[j * tk:(j + 1) * tk, LANES:2 * LANES].astype(F32)
            vt_ref[j] = jnp.where(top, vv.T, 1.0).astype(BF16)

    _heads_side_by_side(qi_ref, qm_ref, IDX_HEADS)
    head_w = jnp.concatenate(
        [wi_ref[h:h + 1, :] for h in range(IDX_HEADS)], axis=1
    ) * (IDX_HEADS ** -0.5 * IDX_DIM ** -0.5)
    key_minus_query = (lax.broadcasted_iota(jnp.int32, (tk, tq), 0)
                       - lax.broadcasted_iota(jnp.int32, (tk, tq), 1))

    def score_body(j, carry):
        mx, mn = carry
        kc = ki_ref[pl.ds(pl.multiple_of(j * tk, tk), tk), 0:IDX_DIM]
        d = _dot(kc, qm_ref[:, 0:IDX_HEADS * tq])
        d = jnp.broadcast_to(head_w, d.shape) * jnp.maximum(d, 0.0)
        sc = (d[:, 0:tq] + d[:, tq:2 * tq]) + (d[:, 2 * tq:3 * tq] + d[:, 3 * tq:4 * tq])
        causal = key_minus_query <= (qt - j) * tq
        score_ref[j] = jnp.where(causal, sc, NEG)
        mx = jnp.maximum(mx, _fold_rows(jnp.where(causal, sc, NEG), jnp.maximum))
        mn = jnp.minimum(mn, _fold_rows(jnp.where(causal, sc, -NEG), jnp.minimum))
        return mx, mn

    mx, mn = lax.fori_loop(0, n_chunks, score_body,
                           (jnp.full((SUBLANES, tq), NEG, F32), jnp.full((SUBLANES, tq), -NEG, F32)))

    t_q = qt * tq + lax.broadcasted_iota(jnp.int32, (1, tq), 1)
    thr_ref[...] = jnp.full((SUBLANES, tq), 0.5 * NEG, F32)

    def count(pred):
        def body(j, acc):
            return acc + _fold_rows(pred(score_ref[j]), jnp.add)
        acc = lax.fori_loop(0, n_chunks, body, jnp.zeros((SUBLANES, tq), F32))
        return jnp.sum(acc, axis=0, keepdims=True)

    def count_ge(x):
        xb = jnp.broadcast_to(x, (tk, tq))
        return count(lambda s: jnp.where(s >= xb, 1.0, 0.0))

    @pl.when((qt + 1) * tq > n_keys)
    def _search():
        need = (t_q >= n_keys).astype(F32)
        lo0 = jnp.min(mn, axis=0, keepdims=True)
        hi0 = jnp.max(mx, axis=0, keepdims=True)
        hi_top = hi0 + jnp.maximum(jnp.abs(hi0) * 2.0 ** -22, 1e-30)
        c_ge0 = count_ge(jnp.zeros((1, tq), F32))
        c_gt0 = count(lambda s: jnp.where(s > 0.0, 1.0, 0.0))
        n_valid = (t_q + 1).astype(F32)
        zero_hit = c_ge0 == kf
        zero_tie = (c_gt0 < kf) & (c_ge0 > kf)
        positive = c_gt0 >= kf
        hi_neg = hi0 < 0.0
        lo1 = jnp.where(positive, jnp.maximum(lo0, 0.0), lo0)
        c_lo1 = jnp.where(positive, c_ge0, n_valid)
        hi1 = jnp.where(positive | hi_neg, hi_top, 0.0)
        c_hi1 = jnp.where(positive | hi_neg, 0.0, c_ge0)
        hit0 = need * zero_hit.astype(F32)
        tie0 = need * zero_tie.astype(F32)
        done0 = jnp.maximum(1.0 - need, jnp.maximum(hit0, tie0))
        lo1 = jnp.where(tie0 > 0, 0.0, lo1)
        c_hi1 = jnp.where(tie0 > 0, c_gt0, c_hi1)
        x0 = jnp.zeros((1, tq), F32)
        state0 = (lo1, hi1, c_lo1, c_hi1, done0, tie0, x0, jnp.max(1.0 - done0))

        def cond(st):
            return st[7] > 0.0

        def body(st):
            for _ in range(SEARCH_UNROLL):
                st = step(st)
            return st

        def step(st):
            lo, hi, c_lo, c_hi, done, tie, x_fin, _ = st
            x = 0.5 * lo + 0.5 * hi
            inside = ((x > lo) & (x < hi)).astype(F32)
            active = 1.0 - done
            probe = active * inside
            c = count_ge(jnp.where(probe > 0, x, x_fin))
            hit = probe * (c == kf).astype(F32)
            more = probe * (c > kf).astype(F32)
            less = probe * (c < kf).astype(F32)
            new_tie = active * (1.0 - inside)
            lo = jnp.where(more > 0, x, lo)
            c_lo = jnp.where(more > 0, c, c_lo)
            hi = jnp.where(less > 0, x, hi)
            c_hi = jnp.where(less > 0, c, c_hi)
            x_fin = jnp.where(hit > 0, x, x_fin)
            tie = jnp.maximum(tie, new_tie)
            done = jnp.maximum(done, jnp.maximum(hit, new_tie))
            return (lo, hi, c_lo, c_hi, done, tie, x_fin, jnp.max(1.0 - done))

        st = state0
        for _ in range(SEARCH_FIXED):
            st = step(st)
        st = lax.while_loop(cond, body, st)
        lo, _, _, c_hi, _, tie, x_fin, _ = st
        thr = jnp.where(need > 0, jnp.where(tie > 0, lo, x_fin), 0.5 * NEG)
        thr_ref[...] = jnp.broadcast_to(thr, (SUBLANES, tq))

        @pl.when(jnp.max(tie) > 0.0)
        def _ties():
            want = jnp.broadcast_to(kf - c_hi, (tk, tq))
            thr_t = jnp.broadcast_to(jnp.where(tie > 0, thr, -NEG), (tk, tq))
            upto = (lax.broadcasted_iota(jnp.int32, (tk, tk), 1)
                    <= lax.broadcasted_iota(jnp.int32, (tk, tk), 0))
            prefix = jnp.where(upto, 1.0, 0.0).astype(BF16)

            def body(j, seen):
                s = score_ref[j]
                eq = s == thr_t
                rank = _dot(prefix, jnp.where(eq, 1.0, 0.0).astype(BF16))
                rank = rank + jnp.broadcast_to(seen, (tk, tq))
                score_ref[j] = jnp.where(eq, jnp.where(rank > want, NEG, s), s)
                return rank[tk - 1:tk, :]

            lax.fori_loop(0, n_chunks, body, jnp.zeros((1, tq), F32))

    _heads_side_by_side(q_ref, qm_ref, DSA_HEADS)
    m_ref[...] = jnp.full(m_ref.shape, NEG, F32)
    acc_ref[...] = jnp.zeros(acc_ref.shape, F32)
    thr_b = jnp.broadcast_to(thr_ref[0:1, :], (tk, tq))

    def qk(j, s_ref):
        rows = pl.ds(pl.multiple_of(j * tk, tk), tk)
        s_ref[...] = _dot(kv_ref[rows, 0:HEAD_DIM], qm_ref[...])

    def logits(j, s, kind):
        off = jnp.where(score_ref[j] >= thr_b, 0.0, NEG)
        s = s + jnp.concatenate([off] * DSA_HEADS, axis=1)
        if kind == 'prev':
            s = s + band_ref[0:tk, :]
        elif kind == 'diag':
            s = s + band_ref[tk:2 * tk, :]
        return s

    _attend_chunks(qt, qk, logits, lambda j, p: _dot(vt_ref[j], p), bufs, m_ref, acc_ref)
    _finish_heads_t(acc_ref, o_ref, DSA_HEADS, odd_swapped=False)


def _dsa(qb_t, qi_t, wi_t, kv, ki, band, bsz, seq, n_keys):
    tq = ATT_TILE
    nq = seq // tq
    qrow = lambda w: pl.BlockSpec((tq, w), lambda b, q: (b * nq + q, 0))
    qtile = lambda w: pl.BlockSpec((None, w, tq), lambda b, q: (b * nq + q, 0, 0))
    seq_blk = lambda w: pl.BlockSpec((seq, w), lambda b, q: (b, 0))
    return pl.pallas_call(
        functools.partial(_dsa_kernel, n_keys=n_keys, seq=seq), grid=(bsz, nq),
        in_specs=[qtile(DSA_WIDTH), qtile(IDX_HEADS * IDX_DIM), qtile(SUBLANES), seq_blk(2 * LANES),
                  seq_blk(LANES), _resident((2 * tq, DSA_HEADS * tq))],
        out_specs=qrow(DSA_WIDTH),
        out_shape=jax.ShapeDtypeStruct((bsz * seq, DSA_WIDTH), BF16),
        scratch_shapes=[pltpu.VMEM((nq, tq, tq), F32),
                        pltpu.VMEM((nq, LANES, tq), BF16),
                        pltpu.VMEM((HEAD_DIM, DSA_HEADS * tq), BF16),
                        pltpu.VMEM((SUBLANES, DSA_HEADS * tq), F32),
                        pltpu.VMEM((LANES, DSA_HEADS * tq), F32),
                        pltpu.VMEM((SUBLANES, tq), F32)] + _pipeline_buffers(DSA_HEADS * tq),
        compiler_params=_cparams(("parallel", "arbitrary")), name="dsa",
    )(qb_t, qi_t, wi_t, kv, ki, band)


def _band_kernel(bucket_ref, rb_ref, o_ref):
    bucket = bucket_ref[...]
    tq = bucket.shape[1]
    for h in range(DSA_HEADS):
        acc = jnp.zeros(bucket.shape, F32)
        for k in range(REL_BUCKETS - 1):
            acc = jnp.where(bucket == k, (rb_ref[k, h] - rb_ref[REL_BUCKETS - 1, h]) * LOG2E, acc)
        o_ref[:, h * tq:(h + 1) * tq] = acc


def _t5_bucket_np(dist):
    d = np.maximum(dist, 0)
    df = np.maximum(d, 1).astype(np.float32)
    log_b = REL_MAX_EXACT + (np.log(df / REL_MAX_EXACT) / math.log(REL_MAX_DIST / REL_MAX_EXACT)
                             * (REL_BUCKETS - REL_MAX_EXACT)).astype(np.int32)
    log_b = np.minimum(log_b, REL_BUCKETS - 1)
    return np.where(d < REL_MAX_EXACT, d, log_b).astype(np.int32)


def _rel_band(rel_bias):
    tq = ATT_TILE
    i = np.arange(tq)[None, :]
    j = np.arange(2 * tq)[:, None]
    bucket = jnp.asarray(_t5_bucket_np(i + tq - j))
    return pl.pallas_call(
        _band_kernel,
        in_specs=[pl.BlockSpec(memory_space=pltpu.VMEM), pl.BlockSpec(memory_space=pltpu.SMEM)],
        out_specs=pl.BlockSpec(memory_space=pltpu.VMEM),
        out_shape=jax.ShapeDtypeStruct((2 * tq, DSA_HEADS * tq), F32), name="rel_band",
    )(bucket, rel_bias)


def _cum_kernel(aux_ref, bf_ref, col_ref, row_ref, *, seq):
    x = aux_ref[...] + bf_ref[...]
    logf = -(jnp.maximum(-x, 0.0) + jnp.log1p(jnp.exp(-jnp.abs(x))))
    c = logf.T[0:SUBLANES]
    pos = lax.broadcasted_iota(jnp.int32, c.shape, 1)
    shift = 1
    while shift < seq:
        c = c + jnp.where(pos >= shift, pltpu.roll(c, shift, 1), 0.0)
        shift *= 2
    c = c * LOG2E
    col_ref[...] = jnp.concatenate([c, jnp.zeros((LANES - SUBLANES, seq), F32)], axis=0).T
    for j in range(seq // ATT_TILE):
        row_ref[j] = c[:, j * ATT_TILE:(j + 1) * ATT_TILE]


def _fox_cum(aux, b_f, bsz, seq):
    nq = seq // ATT_TILE
    bf = jnp.zeros((1, LANES), F32).at[0, AUX_F:AUX_F + FOX_HEADS].set(b_f)
    return pl.pallas_call(
        functools.partial(_cum_kernel, seq=seq), grid=(bsz,),
        in_specs=[pl.BlockSpec((seq, LANES), lambda b: (b, 0)), _resident((1, LANES))],
        out_specs=[pl.BlockSpec((seq, LANES), lambda b: (b, 0)),
                   pl.BlockSpec((None, nq, SUBLANES, ATT_TILE), lambda b: (b, 0, 0, 0))],
        out_shape=[jax.ShapeDtypeStruct((bsz * seq, LANES), F32),
                   jax.ShapeDtypeStruct((bsz, nq, SUBLANES, ATT_TILE), F32)],
        compiler_params=_cparams(("parallel",)), name="fox_cum")(aux, bf)


def _bf16_pieces(x):
    hi = x.astype(BF16).astype(F32)
    rest = x - hi
    mid = rest.astype(BF16).astype(F32)
    return [hi, mid, (rest - mid).astype(BF16).astype(F32)]


def _fox_kernel(q_ref, k_ref, v_ref, ccol_ref, crow_ref, o_ref,
                vte_ref, vto_ref, ka_ref, qm_ref, m_ref, acc_ref, *bufs, seq):
    tq = tk = ATT_TILE
    qt = pl.program_id(1)
    heads = range(FOX_HEADS)

    @pl.when(qt == 0)
    def _():
        top = lax.broadcasted_iota(jnp.int32, (LANES, tk), 0) < HEAD_DIM
        for j in range(seq // tk):
            rows = slice(j * tk, (j + 1) * tk)
            for pr in range(FOX_HEADS // 2):
                v_t = v_ref[rows, pr * LANES:(pr + 1) * LANES].astype(F32).T
                vte_ref[pr, j] = jnp.where(top, v_t, 1.0).astype(BF16)
                vto_ref[pr, j] = jnp.where(top, 1.0, v_t).astype(BF16)
            cc = ccol_ref[rows, :]
            lane = lax.broadcasted_iota(jnp.int32, (tk, LANES), 1)
            for h in heads:
                k_h = k_ref[rows, (h // 2) * LANES:(h // 2 + 1) * LANES].astype(F32)
                if h % 2 == 1:
                    k_h = pltpu.roll(k_h, HEAD_DIM, 1)
                pieces = _bf16_pieces(jnp.broadcast_to(cc[:, AUX_F + h:AUX_F + h + 1], (tk, LANES)))
                ext = jnp.where(lane < HEAD_DIM + 2 * len(pieces), 1.0, 0.0)
                for i, piece in enumerate(pieces):
                    ext = jnp.where(lane == HEAD_DIM + i, piece, ext)
                ka_ref[h, rows, :] = jnp.where(lane < HEAD_DIM, k_h, ext).astype(BF16)

    c_q = crow_ref[qt]
    row = lax.broadcasted_iota(jnp.int32, (LANES - HEAD_DIM, tq), 0)
    for h in heads:
        pieces = _bf16_pieces(c_q[AUX_F + h:AUX_F + h + 1, :])
        ext = jnp.where(row < len(pieces), -1.0, 0.0)
        for i, piece in enumerate(pieces):
            ext = jnp.where(row == len(pieces) + i, jnp.broadcast_to(piece, ext.shape), ext)
        q_h = q_ref[h * HEAD_DIM:(h + 1) * HEAD_DIM, :]
        qm_ref[:, h * tq:(h + 1) * tq] = jnp.concatenate([q_h, ext.astype(BF16)], axis=0)
    m_ref[...] = jnp.full(m_ref.shape, NEG, F32)
    acc_ref[...] = jnp.zeros(acc_ref.shape, F32)
    causal_off = jnp.where(lax.broadcasted_iota(jnp.int32, (tk, tq), 0)
                           <= lax.broadcasted_iota(jnp.int32, (tk, tq), 1), 0.0, NEG)

    def qk(j, s_ref):
        rows = pl.ds(pl.multiple_of(j * tk, tk), tk)
        for h in heads:
            cols = slice(h * tq, (h + 1) * tq)
            s_ref[:, cols] = _dot(ka_ref[h, rows, :], qm_ref[:, cols])

    def logits(j, s, kind):
        if kind == 'diag':
            s = s + jnp.concatenate([causal_off] * FOX_HEADS, axis=1)
        return s

    def pv(j, p):
        return jnp.concatenate(
            [_dot(vte_ref[h // 2, j] if h % 2 == 0 else vto_ref[h // 2, j],
                  p[:, h * tq:(h + 1) * tq]) for h in heads], axis=1)

    _attend_chunks(qt, qk, logits, pv, bufs, m_ref, acc_ref)
    _finish_heads_t(acc_ref, o_ref, FOX_HEADS, odd_swapped=True)


def _fox(q_t, kv, c_col, c_row, bsz, seq):
    tq = ATT_TILE
    nq = seq // tq
    npair = FOX_HEADS // 2
    return pl.pallas_call(
        functools.partial(_fox_kernel, seq=seq), grid=(bsz, nq),
        in_specs=[pl.BlockSpec((None, FOX_WIDTH, tq), lambda b, q: (b * nq + q, 0, 0)),
                  pl.BlockSpec((seq, FOX_WIDTH), lambda b, q: (b, 0)),
                  pl.BlockSpec((seq, FOX_WIDTH), lambda b, q: (b, 1)),
                  pl.BlockSpec((seq, LANES), lambda b, q: (b, 0)),
                  pl.BlockSpec((None, nq, SUBLANES, tq), lambda b, q: (b, 0, 0, 0))],
        out_specs=pl.BlockSpec((tq, FOX_WIDTH), lambda b, q: (b * nq + q, 0)),
        out_shape=jax.ShapeDtypeStruct((bsz * seq, FOX_WIDTH), BF16),
        scratch_shapes=[pltpu.VMEM((npair, nq, LANES, tq), BF16),
                        pltpu.VMEM((npair, nq, LANES, tq), BF16),
                        pltpu.VMEM((FOX_HEADS, seq, LANES), BF16),
                        pltpu.VMEM((LANES, FOX_HEADS * tq), BF16),
                        pltpu.VMEM((SUBLANES, FOX_HEADS * tq), F32),
                        pltpu.VMEM((LANES, FOX_HEADS * tq), F32)] + _pipeline_buffers(FOX_HEADS * tq),
        compiler_params=_cparams(("parallel", "arbitrary")), name="fox",
    )(q_t, kv, kv, c_col, c_row)


def _gelu(y):
    return 0.5 * y * (1.0 + jnp.tanh(math.sqrt(2.0 / math.pi) * (y + 0.044715 * (y * y * y))))


def _merge_kernel(x_ref, yc_ref, u_ref, d_ref, wglu_ref, b_ref, c_ref, gate_ref,
                  wa_ref, wb_ref, wc_ref, wo_ref, o_ref, tmp_ref):
    d = D_MODEL
    y = _chunks_to_rows(yc_ref, tmp_ref, SSM_WIDTH) + d_ref[...] * u_ref[...]
    z = _dot(_gelu(y).astype(BF16), wglu_ref[...])
    a = z[:, :SSM_WIDTH] * _sigmoid(z[:, SSM_WIDTH:])
    merged = _sigmoid(gate_ref[:, 0:d].astype(F32)) * _dot(a.astype(BF16), wa_ref[...])
    merged += _sigmoid(gate_ref[:, d:2 * d].astype(F32)) * _dot(b_ref[...], wb_ref[...])
    merged += _sigmoid(gate_ref[:, 2 * d:3 * d].astype(F32)) * _dot(c_ref[...], wc_ref[...])
    o_ref[...] = x_ref[...] + _dot(merged.astype(BF16), wo_ref[...])


def _merge(xt, yc, u, d_skip, w_glu, b, c, gates, wa, wb, wc, wo, tm=1024):
    m, d = xt.shape
    row = lambda w: pl.BlockSpec((tm, w), lambda i: (i, 0))
    return pl.pallas_call(
        _merge_kernel, grid=(m // tm,),
        in_specs=[row(d), pl.BlockSpec((tm // S5_CHUNK, S5_COLS), lambda i: (i, 0)),
                  row(SSM_WIDTH), _resident((1, SSM_WIDTH)),
                  _resident((SSM_WIDTH, 2 * SSM_WIDTH)), row(DSA_WIDTH), row(FOX_WIDTH), row(3 * d),
                  _resident((SSM_WIDTH, d)), _resident((DSA_WIDTH, d)), _resident((FOX_WIDTH, d)),
                  _resident((d, d))],
        out_specs=row(d), out_shape=jax.ShapeDtypeStruct((m, d), F32),
        scratch_shapes=[pltpu.VMEM((SSM_WIDTH // LANES, tm, LANES), F32)],
        compiler_params=_cparams(("parallel",)), name="merge",
    )(xt, yc, u, d_skip.reshape(1, -1), w_glu, b, c, gates, wa, wb, wc, wo)


def _pack_w_in(w_in):
    d = w_in.shape[0]
    splits = (SSM_WIDTH, DSA_WIDTH, DSA_LATENT, IDX_HEADS * IDX_DIM, IDX_DIM, IDX_HEADS,
              FOX_WIDTH, FOX_WIDTH, FOX_WIDTH, FOX_HEADS, D_MODEL, D_MODEL, D_MODEL)
    pts = np.cumsum(splits)[:-1]
    (w_u, w_qb, w_ckv, w_qi, w_ki, w_wi, w_qc, w_kc, w_vc, w_fc, w_ga, w_gb, w_gc) = jnp.split(
        w_in, pts, axis=1)
    scale = HEAD_DIM ** -0.5 * LOG2E
    aux = jnp.zeros((d, LANES), F32)
    aux = aux.at[:, AUX_F:AUX_F + FOX_HEADS].set(w_fc).at[:, AUX_W:AUX_W + IDX_HEADS].set(w_wi)
    cols = [w_u, w_qb * scale, w_qi, w_ki, w_ki, w_ckv, aux, w_qc * scale, w_kc, w_vc,
            w_ga, w_gb, w_gc]
    return jnp.concatenate(cols, axis=1).astype(BF16)


def kernel(x, ffn1_norm, ffn1_w_gate, ffn1_w_up, ffn1_w_down, mix_norm, w_in, ssm_lambda_re, ssm_lambda_im, ssm_log_dt, ssm_b_re, ssm_b_im, ssm_c_re, ssm_c_im, ssm_d, ssm_w_glu, dsa_kv_norm, dsa_w_uk, dsa_w_uv, rel_bias, fox_b_f, w_branch_ssm, w_branch_dsa, w_branch_fox, w_out, ffn2_norm, ffn2_w_gate, ffn2_w_up, ffn2_w_down, final_norm):
    bsz, seq, d = x.shape
    depth = w_in.shape[0]
    n_keys = min(TOPK_MAX, seq // 4)
    assert d == D_MODEL and seq % ATT_TILE == 0 and (bsz * seq) % (S5_CHUNK * 512) == 0
    assert n_keys % LANES == 0 and bsz % SUBLANES == 0
    m = bsz * seq
    n_chunks = seq // S5_CHUNK
    xt = x.reshape(m, d)
    band = _rel_band(rel_bias)
    bf = lambda w: w.astype(BF16)
    for l in range(depth):
        xt = _ffn(xt, ffn1_norm[l], bf(ffn1_w_gate[l]), bf(ffn1_w_up[l]), bf(ffn1_w_down[l]))
        w_kv = bf(jnp.concatenate([dsa_w_uk[l], dsa_w_uk[l], dsa_w_uv[l], dsa_w_uv[l]], axis=1))
        u, qb_t, qi_t, ki, kv, aux, wi_t, qc_t, kv_fox, gates, u8 = _inproj(
            xt, mix_norm[l], _pack_w_in(w_in[l]), dsa_kv_norm[l], w_kv)
        m_intra, m_sum, m_out, a_re, a_im = _s5_matrices(
            ssm_lambda_re[l], ssm_lambda_im[l], ssm_log_dt[l], ssm_b_re[l], ssm_b_im[l],
            ssm_c_re[l], ssm_c_im[l])
        y_intra, p = _s5a(u8, m_intra, m_sum)
        h_re, h_im = _s5b(p, a_re, a_im, n_chunks, nb=SUBLANES)
        yc = _s5c(y_intra, h_re, h_im, m_out)
        b_out = _dsa(qb_t, qi_t, wi_t, kv, ki, band, bsz, seq, n_keys)
        c_col, c_row = _fox_cum(aux, fox_b_f[l], bsz, seq)
        c_out = _fox(qc_t, kv_fox, c_col, c_row, bsz, seq)
        xt = _merge(xt, yc, u, ssm_d[l], bf(ssm_w_glu[l]), b_out, c_out, gates,
                    bf(w_branch_ssm[l]), bf(w_branch_dsa[l]), bf(w_branch_fox[l]), bf(w_out[l]))
        last = final_norm if l == depth - 1 else None
        xt = _ffn(xt, ffn2_norm[l], bf(ffn2_w_gate[l]), bf(ffn2_w_up[l]), bf(ffn2_w_down[l]),
                  final_norm=last)
    return xt.reshape(bsz, seq, d)
```

```python
import functools
import math

import numpy as np
import jax
import jax.numpy as jnp
from jax import lax
from jax.experimental import pallas as pl
from jax.experimental.pallas import tpu as pltpu

F32 = jnp.float32
BF16 = jnp.bfloat16

D_MODEL = 1024
SSM_WIDTH = D_MODEL // 4
SSM_GROUP = 16
SSM_GROUPS = SSM_WIDTH // SSM_GROUP
SSM_STATE = 64
HEAD_DIM = 64
DSA_HEADS = 6
DSA_WIDTH = DSA_HEADS * HEAD_DIM
DSA_LATENT = 2 * HEAD_DIM
IDX_HEADS = 4
IDX_DIM = 64
TOPK_MAX = 256
FOX_HEADS = 6
FOX_WIDTH = FOX_HEADS * HEAD_DIM
REL_BUCKETS = 32
REL_MAX_EXACT = 16
REL_MAX_DIST = 128
FFN_HIDDEN = 2816
EPS = 1e-6
NEG = -1e30
LOG2E = math.log2(math.e)

LANES = 128
SUBLANES = 8
S5_CHUNK = 8
S5_COLS = S5_CHUNK * SSM_WIDTH
S5_STATE = 2 * SSM_GROUPS * SSM_STATE
ATT_TILE = 256
SEARCH_FIXED = 16
SEARCH_UNROLL = 2
VMEM_LIMIT = 56 * 1024 * 1024

_SEG = dict(u=SSM_WIDTH, qb=DSA_WIDTH, qi=IDX_HEADS * IDX_DIM, ki=LANES, ckv=DSA_LATENT, aux=LANES,
            qkv=3 * FOX_WIDTH, g=3 * D_MODEL)
AUX_F = 0
AUX_W = 8


def _cparams(sem):
    return pltpu.CompilerParams(dimension_semantics=sem, vmem_limit_bytes=VMEM_LIMIT)


def _resident(shape):
    nd = len(shape)
    return pl.BlockSpec(shape, lambda *_: (0,) * nd, pipeline_mode=pl.Buffered(1))


def _rms(x, g):
    return x * lax.rsqrt(jnp.mean(x * x, axis=-1, keepdims=True) + EPS) * g


def _sigmoid(x):
    return 1.0 / (1.0 + jnp.exp(-x))


def _dot(a, b):
    return jnp.dot(a, b, preferred_element_type=F32)


def _ffn_body(x, g_ref, wg_ref, wu_ref, wd_ref, fc):
    h = _rms(x, g_ref[...]).astype(BF16)
    acc = None
    for c in range(FFN_HIDDEN // fc):
        sl = slice(c * fc, (c + 1) * fc)
        gt = _dot(h, wg_ref[:, sl])
        up = _dot(h, wu_ref[:, sl])
        a = (gt * _sigmoid(gt) * up).astype(BF16)
        d = _dot(a, wd_ref[sl, :])
        acc = d if acc is None else acc + d
    return x + 0.5 * acc


def _ffn_kernel(x_ref, g_ref, wg_ref, wu_ref, wd_ref, o_ref, *, fc):
    o_ref[...] = _ffn_body(x_ref[...], g_ref, wg_ref, wu_ref, wd_ref, fc)


def _ffn_final_kernel(x_ref, g_ref, wg_ref, wu_ref, wd_ref, fn_ref, o_ref, *, fc):
    y = _ffn_body(x_ref[...], g_ref, wg_ref, wu_ref, wd_ref, fc)
    o_ref[...] = _rms(y, fn_ref[...])


def _ffn(xt, norm, wg, wu, wd, final_norm=None, tm=1024, fc=256):
    m, d = xt.shape
    f = wg.shape[1]
    row = pl.BlockSpec((tm, d), lambda i: (i, 0))
    vec = _resident((1, d))
    in_specs = [row, vec, _resident((d, f)), _resident((d, f)), _resident((f, d))]
    args = [xt, norm.reshape(1, d), wg, wu, wd]
    if final_norm is None:
        body = functools.partial(_ffn_kernel, fc=fc)
    else:
        body = functools.partial(_ffn_final_kernel, fc=fc)
        in_specs.append(vec)
        args.append(final_norm.reshape(1, d))
    return pl.pallas_call(
        body, grid=(m // tm,), in_specs=in_specs, out_specs=row,
        out_shape=jax.ShapeDtypeStruct((m, d), F32),
        compiler_params=_cparams(("parallel",)), name="ffn")(*args)


def _rows_to_chunks(x, tmp_ref, o_ref):
    n, w = x.shape
    for hh in range(w // LANES):
        tmp_ref[hh] = x[:, hh * LANES:(hh + 1) * LANES]
    for i in range(S5_CHUNK):
        for hh in range(w // LANES):
            o_ref[:, i * w + hh * LANES:i * w + (hh + 1) * LANES] = (
                tmp_ref[hh, pl.ds(i, n // S5_CHUNK, stride=S5_CHUNK), :])


def _chunks_to_rows(x_ref, tmp_ref, w):
    n = x_ref.shape[0]
    for i in range(S5_CHUNK):
        for hh in range(w // LANES):
            tmp_ref[hh, pl.ds(i, n, stride=S5_CHUNK), :] = (
                x_ref[:, i * w + hh * LANES:i * w + (hh + 1) * LANES])
    return jnp.concatenate([tmp_ref[hh] for hh in range(w // LANES)], axis=1)


def _emit_transposed(piece, o_ref, rows=None):
    for t in range(piece.shape[0] // ATT_TILE):
        slab_t = piece[t * ATT_TILE:(t + 1) * ATT_TILE, :].T
        o_ref[t] = (slab_t if rows is None else slab_t[rows]).astype(o_ref.dtype)


def _inproj_kernel(x_ref, g_ref, w_ref, kvn_ref, wkv_ref,
                   u_ref, qbt_ref, qit_ref, ki_ref, kv_ref, aux_ref, wit_ref, qct_ref, kvc_ref,
                   gate_ref, u8_ref, tmp_ref):
    h = _rms(x_ref[...], g_ref[...]).astype(BF16)
    outs = dict(u=u_ref, ki=ki_ref, aux=aux_ref)
    off = 0
    for group in (("u", "qb", "qi", "ki"), ("ckv", "aux", "qkv")):
        width = sum(_SEG[name] for name in group)
        r = _dot(h, w_ref[:, off:off + width])
        off += width
        lo = 0
        for name in group:
            piece = r[:, lo:lo + _SEG[name]]
            lo += _SEG[name]
            if name == "ckv":
                c = _rms(piece, kvn_ref[...]).astype(BF16)
                kv_ref[...] = _dot(c, wkv_ref[...]).astype(BF16)
            elif name == "qb":
                _emit_transposed(piece, qbt_ref)
            elif name == "qi":
                _emit_transposed(piece, qit_ref)
            elif name == "qkv":
                _emit_transposed(piece[:, :FOX_WIDTH], qct_ref)
                kvc_ref[...] = piece[:, FOX_WIDTH:].astype(BF16)
            else:
                outs[name][...] = piece.astype(outs[name].dtype)
            if name == "u":
                _rows_to_chunks(piece, tmp_ref, u8_ref)
            if name == "aux":
                _emit_transposed(piece, wit_ref, rows=slice(AUX_W, AUX_W + SUBLANES))
    for c in range(_SEG["g"] // D_MODEL):
        cols = slice(c * D_MODEL, (c + 1) * D_MODEL)
        gate_ref[:, cols] = _dot(h, w_ref[:, off + c * D_MODEL:off + (c + 1) * D_MODEL]).astype(BF16)


def _inproj(xt, norm, w_all, kv_norm, w_kv, tm=1024):
    m, d = xt.shape
    wtot = w_all.shape[1]
    tq = ATT_TILE

    def rows(w, dt):
        return jax.ShapeDtypeStruct((m, w), dt), pl.BlockSpec((tm, w), lambda i: (i, 0))

    def tiles_t(w, dt):
        return (jax.ShapeDtypeStruct((m // tq, w, tq), dt),
                pl.BlockSpec((tm // tq, w, tq), lambda i: (i, 0, 0)))

    outs = [rows(SSM_WIDTH, F32), tiles_t(DSA_WIDTH, BF16), tiles_t(IDX_HEADS * IDX_DIM, BF16),
            rows(LANES, BF16), rows(2 * LANES, BF16), rows(LANES, F32), tiles_t(SUBLANES, F32),
            tiles_t(FOX_WIDTH, BF16), rows(2 * FOX_WIDTH, BF16), rows(3 * D_MODEL, BF16),
            (jax.ShapeDtypeStruct((m // S5_CHUNK, S5_COLS), F32),
             pl.BlockSpec((tm // S5_CHUNK, S5_COLS), lambda i: (i, 0)))]
    out_shape = [o[0] for o in outs]
    out_specs = [o[1] for o in outs]
    return pl.pallas_call(
        _inproj_kernel, grid=(m // tm,),
        in_specs=[pl.BlockSpec((tm, d), lambda i: (i, 0)), _resident((1, d)),
                  _resident((d, wtot)), _resident((1, DSA_LATENT)),
                  _resident((DSA_LATENT, 2 * LANES))],
        out_specs=out_specs, out_shape=out_shape,
        scratch_shapes=[pltpu.VMEM((SSM_WIDTH // LANES, tm, LANES), F32)],
        compiler_params=_cparams(("parallel",)), name="inproj",
    )(xt, norm.reshape(1, d), w_all, kv_norm.reshape(1, DSA_LATENT), w_kv)


S5_SLABS = S5_STATE // LANES
S5_SCAN_SLABS = 4


def _s5a_kernel(u_ref, mi_ref, ms_ref, yi_ref, p_ref):
    u = u_ref[...].astype(BF16)
    yi_ref[...] = _dot(u, mi_ref[...])
    p = _dot(u, ms_ref[...])
    for k in range(S5_SLABS):
        p_ref[k] = p[:, k * LANES:(k + 1) * LANES]


def _s5a(u8, m_intra, m_sum, tr=512):
    r = u8.shape[0]
    row = pl.BlockSpec((tr, S5_COLS), lambda i: (i, 0))
    slabs = pl.BlockSpec((S5_SLABS, tr, LANES), lambda i: (0, i, 0))
    return pl.pallas_call(
        _s5a_kernel, grid=(r // tr,),
        in_specs=[row, _resident((S5_COLS, S5_COLS)), _resident((S5_COLS, S5_STATE))],
        out_specs=[row, slabs],
        out_shape=[jax.ShapeDtypeStruct((r, S5_COLS), F32),
                   jax.ShapeDtypeStruct((S5_SLABS, r, LANES), F32)],
        compiler_params=_cparams(("parallel",)), name="s5_chunk")(u8, m_intra, m_sum)


def _s5b_kernel(pr_ref, pi_ref, ar_ref, ai_ref, hr_ref, hi_ref, *, nb, nc):
    ar = ar_ref[...]
    ai = ai_ref[...]
    ns = S5_SCAN_SLABS

    def step(c, carry):
        hr, hi = carry
        rows = pl.ds(c, nb, stride=nc)
        for k in range(ns):
            hr_ref[k, rows, :] = hr[:, k * LANES:(k + 1) * LANES]
            hi_ref[k, rows, :] = hi[:, k * LANES:(k + 1) * LANES]
        pr = jnp.concatenate([pr_ref[k, rows, :] for k in range(ns)], axis=1)
        pi = jnp.concatenate([pi_ref[k, rows, :] for k in range(ns)], axis=1)
        return (ar * hr - ai * hi + pr, ar * hi + ai * hr + pi)

    z = jnp.zeros((nb, ns * LANES), F32)
    lax.fori_loop(0, nc, step, (z, z))


def _s5b(p, a_re, a_im, nc, nb):
    r = p.shape[1]
    ns = S5_SCAN_SLABS
    half = S5_SLABS // 2
    re_blk = pl.BlockSpec((ns, nb * nc, LANES), lambda i, j: (j, i, 0))
    im_blk = pl.BlockSpec((ns, nb * nc, LANES), lambda i, j: (j + half // ns, i, 0))
    a_blk = pl.BlockSpec((1, ns * LANES), lambda i, j: (0, j))
    return pl.pallas_call(
        functools.partial(_s5b_kernel, nb=nb, nc=nc), grid=(r // (nb * nc), half // ns),
        in_specs=[re_blk, im_blk, a_blk, a_blk],
        out_specs=[re_blk, re_blk],
        out_shape=[jax.ShapeDtypeStruct((half, r, LANES), F32)] * 2,
        compiler_params=_cparams(("parallel", "parallel")), name="s5_scan",
    )(p, p, a_re, a_im)


def _s5c_kernel(yi_ref, hr_ref, hi_ref, mo_ref, y_ref):
    half = S5_STATE // 2
    hr = jnp.concatenate([hr_ref[k] for k in range(S5_SLABS // 2)], axis=1)
    hi = jnp.concatenate([hi_ref[k] for k in range(S5_SLABS // 2)], axis=1)
    y = yi_ref[...]
    y = y + _dot(hr.astype(BF16), mo_ref[:half, :])
    y = y + _dot(hi.astype(BF16), mo_ref[half:, :])
    y_ref[...] = y


def _s5c(yi, h_re, h_im, m_out, tr=512):
    r = yi.shape[0]
    row = pl.BlockSpec((tr, S5_COLS), lambda i: (i, 0))
    slabs = pl.BlockSpec((S5_SLABS // 2, tr, LANES), lambda i: (0, i, 0))
    return pl.pallas_call(
        _s5c_kernel, grid=(r // tr,),
        in_specs=[row, slabs, slabs, _resident((S5_STATE, S5_COLS))],
        out_specs=row, out_shape=jax.ShapeDtypeStruct((r, S5_COLS), F32),
        compiler_params=_cparams(("parallel",)), name="s5_out")(yi, h_re, h_im, m_out)


def _toeplitz_kernel(lag_ref, o_ref):
    j = pl.program_id(0)
    t, w, _ = lag_ref.shape
    for i in range(t):
        blk = lag_ref[jnp.maximum(i - j, 0)]
        o_ref[:, i * w:(i + 1) * w] = jnp.where(i >= j, blk, jnp.zeros_like(blk))


def _block_toeplitz(lag_blocks):
    t, w, _ = lag_blocks.shape
    return pl.pallas_call(
        _toeplitz_kernel, grid=(t,),
        in_specs=[_resident((t, w, w))],
        out_specs=pl.BlockSpec((w, t * w), lambda j: (j, 0)),
        out_shape=jax.ShapeDtypeStruct((t * w, t * w), lag_blocks.dtype), name="s5_toeplitz",
    )(lag_blocks)


def _s5_matrices(lam_re, lam_im, log_dt, b_re, b_im, c_re, c_im):
    hp = lax.Precision.HIGHEST
    t = S5_CHUNK
    dt = jnp.exp(log_dt)[:, None]
    lr, li = lam_re, lam_im
    mag = jnp.exp(lr * dt)
    ab_re, ab_im = mag * jnp.cos(li * dt), mag * jnp.sin(li * dt)
    den = lr * lr + li * li
    nr, ni = ab_re - 1.0, ab_im
    s_re = (nr * lr + ni * li) / den
    s_im = (ni * lr - nr * li) / den
    bb_re = s_re[..., None] * b_re - s_im[..., None] * b_im
    bb_im = s_re[..., None] * b_im + s_im[..., None] * b_re
    pr, pi = [jnp.ones_like(ab_re)], [jnp.zeros_like(ab_re)]
    for _ in range(t):
        pr.append(pr[-1] * ab_re - pi[-1] * ab_im)
        pi.append(pr[-2] * ab_im + pi[-1] * ab_re)
    pw_re, pw_im = jnp.stack(pr), jnp.stack(pi)
    lo_re, lo_im = jnp.stack(pr[:t]), jnp.stack(pi[:t])
    ab_b_re = lo_re[..., None] * bb_re - lo_im[..., None] * bb_im
    ab_b_im = lo_re[..., None] * bb_im + lo_im[..., None] * bb_re
    kern = (jnp.einsum('gpn,dgnq->dgpq', c_re, ab_b_re, precision=hp)
            - jnp.einsum('gpn,dgnq->dgpq', c_im, ab_b_im, precision=hp))
    eye_g = jnp.eye(SSM_GROUPS, dtype=F32)
    rows = np.arange(SSM_WIDTH)
    expand = jnp.asarray((rows[None, :] % SSM_GROUP == np.arange(SSM_GROUP)[:, None]).astype(np.float32))
    same_group = jnp.asarray(rows[:, None] // SSM_GROUP == rows[None, :] // SSM_GROUP)
    lag_rows = kern.transpose(0, 1, 3, 2).reshape(t, SSM_WIDTH, SSM_GROUP)
    lag_blocks = jnp.where(same_group, jnp.einsum('drp,pc->drc', lag_rows, expand, precision=hp), 0.0)
    m_intra = _block_toeplitz(lag_blocks.astype(BF16))
    hi_re, hi_im = jnp.stack(pr[t - 1::-1]), jnp.stack(pi[t - 1::-1])
    sb_re = hi_re[..., None] * bb_re - hi_im[..., None] * bb_im
    sb_im = hi_re[..., None] * bb_im + hi_im[..., None] * bb_re
    m_sum_re = jnp.einsum('jgnq,gh->jgqhn', sb_re, eye_g, precision=hp)
    m_sum_im = jnp.einsum('jgnq,gh->jgqhn', sb_im, eye_g, precision=hp)
    m_sum = jnp.concatenate([m_sum_re.reshape(S5_COLS, -1), m_sum_im.reshape(S5_COLS, -1)], axis=1)
    up_re, up_im = jnp.stack(pr[1:])[:, :, None, :], jnp.stack(pi[1:])[:, :, None, :]
    w_re = c_re[None] * up_re - c_im[None] * up_im
    w_im = c_re[None] * up_im + c_im[None] * up_re
    m_out_re = jnp.einsum('igpn,gh->gnihp', w_re, eye_g, precision=hp)
    m_out_im = jnp.einsum('igpn,gh->gnihp', -w_im, eye_g, precision=hp)
    m_out = jnp.concatenate([m_out_re.reshape(-1, S5_COLS), m_out_im.reshape(-1, S5_COLS)], axis=0)
    a_re = pw_re[t].reshape(1, -1)
    a_im = pw_im[t].reshape(1, -1)
    return m_intra.astype(BF16), m_sum.astype(BF16), m_out.astype(BF16), a_re, a_im


def _fold_rows(x, op):
    parts = [x[r:r + SUBLANES] for r in range(0, x.shape[0], SUBLANES)]
    while len(parts) > 1:
        parts = [op(parts[i], parts[i + 1]) for i in range(0, len(parts), 2)]
    return parts[0]


def _softmax_stage(s, m_ref, alpha_ref, p_ref):
    m_old = m_ref[...]
    m_new = jnp.maximum(m_old, jnp.max(_fold_rows(s, jnp.maximum), axis=0, keepdims=True))
    alpha_ref[...] = jnp.exp2(m_old - m_new)
    p_ref[...] = jnp.exp2(s - jnp.broadcast_to(m_new[0:1], s.shape)).astype(BF16)
    m_ref[...] = m_new


def _value_stage(pv, alpha_ref, p_ref, acc_ref):
    acc = acc_ref[...]
    acc_ref[...] = acc * jnp.broadcast_to(alpha_ref[0:1, :], acc.shape) + pv(p_ref[...])


def _attend_chunks(qt, qk, logits, pv, bufs, m_ref, acc_ref):
    s_a, s_b, p_a, p_b, al_a, al_b = bufs

    def plain(j, kind):
        qk(j, s_a)
        _softmax_stage(logits(j, s_a[...], kind), m_ref, al_a, p_a)
        _value_stage(lambda p: pv(j, p), al_a, p_a, acc_ref)

    @pl.when(qt == 0)
    def _():
        plain(0, 'diag')

    @pl.when(qt >= 1)
    def _():
        n_far = qt - 1
        first = n_far % 2

        @pl.when(first == 1)
        def _():
            plain(0, 'far')

        qk(first, s_a)
        p_b[...] = jnp.zeros(p_b.shape, p_b.dtype)
        al_b[...] = jnp.ones(al_b.shape, al_b.dtype)

        def pair(c, kind0, kind1, last):
            qk(c + 1, s_b)
            _value_stage(lambda p: pv(jnp.maximum(c - 1, 0), p), al_b, p_b, acc_ref)
            _softmax_stage(logits(c, s_a[...], kind0), m_ref, al_a, p_a)
            if not last:
                qk(c + 2, s_a)
            _value_stage(lambda p: pv(c, p), al_a, p_a, acc_ref)
            _softmax_stage(logits(c + 1, s_b[...], kind1), m_ref, al_b, p_b)

        def body(i, carry):
            pair(first + 2 * i, 'far', 'far', False)
            return carry

        lax.fori_loop(0, (n_far - first) // 2, body, 0)
        pair(qt - 1, 'prev', 'diag', True)
        _value_stage(lambda p: pv(qt, p), al_b, p_b, acc_ref)


def _pipeline_buffers(width):
    return ([pltpu.VMEM((ATT_TILE, width), F32)] * 2 + [pltpu.VMEM((ATT_TILE, width), BF16)] * 2
            + [pltpu.VMEM((SUBLANES, width), F32)] * 2)


def _finish_heads_t(acc_ref, o_ref, n_heads, odd_swapped):
    hd = HEAD_DIM
    tq = o_ref.shape[0]
    for pr in range(n_heads // 2):
        ae = acc_ref[:, 2 * pr * tq:(2 * pr + 1) * tq]
        ao = acc_ref[:, (2 * pr + 1) * tq:(2 * pr + 2) * tq]
        oe = ae[0:hd] / ae[hd:2 * hd]
        oo = ao[hd:2 * hd] / ao[0:hd] if odd_swapped else ao[0:hd] / ao[hd:2 * hd]
        pair = jnp.concatenate([oe, oo], axis=0)
        o_ref[:, pr * LANES:(pr + 1) * LANES] = pair.T.astype(o_ref.dtype)


def _heads_side_by_side(qt_ref, qm_ref, n_heads):
    tq = qt_ref.shape[1]
    for h in range(n_heads):
        qm_ref[:, h * tq:(h + 1) * tq] = qt_ref[h * HEAD_DIM:(h + 1) * HEAD_DIM, :]


def _dsa_kernel(q_ref, qi_ref, wi_ref, kv_ref, ki_ref, band_ref, o_ref,
                score_ref, vt_ref, qm_ref, m_ref, acc_ref, thr_ref, *bufs, n_keys, seq):
    tq = tk = ATT_TILE
    qt = pl.program_id(1)
    n_chunks = qt + 1
    kf = float(n_keys)

    @pl.when(qt == 0)
    def _():
        top = lax.broadcasted_iota(jnp.int32, (LANES, tk), 0) < HEAD_DIM
        for j in range(seq // tk):
            vv = kv_ref[j * tk:(j + 1) * tk, LANES:2 * LANES].astype(F32)
            vt_ref[j] = jnp.where(top, vv.T, 1.0).astype(BF16)

    _heads_side_by_side(qi_ref, qm_ref, IDX_HEADS)
    head_w = jnp.concatenate(
        [wi_ref[h:h + 1, :] for h in range(IDX_HEADS)], axis=1
    ) * (IDX_HEADS ** -0.5 * IDX_DIM ** -0.5)
    key_minus_query = (lax.broadcasted_iota(jnp.int32, (tk, tq), 0)
                       - lax.broadcasted_iota(jnp.int32, (tk, tq), 1))

    def score_body(j, carry):
        mx, mn, n_ge0, n_gt0 = carry
        kc = ki_ref[pl.ds(pl.multiple_of(j * tk, tk), tk), 0:IDX_DIM]
        d = _dot(kc, qm_ref[:, 0:IDX_HEADS * tq])
        d = jnp.broadcast_to(head_w, d.shape) * jnp.maximum(d, 0.0)
        sc = (d[:, 0:tq] + d[:, tq:2 * tq]) + (d[:, 2 * tq:3 * tq] + d[:, 3 * tq:4 * tq])
        causal = key_minus_query <= (qt - j) * tq
        masked = jnp.where(causal, sc, NEG)
        score_ref[j] = masked
        mx = jnp.maximum(mx, _fold_rows(masked, jnp.maximum))
        mn = jnp.minimum(mn, _fold_rows(jnp.where(causal, sc, -NEG), jnp.minimum))
        n_ge0 = n_ge0 + _fold_rows(jnp.where(masked >= 0.0, 1.0, 0.0), jnp.add)
        n_gt0 = n_gt0 + _fold_rows(jnp.where(masked > 0.0, 1.0, 0.0), jnp.add)
        return mx, mn, n_ge0, n_gt0

    zeros8 = jnp.zeros((SUBLANES, tq), F32)
    mx, mn, n_ge0, n_gt0 = lax.fori_loop(
        0, n_chunks, score_body,
        (jnp.full((SUBLANES, tq), NEG, F32), jnp.full((SUBLANES, tq), -NEG, F32), zeros8, zeros8))

    t_q = qt * tq + lax.broadcasted_iota(jnp.int32, (1, tq), 1)
    thr_ref[...] = jnp.full((SUBLANES, tq), 0.5 * NEG, F32)

    def count(pred):
        def body(j, acc):
            return acc + _fold_rows(pred(score_ref[j]), jnp.add)
        acc = lax.fori_loop(0, n_chunks, body, jnp.zeros((SUBLANES, tq), F32))
        return jnp.sum(acc, axis=0, keepdims=True)

    def count_ge(x):
        xb = jnp.broadcast_to(x, (tk, tq))
        return count(lambda s: jnp.where(s >= xb, 1.0, 0.0))

    @pl.when((qt + 1) * tq > n_keys)
    def _search():
        need = (t_q >= n_keys).astype(F32)
        lo0 = jnp.min(mn, axis=0, keepdims=True)
        hi0 = jnp.max(mx, axis=0, keepdims=True)
        hi_top = hi0 + jnp.maximum(jnp.abs(hi0) * 2.0 ** -22, 1e-30)
        c_ge0 = jnp.sum(n_ge0, axis=0, keepdims=True)
        c_gt0 = jnp.sum(n_gt0, axis=0, keepdims=True)
        n_valid = (t_q + 1).astype(F32)
        zero_hit = c_ge0 == kf
        zero_tie = (c_gt0 < kf) & (c_ge0 > kf)
        positive = c_gt0 >= kf
        hi_neg = hi0 < 0.0
        lo1 = jnp.where(positive, jnp.maximum(lo0, 0.0), lo0)
        c_lo1 = jnp.where(positive, c_ge0, n_valid)
        hi1 = jnp.where(positive | hi_neg, hi_top, 0.0)
        c_hi1 = jnp.where(positive | hi_neg, 0.0, c_ge0)
        hit0 = need * zero_hit.astype(F32)
        tie0 = need * zero_tie.astype(F32)
        done0 = jnp.maximum(1.0 - need, jnp.maximum(hit0, tie0))
        lo1 = jnp.where(tie0 > 0, 0.0, lo1)
        c_hi1 = jnp.where(tie0 > 0, c_gt0, c_hi1)
        x0 = jnp.zeros((1, tq), F32)
        state0 = (lo1, hi1, c_lo1, c_hi1, done0, tie0, x0, jnp.max(1.0 - done0))

        def cond(st):
            return st[7] > 0.0

        def body(st):
            for _ in range(SEARCH_UNROLL):
                st = step(st)
            return st

        def step(st):
            lo, hi, c_lo, c_hi, done, tie, x_fin, _ = st
            x = 0.5 * lo + 0.5 * hi
            inside = ((x > lo) & (x < hi)).astype(F32)
            active = 1.0 - done
            probe = active * inside
            c = count_ge(jnp.where(probe > 0, x, x_fin))
            hit = probe * (c == kf).astype(F32)
            more = probe * (c > kf).astype(F32)
            less = probe * (c < kf).astype(F32)
            new_tie = active * (1.0 - inside)
            lo = jnp.where(more > 0, x, lo)
            c_lo = jnp.where(more > 0, c, c_lo)
            hi = jnp.where(less > 0, x, hi)
            c_hi = jnp.where(less > 0, c, c_hi)
            x_fin = jnp.where(hit > 0, x, x_fin)
            tie = jnp.maximum(tie, new_tie)
            done = jnp.maximum(done, jnp.maximum(hit, new_tie))
            return (lo, hi, c_lo, c_hi, done, tie, x_fin, jnp.max(1.0 - done))

        st = state0
        for _ in range(SEARCH_FIXED):
            st = step(st)
        st = lax.while_loop(cond, body, st)
        lo, _, _, c_hi, _, tie, x_fin, _ = st
        thr = jnp.where(need > 0, jnp.where(tie > 0, lo, x_fin), 0.5 * NEG)
        thr_ref[...] = jnp.broadcast_to(thr, (SUBLANES, tq))

        @pl.when(jnp.max(tie) > 0.0)
        def _ties():
            want = jnp.broadcast_to(kf - c_hi, (tk, tq))
            thr_t = jnp.broadcast_to(jnp.where(tie > 0, thr, -NEG), (tk, tq))
            upto = (lax.broadcasted_iota(jnp.int32, (tk, tk), 1)
                    <= lax.broadcasted_iota(jnp.int32, (tk, tk), 0))
            prefix = jnp.where(upto, 1.0, 0.0).astype(BF16)

            def body(j, seen):
                s = score_ref[j]
                eq = s == thr_t
                rank = _dot(prefix, jnp.where(eq, 1.0, 0.0).astype(BF16))
                rank = rank + jnp.broadcast_to(seen, (tk, tq))
                score_ref[j] = jnp.where(eq, jnp.where(rank > want, NEG, s), s)
                return rank[tk - 1:tk, :]

            lax.fori_loop(0, n_chunks, body, jnp.zeros((1, tq), F32))

    _heads_side_by_side(q_ref, qm_ref, DSA_HEADS)
    m_ref[...] = jnp.full(m_ref.shape, NEG, F32)
    acc_ref[...] = jnp.zeros(acc_ref.shape, F32)
    thr_b = jnp.broadcast_to(thr_ref[0:1, :], (tk, tq))

    def qk(j, s_ref):
        rows = pl.ds(pl.multiple_of(j * tk, tk), tk)
        s_ref[...] = _dot(kv_ref[rows, 0:HEAD_DIM], qm_ref[...])

    def logits(j, s, kind):
        off = jnp.where(score_ref[j] >= thr_b, 0.0, NEG)
        s = s + jnp.concatenate([off] * DSA_HEADS, axis=1)
        if kind == 'prev':
            s = s + band_ref[0:tk, :]
        elif kind == 'diag':
            s = s + band_ref[tk:2 * tk, :]
        return s

    _attend_chunks(qt, qk, logits, lambda j, p: _dot(vt_ref[j], p), bufs, m_ref, acc_ref)
    _finish_heads_t(acc_ref, o_ref, DSA_HEADS, odd_swapped=False)


def _dsa(qb_t, qi_t, wi_t, kv, ki, band, bsz, seq, n_keys):
    tq = ATT_TILE
    nq = seq // tq
    qrow = lambda w: pl.BlockSpec((tq, w), lambda b, q: (b * nq + q, 0))
    qtile = lambda w: pl.BlockSpec((None, w, tq), lambda b, q: (b * nq + q, 0, 0))
    seq_blk = lambda w: pl.BlockSpec((seq, w), lambda b, q: (b, 0))
    return pl.pallas_call(
        functools.partial(_dsa_kernel, n_keys=n_keys, seq=seq), grid=(bsz, nq),
        in_specs=[qtile(DSA_WIDTH), qtile(IDX_HEADS * IDX_DIM), qtile(SUBLANES), seq_blk(2 * LANES),
                  seq_blk(LANES), _resident((2 * tq, DSA_HEADS * tq))],
        out_specs=qrow(DSA_WIDTH),
        out_shape=jax.ShapeDtypeStruct((bsz * seq, DSA_WIDTH), BF16),
        scratch_shapes=[pltpu.VMEM((nq, tq, tq), F32),
                        pltpu.VMEM((nq, LANES, tq), BF16),
                        pltpu.VMEM((HEAD_DIM, DSA_HEADS * tq), BF16),
                        pltpu.VMEM((SUBLANES, DSA_HEADS * tq), F32),
                        pltpu.VMEM((LANES, DSA_HEADS * tq), F32),
                        pltpu.VMEM((SUBLANES, tq), F32)] + _pipeline_buffers(DSA_HEADS * tq),
        compiler_params=_cparams(("parallel", "arbitrary")), name="dsa",
    )(qb_t, qi_t, wi_t, kv, ki, band)


def _band_kernel(bucket_ref, rb_ref, o_ref):
    bucket = bucket_ref[...]
    tq = bucket.shape[1]
    for h in range(DSA_HEADS):
        acc = jnp.zeros(bucket.shape, F32)
        for k in range(REL_BUCKETS - 1):
            acc = jnp.where(bucket == k, (rb_ref[k, h] - rb_ref[REL_BUCKETS - 1, h]) * LOG2E, acc)
        o_ref[:, h * tq:(h + 1) * tq] = acc


def _t5_bucket_np(dist):
    d = np.maximum(dist, 0)
    df = np.maximum(d, 1).astype(np.float32)
    log_b = REL_MAX_EXACT + (np.log(df / REL_MAX_EXACT) / math.log(REL_MAX_DIST / REL_MAX_EXACT)
                             * (REL_BUCKETS - REL_MAX_EXACT)).astype(np.int32)
    log_b = np.minimum(log_b, REL_BUCKETS - 1)
    return np.where(d < REL_MAX_EXACT, d, log_b).astype(np.int32)


def _rel_band(rel_bias):
    tq = ATT_TILE
    i = np.arange(tq)[None, :]
    j = np.arange(2 * tq)[:, None]
    bucket = jnp.asarray(_t5_bucket_np(i + tq - j))
    return pl.pallas_call(
        _band_kernel,
        in_specs=[pl.BlockSpec(memory_space=pltpu.VMEM), pl.BlockSpec(memory_space=pltpu.SMEM)],
        out_specs=pl.BlockSpec(memory_space=pltpu.VMEM),
        out_shape=jax.ShapeDtypeStruct((2 * tq, DSA_HEADS * tq), F32), name="rel_band",
    )(bucket, rel_bias)


def _cum_kernel(aux_ref, bf_ref, col_ref, row_ref, *, seq):
    x = aux_ref[...] + bf_ref[...]
    logf = -(jnp.maximum(-x, 0.0) + jnp.log1p(jnp.exp(-jnp.abs(x))))
    c = logf.T[0:SUBLANES]
    pos = lax.broadcasted_iota(jnp.int32, c.shape, 1)
    shift = 1
    while shift < seq:
        c = c + jnp.where(pos >= shift, pltpu.roll(c, shift, 1), 0.0)
        shift *= 2
    c = c * LOG2E
    col_ref[...] = jnp.concatenate([c, jnp.zeros((LANES - SUBLANES, seq), F32)], axis=0).T
    for j in range(seq // ATT_TILE):
        row_ref[j] = c[:, j * ATT_TILE:(j + 1) * ATT_TILE]


def _fox_cum(aux, b_f, bsz, seq):
    nq = seq // ATT_TILE
    bf = jnp.zeros((1, LANES), F32).at[0, AUX_F:AUX_F + FOX_HEADS].set(b_f)
    return pl.pallas_call(
        functools.partial(_cum_kernel, seq=seq), grid=(bsz,),
        in_specs=[pl.BlockSpec((seq, LANES), lambda b: (b, 0)), _resident((1, LANES))],
        out_specs=[pl.BlockSpec((seq, LANES), lambda b: (b, 0)),
                   pl.BlockSpec((None, nq, SUBLANES, ATT_TILE), lambda b: (b, 0, 0, 0))],
        out_shape=[jax.ShapeDtypeStruct((bsz * seq, LANES), F32),
                   jax.ShapeDtypeStruct((bsz, nq, SUBLANES, ATT_TILE), F32)],
        compiler_params=_cparams(("parallel",)), name="fox_cum")(aux, bf)


def _bf16_pieces(x):
    hi = x.astype(BF16).astype(F32)
    rest = x - hi
    mid = rest.astype(BF16).astype(F32)
    return [hi, mid, (rest - mid).astype(BF16).astype(F32)]


def _fox_kernel(q_ref, k_ref, v_ref, ccol_ref, crow_ref, o_ref,
                vte_ref, vto_ref, ka_ref, qm_ref, m_ref, acc_ref, *bufs, seq):
    tq = tk = ATT_TILE
    qt = pl.program_id(1)
    heads = range(FOX_HEADS)

    @pl.when(qt == 0)
    def _():
        top = lax.broadcasted_iota(jnp.int32, (LANES, tk), 0) < HEAD_DIM
        for j in range(seq // tk):
            rows = slice(j * tk, (j + 1) * tk)
            for pr in range(FOX_HEADS // 2):
                v_t = v_ref[rows, pr * LANES:(pr + 1) * LANES].astype(F32).T
                vte_ref[pr, j] = jnp.where(top, v_t, 1.0).astype(BF16)
                vto_ref[pr, j] = jnp.where(top, 1.0, v_t).astype(BF16)
            cc = ccol_ref[rows, :]
            lane = lax.broadcasted_iota(jnp.int32, (tk, LANES), 1)
            for h in heads:
                k_h = k_ref[rows, (h // 2) * LANES:(h // 2 + 1) * LANES].astype(F32)
                if h % 2 == 1:
                    k_h = pltpu.roll(k_h, HEAD_DIM, 1)
                pieces = _bf16_pieces(jnp.broadcast_to(cc[:, AUX_F + h:AUX_F + h + 1], (tk, LANES)))
                ext = jnp.where(lane < HEAD_DIM + 2 * len(pieces), 1.0, 0.0)
                for i, piece in enumerate(pieces):
                    ext = jnp.where(lane == HEAD_DIM + i, piece, ext)
                ka_ref[h, rows, :] = jnp.where(lane < HEAD_DIM, k_h, ext).astype(BF16)

    c_q = crow_ref[qt]
    row = lax.broadcasted_iota(jnp.int32, (LANES - HEAD_DIM, tq), 0)
    for h in heads:
        pieces = _bf16_pieces(c_q[AUX_F + h:AUX_F + h + 1, :])
        ext = jnp.where(row < len(pieces), -1.0, 0.0)
        for i, piece in enumerate(pieces):
            ext = jnp.where(row == len(pieces) + i, jnp.broadcast_to(piece, ext.shape), ext)
        q_h = q_ref[h * HEAD_DIM:(h + 1) * HEAD_DIM, :]
        qm_ref[:, h * tq:(h + 1) * tq] = jnp.concatenate([q_h, ext.astype(BF16)], axis=0)
    m_ref[...] = jnp.full(m_ref.shape, NEG, F32)
    acc_ref[...] = jnp.zeros(acc_ref.shape, F32)
    causal_off = jnp.where(lax.broadcasted_iota(jnp.int32, (tk, tq), 0)
                           <= lax.broadcasted_iota(jnp.int32, (tk, tq), 1), 0.0, NEG)

    def qk(j, s_ref):
        rows = pl.ds(pl.multiple_of(j * tk, tk), tk)
        for h in heads:
            cols = slice(h * tq, (h + 1) * tq)
            s_ref[:, cols] = _dot(ka_ref[h, rows, :], qm_ref[:, cols])

    def logits(j, s, kind):
        if kind == 'diag':
            s = s + jnp.concatenate([causal_off] * FOX_HEADS, axis=1)
        return s

    def pv(j, p):
        return jnp.concatenate(
            [_dot(vte_ref[h // 2, j] if h % 2 == 0 else vto_ref[h // 2, j],
                  p[:, h * tq:(h + 1) * tq]) for h in heads], axis=1)

    _attend_chunks(qt, qk, logits, pv, bufs, m_ref, acc_ref)
    _finish_heads_t(acc_ref, o_ref, FOX_HEADS, odd_swapped=True)


def _fox(q_t, kv, c_col, c_row, bsz, seq):
    tq = ATT_TILE
    nq = seq // tq
    npair = FOX_HEADS // 2
    return pl.pallas_call(
        functools.partial(_fox_kernel, seq=seq), grid=(bsz, nq),
        in_specs=[pl.BlockSpec((None, FOX_WIDTH, tq), lambda b, q: (b * nq + q, 0, 0)),
                  pl.BlockSpec((seq, FOX_WIDTH), lambda b, q: (b, 0)),
                  pl.BlockSpec((seq, FOX_WIDTH), lambda b, q: (b, 1)),
                  pl.BlockSpec((seq, LANES), lambda b, q: (b, 0)),
                  pl.BlockSpec((None, nq, SUBLANES, tq), lambda b, q: (b, 0, 0, 0))],
        out_specs=pl.BlockSpec((tq, FOX_WIDTH), lambda b, q: (b * nq + q, 0)),
        out_shape=jax.ShapeDtypeStruct((bsz * seq, FOX_WIDTH), BF16),
        scratch_shapes=[pltpu.VMEM((npair, nq, LANES, tq), BF16),
                        pltpu.VMEM((npair, nq, LANES, tq), BF16),
                        pltpu.VMEM((FOX_HEADS, seq, LANES), BF16),
                        pltpu.VMEM((LANES, FOX_HEADS * tq), BF16),
                        pltpu.VMEM((SUBLANES, FOX_HEADS * tq), F32),
                        pltpu.VMEM((LANES, FOX_HEADS * tq), F32)] + _pipeline_buffers(FOX_HEADS * tq),
        compiler_params=_cparams(("parallel", "arbitrary")), name="fox",
    )(q_t, kv, kv, c_col, c_row)


def _gelu(y):
    return 0.5 * y * (1.0 + jnp.tanh(math.sqrt(2.0 / math.pi) * (y + 0.044715 * (y * y * y))))


def _merge_kernel(x_ref, yc_ref, u_ref, d_ref, wglu_ref, b_ref, c_ref, gate_ref,
                  wa_ref, wb_ref, wc_ref, wo_ref, o_ref, tmp_ref):
    d = D_MODEL
    y = _chunks_to_rows(yc_ref, tmp_ref, SSM_WIDTH) + d_ref[...] * u_ref[...]
    z = _dot(_gelu(y).astype(BF16), wglu_ref[...])
    a = z[:, :SSM_WIDTH] * _sigmoid(z[:, SSM_WIDTH:])
    merged = _sigmoid(gate_ref[:, 0:d].astype(F32)) * _dot(a.astype(BF16), wa_ref[...])
    merged += _sigmoid(gate_ref[:, d:2 * d].astype(F32)) * _dot(b_ref[...], wb_ref[...])
    merged += _sigmoid(gate_ref[:, 2 * d:3 * d].astype(F32)) * _dot(c_ref[...], wc_ref[...])
    o_ref[...] = x_ref[...] + _dot(merged.astype(BF16), wo_ref[...])


def _merge(xt, yc, u, d_skip, w_glu, b, c, gates, wa, wb, wc, wo, tm=1024):
    m, d = xt.shape
    row = lambda w: pl.BlockSpec((tm, w), lambda i: (i, 0))
    return pl.pallas_call(
        _merge_kernel, grid=(m // tm,),
        in_specs=[row(d), pl.BlockSpec((tm // S5_CHUNK, S5_COLS), lambda i: (i, 0)),
                  row(SSM_WIDTH), _resident((1, SSM_WIDTH)),
                  _resident((SSM_WIDTH, 2 * SSM_WIDTH)), row(DSA_WIDTH), row(FOX_WIDTH), row(3 * d),
                  _resident((SSM_WIDTH, d)), _resident((DSA_WIDTH, d)), _resident((FOX_WIDTH, d)),
                  _resident((d, d))],
        out_specs=row(d), out_shape=jax.ShapeDtypeStruct((m, d), F32),
        scratch_shapes=[pltpu.VMEM((SSM_WIDTH // LANES, tm, LANES), F32)],
        compiler_params=_cparams(("parallel",)), name="merge",
    )(xt, yc, u, d_skip.reshape(1, -1), w_glu, b, c, gates, wa, wb, wc, wo)


def _pack_w_in(w_in):
    d = w_in.shape[0]
    splits = (SSM_WIDTH, DSA_WIDTH, DSA_LATENT, IDX_HEADS * IDX_DIM, IDX_DIM, IDX_HEADS,
              FOX_WIDTH, FOX_WIDTH, FOX_WIDTH, FOX_HEADS, D_MODEL, D_MODEL, D_MODEL)
    pts = np.cumsum(splits)[:-1]
    (w_u, w_qb, w_ckv, w_qi, w_ki, w_wi, w_qc, w_kc, w_vc, w_fc, w_ga, w_gb, w_gc) = jnp.split(
        w_in, pts, axis=1)
    scale = HEAD_DIM ** -0.5 * LOG2E
    aux = jnp.zeros((d, LANES), F32)
    aux = aux.at[:, AUX_F:AUX_F + FOX_HEADS].set(w_fc).at[:, AUX_W:AUX_W + IDX_HEADS].set(w_wi)
    cols = [w_u, w_qb * scale, w_qi, w_ki, w_ki, w_ckv, aux, w_qc * scale, w_kc, w_vc,
            w_ga, w_gb, w_gc]
    return jnp.concatenate(cols, axis=1).astype(BF16)


def kernel(x, ffn1_norm, ffn1_w_gate, ffn1_w_up, ffn1_w_down, mix_norm, w_in, ssm_lambda_re, ssm_lambda_im, ssm_log_dt, ssm_b_re, ssm_b_im, ssm_c_re, ssm_c_im, ssm_d, ssm_w_glu, dsa_kv_norm, dsa_w_uk, dsa_w_uv, rel_bias, fox_b_f, w_branch_ssm, w_branch_dsa, w_branch_fox, w_out, ffn2_norm, ffn2_w_gate, ffn2_w_up, ffn2_w_down, final_norm):
    bsz, seq, d = x.shape
    depth = w_in.shape[0]
    n_keys = min(TOPK_MAX, seq // 4)
    assert d == D_MODEL and seq % ATT_TILE == 0 and (bsz * seq) % (S5_CHUNK * 512) == 0
    assert n_keys % LANES == 0 and bsz % SUBLANES == 0
    m = bsz * seq
    n_chunks = seq // S5_CHUNK
    xt = x.reshape(m, d)
    band = _rel_band(rel_bias)
    bf = lambda w: w.astype(BF16)
    for l in range(depth):
        xt = _ffn(xt, ffn1_norm[l], bf(ffn1_w_gate[l]), bf(ffn1_w_up[l]), bf(ffn1_w_down[l]))
        w_kv = bf(jnp.concatenate([dsa_w_uk[l], dsa_w_uk[l], dsa_w_uv[l], dsa_w_uv[l]], axis=1))
        u, qb_t, qi_t, ki, kv, aux, wi_t, qc_t, kv_fox, gates, u8 = _inproj(
            xt, mix_norm[l], _pack_w_in(w_in[l]), dsa_kv_norm[l], w_kv)
        m_intra, m_sum, m_out, a_re, a_im = _s5_matrices(
            ssm_lambda_re[l], ssm_lambda_im[l], ssm_log_dt[l], ssm_b_re[l], ssm_b_im[l],
            ssm_c_re[l], ssm_c_im[l])
        y_intra, p = _s5a(u8, m_intra, m_sum)
        h_re, h_im = _s5b(p, a_re, a_im, n_chunks, nb=SUBLANES)
        yc = _s5c(y_intra, h_re, h_im, m_out)
        b_out = _dsa(qb_t, qi_t, wi_t, kv, ki, band, bsz, seq, n_keys)
        c_col, c_row = _fox_cum(aux, fox_b_f[l], bsz, seq)
        c_out = _fox(qc_t, kv_fox, c_col, c_row, bsz, seq)
        xt = _merge(xt, yc, u, ssm_d[l], bf(ssm_w_glu[l]), b_out, c_out, gates,
                    bf(w_branch_ssm[l]), bf(w_branch_dsa[l]), bf(w_branch_fox[l]), bf(w_out[l]))
        last = final_norm if l == depth - 1 else None
        xt = _ffn(xt, ffn2_norm[l], bf(ffn2_w_gate[l]), bf(ffn2_w_up[l]), bf(ffn2_w_down[l]),
                  final_norm=last)
    return xt.reshape(bsz, seq, d)
```

```python
import functools
import math

import numpy as np
import jax
import jax.numpy as jnp
from jax import lax
from jax.experimental import pallas as pl
from jax.experimental.pallas import tpu as pltpu

F32 = jnp.float32
BF16 = jnp.bfloat16

D_MODEL = 1024
SSM_WIDTH = D_MODEL // 4
SSM_GROUP = 16
SSM_GROUPS = SSM_WIDTH // SSM_GROUP
SSM_STATE = 64
HEAD_DIM = 64
DSA_HEADS = 6
DSA_WIDTH = DSA_HEADS * HEAD_DIM
DSA_LATENT = 2 * HEAD_DIM
IDX_HEADS = 4
IDX_DIM = 64
TOPK_MAX = 256
FOX_HEADS = 6
FOX_WIDTH = FOX_HEADS * HEAD_DIM
REL_BUCKETS = 32
REL_MAX_EXACT = 16
REL_MAX_DIST = 128
FFN_HIDDEN = 2816
EPS = 1e-6
NEG = -1e30
LOG2E = math.log2(math.e)

LANES = 128
SUBLANES = 8
S5_CHUNK = 8
S5_COLS = S5_CHUNK * SSM_WIDTH
S5_STATE = 2 * SSM_GROUPS * SSM_STATE
ATT_TILE = 256
SEARCH_FIXED = 16
SEARCH_UNROLL = 2
VMEM_LIMIT = 56 * 1024 * 1024

_SEG = dict(u=SSM_WIDTH, qb=DSA_WIDTH, qi=IDX_HEADS * IDX_DIM, ki=LANES, ckv=DSA_LATENT, aux=LANES,
            qkv=3 * FOX_WIDTH, g=3 * D_MODEL)
AUX_F = 0
AUX_W = 8


def _cparams(sem):
    return pltpu.CompilerParams(dimension_semantics=sem, vmem_limit_bytes=VMEM_LIMIT)


def _resident(shape):
    nd = len(shape)
    return pl.BlockSpec(shape, lambda *_: (0,) * nd, pipeline_mode=pl.Buffered(1))


def _rms(x, g):
    return x * lax.rsqrt(jnp.mean(x * x, axis=-1, keepdims=True) + EPS) * g


def _gate(x):
    return 0.5 + 0.5 * jnp.tanh(0.5 * x)


def _dot(a, b):
    return jnp.dot(a, b, preferred_element_type=F32)


def _ffn_body(x, g_ref, wg_ref, wu_ref, wd_ref, fc):
    h = _rms(x, g_ref[...]).astype(BF16)
    acc = None
    for c in range(FFN_HIDDEN // fc):
        sl = slice(c * fc, (c + 1) * fc)
        gt = _dot(h, wg_ref[:, sl])
        up = _dot(h, wu_ref[:, sl])
        a = (gt * _gate(gt) * up).astype(BF16)
        d = _dot(a, wd_ref[sl, :])
        acc = d if acc is None else acc + d
    return x + 0.5 * acc


def _ffn_kernel(x_ref, g_ref, wg_ref, wu_ref, wd_ref, o_ref, *, fc):
    o_ref[...] = _ffn_body(x_ref[...], g_ref, wg_ref, wu_ref, wd_ref, fc)


def _ffn_final_kernel(x_ref, g_ref, wg_ref, wu_ref, wd_ref, fn_ref, o_ref, *, fc):
    y = _ffn_body(x_ref[...], g_ref, wg_ref, wu_ref, wd_ref, fc)
    o_ref[...] = _rms(y, fn_ref[...])


def _ffn(xt, norm, wg, wu, wd, final_norm=None, tm=1024, fc=256):
    m, d = xt.shape
    f = wg.shape[1]
    row = pl.BlockSpec((tm, d), lambda i: (i, 0))
    vec = _resident((1, d))
    in_specs = [row, vec, _resident((d, f)), _resident((d, f)), _resident((f, d))]
    args = [xt, norm.reshape(1, d), wg, wu, wd]
    if final_norm is None:
        body = functools.partial(_ffn_kernel, fc=fc)
    else:
        body = functools.partial(_ffn_final_kernel, fc=fc)
        in_specs.append(vec)
        args.append(final_norm.reshape(1, d))
    return pl.pallas_call(
        body, grid=(m // tm,), in_specs=in_specs, out_specs=row,
        out_shape=jax.ShapeDtypeStruct((m, d), F32),
        compiler_params=_cparams(("parallel",)), name="ffn")(*args)


def _rows_to_chunks(x, tmp_ref, o_ref):
    n, w = x.shape
    for hh in range(w // LANES):
        tmp_ref[hh] = x[:, hh * LANES:(hh + 1) * LANES]
    for i in range(S5_CHUNK):
        for hh in range(w // LANES):
            o_ref[:, i * w + hh * LANES:i * w + (hh + 1) * LANES] = (
                tmp_ref[hh, pl.ds(i, n // S5_CHUNK, stride=S5_CHUNK), :])


def _chunks_to_rows(x_ref, tmp_ref, w):
    n = x_ref.shape[0]
    for i in range(S5_CHUNK):
        for hh in range(w // LANES):
            tmp_ref[hh, pl.ds(i, n, stride=S5_CHUNK), :] = (
                x_ref[:, i * w + hh * LANES:i * w + (hh + 1) * LANES])
    return jnp.concatenate([tmp_ref[hh] for hh in range(w // LANES)], axis=1)


def _emit_transposed(piece, o_ref, rows=None):
    for t in range(piece.shape[0] // ATT_TILE):
        slab_t = piece[t * ATT_TILE:(t + 1) * ATT_TILE, :].T
        o_ref[t] = (slab_t if rows is None else slab_t[rows]).astype(o_ref.dtype)


def _inproj_kernel(x_ref, g_ref, w_ref, kvn_ref, wkv_ref,
                   u_ref, qbt_ref, qit_ref, ki_ref, kv_ref, aux_ref, wit_ref, qct_ref, kvc_ref,
                   gate_ref, u8_ref, tmp_ref):
    h = _rms(x_ref[...], g_ref[...]).astype(BF16)
    outs = dict(u=u_ref, ki=ki_ref, aux=aux_ref)
    off = 0
    for group in (("u", "qb", "qi", "ki"), ("ckv", "aux", "qkv")):
        width = sum(_SEG[name] for name in group)
        r = _dot(h, w_ref[:, off:off + width])
        off += width
        lo = 0
        for name in group:
            piece = r[:, lo:lo + _SEG[name]]
            lo += _SEG[name]
            if name == "ckv":
                c = _rms(piece, kvn_ref[...]).astype(BF16)
                kv_ref[...] = _dot(c, wkv_ref[...]).astype(BF16)
            elif name == "qb":
                _emit_transposed(piece, qbt_ref)
            elif name == "qi":
                _emit_transposed(piece, qit_ref)
            elif name == "qkv":
                _emit_transposed(piece[:, :FOX_WIDTH], qct_ref)
                kvc_ref[...] = piece[:, FOX_WIDTH:].astype(BF16)
            else:
                outs[name][...] = piece.astype(outs[name].dtype)
            if name == "u":
                _rows_to_chunks(piece, tmp_ref, u8_ref)
            if name == "aux":
                _emit_transposed(piece, wit_ref, rows=slice(AUX_W, AUX_W + SUBLANES))
    for c in range(_SEG["g"] // D_MODEL):
        cols = slice(c * D_MODEL, (c + 1) * D_MODEL)
        gate_ref[:, cols] = _dot(h, w_ref[:, off + c * D_MODEL:off + (c + 1) * D_MODEL]).astype(BF16)


def _inproj(xt, norm, w_all, kv_norm, w_kv, tm=1024):
    m, d = xt.shape
    wtot = w_all.shape[1]
    tq = ATT_TILE

    def rows(w, dt):
        return jax.ShapeDtypeStruct((m, w), dt), pl.BlockSpec((tm, w), lambda i: (i, 0))

    def tiles_t(w, dt):
        return (jax.ShapeDtypeStruct((m // tq, w, tq), dt),
                pl.BlockSpec((tm // tq, w, tq), lambda i: (i, 0, 0)))

    outs = [rows(SSM_WIDTH, F32), tiles_t(DSA_WIDTH, BF16), tiles_t(IDX_HEADS * IDX_DIM, BF16),
            rows(LANES, BF16), rows(2 * LANES, BF16), rows(LANES, F32), tiles_t(SUBLANES, F32),
            tiles_t(FOX_WIDTH, BF16), rows(2 * FOX_WIDTH, BF16), rows(3 * D_MODEL, BF16),
            (jax.ShapeDtypeStruct((m // S5_CHUNK, S5_COLS), F32),
             pl.BlockSpec((tm // S5_CHUNK, S5_COLS), lambda i: (i, 0)))]
    out_shape = [o[0] for o in outs]
    out_specs = [o[1] for o in outs]
    return pl.pallas_call(
        _inproj_kernel, grid=(m // tm,),
        in_specs=[pl.BlockSpec((tm, d), lambda i: (i, 0)), _resident((1, d)),
                  _resident((d, wtot)), _resident((1, DSA_LATENT)),
                  _resident((DSA_LATENT, 2 * LANES))],
        out_specs=out_specs, out_shape=out_shape,
        scratch_shapes=[pltpu.VMEM((SSM_WIDTH // LANES, tm, LANES), F32)],
        compiler_params=_cparams(("parallel",)), name="inproj",
    )(xt, norm.reshape(1, d), w_all, kv_norm.reshape(1, DSA_LATENT), w_kv)


S5_SLABS = S5_STATE // LANES
S5_SCAN_SLABS = 4


def _s5a_kernel(u_ref, mi_ref, ms_ref, yi_ref, p_ref):
    u = u_ref[...].astype(BF16)
    yi_ref[...] = _dot(u, mi_ref[...])
    p = _dot(u, ms_ref[...])
    for k in range(S5_SLABS):
        p_ref[k] = p[:, k * LANES:(k + 1) * LANES]


def _s5a(u8, m_intra, m_sum, tr=512):
    r = u8.shape[0]
    row = pl.BlockSpec((tr, S5_COLS), lambda i: (i, 0))
    slabs = pl.BlockSpec((S5_SLABS, tr, LANES), lambda i: (0, i, 0))
    return pl.pallas_call(
        _s5a_kernel, grid=(r // tr,),
        in_specs=[row, _resident((S5_COLS, S5_COLS)), _resident((S5_COLS, S5_STATE))],
        out_specs=[row, slabs],
        out_shape=[jax.ShapeDtypeStruct((r, S5_COLS), F32),
                   jax.ShapeDtypeStruct((S5_SLABS, r, LANES), F32)],
        compiler_params=_cparams(("parallel",)), name="s5_chunk")(u8, m_intra, m_sum)


def _s5b_kernel(pr_ref, pi_ref, ar_ref, ai_ref, hr_ref, hi_ref, *, nb, nc):
    ar = ar_ref[...]
    ai = ai_ref[...]
    ns = S5_SCAN_SLABS

    def step(c, carry):
        hr, hi = carry
        rows = pl.ds(c, nb, stride=nc)
        for k in range(ns):
            hr_ref[k, rows, :] = hr[:, k * LANES:(k + 1) * LANES]
            hi_ref[k, rows, :] = hi[:, k * LANES:(k + 1) * LANES]
        pr = jnp.concatenate([pr_ref[k, rows, :] for k in range(ns)], axis=1)
        pi = jnp.concatenate([pi_ref[k, rows, :] for k in range(ns)], axis=1)
        return (ar * hr - ai * hi + pr, ar * hi + ai * hr + pi)

    z = jnp.zeros((nb, ns * LANES), F32)
    lax.fori_loop(0, nc, step, (z, z))


def _s5b(p, a_re, a_im, nc, nb):
    r = p.shape[1]
    ns = S5_SCAN_SLABS
    half = S5_SLABS // 2
    re_blk = pl.BlockSpec((ns, nb * nc, LANES), lambda i, j: (j, i, 0))
    im_blk = pl.BlockSpec((ns, nb * nc, LANES), lambda i, j: (j + half // ns, i, 0))
    a_blk = pl.BlockSpec((1, ns * LANES), lambda i, j: (0, j))
    return pl.pallas_call(
        functools.partial(_s5b_kernel, nb=nb, nc=nc), grid=(r // (nb * nc), half // ns),
        in_specs=[re_blk, im_blk, a_blk, a_blk],
        out_specs=[re_blk, re_blk],
        out_shape=[jax.ShapeDtypeStruct((half, r, LANES), F32)] * 2,
        compiler_params=_cparams(("parallel", "parallel")), name="s5_scan",
    )(p, p, a_re, a_im)


def _s5c_kernel(yi_ref, hr_ref, hi_ref, mo_ref, y_ref):
    half = S5_STATE // 2
    hr = jnp.concatenate([hr_ref[k] for k in range(S5_SLABS // 2)], axis=1)
    hi = jnp.concatenate([hi_ref[k] for k in range(S5_SLABS // 2)], axis=1)
    y = yi_ref[...]
    y = y + _dot(hr.astype(BF16), mo_ref[:half, :])
    y = y + _dot(hi.astype(BF16), mo_ref[half:, :])
    y_ref[...] = y


def _s5c(yi, h_re, h_im, m_out, tr=512):
    r = yi.shape[0]
    row = pl.BlockSpec((tr, S5_COLS), lambda i: (i, 0))
    slabs = pl.BlockSpec((S5_SLABS // 2, tr, LANES), lambda i: (0, i, 0))
    return pl.pallas_call(
        _s5c_kernel, grid=(r // tr,),
        in_specs=[row, slabs, slabs, _resident((S5_STATE, S5_COLS))],
        out_specs=row, out_shape=jax.ShapeDtypeStruct((r, S5_COLS), F32),
        compiler_params=_cparams(("parallel",)), name="s5_out")(yi, h_re, h_im, m_out)


def _toeplitz_kernel(lag_ref, o_ref):
    j = pl.program_id(0)
    t, w, _ = lag_ref.shape
    for i in range(t):
        blk = lag_ref[jnp.maximum(i - j, 0)]
        o_ref[:, i * w:(i + 1) * w] = jnp.where(i >= j, blk, jnp.zeros_like(blk))


def _block_toeplitz(lag_blocks):
    t, w, _ = lag_blocks.shape
    return pl.pallas_call(
        _toeplitz_kernel, grid=(t,),
        in_specs=[_resident((t, w, w))],
        out_specs=pl.BlockSpec((w, t * w), lambda j: (j, 0)),
        out_shape=jax.ShapeDtypeStruct((t * w, t * w), lag_blocks.dtype), name="s5_toeplitz",
    )(lag_blocks)


def _s5_matrices(lam_re, lam_im, log_dt, b_re, b_im, c_re, c_im):
    hp = lax.Precision.HIGHEST
    t = S5_CHUNK
    dt = jnp.exp(log_dt)[:, None]
    lr, li = lam_re, lam_im
    mag = jnp.exp(lr * dt)
    ab_re, ab_im = mag * jnp.cos(li * dt), mag * jnp.sin(li * dt)
    den = lr * lr + li * li
    nr, ni = ab_re - 1.0, ab_im
    s_re = (nr * lr + ni * li) / den
    s_im = (ni * lr - nr * li) / den
    bb_re = s_re[..., None] * b_re - s_im[..., None] * b_im
    bb_im = s_re[..., None] * b_im + s_im[..., None] * b_re
    pr, pi = [jnp.ones_like(ab_re)], [jnp.zeros_like(ab_re)]
    for _ in range(t):
        pr.append(pr[-1] * ab_re - pi[-1] * ab_im)
        pi.append(pr[-2] * ab_im + pi[-1] * ab_re)
    pw_re, pw_im = jnp.stack(pr), jnp.stack(pi)
    lo_re, lo_im = jnp.stack(pr[:t]), jnp.stack(pi[:t])
    ab_b_re = lo_re[..., None] * bb_re - lo_im[..., None] * bb_im
    ab_b_im = lo_re[..., None] * bb_im + lo_im[..., None] * bb_re
    kern = (jnp.einsum('gpn,dgnq->dgpq', c_re, ab_b_re, precision=hp)
            - jnp.einsum('gpn,dgnq->dgpq', c_im, ab_b_im, precision=hp))
    eye_g = jnp.eye(SSM_GROUPS, dtype=F32)
    rows = np.arange(SSM_WIDTH)
    expand = jnp.asarray((rows[None, :] % SSM_GROUP == np.arange(SSM_GROUP)[:, None]).astype(np.float32))
    same_group = jnp.asarray(rows[:, None] // SSM_GROUP == rows[None, :] // SSM_GROUP)
    lag_rows = kern.transpose(0, 1, 3, 2).reshape(t, SSM_WIDTH, SSM_GROUP)
    lag_blocks = jnp.where(same_group, jnp.einsum('drp,pc->drc', lag_rows, expand, precision=hp), 0.0)
    m_intra = _block_toeplitz(lag_blocks.astype(BF16))
    hi_re, hi_im = jnp.stack(pr[t - 1::-1]), jnp.stack(pi[t - 1::-1])
    sb_re = hi_re[..., None] * bb_re - hi_im[..., None] * bb_im
    sb_im = hi_re[..., None] * bb_im + hi_im[..., None] * bb_re
    m_sum_re = jnp.einsum('jgnq,gh->jgqhn', sb_re, eye_g, precision=hp)
    m_sum_im = jnp.einsum('jgnq,gh->jgqhn', sb_im, eye_g, precision=hp)
    m_sum = jnp.concatenate([m_sum_re.reshape(S5_COLS, -1), m_sum_im.reshape(S5_COLS, -1)], axis=1)
    up_re, up_im = jnp.stack(pr[1:])[:, :, None, :], jnp.stack(pi[1:])[:, :, None, :]
    w_re = c_re[None] * up_re - c_im[None] * up_im
    w_im = c_re[None] * up_im + c_im[None] * up_re
    m_out_re = jnp.einsum('igpn,gh->gnihp', w_re, eye_g, precision=hp)
    m_out_im = jnp.einsum('igpn,gh->gnihp', -w_im, eye_g, precision=hp)
    m_out = jnp.concatenate([m_out_re.reshape(-1, S5_COLS), m_out_im.reshape(-1, S5_COLS)], axis=0)
    a_re = pw_re[t].reshape(1, -1)
    a_im = pw_im[t].reshape(1, -1)
    return m_intra.astype(BF16), m_sum.astype(BF16), m_out.astype(BF16), a_re, a_im


def _fold_rows(x, op):
    parts = [x[r:r + SUBLANES] for r in range(0, x.shape[0], SUBLANES)]
    while len(parts) > 1:
        parts = [op(parts[i], parts[i + 1]) for i in range(0, len(parts), 2)]
    return parts[0]


def _softmax_stage(s, m_ref, alpha_ref, p_ref):
    m_old = m_ref[...]
    m_new = jnp.maximum(m_old, jnp.max(_fold_rows(s, jnp.maximum), axis=0, keepdims=True))
    alpha_ref[...] = jnp.exp2(m_old - m_new)
    p_ref[...] = jnp.exp2(s - jnp.broadcast_to(m_new[0:1], s.shape)).astype(BF16)
    m_ref[...] = m_new


def _value_stage(pv, alpha_ref, p_ref, acc_ref):
    acc = acc_ref[...]
    acc_ref[...] = acc * jnp.broadcast_to(alpha_ref[0:1, :], acc.shape) + pv(p_ref[...])


def _attend_chunks(qt, qk, logits, pv, bufs, m_ref, acc_ref):
    s_a, s_b, p_a, p_b, al_a, al_b = bufs

    def plain(j, kind):
        qk(j, s_a)
        _softmax_stage(logits(j, s_a[...], kind), m_ref, al_a, p_a)
        _value_stage(lambda p: pv(j, p), al_a, p_a, acc_ref)

    @pl.when(qt == 0)
    def _():
        plain(0, 'diag')

    @pl.when(qt >= 1)
    def _():
        n_far = qt - 1
        first = n_far % 2

        @pl.when(first == 1)
        def _():
            plain(0, 'far')

        qk(first, s_a)
        p_b[...] = jnp.zeros(p_b.shape, p_b.dtype)
        al_b[...] = jnp.ones(al_b.shape, al_b.dtype)

        def pair(c, kind0, kind1, last):
            qk(c + 1, s_b)
            _value_stage(lambda p: pv(jnp.maximum(c - 1, 0), p), al_b, p_b, acc_ref)
            _softmax_stage(logits(c, s_a[...], kind0), m_ref, al_a, p_a)
            if not last:
                qk(c + 2, s_a)
            _value_stage(lambda p: pv(c, p), al_a, p_a, acc_ref)
            _softmax_stage(logits(c + 1, s_b[...], kind1), m_ref, al_b, p_b)

        def body(i, carry):
            pair(first + 2 * i, 'far', 'far', False)
            return carry

        lax.fori_loop(0, (n_far - first) // 2, body, 0)
        pair(qt - 1, 'prev', 'diag', True)
        _value_stage(lambda p: pv(qt, p), al_b, p_b, acc_ref)


def _pipeline_buffers(width):
    return ([pltpu.VMEM((ATT_TILE, width), F32)] * 2 + [pltpu.VMEM((ATT_TILE, width), BF16)] * 2
            + [pltpu.VMEM((SUBLANES, width), F32)] * 2)


def _finish_heads_t(acc_ref, o_ref, n_heads, odd_swapped):
    hd = HEAD_DIM
    tq = o_ref.shape[0]
    for pr in range(n_heads // 2):
        ae = acc_ref[:, 2 * pr * tq:(2 * pr + 1) * tq]
        ao = acc_ref[:, (2 * pr + 1) * tq:(2 * pr + 2) * tq]
        oe = ae[0:hd] / ae[hd:2 * hd]
        oo = ao[hd:2 * hd] / ao[0:hd] if odd_swapped else ao[0:hd] / ao[hd:2 * hd]
        pair = jnp.concatenate([oe, oo], axis=0)
        o_ref[:, pr * LANES:(pr + 1) * LANES] = pair.T.astype(o_ref.dtype)


def _heads_side_by_side(qt_ref, qm_ref, n_heads):
    tq = qt_ref.shape[1]
    for h in range(n_heads):
        qm_ref[:, h * tq:(h + 1) * tq] = qt_ref[h * HEAD_DIM:(h + 1) * HEAD_DIM, :]


def _dsa_kernel(q_ref, qi_ref, wi_ref, kv_ref, ki_ref, band_ref, o_ref,
                score_ref, vt_ref, qm_ref, m_ref, acc_ref, thr_ref, *bufs, n_keys, seq):
    tq = tk = ATT_TILE
    qt = pl.program_id(1)
    n_chunks = qt + 1
    kf = float(n_keys)

    @pl.when(qt == 0)
    def _():
        top = lax.broadcasted_iota(jnp.int32, (LANES, tk), 0) < HEAD_DIM
        for j in range(seq // tk):
            vv = kv_ref[j * tk:(j + 1) * tk, LANES:2 * LANES].astype(F32)
            vt_ref[j] = jnp.where(top, vv.T, 1.0).astype(BF16)

    _heads_side_by_side(qi_ref, qm_ref, IDX_HEADS)
    head_w = jnp.concatenate(
        [wi_ref[h:h + 1, :] for h in range(IDX_HEADS)], axis=1
    ) * (IDX_HEADS ** -0.5 * IDX_DIM ** -0.5)
    key_minus_query = (lax.broadcasted_iota(jnp.int32, (tk, tq), 0)
                       - lax.broadcasted_iota(jnp.int32, (tk, tq), 1))

    def score_body(j, carry):
        mx, mn, n_ge0, n_gt0 = carry
        kc = ki_ref[pl.ds(pl.multiple_of(j * tk, tk), tk), 0:IDX_DIM]
        d = _dot(kc, qm_ref[:, 0:IDX_HEADS * tq])
        d = jnp.broadcast_to(head_w, d.shape) * jnp.maximum(d, 0.0)
        sc = (d[:, 0:tq] + d[:, tq:2 * tq]) + (d[:, 2 * tq:3 * tq] + d[:, 3 * tq:4 * tq])
        causal = key_minus_query <= (qt - j) * tq
        masked = jnp.where(causal, sc, NEG)
        score_ref[j] = masked
        mx = jnp.maximum(mx, _fold_rows(masked, jnp.maximum))
        mn = jnp.minimum(mn, _fold_rows(jnp.where(causal, sc, -NEG), jnp.minimum))
        n_ge0 = n_ge0 + _fold_rows(jnp.where(masked >= 0.0, 1.0, 0.0), jnp.add)
        n_gt0 = n_gt0 + _fold_rows(jnp.where(masked > 0.0, 1.0, 0.0), jnp.add)
        return mx, mn, n_ge0, n_gt0

    zeros8 = jnp.zeros((SUBLANES, tq), F32)
    mx, mn, n_ge0, n_gt0 = lax.fori_loop(
        0, n_chunks, score_body,
        (jnp.full((SUBLANES, tq), NEG, F32), jnp.full((SUBLANES, tq), -NEG, F32), zeros8, zeros8))

    t_q = qt * tq + lax.broadcasted_iota(jnp.int32, (1, tq), 1)
    thr_ref[...] = jnp.full((SUBLANES, tq), 0.5 * NEG, F32)

    def count(pred):
        def body(j, acc):
            return acc + _fold_rows(pred(score_ref[j]), jnp.add)
        acc = lax.fori_loop(0, n_chunks, body, jnp.zeros((SUBLANES, tq), F32))
        return jnp.sum(acc, axis=0, keepdims=True)

    def count_ge(x):
        xb = jnp.broadcast_to(x, (tk, tq))
        return count(lambda s: jnp.where(s >= xb, 1.0, 0.0))

    @pl.when((qt + 1) * tq > n_keys)
    def _search():
        need = (t_q >= n_keys).astype(F32)
        lo0 = jnp.min(mn, axis=0, keepdims=True)
        hi0 = jnp.max(mx, axis=0, keepdims=True)
        hi_top = hi0 + jnp.maximum(jnp.abs(hi0) * 2.0 ** -22, 1e-30)
        c_ge0 = jnp.sum(n_ge0, axis=0, keepdims=True)
        c_gt0 = jnp.sum(n_gt0, axis=0, keepdims=True)
        n_valid = (t_q + 1).astype(F32)
        zero_hit = c_ge0 == kf
        zero_tie = (c_gt0 < kf) & (c_ge0 > kf)
        positive = c_gt0 >= kf
        hi_neg = hi0 < 0.0
        lo1 = jnp.where(positive, jnp.maximum(lo0, 0.0), lo0)
        c_lo1 = jnp.where(positive, c_ge0, n_valid)
        hi1 = jnp.where(positive | hi_neg, hi_top, 0.0)
        c_hi1 = jnp.where(positive | hi_neg, 0.0, c_ge0)
        hit0 = need * zero_hit.astype(F32)
        tie0 = need * zero_tie.astype(F32)
        done0 = jnp.maximum(1.0 - need, jnp.maximum(hit0, tie0))
        lo1 = jnp.where(tie0 > 0, 0.0, lo1)
        c_hi1 = jnp.where(tie0 > 0, c_gt0, c_hi1)
        x0 = jnp.zeros((1, tq), F32)
        state0 = (lo1, hi1, c_lo1, c_hi1, done0, tie0, x0, jnp.max(1.0 - done0))

        def cond(st):
            return st[7] > 0.0

        def body(st):
            for _ in range(SEARCH_UNROLL):
                st = step(st)
            return st

        def step(st):
            lo, hi, c_lo, c_hi, done, tie, x_fin, _ = st
            x = 0.5 * lo + 0.5 * hi
            inside = ((x > lo) & (x < hi)).astype(F32)
            active = 1.0 - done
            probe = active * inside
            c = count_ge(jnp.where(probe > 0, x, x_fin))
            hit = probe * (c == kf).astype(F32)
            more = probe * (c > kf).astype(F32)
            less = probe * (c < kf).astype(F32)
            new_tie = active * (1.0 - inside)
            lo = jnp.where(more > 0, x, lo)
            c_lo = jnp.where(more > 0, c, c_lo)
            hi = jnp.where(less > 0, x, hi)
            c_hi = jnp.where(less > 0, c, c_hi)
            x_fin = jnp.where(hit > 0, x, x_fin)
            tie = jnp.maximum(tie, new_tie)
            done = jnp.maximum(done, jnp.maximum(hit, new_tie))
            return (lo, hi, c_lo, c_hi, done, tie, x_fin, jnp.max(1.0 - done))

        st = state0
        for _ in range(SEARCH_FIXED):
            st = step(st)
        st = lax.while_loop(cond, body, st)
        lo, _, _, c_hi, _, tie, x_fin, _ = st
        thr = jnp.where(need > 0, jnp.where(tie > 0, lo, x_fin), 0.5 * NEG)
        thr_ref[...] = jnp.broadcast_to(thr, (SUBLANES, tq))

        @pl.when(jnp.max(tie) > 0.0)
        def _ties():
            want = jnp.broadcast_to(kf - c_hi, (tk, tq))
            thr_t = jnp.broadcast_to(jnp.where(tie > 0, thr, -NEG), (tk, tq))
            upto = (lax.broadcasted_iota(jnp.int32, (tk, tk), 1)
                    <= lax.broadcasted_iota(jnp.int32, (tk, tk), 0))
            prefix = jnp.where(upto, 1.0, 0.0).astype(BF16)

            def body(j, seen):
                s = score_ref[j]
                eq = s == thr_t
                rank = _dot(prefix, jnp.where(eq, 1.0, 0.0).astype(BF16))
                rank = rank + jnp.broadcast_to(seen, (tk, tq))
                score_ref[j] = jnp.where(eq, jnp.where(rank > want, NEG, s), s)
                return rank[tk - 1:tk, :]

            lax.fori_loop(0, n_chunks, body, jnp.zeros((1, tq), F32))

    _heads_side_by_side(q_ref, qm_ref, DSA_HEADS)
    m_ref[...] = jnp.full(m_ref.shape, NEG, F32)
    acc_ref[...] = jnp.zeros(acc_ref.shape, F32)
    thr_b = jnp.broadcast_to(thr_ref[0:1, :], (tk, tq))

    def qk(j, s_ref):
        rows = pl.ds(pl.multiple_of(j * tk, tk), tk)
        s_ref[...] = _dot(kv_ref[rows, 0:HEAD_DIM], qm_ref[...])

    def logits(j, s, kind):
        off = jnp.where(score_ref[j] >= thr_b, 0.0, NEG)
        s = s + jnp.concatenate([off] * DSA_HEADS, axis=1)
        if kind == 'prev':
            s = s + band_ref[0:tk, :]
        elif kind == 'diag':
            s = s + band_ref[tk:2 * tk, :]
        return s

    _attend_chunks(qt, qk, logits, lambda j, p: _dot(vt_ref[j], p), bufs, m_ref, acc_ref)
    _finish_heads_t(acc_ref, o_ref, DSA_HEADS, odd_swapped=False)


def _dsa(qb_t, qi_t, wi_t, kv, ki, band, bsz, seq, n_keys):
    tq = ATT_TILE
    nq = seq // tq
    qrow = lambda w: pl.BlockSpec((tq, w), lambda b, q: (b * nq + q, 0))
    qtile = lambda w: pl.BlockSpec((None, w, tq), lambda b, q: (b * nq + q, 0, 0))
    seq_blk = lambda w: pl.BlockSpec((seq, w), lambda b, q: (b, 0))
    return pl.pallas_call(
        functools.partial(_dsa_kernel, n_keys=n_keys, seq=seq), grid=(bsz, nq),
        in_specs=[qtile(DSA_WIDTH), qtile(IDX_HEADS * IDX_DIM), qtile(SUBLANES), seq_blk(2 * LANES),
                  seq_blk(LANES), _resident((2 * tq, DSA_HEADS * tq))],
        out_specs=qrow(DSA_WIDTH),
        out_shape=jax.ShapeDtypeStruct((bsz * seq, DSA_WIDTH), BF16),
        scratch_shapes=[pltpu.VMEM((nq, tq, tq), F32),
                        pltpu.VMEM((nq, LANES, tq), BF16),
                        pltpu.VMEM((HEAD_DIM, DSA_HEADS * tq), BF16),
                        pltpu.VMEM((SUBLANES, DSA_HEADS * tq), F32),
                        pltpu.VMEM((LANES, DSA_HEADS * tq), F32),
                        pltpu.VMEM((SUBLANES, tq), F32)] + _pipeline_buffers(DSA_HEADS * tq),
        compiler_params=_cparams(("parallel", "arbitrary")), name="dsa",
    )(qb_t, qi_t, wi_t, kv, ki, band)


def _band_kernel(bucket_ref, rb_ref, o_ref):
    bucket = bucket_ref[...]
    tq = bucket.shape[1]
    for h in range(DSA_HEADS):
        acc = jnp.zeros(bucket.shape, F32)
        for k in range(REL_BUCKETS - 1):
            acc = jnp.where(bucket == k, (rb_ref[k, h] - rb_ref[REL_BUCKETS - 1, h]) * LOG2E, acc)
        o_ref[:, h * tq:(h + 1) * tq] = acc


def _t5_bucket_np(dist):
    d = np.maximum(dist, 0)
    df = np.maximum(d, 1).astype(np.float32)
    log_b = REL_MAX_EXACT + (np.log(df / REL_MAX_EXACT) / math.log(REL_MAX_DIST / REL_MAX_EXACT)
                             * (REL_BUCKETS - REL_MAX_EXACT)).astype(np.int32)
    log_b = np.minimum(log_b, REL_BUCKETS - 1)
    return np.where(d < REL_MAX_EXACT, d, log_b).astype(np.int32)


def _rel_band(rel_bias):
    tq = ATT_TILE
    i = np.arange(tq)[None, :]
    j = np.arange(2 * tq)[:, None]
    bucket = jnp.asarray(_t5_bucket_np(i + tq - j))
    return pl.pallas_call(
        _band_kernel,
        in_specs=[pl.BlockSpec(memory_space=pltpu.VMEM), pl.BlockSpec(memory_space=pltpu.SMEM)],
        out_specs=pl.BlockSpec(memory_space=pltpu.VMEM),
        out_shape=jax.ShapeDtypeStruct((2 * tq, DSA_HEADS * tq), F32), name="rel_band",
    )(bucket, rel_bias)


def _cum_kernel(aux_ref, bf_ref, col_ref, row_ref, *, seq):
    x = aux_ref[...] + bf_ref[...]
    logf = -(jnp.maximum(-x, 0.0) + jnp.log1p(jnp.exp(-jnp.abs(x))))
    c = logf.T[0:SUBLANES]
    pos = lax.broadcasted_iota(jnp.int32, c.shape, 1)
    shift = 1
    while shift < seq:
        c = c + jnp.where(pos >= shift, pltpu.roll(c, shift, 1), 0.0)
        shift *= 2
    c = c * LOG2E
    col_ref[...] = jnp.concatenate([c, jnp.zeros((LANES - SUBLANES, seq), F32)], axis=0).T
    for j in range(seq // ATT_TILE):
        row_ref[j] = c[:, j * ATT_TILE:(j + 1) * ATT_TILE]


def _fox_cum(aux, b_f, bsz, seq):
    nq = seq // ATT_TILE
    bf = jnp.zeros((1, LANES), F32).at[0, AUX_F:AUX_F + FOX_HEADS].set(b_f)
    return pl.pallas_call(
        functools.partial(_cum_kernel, seq=seq), grid=(bsz,),
        in_specs=[pl.BlockSpec((seq, LANES), lambda b: (b, 0)), _resident((1, LANES))],
        out_specs=[pl.BlockSpec((seq, LANES), lambda b: (b, 0)),
                   pl.BlockSpec((None, nq, SUBLANES, ATT_TILE), lambda b: (b, 0, 0, 0))],
        out_shape=[jax.ShapeDtypeStruct((bsz * seq, LANES), F32),
                   jax.ShapeDtypeStruct((bsz, nq, SUBLANES, ATT_TILE), F32)],
        compiler_params=_cparams(("parallel",)), name="fox_cum")(aux, bf)


def _bf16_pieces(x):
    hi = x.astype(BF16).astype(F32)
    rest = x - hi
    mid = rest.astype(BF16).astype(F32)
    return [hi, mid, (rest - mid).astype(BF16).astype(F32)]


def _fox_kernel(q_ref, k_ref, v_ref, ccol_ref, crow_ref, o_ref,
                vte_ref, vto_ref, ka_ref, qm_ref, m_ref, acc_ref, *bufs, seq):
    tq = tk = ATT_TILE
    qt = pl.program_id(1)
    heads = range(FOX_HEADS)

    @pl.when(qt == 0)
    def _():
        top = lax.broadcasted_iota(jnp.int32, (LANES, tk), 0) < HEAD_DIM
        for j in range(seq // tk):
            rows = slice(j * tk, (j + 1) * tk)
            for pr in range(FOX_HEADS // 2):
                v_t = v_ref[rows, pr * LANES:(pr + 1) * LANES].astype(F32).T
                vte_ref[pr, j] = jnp.where(top, v_t, 1.0).astype(BF16)
                vto_ref[pr, j] = jnp.where(top, 1.0, v_t).astype(BF16)
            cc = ccol_ref[rows, :]
            lane = lax.broadcasted_iota(jnp.int32, (tk, LANES), 1)
            for h in heads:
                k_h = k_ref[rows, (h // 2) * LANES:(h // 2 + 1) * LANES].astype(F32)
                if h % 2 == 1:
                    k_h = pltpu.roll(k_h, HEAD_DIM, 1)
                pieces = _bf16_pieces(jnp.broadcast_to(cc[:, AUX_F + h:AUX_F + h + 1], (tk, LANES)))
                ext = jnp.where(lane < HEAD_DIM + 2 * len(pieces), 1.0, 0.0)
                for i, piece in enumerate(pieces):
                    ext = jnp.where(lane == HEAD_DIM + i, piece, ext)
                ka_ref[h, rows, :] = jnp.where(lane < HEAD_DIM, k_h, ext).astype(BF16)

    c_q = crow_ref[qt]
    row = lax.broadcasted_iota(jnp.int32, (LANES - HEAD_DIM, tq), 0)
    for h in heads:
        pieces = _bf16_pieces(c_q[AUX_F + h:AUX_F + h + 1, :])
        ext = jnp.where(row < len(pieces), -1.0, 0.0)
        for i, piece in enumerate(pieces):
            ext = jnp.where(row == len(pieces) + i, jnp.broadcast_to(piece, ext.shape), ext)
        q_h = q_ref[h * HEAD_DIM:(h + 1) * HEAD_DIM, :]
        qm_ref[:, h * tq:(h + 1) * tq] = jnp.concatenate([q_h, ext.astype(BF16)], axis=0)
    m_ref[...] = jnp.full(m_ref.shape, NEG, F32)
    acc_ref[...] = jnp.zeros(acc_ref.shape, F32)
    causal_off = jnp.where(lax.broadcasted_iota(jnp.int32, (tk, tq), 0)
                           <= lax.broadcasted_iota(jnp.int32, (tk, tq), 1), 0.0, NEG)

    def qk(j, s_ref):
        rows = pl.ds(pl.multiple_of(j * tk, tk), tk)
        for h in heads:
            cols = slice(h * tq, (h + 1) * tq)
            s_ref[:, cols] = _dot(ka_ref[h, rows, :], qm_ref[:, cols])

    def logits(j, s, kind):
        if kind == 'diag':
            s = s + jnp.concatenate([causal_off] * FOX_HEADS, axis=1)
        return s

    def pv(j, p):
        return jnp.concatenate(
            [_dot(vte_ref[h // 2, j] if h % 2 == 0 else vto_ref[h // 2, j],
                  p[:, h * tq:(h + 1) * tq]) for h in heads], axis=1)

    _attend_chunks(qt, qk, logits, pv, bufs, m_ref, acc_ref)
    _finish_heads_t(acc_ref, o_ref, FOX_HEADS, odd_swapped=True)


def _fox(q_t, kv, c_col, c_row, bsz, seq):
    tq = ATT_TILE
    nq = seq // tq
    npair = FOX_HEADS // 2
    return pl.pallas_call(
        functools.partial(_fox_kernel, seq=seq), grid=(bsz, nq),
        in_specs=[pl.BlockSpec((None, FOX_WIDTH, tq), lambda b, q: (b * nq + q, 0, 0)),
                  pl.BlockSpec((seq, FOX_WIDTH), lambda b, q: (b, 0)),
                  pl.BlockSpec((seq, FOX_WIDTH), lambda b, q: (b, 1)),
                  pl.BlockSpec((seq, LANES), lambda b, q: (b, 0)),
                  pl.BlockSpec((None, nq, SUBLANES, tq), lambda b, q: (b, 0, 0, 0))],
        out_specs=pl.BlockSpec((tq, FOX_WIDTH), lambda b, q: (b * nq + q, 0)),
        out_shape=jax.ShapeDtypeStruct((bsz * seq, FOX_WIDTH), BF16),
        scratch_shapes=[pltpu.VMEM((npair, nq, LANES, tq), BF16),
                        pltpu.VMEM((npair, nq, LANES, tq), BF16),
                        pltpu.VMEM((FOX_HEADS, seq, LANES), BF16),
                        pltpu.VMEM((LANES, FOX_HEADS * tq), BF16),
                        pltpu.VMEM((SUBLANES, FOX_HEADS * tq), F32),
                        pltpu.VMEM((LANES, FOX_HEADS * tq), F32)] + _pipeline_buffers(FOX_HEADS * tq),
        compiler_params=_cparams(("parallel", "arbitrary")), name="fox",
    )(q_t, kv, kv, c_col, c_row)


def _gelu(y):
    return 0.5 * y * (1.0 + jnp.tanh(math.sqrt(2.0 / math.pi) * (y + 0.044715 * (y * y * y))))


def _merge_kernel(x_ref, yc_ref, u_ref, d_ref, wglu_ref, b_ref, c_ref, gate_ref,
                  wa_ref, wb_ref, wc_ref, wo_ref, o_ref, tmp_ref):
    d = D_MODEL
    y = _chunks_to_rows(yc_ref, tmp_ref, SSM_WIDTH) + d_ref[...] * u_ref[...]
    z = _dot(_gelu(y).astype(BF16), wglu_ref[...])
    a = z[:, :SSM_WIDTH] * _gate(z[:, SSM_WIDTH:])
    merged = _gate(gate_ref[:, 0:d].astype(F32)) * _dot(a.astype(BF16), wa_ref[...])
    merged += _gate(gate_ref[:, d:2 * d].astype(F32)) * _dot(b_ref[...], wb_ref[...])
    merged += _gate(gate_ref[:, 2 * d:3 * d].astype(F32)) * _dot(c_ref[...], wc_ref[...])
    o_ref[...] = x_ref[...] + _dot(merged.astype(BF16), wo_ref[...])


def _merge(xt, yc, u, d_skip, w_glu, b, c, gates, wa, wb, wc, wo, tm=1024):
    m, d = xt.shape
    row = lambda w: pl.BlockSpec((tm, w), lambda i: (i, 0))
    return pl.pallas_call(
        _merge_kernel, grid=(m // tm,),
        in_specs=[row(d), pl.BlockSpec((tm // S5_CHUNK, S5_COLS), lambda i: (i, 0)),
                  row(SSM_WIDTH), _resident((1, SSM_WIDTH)),
                  _resident((SSM_WIDTH, 2 * SSM_WIDTH)), row(DSA_WIDTH), row(FOX_WIDTH), row(3 * d),
                  _resident((SSM_WIDTH, d)), _resident((DSA_WIDTH, d)), _resident((FOX_WIDTH, d)),
                  _resident((d, d))],
        out_specs=row(d), out_shape=jax.ShapeDtypeStruct((m, d), F32),
        scratch_shapes=[pltpu.VMEM((SSM_WIDTH // LANES, tm, LANES), F32)],
        compiler_params=_cparams(("parallel",)), name="merge",
    )(xt, yc, u, d_skip.reshape(1, -1), w_glu, b, c, gates, wa, wb, wc, wo)


def _pack_w_in(w_in):
    d = w_in.shape[0]
    splits = (SSM_WIDTH, DSA_WIDTH, DSA_LATENT, IDX_HEADS * IDX_DIM, IDX_DIM, IDX_HEADS,
              FOX_WIDTH, FOX_WIDTH, FOX_WIDTH, FOX_HEADS, D_MODEL, D_MODEL, D_MODEL)
    pts = np.cumsum(splits)[:-1]
    (w_u, w_qb, w_ckv, w_qi, w_ki, w_wi, w_qc, w_kc, w_vc, w_fc, w_ga, w_gb, w_gc) = jnp.split(
        w_in, pts, axis=1)
    scale = HEAD_DIM ** -0.5 * LOG2E
    aux = jnp.zeros((d, LANES), F32)
    aux = aux.at[:, AUX_F:AUX_F + FOX_HEADS].set(w_fc).at[:, AUX_W:AUX_W + IDX_HEADS].set(w_wi)
    cols = [w_u, w_qb * scale, w_qi, w_ki, w_ki, w_ckv, aux, w_qc * scale, w_kc, w_vc,
            w_ga, w_gb, w_gc]
    return jnp.concatenate(cols, axis=1).astype(BF16)


def kernel(x, ffn1_norm, ffn1_w_gate, ffn1_w_up, ffn1_w_down, mix_norm, w_in, ssm_lambda_re, ssm_lambda_im, ssm_log_dt, ssm_b_re, ssm_b_im, ssm_c_re, ssm_c_im, ssm_d, ssm_w_glu, dsa_kv_norm, dsa_w_uk, dsa_w_uv, rel_bias, fox_b_f, w_branch_ssm, w_branch_dsa, w_branch_fox, w_out, ffn2_norm, ffn2_w_gate, ffn2_w_up, ffn2_w_down, final_norm):
    bsz, seq, d = x.shape
    depth = w_in.shape[0]
    n_keys = min(TOPK_MAX, seq // 4)
    assert d == D_MODEL and seq % ATT_TILE == 0 and (bsz * seq) % (S5_CHUNK * 512) == 0
    assert n_keys % LANES == 0 and bsz % SUBLANES == 0
    m = bsz * seq
    n_chunks = seq // S5_CHUNK
    xt = x.reshape(m, d)
    band = _rel_band(rel_bias)
    bf = lambda w: w.astype(BF16)
    for l in range(depth):
        xt = _ffn(xt, ffn1_norm[l], bf(ffn1_w_gate[l]), bf(ffn1_w_up[l]), bf(ffn1_w_down[l]))
        w_kv = bf(jnp.concatenate([dsa_w_uk[l], dsa_w_uk[l], dsa_w_uv[l], dsa_w_uv[l]], axis=1))
        u, qb_t, qi_t, ki, kv, aux, wi_t, qc_t, kv_fox, gates, u8 = _inproj(
            xt, mix_norm[l], _pack_w_in(w_in[l]), dsa_kv_norm[l], w_kv)
        m_intra, m_sum, m_out, a_re, a_im = _s5_matrices(
            ssm_lambda_re[l], ssm_lambda_im[l], ssm_log_dt[l], ssm_b_re[l], ssm_b_im[l],
            ssm_c_re[l], ssm_c_im[l])
        y_intra, p = _s5a(u8, m_intra, m_sum)
        h_re, h_im = _s5b(p, a_re, a_im, n_chunks, nb=SUBLANES)
        yc = _s5c(y_intra, h_re, h_im, m_out)
        b_out = _dsa(qb_t, qi_t, wi_t, kv, ki, band, bsz, seq, n_keys)
        c_col, c_row = _fox_cum(aux, fox_b_f[l], bsz, seq)
        c_out = _fox(qc_t, kv_fox, c_col, c_row, bsz, seq)
        xt = _merge(xt, yc, u, ssm_d[l], bf(ssm_w_glu[l]), b_out, c_out, gates,
                    bf(w_branch_ssm[l]), bf(w_branch_dsa[l]), bf(w_branch_fox[l]), bf(w_out[l]))
        last = final_norm if l == depth - 1 else None
        xt = _ffn(xt, ffn2_norm[l], bf(ffn2_w_gate[l]), bf(ffn2_w_up[l]), bf(ffn2_w_down[l]),
                  final_norm=last)
    return xt.reshape(bsz, seq, d)
```

```python
import functools
import math

import numpy as np
import jax
import jax.numpy as jnp
from jax import lax
from jax.experimental import pallas as pl
from jax.experimental.pallas import tpu as pltpu

F32 = jnp.float32
BF16 = jnp.bfloat16

D_MODEL = 1024
SSM_WIDTH = D_MODEL // 4
SSM_GROUP = 16
SSM_GROUPS = SSM_WIDTH // SSM_GROUP
SSM_STATE = 64
HEAD_DIM = 64
DSA_HEADS = 6
DSA_WIDTH = DSA_HEADS * HEAD_DIM
DSA_LATENT = 2 * HEAD_DIM
IDX_HEADS = 4
IDX_DIM = 64
TOPK_MAX = 256
FOX_HEADS = 6
FOX_WIDTH = FOX_HEADS * HEAD_DIM
REL_BUCKETS = 32
REL_MAX_EXACT = 16
REL_MAX_DIST = 128
FFN_HIDDEN = 2816
EPS = 1e-6
NEG = -1e30
LOG2E = math.log2(math.e)

LANES = 128
SUBLANES = 8
S5_CHUNK = 8
S5_COLS = S5_CHUNK * SSM_WIDTH
S5_STATE = 2 * SSM_GROUPS * SSM_STATE
ATT_TILE = 256
SEARCH_FIXED = 16
SEARCH_UNROLL = 2
VMEM_LIMIT = 56 * 1024 * 1024

_SEG = dict(u=SSM_WIDTH, qb=DSA_WIDTH, qi=IDX_HEADS * IDX_DIM, ki=LANES, ckv=DSA_LATENT, aux=LANES,
            qkv=3 * FOX_WIDTH, g=3 * D_MODEL)
AUX_F = 0
AUX_W = 8


def _cparams(sem):
    return pltpu.CompilerParams(dimension_semantics=sem, vmem_limit_bytes=VMEM_LIMIT)


def _resident(shape):
    nd = len(shape)
    return pl.BlockSpec(shape, lambda *_: (0,) * nd, pipeline_mode=pl.Buffered(1))


def _rms(x, g):
    return x * lax.rsqrt(jnp.mean(x * x, axis=-1, keepdims=True) + EPS) * g


def _gate(x):
    return 0.5 + 0.5 * jnp.tanh(0.5 * x)


def _dot(a, b):
    return jnp.dot(a, b, preferred_element_type=F32)


def _ffn_body(x, g_ref, wg_ref, wu_ref, wd_ref, fc):
    h = _rms(x, g_ref[...]).astype(BF16)
    acc = None
    for c in range(FFN_HIDDEN // fc):
        sl = slice(c * fc, (c + 1) * fc)
        gt = _dot(h, wg_ref[:, sl])
        up = _dot(h, wu_ref[:, sl])
        a = (gt * _gate(gt) * up).astype(BF16)
        d = _dot(a, wd_ref[sl, :])
        acc = d if acc is None else acc + d
    return x + 0.5 * acc


def _ffn_kernel(x_ref, g_ref, wg_ref, wu_ref, wd_ref, o_ref, *, fc):
    o_ref[...] = _ffn_body(x_ref[...], g_ref, wg_ref, wu_ref, wd_ref, fc)


def _ffn_final_kernel(x_ref, g_ref, wg_ref, wu_ref, wd_ref, fn_ref, o_ref, *, fc):
    y = _ffn_body(x_ref[...], g_ref, wg_ref, wu_ref, wd_ref, fc)
    o_ref[...] = _rms(y, fn_ref[...])


def _ffn(xt, norm, wg, wu, wd, final_norm=None, tm=1024, fc=256):
    m, d = xt.shape
    f = wg.shape[1]
    row = pl.BlockSpec((tm, d), lambda i: (i, 0))
    vec = _resident((1, d))
    in_specs = [row, vec, _resident((d, f)), _resident((d, f)), _resident((f, d))]
    args = [xt, norm.reshape(1, d), wg, wu, wd]
    if final_norm is None:
        body = functools.partial(_ffn_kernel, fc=fc)
    else:
        body = functools.partial(_ffn_final_kernel, fc=fc)
        in_specs.append(vec)
        args.append(final_norm.reshape(1, d))
    return pl.pallas_call(
        body, grid=(m // tm,), in_specs=in_specs, out_specs=row,
        out_shape=jax.ShapeDtypeStruct((m, d), F32),
        compiler_params=_cparams(("parallel",)), name="ffn")(*args)


def _rows_to_chunks(x, tmp_ref, o_ref):
    n, w = x.shape
    for hh in range(w // LANES):
        tmp_ref[hh] = x[:, hh * LANES:(hh + 1) * LANES]
    for i in range(S5_CHUNK):
        for hh in range(w // LANES):
            o_ref[:, i * w + hh * LANES:i * w + (hh + 1) * LANES] = (
                tmp_ref[hh, pl.ds(i, n // S5_CHUNK, stride=S5_CHUNK), :])


def _chunks_to_rows(x_ref, tmp_ref, w):
    n = x_ref.shape[0]
    for i in range(S5_CHUNK):
        for hh in range(w // LANES):
            tmp_ref[hh, pl.ds(i, n, stride=S5_CHUNK), :] = (
                x_ref[:, i * w + hh * LANES:i * w + (hh + 1) * LANES])
    return jnp.concatenate([tmp_ref[hh] for hh in range(w // LANES)], axis=1)


def _emit_transposed(piece, o_ref, rows=None):
    for t in range(piece.shape[0] // ATT_TILE):
        slab_t = piece[t * ATT_TILE:(t + 1) * ATT_TILE, :].T
        o_ref[t] = (slab_t if rows is None else slab_t[rows]).astype(o_ref.dtype)


def _inproj_kernel(x_ref, g_ref, w_ref, kvn_ref, wkv_ref,
                   u_ref, qbt_ref, qit_ref, ki_ref, kv_ref, aux_ref, wit_ref, qct_ref, kvc_ref,
                   gate_ref, u8_ref, tmp_ref):
    h = _rms(x_ref[...], g_ref[...]).astype(BF16)
    outs = dict(u=u_ref, ki=ki_ref, aux=aux_ref)
    off = 0
    for group in (("u", "qb", "qi", "ki"), ("ckv", "aux", "qkv")):
        width = sum(_SEG[name] for name in group)
        r = _dot(h, w_ref[:, off:off + width])
        off += width
        lo = 0
        for name in group:
            piece = r[:, lo:lo + _SEG[name]]
            lo += _SEG[name]
            if name == "ckv":
                c = _rms(piece, kvn_ref[...]).astype(BF16)
                kv_ref[...] = _dot(c, wkv_ref[...]).astype(BF16)
            elif name == "qb":
                _emit_transposed(piece, qbt_ref)
            elif name == "qi":
                _emit_transposed(piece, qit_ref)
            elif name == "qkv":
                _emit_transposed(piece[:, :FOX_WIDTH], qct_ref)
                kvc_ref[...] = piece[:, FOX_WIDTH:].astype(BF16)
            else:
                outs[name][...] = piece.astype(outs[name].dtype)
            if name == "u":
                _rows_to_chunks(piece, tmp_ref, u8_ref)
            if name == "aux":
                _emit_transposed(piece, wit_ref, rows=slice(AUX_W, AUX_W + SUBLANES))
    for c in range(_SEG["g"] // D_MODEL):
        cols = slice(c * D_MODEL, (c + 1) * D_MODEL)
        gate_ref[:, cols] = _dot(h, w_ref[:, off + c * D_MODEL:off + (c + 1) * D_MODEL]).astype(BF16)


def _inproj(xt, norm, w_all, kv_norm, w_kv, tm=1024):
    m, d = xt.shape
    wtot = w_all.shape[1]
    tq = ATT_TILE

    def rows(w, dt):
        return jax.ShapeDtypeStruct((m, w), dt), pl.BlockSpec((tm, w), lambda i: (i, 0))

    def tiles_t(w, dt):
        return (jax.ShapeDtypeStruct((m // tq, w, tq), dt),
                pl.BlockSpec((tm // tq, w, tq), lambda i: (i, 0, 0)))

    outs = [rows(SSM_WIDTH, F32), tiles_t(DSA_WIDTH, BF16), tiles_t(IDX_HEADS * IDX_DIM, BF16),
            rows(LANES, BF16), rows(2 * LANES, BF16), rows(LANES, F32), tiles_t(SUBLANES, F32),
            tiles_t(FOX_WIDTH, BF16), rows(2 * FOX_WIDTH, BF16), rows(3 * D_MODEL, BF16),
            (jax.ShapeDtypeStruct((m // S5_CHUNK, S5_COLS), F32),
             pl.BlockSpec((tm // S5_CHUNK, S5_COLS), lambda i: (i, 0)))]
    out_shape = [o[0] for o in outs]
    out_specs = [o[1] for o in outs]
    return pl.pallas_call(
        _inproj_kernel, grid=(m // tm,),
        in_specs=[pl.BlockSpec((tm, d), lambda i: (i, 0)), _resident((1, d)),
                  _resident((d, wtot)), _resident((1, DSA_LATENT)),
                  _resident((DSA_LATENT, 2 * LANES))],
        out_specs=out_specs, out_shape=out_shape,
        scratch_shapes=[pltpu.VMEM((SSM_WIDTH // LANES, tm, LANES), F32)],
        compiler_params=_cparams(("parallel",)), name="inproj",
    )(xt, norm.reshape(1, d), w_all, kv_norm.reshape(1, DSA_LATENT), w_kv)


S5_SLABS = S5_STATE // LANES
S5_SCAN_SLABS = 4


def _s5a_kernel(u_ref, mi_ref, ms_ref, yi_ref, p_ref):
    u = u_ref[...].astype(BF16)
    yi_ref[...] = _dot(u, mi_ref[...])
    p = _dot(u, ms_ref[...])
    for k in range(S5_SLABS):
        p_ref[k] = p[:, k * LANES:(k + 1) * LANES]


def _s5a(u8, m_intra, m_sum, tr=512):
    r = u8.shape[0]
    row = pl.BlockSpec((tr, S5_COLS), lambda i: (i, 0))
    slabs = pl.BlockSpec((S5_SLABS, tr, LANES), lambda i: (0, i, 0))
    return pl.pallas_call(
        _s5a_kernel, grid=(r // tr,),
        in_specs=[row, _resident((S5_COLS, S5_COLS)), _resident((S5_COLS, S5_STATE))],
        out_specs=[row, slabs],
        out_shape=[jax.ShapeDtypeStruct((r, S5_COLS), F32),
                   jax.ShapeDtypeStruct((S5_SLABS, r, LANES), F32)],
        compiler_params=_cparams(("parallel",)), name="s5_chunk")(u8, m_intra, m_sum)


def _s5b_kernel(pr_ref, pi_ref, ar_ref, ai_ref, hr_ref, hi_ref, *, nb, nc):
    ar = ar_ref[...]
    ai = ai_ref[...]
    ns = S5_SCAN_SLABS

    def step(c, carry):
        hr, hi = carry
        rows = pl.ds(c, nb, stride=nc)
        for k in range(ns):
            hr_ref[k, rows, :] = hr[:, k * LANES:(k + 1) * LANES]
            hi_ref[k, rows, :] = hi[:, k * LANES:(k + 1) * LANES]
        pr = jnp.concatenate([pr_ref[k, rows, :] for k in range(ns)], axis=1)
        pi = jnp.concatenate([pi_ref[k, rows, :] for k in range(ns)], axis=1)
        return (ar * hr - ai * hi + pr, ar * hi + ai * hr + pi)

    z = jnp.zeros((nb, ns * LANES), F32)
    lax.fori_loop(0, nc, step, (z, z))


def _s5b(p, a_re, a_im, nc, nb):
    r = p.shape[1]
    ns = S5_SCAN_SLABS
    half = S5_SLABS // 2
    re_blk = pl.BlockSpec((ns, nb * nc, LANES), lambda i, j: (j, i, 0))
    im_blk = pl.BlockSpec((ns, nb * nc, LANES), lambda i, j: (j + half // ns, i, 0))
    a_blk = pl.BlockSpec((1, ns * LANES), lambda i, j: (0, j))
    return pl.pallas_call(
        functools.partial(_s5b_kernel, nb=nb, nc=nc), grid=(r // (nb * nc), half // ns),
        in_specs=[re_blk, im_blk, a_blk, a_blk],
        out_specs=[re_blk, re_blk],
        out_shape=[jax.ShapeDtypeStruct((half, r, LANES), F32)] * 2,
        compiler_params=_cparams(("parallel", "parallel")), name="s5_scan",
    )(p, p, a_re, a_im)


def _s5c_kernel(yi_ref, hr_ref, hi_ref, mo_ref, y_ref):
    half = S5_STATE // 2
    hr = jnp.concatenate([hr_ref[k] for k in range(S5_SLABS // 2)], axis=1)
    hi = jnp.concatenate([hi_ref[k] for k in range(S5_SLABS // 2)], axis=1)
    y = yi_ref[...]
    y = y + _dot(hr.astype(BF16), mo_ref[:half, :])
    y = y + _dot(hi.astype(BF16), mo_ref[half:, :])
    y_ref[...] = y


def _s5c(yi, h_re, h_im, m_out, tr=512):
    r = yi.shape[0]
    row = pl.BlockSpec((tr, S5_COLS), lambda i: (i, 0))
    slabs = pl.BlockSpec((S5_SLABS // 2, tr, LANES), lambda i: (0, i, 0))
    return pl.pallas_call(
        _s5c_kernel, grid=(r // tr,),
        in_specs=[row, slabs, slabs, _resident((S5_STATE, S5_COLS))],
        out_specs=row, out_shape=jax.ShapeDtypeStruct((r, S5_COLS), F32),
        compiler_params=_cparams(("parallel",)), name="s5_out")(yi, h_re, h_im, m_out)


def _toeplitz_kernel(lag_ref, o_ref):
    j = pl.program_id(0)
    t, w, _ = lag_ref.shape
    for i in range(t):
        blk = lag_ref[jnp.maximum(i - j, 0)]
        o_ref[:, i * w:(i + 1) * w] = jnp.where(i >= j, blk, jnp.zeros_like(blk))


def _block_toeplitz(lag_blocks):
    t, w, _ = lag_blocks.shape
    return pl.pallas_call(
        _toeplitz_kernel, grid=(t,),
        in_specs=[_resident((t, w, w))],
        out_specs=pl.BlockSpec((w, t * w), lambda j: (j, 0)),
        out_shape=jax.ShapeDtypeStruct((t * w, t * w), lag_blocks.dtype), name="s5_toeplitz",
    )(lag_blocks)


def _s5_matrices(lam_re, lam_im, log_dt, b_re, b_im, c_re, c_im):
    hp = lax.Precision.HIGHEST
    t = S5_CHUNK
    dt = jnp.exp(log_dt)[:, None]
    lr, li = lam_re, lam_im
    mag = jnp.exp(lr * dt)
    ab_re, ab_im = mag * jnp.cos(li * dt), mag * jnp.sin(li * dt)
    den = lr * lr + li * li
    nr, ni = ab_re - 1.0, ab_im
    s_re = (nr * lr + ni * li) / den
    s_im = (ni * lr - nr * li) / den
    bb_re = s_re[..., None] * b_re - s_im[..., None] * b_im
    bb_im = s_re[..., None] * b_im + s_im[..., None] * b_re
    pr, pi = [jnp.ones_like(ab_re)], [jnp.zeros_like(ab_re)]
    for _ in range(t):
        pr.append(pr[-1] * ab_re - pi[-1] * ab_im)
        pi.append(pr[-2] * ab_im + pi[-1] * ab_re)
    pw_re, pw_im = jnp.stack(pr), jnp.stack(pi)
    lo_re, lo_im = jnp.stack(pr[:t]), jnp.stack(pi[:t])
    ab_b_re = lo_re[..., None] * bb_re - lo_im[..., None] * bb_im
    ab_b_im = lo_re[..., None] * bb_im + lo_im[..., None] * bb_re
    kern = (jnp.einsum('gpn,dgnq->dgpq', c_re, ab_b_re, precision=hp)
            - jnp.einsum('gpn,dgnq->dgpq', c_im, ab_b_im, precision=hp))
    eye_g = jnp.eye(SSM_GROUPS, dtype=F32)
    rows = np.arange(SSM_WIDTH)
    expand = jnp.asarray((rows[None, :] % SSM_GROUP == np.arange(SSM_GROUP)[:, None]).astype(np.float32))
    same_group = jnp.asarray(rows[:, None] // SSM_GROUP == rows[None, :] // SSM_GROUP)
    lag_rows = kern.transpose(0, 1, 3, 2).reshape(t, SSM_WIDTH, SSM_GROUP)
    lag_blocks = jnp.where(same_group, jnp.einsum('drp,pc->drc', lag_rows, expand, precision=hp), 0.0)
    m_intra = _block_toeplitz(lag_blocks.astype(BF16))
    hi_re, hi_im = jnp.stack(pr[t - 1::-1]), jnp.stack(pi[t - 1::-1])
    sb_re = hi_re[..., None] * bb_re - hi_im[..., None] * bb_im
    sb_im = hi_re[..., None] * bb_im + hi_im[..., None] * bb_re
    m_sum_re = jnp.einsum('jgnq,gh->jgqhn', sb_re, eye_g, precision=hp)
    m_sum_im = jnp.einsum('jgnq,gh->jgqhn', sb_im, eye_g, precision=hp)
    m_sum = jnp.concatenate([m_sum_re.reshape(S5_COLS, -1), m_sum_im.reshape(S5_COLS, -1)], axis=1)
    up_re, up_im = jnp.stack(pr[1:])[:, :, None, :], jnp.stack(pi[1:])[:, :, None, :]
    w_re = c_re[None] * up_re - c_im[None] * up_im
    w_im = c_re[None] * up_im + c_im[None] * up_re
    m_out_re = jnp.einsum('igpn,gh->gnihp', w_re, eye_g, precision=hp)
    m_out_im = jnp.einsum('igpn,gh->gnihp', -w_im, eye_g, precision=hp)
    m_out = jnp.concatenate([m_out_re.reshape(-1, S5_COLS), m_out_im.reshape(-1, S5_COLS)], axis=0)
    a_re = pw_re[t].reshape(1, -1)
    a_im = pw_im[t].reshape(1, -1)
    return m_intra.astype(BF16), m_sum.astype(BF16), m_out.astype(BF16), a_re, a_im


def _fold_rows(x, op):
    parts = [x[r:r + SUBLANES] for r in range(0, x.shape[0], SUBLANES)]
    while len(parts) > 1:
        parts = [op(parts[i], parts[i + 1]) for i in range(0, len(parts), 2)]
    return parts[0]


def _softmax_stage(s, m_ref, alpha_ref, p_ref):
    m_old = m_ref[...]
    m_new = jnp.maximum(m_old, jnp.max(_fold_rows(s, jnp.maximum), axis=0, keepdims=True))
    alpha_ref[...] = jnp.exp2(m_old - m_new)
    p_ref[...] = jnp.exp2(s - jnp.broadcast_to(m_new[0:1], s.shape)).astype(BF16)
    m_ref[...] = m_new


def _value_stage(pv, alpha_ref, p_ref, acc_ref):
    acc = acc_ref[...]
    acc_ref[...] = acc * jnp.broadcast_to(alpha_ref[0:1, :], acc.shape) + pv(p_ref[...])


def _attend_chunks(qt, qk, logits, pv, bufs, m_ref, acc_ref):
    s_a, s_b, p_a, p_b, al_a, al_b = bufs

    def plain(j, kind):
        qk(j, s_a)
        _softmax_stage(logits(j, s_a[...], kind), m_ref, al_a, p_a)
        _value_stage(lambda p: pv(j, p), al_a, p_a, acc_ref)

    @pl.when(qt == 0)
    def _():
        plain(0, 'diag')

    @pl.when(qt >= 1)
    def _():
        n_far = qt - 1
        first = n_far % 2

        @pl.when(first == 1)
        def _():
            plain(0, 'far')

        qk(first, s_a)
        p_b[...] = jnp.zeros(p_b.shape, p_b.dtype)
        al_b[...] = jnp.ones(al_b.shape, al_b.dtype)

        def pair(c, kind0, kind1, last):
            qk(c + 1, s_b)
            _value_stage(lambda p: pv(jnp.maximum(c - 1, 0), p), al_b, p_b, acc_ref)
            _softmax_stage(logits(c, s_a[...], kind0), m_ref, al_a, p_a)
            if not last:
                qk(c + 2, s_a)
            _value_stage(lambda p: pv(c, p), al_a, p_a, acc_ref)
            _softmax_stage(logits(c + 1, s_b[...], kind1), m_ref, al_b, p_b)

        def body(i, carry):
            pair(first + 2 * i, 'far', 'far', False)
            return carry

        lax.fori_loop(0, (n_far - first) // 2, body, 0)
        pair(qt - 1, 'prev', 'diag', True)
        _value_stage(lambda p: pv(qt, p), al_b, p_b, acc_ref)


def _pipeline_buffers(width):
    return ([pltpu.VMEM((ATT_TILE, width), F32)] * 2 + [pltpu.VMEM((ATT_TILE, width), BF16)] * 2
            + [pltpu.VMEM((SUBLANES, width), F32)] * 2)


def _finish_heads_t(acc_ref, o_ref, n_heads, odd_swapped):
    hd = HEAD_DIM
    tq = o_ref.shape[0]
    for pr in range(n_heads // 2):
        ae = acc_ref[:, 2 * pr * tq:(2 * pr + 1) * tq]
        ao = acc_ref[:, (2 * pr + 1) * tq:(2 * pr + 2) * tq]
        oe = ae[0:hd] / ae[hd:2 * hd]
        oo = ao[hd:2 * hd] / ao[0:hd] if odd_swapped else ao[0:hd] / ao[hd:2 * hd]
        pair = jnp.concatenate([oe, oo], axis=0)
        o_ref[:, pr * LANES:(pr + 1) * LANES] = pair.T.astype(o_ref.dtype)


def _heads_side_by_side(qt_ref, qm_ref, n_heads):
    tq = qt_ref.shape[1]
    for h in range(n_heads):
        qm_ref[:, h * tq:(h + 1) * tq] = qt_ref[h * HEAD_DIM:(h + 1) * HEAD_DIM, :]


def _dsa_kernel(q_ref, qi_ref, wi_ref, kv_ref, ki_ref, band_ref, o_ref,
                score_ref, vt_ref, qm_ref, m_ref, acc_ref, thr_ref, *bufs, n_keys, seq):
    tq = tk = ATT_TILE
    qt = pl.program_id(1)
    n_chunks = qt + 1
    kf = float(n_keys)

    @pl.when(qt == 0)
    def _():
        top = lax.broadcasted_iota(jnp.int32, (LANES, tk), 0) < HEAD_DIM
        for j in range(seq // tk):
            vv = kv_ref[j * tk:(j + 1) * tk, LANES:2 * LANES].astype(F32)
            vt_ref[j] = jnp.where(top, vv.T, 1.0).astype(BF16)

    _heads_side_by_side(qi_ref, qm_ref, IDX_HEADS)
    head_w = jnp.concatenate(
        [wi_ref[h:h + 1, :] for h in range(IDX_HEADS)], axis=1
    ) * (IDX_HEADS ** -0.5 * IDX_DIM ** -0.5)
    key_minus_query = (lax.broadcasted_iota(jnp.int32, (tk, tq), 0)
                       - lax.broadcasted_iota(jnp.int32, (tk, tq), 1))

    def score_body(j, carry):
        mx, mn, n_ge0, n_gt0 = carry
        kc = ki_ref[pl.ds(pl.multiple_of(j * tk, tk), tk), 0:IDX_DIM]
        d = _dot(kc, qm_ref[:, 0:IDX_HEADS * tq])
        d = jnp.broadcast_to(head_w, d.shape) * jnp.maximum(d, 0.0)
        sc = (d[:, 0:tq] + d[:, tq:2 * tq]) + (d[:, 2 * tq:3 * tq] + d[:, 3 * tq:4 * tq])
        causal = key_minus_query <= (qt - j) * tq
        masked = jnp.where(causal, sc, NEG)
        score_ref[j] = masked
        mx = jnp.maximum(mx, _fold_rows(masked, jnp.maximum))
        mn = jnp.minimum(mn, _fold_rows(jnp.where(causal, sc, -NEG), jnp.minimum))
        n_ge0 = n_ge0 + _fold_rows(jnp.where(masked >= 0.0, 1.0, 0.0), jnp.add)
        n_gt0 = n_gt0 + _fold_rows(jnp.where(masked > 0.0, 1.0, 0.0), jnp.add)
        return mx, mn, n_ge0, n_gt0

    zeros8 = jnp.zeros((SUBLANES, tq), F32)
    mx, mn, n_ge0, n_gt0 = lax.fori_loop(
        0, n_chunks, score_body,
        (jnp.full((SUBLANES, tq), NEG, F32), jnp.full((SUBLANES, tq), -NEG, F32), zeros8, zeros8))

    t_q = qt * tq + lax.broadcasted_iota(jnp.int32, (1, tq), 1)
    thr_ref[...] = jnp.full((SUBLANES, tq), 0.5 * NEG, F32)

    def count(pred):
        def body(j, acc):
            return acc + _fold_rows(pred(score_ref[j]), jnp.add)
        acc = lax.fori_loop(0, n_chunks, body, jnp.zeros((SUBLANES, tq), F32))
        return jnp.sum(acc, axis=0, keepdims=True)

    def count_ge(x):
        xb = jnp.broadcast_to(x, (tk, tq))
        return count(lambda s: jnp.where(s >= xb, 1.0, 0.0))

    @pl.when((qt + 1) * tq > n_keys)
    def _search():
        need = (t_q >= n_keys).astype(F32)
        lo0 = jnp.min(mn, axis=0, keepdims=True)
        hi0 = jnp.max(mx, axis=0, keepdims=True)
        hi_top = hi0 + jnp.maximum(jnp.abs(hi0) * 2.0 ** -22, 1e-30)
        c_ge0 = jnp.sum(n_ge0, axis=0, keepdims=True)
        c_gt0 = jnp.sum(n_gt0, axis=0, keepdims=True)
        n_valid = (t_q + 1).astype(F32)
        zero_hit = c_ge0 == kf
        zero_tie = (c_gt0 < kf) & (c_ge0 > kf)
        positive = c_gt0 >= kf
        hi_neg = hi0 < 0.0
        lo1 = jnp.where(positive, jnp.maximum(lo0, 0.0), lo0)
        c_lo1 = jnp.where(positive, c_ge0, n_valid)
        hi1 = jnp.where(positive | hi_neg, hi_top, 0.0)
        c_hi1 = jnp.where(positive | hi_neg, 0.0, c_ge0)
        hit0 = need * zero_hit.astype(F32)
        tie0 = need * zero_tie.astype(F32)
        done0 = jnp.maximum(1.0 - need, jnp.maximum(hit0, tie0))
        lo1 = jnp.where(tie0 > 0, 0.0, lo1)
        c_hi1 = jnp.where(tie0 > 0, c_gt0, c_hi1)
        x0 = jnp.zeros((1, tq), F32)
        state0 = (lo1, hi1, c_lo1, c_hi1, done0, tie0, x0, jnp.max(1.0 - done0))

        def cond(st):
            return st[7] > 0.0

        def body(st):
            for _ in range(SEARCH_UNROLL):
                st = step(st)
            return st

        def step(st):
            lo, hi, c_lo, c_hi, done, tie, x_fin, _ = st
            x = 0.5 * lo + 0.5 * hi
            inside = ((x > lo) & (x < hi)).astype(F32)
            active = 1.0 - done
            probe = active * inside
            c = count_ge(jnp.where(probe > 0, x, x_fin))
            hit = probe * (c == kf).astype(F32)
            more = probe * (c > kf).astype(F32)
            less = probe * (c < kf).astype(F32)
            new_tie = active * (1.0 - inside)
            lo = jnp.where(more > 0, x, lo)
            c_lo = jnp.where(more > 0, c, c_lo)
            hi = jnp.where(less > 0, x, hi)
            c_hi = jnp.where(less > 0, c, c_hi)
            x_fin = jnp.where(hit > 0, x, x_fin)
            tie = jnp.maximum(tie, new_tie)
            done = jnp.maximum(done, jnp.maximum(hit, new_tie))
            return (lo, hi, c_lo, c_hi, done, tie, x_fin, jnp.max(1.0 - done))

        st = state0
        for _ in range(SEARCH_FIXED):
            st = step(st)
        st = lax.while_loop(cond, body, st)
        lo, _, _, c_hi, _, tie, x_fin, _ = st
        thr = jnp.where(need > 0, jnp.where(tie > 0, lo, x_fin), 0.5 * NEG)
        thr_ref[...] = jnp.broadcast_to(thr, (SUBLANES, tq))

        @pl.when(jnp.max(tie) > 0.0)
        def _ties():
            want = jnp.broadcast_to(kf - c_hi, (tk, tq))
            thr_t = jnp.broadcast_to(jnp.where(tie > 0, thr, -NEG), (tk, tq))
            upto = (lax.broadcasted_iota(jnp.int32, (tk, tk), 1)
                    <= lax.broadcasted_iota(jnp.int32, (tk, tk), 0))
            prefix = jnp.where(upto, 1.0, 0.0).astype(BF16)

            def body(j, seen):
                s = score_ref[j]
                eq = s == thr_t
                rank = _dot(prefix, jnp.where(eq, 1.0, 0.0).astype(BF16))
                rank = rank + jnp.broadcast_to(seen, (tk, tq))
                score_ref[j] = jnp.where(eq, jnp.where(rank > want, NEG, s), s)
                return rank[tk - 1:tk, :]

            lax.fori_loop(0, n_chunks, body, jnp.zeros((1, tq), F32))

    _heads_side_by_side(q_ref, qm_ref, DSA_HEADS)
    m_ref[...] = jnp.full(m_ref.shape, NEG, F32)
    acc_ref[...] = jnp.zeros(acc_ref.shape, F32)
    thr_b = jnp.broadcast_to(thr_ref[0:1, :], (tk, tq))

    def qk(j, s_ref):
        rows = pl.ds(pl.multiple_of(j * tk, tk), tk)
        s_ref[...] = _dot(kv_ref[rows, 0:HEAD_DIM], qm_ref[...])

    def logits(j, s, kind):
        off = jnp.where(score_ref[j] >= thr_b, 0.0, NEG)
        s = s + jnp.concatenate([off] * DSA_HEADS, axis=1)
        if kind == 'prev':
            s = s + band_ref[0:tk, :]
        elif kind == 'diag':
            s = s + band_ref[tk:2 * tk, :]
        return s

    _attend_chunks(qt, qk, logits, lambda j, p: _dot(vt_ref[j], p), bufs, m_ref, acc_ref)
    _finish_heads_t(acc_ref, o_ref, DSA_HEADS, odd_swapped=False)


def _dsa(qb_t, qi_t, wi_t, kv, ki, band, bsz, seq, n_keys):
    tq = ATT_TILE
    nq = seq // tq
    qrow = lambda w: pl.BlockSpec((tq, w), lambda b, q: (b * nq + q, 0))
    qtile = lambda w: pl.BlockSpec((None, w, tq), lambda b, q: (b * nq + q, 0, 0))
    seq_blk = lambda w: pl.BlockSpec((seq, w), lambda b, q: (b, 0))
    return pl.pallas_call(
        functools.partial(_dsa_kernel, n_keys=n_keys, seq=seq), grid=(bsz, nq),
        in_specs=[qtile(DSA_WIDTH), qtile(IDX_HEADS * IDX_DIM), qtile(SUBLANES), seq_blk(2 * LANES),
                  seq_blk(LANES), _resident((2 * tq, DSA_HEADS * tq))],
        out_specs=qrow(DSA_WIDTH),
        out_shape=jax.ShapeDtypeStruct((bsz * seq, DSA_WIDTH), BF16),
        scratch_shapes=[pltpu.VMEM((nq, tq, tq), F32),
                        pltpu.VMEM((nq, LANES, tq), BF16),
                        pltpu.VMEM((HEAD_DIM, DSA_HEADS * tq), BF16),
                        pltpu.VMEM((SUBLANES, DSA_HEADS * tq), F32),
                        pltpu.VMEM((LANES, DSA_HEADS * tq), F32),
                        pltpu.VMEM((SUBLANES, tq), F32)] + _pipeline_buffers(DSA_HEADS * tq),
        compiler_params=_cparams(("parallel", "arbitrary")), name="dsa",
    )(qb_t, qi_t, wi_t, kv, ki, band)


def _band_kernel(bucket_ref, rb_ref, o_ref):
    bucket = bucket_ref[...]
    tq = bucket.shape[1]
    for h in range(DSA_HEADS):
        acc = jnp.zeros(bucket.shape, F32)
        for k in range(REL_BUCKETS - 1):
            acc = jnp.where(bucket == k, (rb_ref[k, h] - rb_ref[REL_BUCKETS - 1, h]) * LOG2E, acc)
        o_ref[:, h * tq:(h + 1) * tq] = acc


def _t5_bucket_np(dist):
    d = np.maximum(dist, 0)
    df = np.maximum(d, 1).astype(np.float32)
    log_b = REL_MAX_EXACT + (np.log(df / REL_MAX_EXACT) / math.log(REL_MAX_DIST / REL_MAX_EXACT)
                             * (REL_BUCKETS - REL_MAX_EXACT)).astype(np.int32)
    log_b = np.minimum(log_b, REL_BUCKETS - 1)
    return np.where(d < REL_MAX_EXACT, d, log_b).astype(np.int32)


def _rel_band(rel_bias):
    tq = ATT_TILE
    i = np.arange(tq)[None, :]
    j = np.arange(2 * tq)[:, None]
    bucket = jnp.asarray(_t5_bucket_np(i + tq - j))
    return pl.pallas_call(
        _band_kernel,
        in_specs=[pl.BlockSpec(memory_space=pltpu.VMEM), pl.BlockSpec(memory_space=pltpu.SMEM)],
        out_specs=pl.BlockSpec(memory_space=pltpu.VMEM),
        out_shape=jax.ShapeDtypeStruct((2 * tq, DSA_HEADS * tq), F32), name="rel_band",
    )(bucket, rel_bias)


def _cum_kernel(aux_ref, bf_ref, col_ref, row_ref, *, seq):
    x = aux_ref[...] + bf_ref[...]
    logf = -(jnp.maximum(-x, 0.0) + jnp.log1p(jnp.exp(-jnp.abs(x))))
    c = logf.T[0:SUBLANES]
    pos = lax.broadcasted_iota(jnp.int32, c.shape, 1)
    shift = 1
    while shift < seq:
        c = c + jnp.where(pos >= shift, pltpu.roll(c, shift, 1), 0.0)
        shift *= 2
    c = c * LOG2E
    col_ref[...] = jnp.concatenate([c, jnp.zeros((LANES - SUBLANES, seq), F32)], axis=0).T
    for j in range(seq // ATT_TILE):
        row_ref[j] = c[:, j * ATT_TILE:(j + 1) * ATT_TILE]


def _fox_cum(aux, b_f, bsz, seq):
    nq = seq // ATT_TILE
    bf = jnp.zeros((1, LANES), F32).at[0, AUX_F:AUX_F + FOX_HEADS].set(b_f)
    return pl.pallas_call(
        functools.partial(_cum_kernel, seq=seq), grid=(bsz,),
        in_specs=[pl.BlockSpec((seq, LANES), lambda b: (b, 0)), _resident((1, LANES))],
        out_specs=[pl.BlockSpec((seq, LANES), lambda b: (b, 0)),
                   pl.BlockSpec((None, nq, SUBLANES, ATT_TILE), lambda b: (b, 0, 0, 0))],
        out_shape=[jax.ShapeDtypeStruct((bsz * seq, LANES), F32),
                   jax.ShapeDtypeStruct((bsz, nq, SUBLANES, ATT_TILE), F32)],
        compiler_params=_cparams(("parallel",)), name="fox_cum")(aux, bf)


def _bf16_pieces(x):
    hi = x.astype(BF16).astype(F32)
    rest = x - hi
    mid = rest.astype(BF16).astype(F32)
    return [hi, mid, (rest - mid).astype(BF16).astype(F32)]


def _fox_kernel(q_ref, k_ref, v_ref, ccol_ref, crow_ref, o_ref,
                vte_ref, vto_ref, ka_ref, qm_ref, m_ref, acc_ref, *bufs, seq):
    tq = tk = ATT_TILE
    qt = pl.program_id(1)
    heads = range(FOX_HEADS)

    @pl.when(qt == 0)
    def _():
        top = lax.broadcasted_iota(jnp.int32, (LANES, tk), 0) < HEAD_DIM
        for j in range(seq // tk):
            rows = slice(j * tk, (j + 1) * tk)
            for pr in range(FOX_HEADS // 2):
                v_t = v_ref[rows, pr * LANES:(pr + 1) * LANES].astype(F32).T
                vte_ref[pr, j] = jnp.where(top, v_t, 1.0).astype(BF16)
                vto_ref[pr, j] = jnp.where(top, 1.0, v_t).astype(BF16)
            cc = ccol_ref[rows, :]
            lane = lax.broadcasted_iota(jnp.int32, (tk, LANES), 1)
            for h in heads:
                k_pair = k_ref[rows, (h // 2) * LANES:(h // 2 + 1) * LANES].astype(F32)
                base = HEAD_DIM * (1 - h % 2)
                pieces = _bf16_pieces(jnp.broadcast_to(cc[:, AUX_F + h:AUX_F + h + 1], (tk, LANES)))
                ext = jnp.where((lane >= base) & (lane < base + 2 * len(pieces)), 1.0, 0.0)
                for i, piece in enumerate(pieces):
                    ext = jnp.where(lane == base + i, piece, ext)
                own = (lane < HEAD_DIM) if h % 2 == 0 else (lane >= HEAD_DIM)
                ka_ref[h, rows, :] = jnp.where(own, k_pair, ext).astype(BF16)

    c_q = crow_ref[qt]
    row = lax.broadcasted_iota(jnp.int32, (LANES - HEAD_DIM, tq), 0)
    for h in heads:
        pieces = _bf16_pieces(c_q[AUX_F + h:AUX_F + h + 1, :])
        ext = jnp.where(row < len(pieces), -1.0, 0.0)
        for i, piece in enumerate(pieces):
            ext = jnp.where(row == len(pieces) + i, jnp.broadcast_to(piece, ext.shape), ext)
        q_h = q_ref[h * HEAD_DIM:(h + 1) * HEAD_DIM, :]
        halves = [q_h, ext.astype(BF16)] if h % 2 == 0 else [ext.astype(BF16), q_h]
        qm_ref[:, h * tq:(h + 1) * tq] = jnp.concatenate(halves, axis=0)
    m_ref[...] = jnp.full(m_ref.shape, NEG, F32)
    acc_ref[...] = jnp.zeros(acc_ref.shape, F32)
    causal_off = jnp.where(lax.broadcasted_iota(jnp.int32, (tk, tq), 0)
                           <= lax.broadcasted_iota(jnp.int32, (tk, tq), 1), 0.0, NEG)

    def qk(j, s_ref):
        rows = pl.ds(pl.multiple_of(j * tk, tk), tk)
        for h in heads:
            cols = slice(h * tq, (h + 1) * tq)
            s_ref[:, cols] = _dot(ka_ref[h, rows, :], qm_ref[:, cols])

    def logits(j, s, kind):
        if kind == 'diag':
            s = s + jnp.concatenate([causal_off] * FOX_HEADS, axis=1)
        return s

    def pv(j, p):
        return jnp.concatenate(
            [_dot(vte_ref[h // 2, j] if h % 2 == 0 else vto_ref[h // 2, j],
                  p[:, h * tq:(h + 1) * tq]) for h in heads], axis=1)

    _attend_chunks(qt, qk, logits, pv, bufs, m_ref, acc_ref)
    _finish_heads_t(acc_ref, o_ref, FOX_HEADS, odd_swapped=True)


def _fox(q_t, kv, c_col, c_row, bsz, seq):
    tq = ATT_TILE
    nq = seq // tq
    npair = FOX_HEADS // 2
    return pl.pallas_call(
        functools.partial(_fox_kernel, seq=seq), grid=(bsz, nq),
        in_specs=[pl.BlockSpec((None, FOX_WIDTH, tq), lambda b, q: (b * nq + q, 0, 0)),
                  pl.BlockSpec((seq, FOX_WIDTH), lambda b, q: (b, 0)),
                  pl.BlockSpec((seq, FOX_WIDTH), lambda b, q: (b, 1)),
                  pl.BlockSpec((seq, LANES), lambda b, q: (b, 0)),
                  pl.BlockSpec((None, nq, SUBLANES, tq), lambda b, q: (b, 0, 0, 0))],
        out_specs=pl.BlockSpec((tq, FOX_WIDTH), lambda b, q: (b * nq + q, 0)),
        out_shape=jax.ShapeDtypeStruct((bsz * seq, FOX_WIDTH), BF16),
        scratch_shapes=[pltpu.VMEM((npair, nq, LANES, tq), BF16),
                        pltpu.VMEM((npair, nq, LANES, tq), BF16),
                        pltpu.VMEM((FOX_HEADS, seq, LANES), BF16),
                        pltpu.VMEM((LANES, FOX_HEADS * tq), BF16),
                        pltpu.VMEM((SUBLANES, FOX_HEADS * tq), F32),
                        pltpu.VMEM((LANES, FOX_HEADS * tq), F32)] + _pipeline_buffers(FOX_HEADS * tq),
        compiler_params=_cparams(("parallel", "arbitrary")), name="fox",
    )(q_t, kv, kv, c_col, c_row)


def _gelu(y):
    return 0.5 * y * (1.0 + jnp.tanh(math.sqrt(2.0 / math.pi) * (y + 0.044715 * (y * y * y))))


def _merge_kernel(x_ref, yc_ref, u_ref, d_ref, wglu_ref, b_ref, c_ref, gate_ref,
                  wa_ref, wb_ref, wc_ref, wo_ref, o_ref, tmp_ref):
    d = D_MODEL
    y = _chunks_to_rows(yc_ref, tmp_ref, SSM_WIDTH) + d_ref[...] * u_ref[...]
    z = _dot(_gelu(y).astype(BF16), wglu_ref[...])
    a = z[:, :SSM_WIDTH] * _gate(z[:, SSM_WIDTH:])
    merged = _gate(gate_ref[:, 0:d].astype(F32)) * _dot(a.astype(BF16), wa_ref[...])
    merged += _gate(gate_ref[:, d:2 * d].astype(F32)) * _dot(b_ref[...], wb_ref[...])
    merged += _gate(gate_ref[:, 2 * d:3 * d].astype(F32)) * _dot(c_ref[...], wc_ref[...])
    o_ref[...] = x_ref[...] + _dot(merged.astype(BF16), wo_ref[...])


def _merge(xt, yc, u, d_skip, w_glu, b, c, gates, wa, wb, wc, wo, tm=1024):
    m, d = xt.shape
    row = lambda w: pl.BlockSpec((tm, w), lambda i: (i, 0))
    return pl.pallas_call(
        _merge_kernel, grid=(m // tm,),
        in_specs=[row(d), pl.BlockSpec((tm // S5_CHUNK, S5_COLS), lambda i: (i, 0)),
                  row(SSM_WIDTH), _resident((1, SSM_WIDTH)),
                  _resident((SSM_WIDTH, 2 * SSM_WIDTH)), row(DSA_WIDTH), row(FOX_WIDTH), row(3 * d),
                  _resident((SSM_WIDTH, d)), _resident((DSA_WIDTH, d)), _resident((FOX_WIDTH, d)),
                  _resident((d, d))],
        out_specs=row(d), out_shape=jax.ShapeDtypeStruct((m, d), F32),
        scratch_shapes=[pltpu.VMEM((SSM_WIDTH // LANES, tm, LANES), F32)],
        compiler_params=_cparams(("parallel",)), name="merge",
    )(xt, yc, u, d_skip.reshape(1, -1), w_glu, b, c, gates, wa, wb, wc, wo)


def _pack_w_in(w_in):
    d = w_in.shape[0]
    splits = (SSM_WIDTH, DSA_WIDTH, DSA_LATENT, IDX_HEADS * IDX_DIM, IDX_DIM, IDX_HEADS,
              FOX_WIDTH, FOX_WIDTH, FOX_WIDTH, FOX_HEADS, D_MODEL, D_MODEL, D_MODEL)
    pts = np.cumsum(splits)[:-1]
    (w_u, w_qb, w_ckv, w_qi, w_ki, w_wi, w_qc, w_kc, w_vc, w_fc, w_ga, w_gb, w_gc) = jnp.split(
        w_in, pts, axis=1)
    scale = HEAD_DIM ** -0.5 * LOG2E
    aux = jnp.zeros((d, LANES), F32)
    aux = aux.at[:, AUX_F:AUX_F + FOX_HEADS].set(w_fc).at[:, AUX_W:AUX_W + IDX_HEADS].set(w_wi)
    cols = [w_u, w_qb * scale, w_qi, w_ki, w_ki, w_ckv, aux, w_qc * scale, w_kc, w_vc,
            w_ga, w_gb, w_gc]
    return jnp.concatenate(cols, axis=1).astype(BF16)


def kernel(x, ffn1_norm, ffn1_w_gate, ffn1_w_up, ffn1_w_down, mix_norm, w_in, ssm_lambda_re, ssm_lambda_im, ssm_log_dt, ssm_b_re, ssm_b_im, ssm_c_re, ssm_c_im, ssm_d, ssm_w_glu, dsa_kv_norm, dsa_w_uk, dsa_w_uv, rel_bias, fox_b_f, w_branch_ssm, w_branch_dsa, w_branch_fox, w_out, ffn2_norm, ffn2_w_gate, ffn2_w_up, ffn2_w_down, final_norm):
    bsz, seq, d = x.shape
    depth = w_in.shape[0]
    n_keys = min(TOPK_MAX, seq // 4)
    assert d == D_MODEL and seq % ATT_TILE == 0 and (bsz * seq) % (S5_CHUNK * 512) == 0
    assert n_keys % LANES == 0 and bsz % SUBLANES == 0
    m = bsz * seq
    n_chunks = seq // S5_CHUNK
    xt = x.reshape(m, d)
    band = _rel_band(rel_bias)
    bf = lambda w: w.astype(BF16)
    for l in range(depth):
        xt = _ffn(xt, ffn1_norm[l], bf(ffn1_w_gate[l]), bf(ffn1_w_up[l]), bf(ffn1_w_down[l]))
        w_kv = bf(jnp.concatenate([dsa_w_uk[l], dsa_w_uk[l], dsa_w_uv[l], dsa_w_uv[l]], axis=1))
        u, qb_t, qi_t, ki, kv, aux, wi_t, qc_t, kv_fox, gates, u8 = _inproj(
            xt, mix_norm[l], _pack_w_in(w_in[l]), dsa_kv_norm[l], w_kv)
        m_intra, m_sum, m_out, a_re, a_im = _s5_matrices(
            ssm_lambda_re[l], ssm_lambda_im[l], ssm_log_dt[l], ssm_b_re[l], ssm_b_im[l],
            ssm_c_re[l], ssm_c_im[l])
        y_intra, p = _s5a(u8, m_intra, m_sum)
        h_re, h_im = _s5b(p, a_re, a_im, n_chunks, nb=SUBLANES)
        yc = _s5c(y_intra, h_re, h_im, m_out)
        b_out = _dsa(qb_t, qi_t, wi_t, kv, ki, band, bsz, seq, n_keys)
        c_col, c_row = _fox_cum(aux, fox_b_f[l], bsz, seq)
        c_out = _fox(qc_t, kv_fox, c_col, c_row, bsz, seq)
        xt = _merge(xt, yc, u, ssm_d[l], bf(ssm_w_glu[l]), b_out, c_out, gates,
                    bf(w_branch_ssm[l]), bf(w_branch_dsa[l]), bf(w_branch_fox[l]), bf(w_out[l]))
        last = final_norm if l == depth - 1 else None
        xt = _ffn(xt, ffn2_norm[l], bf(ffn2_w_gate[l]), bf(ffn2_w_up[l]), bf(ffn2_w_down[l]),
                  final_norm=last)
    return xt.reshape(bsz, seq, d)
```
